```python
import math
import jax, jax.numpy as jnp
from jax import lax
import numpy as np

D_MODEL = 1024
BATCH = 1
SEQ = 16384
DEPTH = 4

GRID_W = 64
CTX_LEN = 256
HEAD_DIM = 64
BLK = 128
WINDOW = 128
ROPE_THETA = 10000.0
EPS = 1e-6
NEG = -1e30
N_HEADS_A = 8
N_KV_A = 2
GROUP_A = N_HEADS_A // N_KV_A
N_HEADS_B = 8
N_KV_B = 2
GROUP_B = N_HEADS_B // N_KV_B
QA_W = N_HEADS_A * HEAD_DIM
KVA_W = N_KV_A * HEAD_DIM
QB_W = N_HEADS_B * HEAD_DIM
KVB_W = N_KV_B * HEAD_DIM
IN_AB = QA_W + 2 * KVA_W + QB_W + 2 * KVB_W
AB_SPLITS = (QA_W, QA_W + KVA_W, QA_W + 2 * KVA_W, QA_W + 2 * KVA_W + QB_W, QA_W + 2 * KVA_W + QB_W + KVB_W)
MIX_AB = QA_W + QB_W
N_HEADS_C = 8
DV_C = 2 * HEAD_DIM
IN_C = 3 * N_HEADS_C * DV_C
MIX_C = N_HEADS_C * DV_C
N_GROUPS = 4
EXPERTS_PER_GROUP = 8
N_EXPERTS = N_GROUPS * EXPERTS_PER_GROUP
TOP_K = 2
D_EXPERT = 512
N_EVEN = (DEPTH + 1) // 2
N_ODD = DEPTH // 2

kernel_name = 'hybrid_dit_window_axial_diffattn_hmoe'


def rms_norm(x, g):
    xf = x.astype(jnp.float32)
    y = xf * lax.rsqrt(jnp.mean(xf * xf, axis=-1, keepdims=True) + EPS)
    return (y * g.astype(jnp.float32)).astype(x.dtype)


def modulate(x, g, shift, scale):
    return rms_norm(x, g) * (1 + scale) + shift


def axial_rope_tables(rows_n):
    rows = jnp.broadcast_to(jnp.arange(rows_n, dtype=jnp.float32)[:, None], (rows_n, GRID_W)).reshape(-1)
    cols = jnp.broadcast_to(jnp.arange(GRID_W, dtype=jnp.float32)[None, :], (rows_n, GRID_W)).reshape(-1)
    half = HEAD_DIM // 2
    inv = ROPE_THETA ** (-jnp.arange(0, half, 2, dtype=jnp.float32) / half)
    ang = jnp.concatenate([rows[:, None] * inv, cols[:, None] * inv], axis=-1)
    return jnp.cos(ang), jnp.sin(ang)


def apply_rope(x, cos, sin):
    shp = (1, cos.shape[0]) + (1,) * (x.ndim - 3) + (cos.shape[1],)
    cos = cos.reshape(shp)
    sin = sin.reshape(shp)
    x1, x2 = jnp.split(x.astype(jnp.float32), 2, axis=-1)
    return jnp.concatenate([x1 * cos - x2 * sin, x2 * cos + x1 * sin], axis=-1).astype(x.dtype)


def gqa_sweep(q, k, v):
    b, l, hkv, g, d = q.shape
    qb = q.reshape(b, l // BLK, BLK, hkv, g, d).transpose(1, 0, 2, 3, 4, 5)
    scale = d ** -0.5

    def one_block(qi):
        s = jnp.einsum('bqhgd,bkhd->bhgqk', qi, k).astype(jnp.float32) * scale
        p = jax.nn.softmax(s, axis=-1).astype(v.dtype)
        return jnp.einsum('bhgqk,bkhd->bqhgd', p, v)

    o = lax.map(one_block, qb)
    return o.transpose(1, 0, 2, 3, 4, 5).reshape(b, l, hkv, g, d)


def window_sink_attention(q, k, v, kc, vc, sink):
    b, seq, hkv, g, d = q.shape
    nb = seq // BLK
    qb = q.reshape(b, nb, BLK, hkv, g, d)

    def bands(t):
        tp = jnp.pad(t, ((0, 0), (BLK, BLK), (0, 0), (0, 0))).reshape(b, nb + 2, BLK, hkv, d)
        return jnp.concatenate([tp[:, :-2], tp[:, 1:-1], tp[:, 2:]], axis=2)

    kw, vw = bands(k), bands(v)
    qi = jnp.arange(BLK)[:, None]
    ki = jnp.arange(3 * BLK)[None, :]
    rel = ki - BLK - qi
    kpos = jnp.arange(nb)[:, None, None] * BLK + ki[None] - BLK
    mask = (jnp.abs(rel)[None] <= WINDOW) & (kpos >= 0) & (kpos < seq)
    scale = d ** -0.5
    s_loc = jnp.einsum('bnqhgd,bnkhd->bnhgqk', qb, kw).astype(jnp.float32) * scale
    s_loc = jnp.where(mask[None, :, None, None], s_loc, NEG)
    s_ctx = jnp.einsum('bnqhgd,bchd->bnhgqc', qb, kc).astype(jnp.float32) * scale
    sink_col = jnp.broadcast_to(sink.astype(jnp.float32)[None, None, :, :, None, None], s_loc.shape[:-1] + (1,))
    p = jax.nn.softmax(jnp.concatenate([s_loc, s_ctx, sink_col], axis=-1), axis=-1).astype(v.dtype)
    o = (jnp.einsum('bnhgqk,bnkhd->bnqhgd', p[..., :3 * BLK], vw)
         + jnp.einsum('bnhgqc,bchd->bnqhgd', p[..., 3 * BLK:-1], vc))
    return o.reshape(b, seq, hkv, g, d)


def ctx_sink_attention(q, k, v, sink):
    scale = q.shape[-1] ** -0.5
    s = jnp.einsum('bqhgd,bkhd->bhgqk', q, k).astype(jnp.float32) * scale
    sink_col = jnp.broadcast_to(sink.astype(jnp.float32)[None, :, :, None, None], s.shape[:-1] + (1,))
    p = jax.nn.softmax(jnp.concatenate([s, sink_col], axis=-1), axis=-1)[..., :-1].astype(v.dtype)
    return jnp.einsum('bhgqk,bkhd->bqhgd', p, v)


def diff_sweep(q, k, v, lam):
    b, l, h, _, d = q.shape
    qb = q.reshape(b, l // BLK, BLK, h, 2, d).transpose(1, 0, 2, 3, 4, 5)
    scale = d ** -0.5

    def one_block(qi):
        s = jnp.einsum('bqhcd,bkhcd->bhcqk', qi, k).astype(jnp.float32) * scale
        p = jax.nn.softmax(s, axis=-1)
        a = p[:, :, 0] - lam * p[:, :, 1]
        return jnp.einsum('bhqk,bkhe->bqhe', a.astype(v.dtype), v)

    o = lax.map(one_block, qb)
    return o.transpose(1, 0, 2, 3, 4).reshape(b, l, h, v.shape[-1])


def mixer_ab(h_lat, h_ctx, cos, sin, w_in, w_out, qn_a, kn_a, sink_a, qn_b, kn_b, with_ctx_out):
    def project(h, rope):
        bb, ll, _ = h.shape
        qa, ka, va, qb, kb, vb = jnp.split(h @ w_in, AB_SPLITS, axis=-1)
        qa = rms_norm(qa.reshape(bb, ll, N_KV_A, GROUP_A, HEAD_DIM), qn_a)
        ka = rms_norm(ka.reshape(bb, ll, N_KV_A, HEAD_DIM), kn_a)
        va = va.reshape(bb, ll, N_KV_A, HEAD_DIM)
        qb = rms_norm(qb.reshape(bb, ll, N_KV_B, GROUP_B, HEAD_DIM), qn_b)
        kb = rms_norm(kb.reshape(bb, ll, N_KV_B, HEAD_DIM), kn_b)
        vb = vb.reshape(bb, ll, N_KV_B, HEAD_DIM)
        if rope:
            qa, ka = apply_rope(qa, cos, sin), apply_rope(ka, cos, sin)
            qb, kb = apply_rope(qb, cos, sin), apply_rope(kb, cos, sin)
        return qa, ka, va, qb, kb, vb

    b, seq, _ = h_lat.shape
    qa, ka, va, qb, kb, vb = project(h_lat, True)
    cqa, cka, cva, cqb, ckb, cvb = project(h_ctx, False)
    sink = sink_a.reshape(N_KV_A, GROUP_A)
    oa = window_sink_attention(qa, ka, va, cka, cva, sink)
    ob = gqa_sweep(qb, jnp.concatenate([kb, ckb], axis=1), jnp.concatenate([vb, cvb], axis=1))
    o_lat = jnp.concatenate([oa.reshape(b, seq, QA_W), ob.reshape(b, seq, QB_W)], axis=-1) @ w_out
    o_ctx = None
    if with_ctx_out:
        cl = h_ctx.shape[1]
        oca = ctx_sink_attention(cqa, cka, cva, sink)
        ocb = gqa_sweep(cqb, ckb, cvb)
        o_ctx = jnp.concatenate([oca.reshape(b, cl, QA_W), ocb.reshape(b, cl, QB_W)], axis=-1) @ w_out
    return o_lat, o_ctx


def mixer_c(h_lat, h_ctx, cos, sin, w_in, w_out, qn, kn, lam_p, subln, lam_init, with_ctx_out):
    lp = lam_p.astype(jnp.float32)
    lam = jnp.exp(jnp.sum(lp[0] * lp[1])) - jnp.exp(jnp.sum(lp[2] * lp[3])) + lam_init

    def project(h, rope):
        bb, ll, _ = h.shape
        q, k, v = jnp.split(h @ w_in, 3, axis=-1)
        q = rms_norm(q.reshape(bb, ll, N_HEADS_C, 2, HEAD_DIM), qn)
        k = rms_norm(k.reshape(bb, ll, N_HEADS_C, 2, HEAD_DIM), kn)
        v = v.reshape(bb, ll, N_HEADS_C, DV_C)
        if rope:
            q, k = apply_rope(q, cos, sin), apply_rope(k, cos, sin)
        return q, k, v

    def finish(o):
        o = rms_norm(o, subln) * (1 - lam_init)
        return o.reshape(o.shape[0], o.shape[1], MIX_C) @ w_out

    q_l, k_l, v_l = project(h_lat, True)
    q_c, k_c, v_c = project(h_ctx, False)
    o_lat = finish(diff_sweep(q_l, jnp.concatenate([k_l, k_c], axis=1), jnp.concatenate([v_l, v_c], axis=1), lam))
    o_ctx = finish(diff_sweep(q_c, k_c, v_c, lam)) if with_ctx_out else None
    return o_lat, o_ctx


def hier_moe(h, w_group, b_group, w_expert, b_expert, w1, w3, w2):
    t, d = h.shape
    pg = jax.nn.softmax((h @ w_group).astype(jnp.float32) + b_group.astype(jnp.float32), axis=-1)
    g_prob, g_idx = lax.top_k(pg, 1)
    le = ((h @ w_expert).astype(jnp.float32) + b_expert.astype(jnp.float32)).reshape(t, N_GROUPS, EXPERTS_PER_GROUP)
    le = jnp.take_along_axis(le, g_idx[:, :, None], axis=1)[:, 0]
    e_prob, e_idx = lax.top_k(jax.nn.softmax(le, axis=-1), TOP_K)
    weights = g_prob * e_prob / jnp.sum(e_prob, axis=-1, keepdims=True)
    expert_id = g_idx * EXPERTS_PER_GROUP + e_idx
    flat_e = expert_id.reshape(-1)
    flat_tok = jnp.repeat(jnp.arange(t), TOP_K)
    flat_w = weights.reshape(-1)
    order = jnp.argsort(flat_e)
    se, stok, sw = flat_e[order], flat_tok[order], flat_w[order]
    counts = jnp.bincount(flat_e, length=N_EXPERTS)
    padded = (counts + BLK - 1) // BLK * BLK
    start = jnp.cumsum(counts) - counts
    pend = jnp.cumsum(padded)
    dest = (pend - padded)[se] + jnp.arange(flat_e.shape[0]) - start[se]
    n_rows = (t * TOP_K + BLK - 1) // BLK * BLK + N_EXPERTS * BLK
    n_blk = n_rows // BLK
    buf = jnp.zeros((n_rows, d), h.dtype).at[dest].set(h[stok])
    blk_e = jnp.minimum(jnp.searchsorted(pend, jnp.arange(n_blk) * BLK, side='right'), N_EXPERTS - 1)

    def expert_block(args):
        xb, e = args
        return (jax.nn.silu(xb @ w1[e]) * (xb @ w3[e])) @ w2[e]

    yb = lax.map(expert_block, (buf.reshape(n_blk, BLK, d), blk_e))
    y = yb.reshape(n_rows, d)[dest] * sw[:, None].astype(h.dtype)
    return jnp.zeros_like(h).at[stok].add(y)


def setup_inputs(seed: int = 0) -> dict:
    key = jax.random.key(seed)
    ks = iter(jax.random.split(key, 40))

    def nrm(shape, scale):
        return jax.random.normal(next(ks), shape, jnp.float32) * scale

    D = D_MODEL
    return {
        'x': nrm((BATCH, SEQ, D), 1.0),
        'c': nrm((BATCH, D), 1.0),
        'ctx': nrm((BATCH, CTX_LEN, D), 1.0),
        'c_ctx': nrm((D,), 1.0),
        'w_mod': nrm((DEPTH, D, 6 * D), 0.5 * D ** -0.5),
        'b_mod': nrm((DEPTH, 6 * D), 0.01),
        'norm_mix': 1.0 + nrm((DEPTH, D), 0.05),
        'norm_ffn': 1.0 + nrm((DEPTH, D), 0.05),
        'w_in_ab': nrm((N_EVEN, D, IN_AB), D ** -0.5),
        'w_out_ab': nrm((N_EVEN, MIX_AB, D), MIX_AB ** -0.5),
        'qn_a': 1.0 + nrm((N_EVEN, HEAD_DIM), 0.05),
        'kn_a': 1.0 + nrm((N_EVEN, HEAD_DIM), 0.05),
        'sink_a': nrm((N_EVEN, N_HEADS_A), 0.5),
        'qn_b': 1.0 + nrm((N_EVEN, HEAD_DIM), 0.05),
        'kn_b': 1.0 + nrm((N_EVEN, HEAD_DIM), 0.05),
        'w_in_c': nrm((N_ODD, D, IN_C), D ** -0.5),
        'w_out_c': nrm((N_ODD, MIX_C, D), MIX_C ** -0.5),
        'qn_c': 1.0 + nrm((N_ODD, HEAD_DIM), 0.05),
        'kn_c': 1.0 + nrm((N_ODD, HEAD_DIM), 0.05),
        'lam_c': nrm((N_ODD, 4, HEAD_DIM), 0.1),
        'subln_c': 1.0 + nrm((N_ODD, DV_C), 0.05),
        'w_group': nrm((DEPTH, D, N_GROUPS), D ** -0.5),
        'b_group': nrm((DEPTH, N_GROUPS), 0.01),
        'w_expert': nrm((DEPTH, D, N_EXPERTS), D ** -0.5),
        'b_expert': nrm((DEPTH, N_EXPERTS), 0.01),
        'w1': nrm((DEPTH, N_EXPERTS, D, D_EXPERT), D ** -0.5),
        'w3': nrm((DEPTH, N_EXPERTS, D, D_EXPERT), D ** -0.5),
        'w2': nrm((DEPTH, N_EXPERTS, D_EXPERT, D), D_EXPERT ** -0.5),
    }


def reference(x, c, ctx, c_ctx, w_mod, b_mod, norm_mix, norm_ffn, w_in_ab, w_out_ab, qn_a, kn_a, sink_a,
              qn_b, kn_b, w_in_c, w_out_c, qn_c, kn_c, lam_c, subln_c, w_group, b_group, w_expert, b_expert,
              w1, w3, w2):
    b, s_lat, d = x.shape
    c_len = ctx.shape[1]
    rows_n = s_lat // GRID_W
    cos, sin = axial_rope_tables(rows_n)
    for l in range(DEPTH):
        last = l == DEPTH - 1
        i = l // 2
        mod = jax.nn.silu(c) @ w_mod[l] + b_mod[l]
        mod_c = jax.nn.silu(c_ctx) @ w_mod[l] + b_mod[l]
        sh1, sc1, gt1, sh2, sc2, gt2 = jnp.split(mod[:, None, :], 6, axis=-1)
        csh1, csc1, cgt1, csh2, csc2, cgt2 = jnp.split(mod_c, 6, axis=-1)
        h_lat = modulate(x, norm_mix[l], sh1, sc1)
        h_ctx = modulate(ctx, norm_mix[l], csh1, csc1)
        if l % 2 == 0:
            o_lat, o_ctx = mixer_ab(h_lat, h_ctx, cos, sin, w_in_ab[i], w_out_ab[i], qn_a[i], kn_a[i], sink_a[i],
                                    qn_b[i], kn_b[i], not last)
        else:
            lam_init = 0.8 - 0.6 * math.exp(-0.3 * l)
            o_lat, o_ctx = mixer_c(h_lat, h_ctx, cos, sin, w_in_c[i], w_out_c[i], qn_c[i], kn_c[i], lam_c[i],
                                   subln_c[i], lam_init, not last)
        x = x + gt1 * o_lat
        if not last:
            ctx = ctx + cgt1 * o_ctx
            tokens = jnp.concatenate([modulate(ctx, norm_ffn[l], csh2, csc2),
                                      modulate(x, norm_ffn[l], sh2, sc2)], axis=1).reshape(-1, d)
            f = hier_moe(tokens, w_group[l], b_group[l], w_expert[l], b_expert[l], w1[l], w3[l], w2[l])
            f = f.reshape(b, c_len + s_lat, d)
            ctx = ctx + cgt2 * f[:, :c_len]
            x = x + gt2 * f[:, c_len:]
        else:
            tokens = modulate(x, norm_ffn[l], sh2, sc2).reshape(-1, d)
            f = hier_moe(tokens, w_group[l], b_group[l], w_expert[l], b_expert[l], w1[l], w3[l], w2[l])
            x = x + gt2 * f.reshape(b, s_lat, d)
    return x
```

```python
import functools
import math

import jax
import jax.numpy as jnp
from jax import lax
from jax.experimental import pallas as pl
from jax.experimental.pallas import tpu as pltpu

F32 = jnp.float32
BF16 = jnp.bfloat16

GRID_W = 64
HEAD_DIM = 64
WINDOW = 128
ROPE_THETA = 10000.0
EPS = 1e-6
NEG = -1e30
N_HEADS_A, N_KV_A = 8, 2
N_HEADS_B, N_KV_B = 8, 2
GROUP_A = N_HEADS_A // N_KV_A
GROUP_B = N_HEADS_B // N_KV_B
QA_W, KVA_W = N_HEADS_A * HEAD_DIM, N_KV_A * HEAD_DIM
QB_W, KVB_W = N_HEADS_B * HEAD_DIM, N_KV_B * HEAD_DIM
N_HEADS_C = 8
DV_C = 2 * HEAD_DIM
N_GROUPS, EXPERTS_PER_GROUP, TOP_K = 4, 8, 2
N_EXPERTS = N_GROUPS * EXPERTS_PER_GROUP
ROUTER_LANES = 128
EXPERT_BLOCK = 256
VMEM_LIMIT = 48 * 1024 * 1024
SCALE = HEAD_DIM ** -0.5


def _params(*sem):
    return pltpu.CompilerParams(dimension_semantics=sem, vmem_limit_bytes=VMEM_LIMIT)


def _pick_block(n, candidates):
    for c in candidates:
        if n % c == 0:
            return c
    raise ValueError(f"no block size in {candidates} divides {n}")


def _mod_vec_kernel(a_ref, w_ref, b_ref, o_ref):
    a = a_ref[...]
    a = a * jax.nn.sigmoid(a)
    o_ref[0] = jnp.dot(a, w_ref[0], preferred_element_type=F32, precision=lax.Precision.HIGHEST) + b_ref[0]


def _mod_vectors(c, c_ctx, w_mod, b_mod):
    depth, d, n = w_mod.shape
    a = jnp.zeros((8, d), F32).at[0].set(c_ctx).at[1].set(c[0])
    bn = _pick_block(n, (1024, 512, 256, 128))
    return pl.pallas_call(
        _mod_vec_kernel,
        grid=(depth, n // bn),
        in_specs=[pl.BlockSpec((8, d), lambda l, j: (0, 0)),
                  pl.BlockSpec((1, d, bn), lambda l, j: (l, 0, j)),
                  pl.BlockSpec((1, 1, bn), lambda l, j: (l, 0, j))],
        out_specs=pl.BlockSpec((1, 8, bn), lambda l, j: (l, 0, j)),
        out_shape=jax.ShapeDtypeStruct((depth, 8, n), F32),
        compiler_params=_params("parallel", "parallel"),
        name="mod_vectors",
    )(a, w_mod, b_mod.reshape(depth, 1, n))


def _modulated(x, g_ref, s_ref, bm, n_ctx):
    y = x * lax.rsqrt(jnp.mean(x * x, axis=-1, keepdims=True) + EPS)
    if n_ctx:
        rows = pl.program_id(0) * bm + lax.broadcasted_iota(jnp.int32, (bm, 1), 0)
        is_ctx = rows < n_ctx
        g = jnp.where(is_ctx, g_ref[0], g_ref[1])
        s = jnp.where(is_ctx, s_ref[0], s_ref[1])
    else:
        g, s = g_ref[1], s_ref[1]
    return y * g + s


def _mod_matmul_kernel(x_ref, g_ref, s_ref, w_ref, o_ref, *, bm, n_ctx):
    h = _modulated(x_ref[...], g_ref, s_ref, bm, n_ctx).astype(BF16)
    o_ref[...] = jnp.dot(h, w_ref[...], preferred_element_type=F32).astype(o_ref.dtype)


def _mod_matmul(x, gain2, shift2, w, n_ctx):
    t, d = x.shape
    n = w.shape[1]
    bm = _pick_block(t, (640, 512, 256, 128))
    return pl.pallas_call(
        functools.partial(_mod_matmul_kernel, bm=bm, n_ctx=n_ctx),
        grid=(t // bm,),
        in_specs=[pl.BlockSpec((bm, d), lambda i: (i, 0)),
                  pl.BlockSpec((2, 1, d), lambda i: (0, 0, 0)),
                  pl.BlockSpec((2, 1, d), lambda i: (0, 0, 0)),
                  pl.BlockSpec((d, n), lambda i: (0, 0))],
        out_specs=pl.BlockSpec((bm, n), lambda i: (i, 0)),
        out_shape=jax.ShapeDtypeStruct((t, n), F32),
        compiler_params=_params("parallel"),
        name="mod_matmul",
    )(x, gain2, shift2, w)


def _mod_router_kernel(x_ref, g_ref, s_ref, wr_ref, tok_ref, logit_ref, *, bm, n_ctx):
    h = _modulated(x_ref[...], g_ref, s_ref, bm, n_ctx)
    tok_ref[...] = h.astype(BF16)
    logit_ref[...] = jnp.dot(h, wr_ref[...], preferred_element_type=F32, precision=lax.Precision.HIGHEST)


def _mod_router(x, gain2, shift2, w_router, n_ctx):
    t, d = x.shape
    bm = _pick_block(t, (640, 512, 256, 128))
    return pl.pallas_call(
        functools.partial(_mod_router_kernel, bm=bm, n_ctx=n_ctx),
        grid=(t // bm,),
        in_specs=[pl.BlockSpec((bm, d), lambda i: (i, 0)),
                  pl.BlockSpec((2, 1, d), lambda i: (0, 0, 0)),
                  pl.BlockSpec((2, 1, d), lambda i: (0, 0, 0)),
                  pl.BlockSpec((d, ROUTER_LANES), lambda i: (0, 0))],
        out_specs=[pl.BlockSpec((bm, d), lambda i: (i, 0)),
                   pl.BlockSpec((bm, ROUTER_LANES), lambda i: (i, 0))],
        out_shape=[jax.ShapeDtypeStruct((t, d), BF16), jax.ShapeDtypeStruct((t, ROUTER_LANES), F32)],
        compiler_params=_params("parallel"),
        name="mod_router",
    )(x, gain2, shift2, w_router)


def _out_proj_kernel(o_ref, w_ref, x_ref, gate_ref, y_ref, *, bm, n_ctx):
    acc = jnp.dot(o_ref[...], w_ref[...], preferred_element_type=F32)
    if n_ctx:
        rows = pl.program_id(0) * bm + lax.broadcasted_iota(jnp.int32, (bm, 1), 0)
        gate = jnp.where(rows < n_ctx, gate_ref[0], gate_ref[1])
    else:
        gate = gate_ref[1]
    y_ref[...] = x_ref[...] + gate * acc


def _out_proj(o, w, x, gate2, n_ctx):
    t, m = o.shape
    d = w.shape[1]
    bm = _pick_block(t, (640, 512, 256, 128))
    return pl.pallas_call(
        functools.partial(_out_proj_kernel, bm=bm, n_ctx=n_ctx),
        grid=(t // bm,),
        in_specs=[pl.BlockSpec((bm, m), lambda i: (i, 0)),
                  pl.BlockSpec((m, d), lambda i: (0, 0)),
                  pl.BlockSpec((bm, d), lambda i: (i, 0)),
                  pl.BlockSpec((2, 1, d), lambda i: (0, 0, 0))],
        out_specs=pl.BlockSpec((bm, d), lambda i: (i, 0)),
        out_shape=jax.ShapeDtypeStruct((t, d), F32),
        compiler_params=_params("parallel"),
        name="out_proj",
    )(o, w, x, gate2)


def _flash_kernel(q_ref, k_ref, v_ref, m0_ref, o_ref, m_sc, l_sc, acc_sc, *, bk, n_full, tail, l0):
    q = q_ref[0, 0]
    m_sc[...] = m0_ref[0]
    l_sc[...] = jnp.full(l_sc.shape, l0, F32)
    acc_sc[...] = jnp.zeros(acc_sc.shape, F32)

    def block(start, size):
        kb = k_ref[0, pl.ds(start, size), :]
        vb = v_ref[0, pl.ds(start, size), :]
        s = lax.dot_general(q, kb, (((1,), (1,)), ((), ())), preferred_element_type=F32)
        m_prev = m_sc[...]
        m_new = jnp.maximum(m_prev, jnp.max(s, axis=-1, keepdims=True))
        alpha = jnp.exp(m_prev - m_new)
        p = jnp.exp(s - m_new)
        l_sc[...] = alpha * l_sc[...] + jnp.sum(p, axis=-1, keepdims=True)
        acc_sc[...] = alpha * acc_sc[...] + jnp.dot(p.astype(BF16), vb, preferred_element_type=F32)
        m_sc[...] = m_new

    if n_full:
        def body(i, carry):
            block(pl.multiple_of(i * bk, bk), bk)
            return carry
        lax.fori_loop(0, n_full, body, 0)
    if tail:
        block(n_full * bk, tail)
    o_ref[0, 0] = (acc_sc[...] / l_sc[...]).astype(o_ref.dtype)


def _flash(q, k, v, m0, l0, bk=512):
    hkv, nq, r, dk = q.shape
    kk, dv = k.shape[1], v.shape[2]
    bk = min(bk, kk)
    n_full, tail = kk // bk, kk % bk
    return pl.pallas_call(
        functools.partial(_flash_kernel, bk=bk, n_full=n_full, tail=tail, l0=l0),
        grid=(hkv, nq),
        in_specs=[pl.BlockSpec((1, 1, r, dk), lambda h, i: (h, i, 0, 0)),
                  pl.BlockSpec((1, kk, dk), lambda h, i: (h, 0, 0)),
                  pl.BlockSpec((1, kk, dv), lambda h, i: (h, 0, 0)),
                  pl.BlockSpec((1, r, 1), lambda h, i: (h, 0, 0))],
        out_specs=pl.BlockSpec((1, 1, r, dv), lambda h, i: (h, i, 0, 0)),
        out_shape=jax.ShapeDtypeStruct((hkv, nq, r, dv), F32),
        scratch_shapes=[pltpu.VMEM((r, 1), F32), pltpu.VMEM((r, 1), F32), pltpu.VMEM((r, dv), F32)],
        compiler_params=_params("parallel", "parallel"),
        name="flash_attention",
    )(q, k, v, m0)


def _window_kernel(q_ref, kp_ref, vp_ref, kc_ref, vc_ref, sink_ref, o_ref, *, bq, seq):
    q = q_ref[0, 0]
    r = q.shape[0]
    w = bq + 2 * WINDOW
    start = pl.multiple_of(pl.program_id(1) * bq, bq)
    kw = kp_ref[0, pl.ds(start, w), :]
    vw = vp_ref[0, pl.ds(start, w), :]
    contract_last = (((1,), (1,)), ((), ()))
    s_loc = lax.dot_general(q, kw, contract_last, preferred_element_type=F32)
    qi = lax.broadcasted_iota(jnp.int32, (r, w), 0) & (bq - 1)
    c = lax.broadcasted_iota(jnp.int32, (r, w), 1)
    rel = c - qi
    kpos = start + c - WINDOW
    mask = (rel >= 0) & (rel <= 2 * WINDOW) & (kpos >= 0) & (kpos < seq)
    s_loc = jnp.where(mask, s_loc, NEG)
    s_ctx = lax.dot_general(q, kc_ref[0], contract_last, preferred_element_type=F32)
    sink = sink_ref[0]
    m = jnp.maximum(sink, jnp.maximum(jnp.max(s_loc, axis=-1, keepdims=True), jnp.max(s_ctx, axis=-1, keepdims=True)))
    p_loc = jnp.exp(s_loc - m)
    p_ctx = jnp.exp(s_ctx - m)
    l = jnp.exp(sink - m) + jnp.sum(p_loc, axis=-1, keepdims=True) + jnp.sum(p_ctx, axis=-1, keepdims=True)
    o = (jnp.dot(p_loc.astype(BF16), vw, preferred_element_type=F32)
         + jnp.dot(p_ctx.astype(BF16), vc_ref[0], preferred_element_type=F32))
    o_ref[0, 0] = (o / l).astype(o_ref.dtype)


def _window_attention(q, kp, vp, kc, vc, sink_rows, bq, seq):
    hkv, nq, r, d = q.shape
    sp, c = kp.shape[1], kc.shape[1]
    return pl.pallas_call(
        functools.partial(_window_kernel, bq=bq, seq=seq),
        grid=(hkv, nq),
        in_specs=[pl.BlockSpec((1, 1, r, d), lambda h, i: (h, i, 0, 0)),
                  pl.BlockSpec((1, sp, d), lambda h, i: (h, 0, 0)),
                  pl.BlockSpec((1, sp, d), lambda h, i: (h, 0, 0)),
                  pl.BlockSpec((1, c, d), lambda h, i: (h, 0, 0)),
                  pl.BlockSpec((1, c, d), lambda h, i: (h, 0, 0)),
                  pl.BlockSpec((1, r, 1), lambda h, i: (h, 0, 0))],
        out_specs=pl.BlockSpec((1, 1, r, d), lambda h, i: (h, i, 0, 0)),
        out_shape=jax.ShapeDtypeStruct((hkv, nq, r, d), F32),
        compiler_params=_params("parallel", "parallel"),
        name="window_attention",
    )(q, kp, vp, kc, vc, sink_rows)


def _rms(x, g):
    return x * lax.rsqrt(jnp.mean(x * x, axis=-1, keepdims=True) + EPS) * g


def _rope_tables(rows_n):
    rows = jnp.broadcast_to(jnp.arange(rows_n, dtype=F32)[:, None], (rows_n, GRID_W)).reshape(-1)
    cols = jnp.broadcast_to(jnp.arange(GRID_W, dtype=F32)[None, :], (rows_n, GRID_W)).reshape(-1)
    half = HEAD_DIM // 2
    inv = ROPE_THETA ** (-jnp.arange(0, half, 2, dtype=F32) / half)
    ang = jnp.concatenate([rows[:, None] * inv, cols[:, None] * inv], axis=-1)
    return jnp.cos(ang), jnp.sin(ang)


def _rope_rows(x, cos, sin, n_ctx):
    t = x.shape[0]
    pad = ((n_ctx, 0), (0, 0))
    cos = jnp.pad(cos, pad, constant_values=1.0).reshape((t,) + (1,) * (x.ndim - 2) + (cos.shape[1],))
    sin = jnp.pad(sin, pad, constant_values=0.0).reshape(cos.shape)
    x1, x2 = jnp.split(x, 2, axis=-1)
    return jnp.concatenate([x1 * cos - x2 * sin, x2 * cos + x1 * sin], axis=-1)


def _stack_q(q, bq):
    tq, hkv, g, d = q.shape
    return q.reshape(tq // bq, bq, hkv, g, d).transpose(2, 0, 3, 1, 4).reshape(hkv, tq // bq, g * bq, d)


def _unstack_o(o, g):
    hkv, nq, r, dv = o.shape
    bq = r // g
    return o.reshape(hkv, nq, g, bq, dv).transpose(1, 3, 0, 2, 4).reshape(nq * bq, hkv, g, dv)


def _kv_major(x):
    return x.transpose(1, 0, 2).astype(BF16)


def _no_sink(hkv, r):
    return jnp.full((hkv, r, 1), NEG, F32)


def _sink_rows(sink, bq):
    hkv, g = sink.shape
    return jnp.broadcast_to(sink.astype(F32)[:, :, None], (hkv, g, bq)).reshape(hkv, g * bq, 1)


def _mixer_ab(proj, n_ctx, cos, sin, qn_a, kn_a, sink_a, qn_b, kn_b, with_ctx):
    t = proj.shape[0]
    seq = t - n_ctx
    o0, o1, o2, o3, o4 = QA_W, QA_W + KVA_W, QA_W + 2 * KVA_W, QA_W + 2 * KVA_W + QB_W, QA_W + 2 * KVA_W + QB_W + KVB_W
    qa = _rope_rows(_rms(proj[:, :o0].reshape(t, N_KV_A, GROUP_A, HEAD_DIM), qn_a), cos, sin, n_ctx) * SCALE
    ka = _rope_rows(_rms(proj[:, o0:o1].reshape(t, N_KV_A, HEAD_DIM), kn_a), cos, sin, n_ctx)
    va = proj[:, o1:o2].reshape(t, N_KV_A, HEAD_DIM)
    qb = _rope_rows(_rms(proj[:, o2:o3].reshape(t, N_KV_B, GROUP_B, HEAD_DIM), qn_b), cos, sin, n_ctx) * SCALE
    kb = _rope_rows(_rms(proj[:, o3:o4].reshape(t, N_KV_B, HEAD_DIM), kn_b), cos, sin, n_ctx)
    vb = proj[:, o4:].reshape(t, N_KV_B, HEAD_DIM)
    sink = sink_a.reshape(N_KV_A, GROUP_A)
    bq = _pick_block(seq, (256, 128))

    ka_m, va_m = _kv_major(ka), _kv_major(va)
    pad = ((0, 0), (WINDOW, WINDOW), (0, 0))
    oa = _window_attention(_stack_q(qa[n_ctx:].astype(BF16), bq), jnp.pad(ka_m[:, n_ctx:], pad), jnp.pad(va_m[:, n_ctx:], pad),
                           ka_m[:, :n_ctx], va_m[:, :n_ctx], _sink_rows(sink, bq), bq, seq)
    oa = _unstack_o(oa, GROUP_A).reshape(seq, QA_W)
    kb_m, vb_m = _kv_major(kb), _kv_major(vb)
    ob = _flash(_stack_q(qb[n_ctx:].astype(BF16), bq), kb_m, vb_m, _no_sink(N_KV_B, GROUP_B * bq), 0.0)
    ob = _unstack_o(ob, GROUP_B).reshape(seq, QB_W)
    o_lat = jnp.concatenate([oa, ob], axis=-1)
    if not with_ctx:
        return o_lat.astype(BF16)
    oca = _flash(_stack_q(qa[:n_ctx].astype(BF16), n_ctx), ka_m[:, :n_ctx], va_m[:, :n_ctx], _sink_rows(sink, n_ctx), 1.0)
    ocb = _flash(_stack_q(qb[:n_ctx].astype(BF16), n_ctx), kb_m[:, :n_ctx], vb_m[:, :n_ctx], _no_sink(N_KV_B, GROUP_B * n_ctx), 0.0)
    o_ctx = jnp.concatenate([_unstack_o(oca, GROUP_A).reshape(n_ctx, QA_W), _unstack_o(ocb, GROUP_B).reshape(n_ctx, QB_W)], axis=-1)
    return jnp.concatenate([o_ctx, o_lat], axis=0).astype(BF16)


def _mixer_c(proj, n_ctx, cos, sin, qn, kn, lam_p, subln, lam_init, with_ctx):
    t = proj.shape[0]
    seq = t - n_ctx
    h, d = N_HEADS_C, HEAD_DIM
    lp = lam_p.astype(F32)
    lam = jnp.exp(jnp.sum(lp[0] * lp[1])) - jnp.exp(jnp.sum(lp[2] * lp[3])) + lam_init
    q, k, v = jnp.split(proj, 3, axis=-1)
    q = _rope_rows(_rms(q.reshape(t, h, 2, d), qn), cos, sin, n_ctx) * SCALE
    k = _rope_rows(_rms(k.reshape(t, h, 2, d), kn), cos, sin, n_ctx)
    eye = jnp.eye(2, dtype=F32)[None, None, :, :, None]
    q_pad = (q[:, :, :, None, :] * eye).reshape(t, h, 2, 2 * d).astype(BF16)
    k_m = _kv_major(k.reshape(t, h, 2 * d))
    v_m = _kv_major(v.reshape(t, h, DV_C))

    def attend(q_rows, keys, vals, bq):
        o = _flash(_stack_q(q_rows, bq), keys, vals, _no_sink(h, 2 * bq), 0.0)
        o = _unstack_o(o, 2)
        o = o[:, :, 0] - lam * o[:, :, 1]
        return (_rms(o, subln) * (1 - lam_init)).reshape(o.shape[0], h * DV_C)

    bq = _pick_block(seq, (512, 256, 128))
    o_lat = attend(q_pad[n_ctx:], k_m, v_m, bq)
    if not with_ctx:
        return o_lat.astype(BF16)
    o_ctx = attend(q_pad[:n_ctx], k_m[:, :n_ctx], v_m[:, :n_ctx], n_ctx)
    return jnp.concatenate([o_ctx, o_lat], axis=0).astype(BF16)


def _moe_kernel(be_ref, nu_ref, x_ref, w1_ref, w3_ref, w2_ref, y_ref):
    i = pl.program_id(0)

    @pl.when(i < nu_ref[0])
    def _():
        x = x_ref[...]
        a = jnp.dot(x, w1_ref[0], preferred_element_type=F32)
        b = jnp.dot(x, w3_ref[0], preferred_element_type=F32)
        hidden = (a * jax.nn.sigmoid(a)) * b
        y_ref[...] = jnp.dot(hidden.astype(BF16), w2_ref[0], preferred_element_type=F32)

    @pl.when(i >= nu_ref[0])
    def _():
        y_ref[...] = jnp.zeros(y_ref.shape, y_ref.dtype)


def _moe_experts(buf, blk_e, n_used, w1, w3, w2):
    n_rows, d = buf.shape
    de = w1.shape[2]
    n_blk = n_rows // EXPERT_BLOCK
    grid_spec = pltpu.PrefetchScalarGridSpec(
        num_scalar_prefetch=2,
        grid=(n_blk,),
        in_specs=[pl.BlockSpec((EXPERT_BLOCK, d), lambda i, be, nu: (i, 0)),
                  pl.BlockSpec((1, d, de), lambda i, be, nu: (be[i], 0, 0)),
                  pl.BlockSpec((1, d, de), lambda i, be, nu: (be[i], 0, 0)),
                  pl.BlockSpec((1, de, d), lambda i, be, nu: (be[i], 0, 0))],
        out_specs=pl.BlockSpec((EXPERT_BLOCK, d), lambda i, be, nu: (i, 0)),
    )
    return pl.pallas_call(
        _moe_kernel,
        grid_spec=grid_spec,
        out_shape=jax.ShapeDtypeStruct((n_rows, d), F32),
        compiler_params=_params("arbitrary"),
        name="moe_experts",
    )(blk_e, n_used, buf, w1, w3, w2)


def _hier_moe(tokens, logits, b_group, b_expert, w1, w3, w2):
    t, d = tokens.shape
    pg = jax.nn.softmax(logits[:, :N_GROUPS] + b_group.astype(F32), axis=-1)
    g_prob, g_idx = lax.top_k(pg, 1)
    le = (logits[:, N_GROUPS:N_GROUPS + N_EXPERTS] + b_expert.astype(F32)).reshape(t, N_GROUPS, EXPERTS_PER_GROUP)
    le = jnp.take_along_axis(le, g_idx[:, :, None], axis=1)[:, 0]
    e_prob, e_idx = lax.top_k(jax.nn.softmax(le, axis=-1), TOP_K)
    weights = g_prob * e_prob / jnp.sum(e_prob, axis=-1, keepdims=True)
    flat_e = (g_idx * EXPERTS_PER_GROUP + e_idx).reshape(-1)
    onehot = (flat_e[:, None] == jnp.arange(N_EXPERTS)[None, :]).astype(jnp.int32)
    rank = jnp.sum((jnp.cumsum(onehot, axis=0) - onehot) * onehot, axis=1)
    counts = jnp.sum(onehot, axis=0)
    padded = (counts + EXPERT_BLOCK - 1) // EXPERT_BLOCK * EXPERT_BLOCK
    pend = jnp.cumsum(padded)
    dest = (pend - padded)[flat_e] + rank
    n_rows = (t * TOP_K + EXPERT_BLOCK - 1) // EXPERT_BLOCK * EXPERT_BLOCK + N_EXPERTS * EXPERT_BLOCK
    n_blk = n_rows // EXPERT_BLOCK
    src = jnp.full((n_rows,), t, jnp.int32).at[dest].set(jnp.arange(t * TOP_K, dtype=jnp.int32) // TOP_K)
    buf = jnp.take(tokens, src, axis=0, mode="fill", fill_value=0)
    blk_e = jnp.minimum(jnp.searchsorted(pend, jnp.arange(n_blk) * EXPERT_BLOCK, side="right"), N_EXPERTS - 1).astype(jnp.int32)
    n_used = (pend[-1] // EXPERT_BLOCK).astype(jnp.int32).reshape(1)
    yb = _moe_experts(buf, blk_e, n_used, w1, w3, w2)
    dest2 = dest.reshape(t, TOP_K)
    return yb[dest2[:, 0]] * weights[:, 0:1] + yb[dest2[:, 1]] * weights[:, 1:2]


def kernel(x, c, ctx, c_ctx, w_mod, b_mod, norm_mix, norm_ffn, w_in_ab, w_out_ab, qn_a, kn_a, sink_a, qn_b, kn_b, w_in_c, w_out_c, qn_c, kn_c, lam_c, subln_c, w_group, b_group, w_expert, b_expert, w1, w3, w2):
    b, s_lat, d = x.shape
    assert b == 1
    c_len = ctx.shape[1]
    depth = w_mod.shape[0]
    cos, sin = _rope_tables(s_lat // GRID_W)
    mods = _mod_vectors(c, c_ctx, w_mod, b_mod)
    xs = jnp.concatenate([ctx[0], x[0]], axis=0)
    n_ctx = c_len
    for l in range(depth):
        last = l == depth - 1
        i = l // 2
        sh1, sc1, gt1, sh2, sc2, gt2 = [mods[l, :2, j * d:(j + 1) * d].reshape(2, 1, d) for j in range(6)]
        gain1 = norm_mix[l].astype(F32) * (1 + sc1)
        if l % 2 == 0:
            proj = _mod_matmul(xs, gain1, sh1, w_in_ab[i].astype(BF16), n_ctx)
            o = _mixer_ab(proj, n_ctx, cos, sin, qn_a[i], kn_a[i], sink_a[i], qn_b[i], kn_b[i], not last)
            w_out = w_out_ab[i]
        else:
            lam_init = 0.8 - 0.6 * math.exp(-0.3 * l)
            proj = _mod_matmul(xs, gain1, sh1, w_in_c[i].astype(BF16), n_ctx)
            o = _mixer_c(proj, n_ctx, cos, sin, qn_c[i], kn_c[i], lam_c[i], subln_c[i], lam_init, not last)
            w_out = w_out_c[i]
        if last:
            xs, n_ctx = xs[n_ctx:], 0
        xs = _out_proj(o, w_out.astype(BF16), xs, gt1, n_ctx)
        gain2 = norm_ffn[l].astype(F32) * (1 + sc2)
        w_router = jnp.zeros((d, ROUTER_LANES), F32).at[:, :N_GROUPS].set(w_group[l]).at[:, N_GROUPS:N_GROUPS + N_EXPERTS].set(w_expert[l])
        tokens, logits = _mod_router(xs, gain2, sh2, w_router, n_ctx)
        f = _hier_moe(tokens, logits, b_group[l], b_expert[l], w1[l].astype(BF16), w3[l].astype(BF16), w2[l].astype(BF16))
        gate = jnp.where((jnp.arange(xs.shape[0]) < n_ctx)[:, None], gt2[0], gt2[1]) if n_ctx else gt2[1]
        xs = xs + gate * f
    return xs[n_ctx:].reshape(b, s_lat, d)
```

```python
import functools
import math

import jax
import jax.numpy as jnp
from jax import lax
from jax.experimental import pallas as pl
from jax.experimental.pallas import tpu as pltpu

F32 = jnp.float32
BF16 = jnp.bfloat16

GRID_W = 64
HEAD_DIM = 64
WINDOW = 128
ROPE_THETA = 10000.0
EPS = 1e-6
NEG = -1e30
N_HEADS_A, N_KV_A = 8, 2
N_HEADS_B, N_KV_B = 8, 2
GROUP_A = N_HEADS_A // N_KV_A
GROUP_B = N_HEADS_B // N_KV_B
QA_W, KVA_W = N_HEADS_A * HEAD_DIM, N_KV_A * HEAD_DIM
QB_W, KVB_W = N_HEADS_B * HEAD_DIM, N_KV_B * HEAD_DIM
N_HEADS_C = 8
DV_C = 2 * HEAD_DIM
N_GROUPS, EXPERTS_PER_GROUP, TOP_K = 4, 8, 2
N_EXPERTS = N_GROUPS * EXPERTS_PER_GROUP
ROUTER_LANES = 128
EXPERT_BLOCK = 256
VMEM_LIMIT = 48 * 1024 * 1024
LANES = 128
LOG2E = math.log2(math.e)
Q_SCALE = HEAD_DIM ** -0.5 * LOG2E
LOG2_LOGIT_BOUND = 60.0


def _params(*sem):
    return pltpu.CompilerParams(dimension_semantics=sem, vmem_limit_bytes=VMEM_LIMIT)


def _pick_block(n, candidates):
    for c in candidates:
        if n % c == 0:
            return c
    raise ValueError(f"no block size in {candidates} divides {n}")


def _mod_vec_kernel(a_ref, w_ref, b_ref, o_ref):
    a = a_ref[...]
    a = a * jax.nn.sigmoid(a)
    o_ref[0] = jnp.dot(a, w_ref[0], preferred_element_type=F32, precision=lax.Precision.HIGHEST) + b_ref[0]


def _mod_vectors(c, c_ctx, w_mod, b_mod):
    depth, d, n = w_mod.shape
    a = jnp.zeros((8, d), F32).at[0].set(c_ctx).at[1].set(c[0])
    bn = _pick_block(n, (1024, 512, 256, 128))
    return pl.pallas_call(
        _mod_vec_kernel,
        grid=(depth, n // bn),
        in_specs=[pl.BlockSpec((8, d), lambda l, j: (0, 0)),
                  pl.BlockSpec((1, d, bn), lambda l, j: (l, 0, j)),
                  pl.BlockSpec((1, 1, bn), lambda l, j: (l, 0, j))],
        out_specs=pl.BlockSpec((1, 8, bn), lambda l, j: (l, 0, j)),
        out_shape=jax.ShapeDtypeStruct((depth, 8, n), F32),
        compiler_params=_params("parallel", "parallel"),
        name="mod_vectors",
    )(a, w_mod, b_mod.reshape(depth, 1, n))


def _modulated(x, g_ref, s_ref, bm, n_ctx):
    y = x * lax.rsqrt(jnp.mean(x * x, axis=-1, keepdims=True) + EPS)
    if n_ctx:
        rows = pl.program_id(0) * bm + lax.broadcasted_iota(jnp.int32, (bm, 1), 0)
        is_ctx = rows < n_ctx
        g = jnp.where(is_ctx, g_ref[0], g_ref[1])
        s = jnp.where(is_ctx, s_ref[0], s_ref[1])
    else:
        g, s = g_ref[1], s_ref[1]
    return y * g + s


def _mod_matmul_kernel(x_ref, g_ref, s_ref, w_ref, o_ref, *, bm, n_ctx):
    h = _modulated(x_ref[...], g_ref, s_ref, bm, n_ctx).astype(BF16)
    o_ref[...] = jnp.dot(h, w_ref[...], preferred_element_type=F32).astype(o_ref.dtype)


def _mod_matmul(x, gain2, shift2, w, n_ctx):
    t, d = x.shape
    n = w.shape[1]
    bm = _pick_block(t, (640, 512, 256, 128))
    return pl.pallas_call(
        functools.partial(_mod_matmul_kernel, bm=bm, n_ctx=n_ctx),
        grid=(t // bm,),
        in_specs=[pl.BlockSpec((bm, d), lambda i: (i, 0)),
                  pl.BlockSpec((2, 1, d), lambda i: (0, 0, 0)),
                  pl.BlockSpec((2, 1, d), lambda i: (0, 0, 0)),
                  pl.BlockSpec((d, n), lambda i: (0, 0))],
        out_specs=pl.BlockSpec((bm, n), lambda i: (i, 0)),
        out_shape=jax.ShapeDtypeStruct((t, n), F32),
        compiler_params=_params("parallel"),
        name="mod_matmul",
    )(x, gain2, shift2, w)


def _mod_router_kernel(x_ref, g_ref, s_ref, wr_ref, tok_ref, logit_ref, *, bm, n_ctx):
    h = _modulated(x_ref[...], g_ref, s_ref, bm, n_ctx)
    tok_ref[...] = h.astype(BF16)
    logit_ref[...] = jnp.dot(h, wr_ref[...], preferred_element_type=F32, precision=lax.Precision.HIGHEST)


def _mod_router(x, gain2, shift2, w_router, n_ctx):
    t, d = x.shape
    bm = _pick_block(t, (640, 512, 256, 128))
    return pl.pallas_call(
        functools.partial(_mod_router_kernel, bm=bm, n_ctx=n_ctx),
        grid=(t // bm,),
        in_specs=[pl.BlockSpec((bm, d), lambda i: (i, 0)),
                  pl.BlockSpec((2, 1, d), lambda i: (0, 0, 0)),
                  pl.BlockSpec((2, 1, d), lambda i: (0, 0, 0)),
                  pl.BlockSpec((d, ROUTER_LANES), lambda i: (0, 0))],
        out_specs=[pl.BlockSpec((bm, d), lambda i: (i, 0)),
                   pl.BlockSpec((bm, ROUTER_LANES), lambda i: (i, 0))],
        out_shape=[jax.ShapeDtypeStruct((t, d), BF16), jax.ShapeDtypeStruct((t, ROUTER_LANES), F32)],
        compiler_params=_params("parallel"),
        name="mod_router",
    )(x, gain2, shift2, w_router)


def _out_proj_kernel(o_ref, w_ref, x_ref, gate_ref, y_ref, *, bm, n_ctx):
    acc = jnp.dot(o_ref[...], w_ref[...], preferred_element_type=F32)
    if n_ctx:
        rows = pl.program_id(0) * bm + lax.broadcasted_iota(jnp.int32, (bm, 1), 0)
        gate = jnp.where(rows < n_ctx, gate_ref[0], gate_ref[1])
    else:
        gate = gate_ref[1]
    y_ref[...] = x_ref[...] + gate * acc


def _out_proj(o, w, x, gate2, n_ctx):
    t, m = o.shape
    d = w.shape[1]
    bm = _pick_block(t, (640, 512, 256, 128))
    return pl.pallas_call(
        functools.partial(_out_proj_kernel, bm=bm, n_ctx=n_ctx),
        grid=(t // bm,),
        in_specs=[pl.BlockSpec((bm, m), lambda i: (i, 0)),
                  pl.BlockSpec((m, d), lambda i: (0, 0)),
                  pl.BlockSpec((bm, d), lambda i: (i, 0)),
                  pl.BlockSpec((2, 1, d), lambda i: (0, 0, 0))],
        out_specs=pl.BlockSpec((bm, d), lambda i: (i, 0)),
        out_shape=jax.ShapeDtypeStruct((t, d), F32),
        compiler_params=_params("parallel"),
        name="out_proj",
    )(o, w, x, gate2)


def _flash_kernel(q_ref, k_ref, v_ref, m0_ref, o_ref, m_sc, l_sc, acc_sc, *, bk, n_full, tail, l0):
    q = q_ref[0, 0]
    m_sc[...] = m0_ref[0]
    l_sc[...] = jnp.full(l_sc.shape, l0, F32)
    acc_sc[...] = jnp.zeros(acc_sc.shape, F32)

    def block(start, size):
        kb = k_ref[0, pl.ds(start, size), :]
        vb = v_ref[0, pl.ds(start, size), :]
        s = lax.dot_general(q, kb, (((1,), (1,)), ((), ())), preferred_element_type=F32)
        m_prev = m_sc[...]
        m_new = jnp.maximum(m_prev, jnp.max(s, axis=-1, keepdims=True))
        alpha = jnp.exp2(m_prev - m_new)
        p = jnp.exp2(s - m_new)
        l_sc[...] = alpha * l_sc[...] + jnp.sum(p, axis=-1, keepdims=True)
        acc_sc[...] = alpha * acc_sc[...] + jnp.dot(p.astype(BF16), vb, preferred_element_type=F32)
        m_sc[...] = m_new

    if n_full:
        def body(i, carry):
            block(pl.multiple_of(i * bk, bk), bk)
            return carry
        lax.fori_loop(0, n_full, body, 0)
    if tail:
        block(n_full * bk, tail)
    o_ref[0, 0] = (acc_sc[...] / l_sc[...]).astype(o_ref.dtype)


def _flash_bounded_kernel(q_ref, k_ref, v_ref, o_ref, acc_sc, *, bk, n_full, tail, dv):
    q = q_ref[0, 0]
    acc_sc[...] = jnp.zeros(acc_sc.shape, F32)

    def block(start, size):
        kb = k_ref[0, pl.ds(start, size), :]
        vb = v_ref[0, pl.ds(start, size), :]
        s = lax.dot_general(q, kb, (((1,), (1,)), ((), ())), preferred_element_type=F32)
        acc_sc[...] += jnp.dot(jnp.exp2(s).astype(BF16), vb, preferred_element_type=F32)

    if n_full:
        def body(i, carry):
            block(pl.multiple_of(i * bk, bk), bk)
            return carry
        lax.fori_loop(0, n_full, body, 0)
    if tail:
        block(n_full * bk, tail)
    acc = acc_sc[...]
    o_ref[0, 0] = (acc[:, :dv] / acc[:, dv:dv + 1]).astype(o_ref.dtype)


def _flash_bounded(q, k, v, bk=512):
    hkv, nq, r, dk = q.shape
    kk, dv = k.shape[1], v.shape[2]
    dva = dv + LANES
    v_aug = jnp.concatenate([v, jnp.ones((hkv, kk, 1), BF16), jnp.zeros((hkv, kk, LANES - 1), BF16)], axis=-1)
    bk = min(bk, kk)
    n_full, tail = kk // bk, kk % bk
    return pl.pallas_call(
        functools.partial(_flash_bounded_kernel, bk=bk, n_full=n_full, tail=tail, dv=dv),
        grid=(hkv, nq),
        in_specs=[pl.BlockSpec((1, 1, r, dk), lambda h, i: (h, i, 0, 0)),
                  pl.BlockSpec((1, kk, dk), lambda h, i: (h, 0, 0)),
                  pl.BlockSpec((1, kk, dva), lambda h, i: (h, 0, 0))],
        out_specs=pl.BlockSpec((1, 1, r, dv), lambda h, i: (h, i, 0, 0)),
        out_shape=jax.ShapeDtypeStruct((hkv, nq, r, dv), F32),
        scratch_shapes=[pltpu.VMEM((r, dva), F32)],
        compiler_params=_params("parallel", "parallel"),
        name="flash_bounded",
    )(q, k, v_aug)


def _attend(q, k, v):
    qn = jnp.max(jnp.sum(jnp.square(q.astype(F32)), axis=-1), axis=(1, 2))
    kn = jnp.max(jnp.sum(jnp.square(k.astype(F32)), axis=-1), axis=1)
    bounded = jnp.max(qn * kn) <= LOG2_LOGIT_BOUND ** 2
    hkv, r = q.shape[0], q.shape[2]
    return lax.cond(bounded, _flash_bounded, lambda q_, k_, v_: _flash(q_, k_, v_, _no_sink(hkv, r), 0.0), q, k, v)


def _flash(q, k, v, m0, l0, bk=512):
    hkv, nq, r, dk = q.shape
    kk, dv = k.shape[1], v.shape[2]
    bk = min(bk, kk)
    n_full, tail = kk // bk, kk % bk
    return pl.pallas_call(
        functools.partial(_flash_kernel, bk=bk, n_full=n_full, tail=tail, l0=l0),
        grid=(hkv, nq),
        in_specs=[pl.BlockSpec((1, 1, r, dk), lambda h, i: (h, i, 0, 0)),
                  pl.BlockSpec((1, kk, dk), lambda h, i: (h, 0, 0)),
                  pl.BlockSpec((1, kk, dv), lambda h, i: (h, 0, 0)),
                  pl.BlockSpec((1, r, 1), lambda h, i: (h, 0, 0))],
        out_specs=pl.BlockSpec((1, 1, r, dv), lambda h, i: (h, i, 0, 0)),
        out_shape=jax.ShapeDtypeStruct((hkv, nq, r, dv), F32),
        scratch_shapes=[pltpu.VMEM((r, 1), F32), pltpu.VMEM((r, 1), F32), pltpu.VMEM((r, dv), F32)],
        compiler_params=_params("parallel", "parallel"),
        name="flash_attention",
    )(q, k, v, m0)


def _window_kernel(q_ref, kp_ref, vp_ref, kc_ref, vc_ref, sink_ref, o_ref, *, bq, seq):
    q = q_ref[0, 0]
    r = q.shape[0]
    w = bq + 2 * WINDOW
    start = pl.multiple_of(pl.program_id(1) * bq, bq)
    kw = kp_ref[0, pl.ds(start, w), :]
    vw = vp_ref[0, pl.ds(start, w), :]
    contract_last = (((1,), (1,)), ((), ()))
    s_loc = lax.dot_general(q, kw, contract_last, preferred_element_type=F32)
    qi = lax.broadcasted_iota(jnp.int32, (r, w), 0) & (bq - 1)
    c = lax.broadcasted_iota(jnp.int32, (r, w), 1)
    rel = c - qi
    kpos = start + c - WINDOW
    mask = (rel >= 0) & (rel <= 2 * WINDOW) & (kpos >= 0) & (kpos < seq)
    s_loc = jnp.where(mask, s_loc, NEG)
    s_ctx = lax.dot_general(q, kc_ref[0], contract_last, preferred_element_type=F32)
    sink = sink_ref[0]
    m = jnp.maximum(sink, jnp.maximum(jnp.max(s_loc, axis=-1, keepdims=True), jnp.max(s_ctx, axis=-1, keepdims=True)))
    p_loc = jnp.exp2(s_loc - m)
    p_ctx = jnp.exp2(s_ctx - m)
    l = jnp.exp2(sink - m) + jnp.sum(p_loc, axis=-1, keepdims=True) + jnp.sum(p_ctx, axis=-1, keepdims=True)
    o = (jnp.dot(p_loc.astype(BF16), vw, preferred_element_type=F32)
         + jnp.dot(p_ctx.astype(BF16), vc_ref[0], preferred_element_type=F32))
    o_ref[0, 0] = (o / l).astype(o_ref.dtype)


def _window_attention(q, kp, vp, kc, vc, sink_rows, bq, seq):
    hkv, nq, r, d = q.shape
    sp, c = kp.shape[1], kc.shape[1]
    return pl.pallas_call(
        functools.partial(_window_kernel, bq=bq, seq=seq),
        grid=(hkv, nq),
        in_specs=[pl.BlockSpec((1, 1, r, d), lambda h, i: (h, i, 0, 0)),
                  pl.BlockSpec((1, sp, d), lambda h, i: (h, 0, 0)),
                  pl.BlockSpec((1, sp, d), lambda h, i: (h, 0, 0)),
                  pl.BlockSpec((1, c, d), lambda h, i: (h, 0, 0)),
                  pl.BlockSpec((1, c, d), lambda h, i: (h, 0, 0)),
                  pl.BlockSpec((1, r, 1), lambda h, i: (h, 0, 0))],
        out_specs=pl.BlockSpec((1, 1, r, d), lambda h, i: (h, i, 0, 0)),
        out_shape=jax.ShapeDtypeStruct((hkv, nq, r, d), F32),
        compiler_params=_params("parallel", "parallel"),
        name="window_attention",
    )(q, kp, vp, kc, vc, sink_rows)


def _rms(x, g):
    return x * lax.rsqrt(jnp.mean(x * x, axis=-1, keepdims=True) + EPS) * g


def _rope_tables(rows_n):
    rows = jnp.broadcast_to(jnp.arange(rows_n, dtype=F32)[:, None], (rows_n, GRID_W)).reshape(-1)
    cols = jnp.broadcast_to(jnp.arange(GRID_W, dtype=F32)[None, :], (rows_n, GRID_W)).reshape(-1)
    half = HEAD_DIM // 2
    inv = ROPE_THETA ** (-jnp.arange(0, half, 2, dtype=F32) / half)
    ang = jnp.concatenate([rows[:, None] * inv, cols[:, None] * inv], axis=-1)
    return jnp.cos(ang), jnp.sin(ang)


def _rope_rows(x, cos, sin, n_ctx):
    t = x.shape[0]
    pad = ((n_ctx, 0), (0, 0))
    cos = jnp.pad(cos, pad, constant_values=1.0).reshape((t,) + (1,) * (x.ndim - 2) + (cos.shape[1],))
    sin = jnp.pad(sin, pad, constant_values=0.0).reshape(cos.shape)
    x1, x2 = jnp.split(x, 2, axis=-1)
    return jnp.concatenate([x1 * cos - x2 * sin, x2 * cos + x1 * sin], axis=-1)


def _stack_q(q, bq):
    tq, hkv, g, d = q.shape
    return q.reshape(tq // bq, bq, hkv, g, d).transpose(2, 0, 3, 1, 4).reshape(hkv, tq // bq, g * bq, d)


def _unstack_o(o, g):
    hkv, nq, r, dv = o.shape
    bq = r // g
    return o.reshape(hkv, nq, g, bq, dv).transpose(1, 3, 0, 2, 4).reshape(nq * bq, hkv, g, dv)


def _kv_major(x):
    return x.transpose(1, 0, 2).astype(BF16)


def _no_sink(hkv, r):
    return jnp.full((hkv, r, 1), NEG, F32)


def _sink_rows(sink, bq):
    hkv, g = sink.shape
    return jnp.broadcast_to(sink.astype(F32)[:, :, None] * LOG2E, (hkv, g, bq)).reshape(hkv, g * bq, 1)


def _mixer_ab(proj, n_ctx, cos, sin, qn_a, kn_a, sink_a, qn_b, kn_b, with_ctx):
    t = proj.shape[0]
    seq = t - n_ctx
    o0, o1, o2, o3, o4 = QA_W, QA_W + KVA_W, QA_W + 2 * KVA_W, QA_W + 2 * KVA_W + QB_W, QA_W + 2 * KVA_W + QB_W + KVB_W
    qa = _rope_rows(_rms(proj[:, :o0].reshape(t, N_KV_A, GROUP_A, HEAD_DIM), qn_a), cos, sin, n_ctx) * Q_SCALE
    ka = _rope_rows(_rms(proj[:, o0:o1].reshape(t, N_KV_A, HEAD_DIM), kn_a), cos, sin, n_ctx)
    va = proj[:, o1:o2].reshape(t, N_KV_A, HEAD_DIM)
    qb = _rope_rows(_rms(proj[:, o2:o3].reshape(t, N_KV_B, GROUP_B, HEAD_DIM), qn_b), cos, sin, n_ctx) * Q_SCALE
    kb = _rope_rows(_rms(proj[:, o3:o4].reshape(t, N_KV_B, HEAD_DIM), kn_b), cos, sin, n_ctx)
    vb = proj[:, o4:].reshape(t, N_KV_B, HEAD_DIM)
    sink = sink_a.reshape(N_KV_A, GROUP_A)
    bq = _pick_block(seq, (256, 128))

    ka_m, va_m = _kv_major(ka), _kv_major(va)
    pad = ((0, 0), (WINDOW, WINDOW), (0, 0))
    oa = _window_attention(_stack_q(qa[n_ctx:].astype(BF16), bq), jnp.pad(ka_m[:, n_ctx:], pad), jnp.pad(va_m[:, n_ctx:], pad),
                           ka_m[:, :n_ctx], va_m[:, :n_ctx], _sink_rows(sink, bq), bq, seq)
    oa = _unstack_o(oa, GROUP_A).reshape(seq, QA_W)
    kb_m, vb_m = _kv_major(kb), _kv_major(vb)
    ob = _attend(_stack_q(qb[n_ctx:].astype(BF16), bq), kb_m, vb_m)
    ob = _unstack_o(ob, GROUP_B).reshape(seq, QB_W)
    o_lat = jnp.concatenate([oa, ob], axis=-1)
    if not with_ctx:
        return o_lat.astype(BF16)
    oca = _flash(_stack_q(qa[:n_ctx].astype(BF16), n_ctx), ka_m[:, :n_ctx], va_m[:, :n_ctx], _sink_rows(sink, n_ctx), 1.0)
    ocb = _attend(_stack_q(qb[:n_ctx].astype(BF16), n_ctx), kb_m[:, :n_ctx], vb_m[:, :n_ctx])
    o_ctx = jnp.concatenate([_unstack_o(oca, GROUP_A).reshape(n_ctx, QA_W), _unstack_o(ocb, GROUP_B).reshape(n_ctx, QB_W)], axis=-1)
    return jnp.concatenate([o_ctx, o_lat], axis=0).astype(BF16)


def _mixer_c(proj, n_ctx, cos, sin, qn, kn, lam_p, subln, lam_init, with_ctx):
    t = proj.shape[0]
    seq = t - n_ctx
    h, d = N_HEADS_C, HEAD_DIM
    lp = lam_p.astype(F32)
    lam = jnp.exp(jnp.sum(lp[0] * lp[1])) - jnp.exp(jnp.sum(lp[2] * lp[3])) + lam_init
    q, k, v = jnp.split(proj, 3, axis=-1)
    q = _rope_rows(_rms(q.reshape(t, h, 2, d), qn), cos, sin, n_ctx) * Q_SCALE
    k = _rope_rows(_rms(k.reshape(t, h, 2, d), kn), cos, sin, n_ctx)
    eye = jnp.eye(2, dtype=F32)[None, None, :, :, None]
    q_pad = (q[:, :, :, None, :] * eye).reshape(t, h, 2, 2 * d).astype(BF16)
    k_m = _kv_major(k.reshape(t, h, 2 * d))
    v_m = _kv_major(v.reshape(t, h, DV_C))

    def attend(q_rows, keys, vals, bq):
        o = _attend(_stack_q(q_rows, bq), keys, vals)
        o = _unstack_o(o, 2)
        o = o[:, :, 0] - lam * o[:, :, 1]
        return (_rms(o, subln) * (1 - lam_init)).reshape(o.shape[0], h * DV_C)

    bq = _pick_block(seq, (512, 256, 128))
    o_lat = attend(q_pad[n_ctx:], k_m, v_m, bq)
    if not with_ctx:
        return o_lat.astype(BF16)
    o_ctx = attend(q_pad[:n_ctx], k_m[:, :n_ctx], v_m[:, :n_ctx], n_ctx)
    return jnp.concatenate([o_ctx, o_lat], axis=0).astype(BF16)


def _moe_kernel(be_ref, nu_ref, x_ref, w1_ref, w3_ref, w2_ref, y_ref):
    i = pl.program_id(0)

    @pl.when(i < nu_ref[0])
    def _():
        x = x_ref[...]
        a = jnp.dot(x, w1_ref[0], preferred_element_type=F32)
        b = jnp.dot(x, w3_ref[0], preferred_element_type=F32)
        hidden = (a * jax.nn.sigmoid(a)) * b
        y_ref[...] = jnp.dot(hidden.astype(BF16), w2_ref[0], preferred_element_type=F32)

    @pl.when(i >= nu_ref[0])
    def _():
        y_ref[...] = jnp.zeros(y_ref.shape, y_ref.dtype)


def _moe_experts(buf, blk_e, n_used, w1, w3, w2):
    n_rows, d = buf.shape
    de = w1.shape[2]
    n_blk = n_rows // EXPERT_BLOCK
    grid_spec = pltpu.PrefetchScalarGridSpec(
        num_scalar_prefetch=2,
        grid=(n_blk,),
        in_specs=[pl.BlockSpec((EXPERT_BLOCK, d), lambda i, be, nu: (i, 0)),
                  pl.BlockSpec((1, d, de), lambda i, be, nu: (be[i], 0, 0)),
                  pl.BlockSpec((1, d, de), lambda i, be, nu: (be[i], 0, 0)),
                  pl.BlockSpec((1, de, d), lambda i, be, nu: (be[i], 0, 0))],
        out_specs=pl.BlockSpec((EXPERT_BLOCK, d), lambda i, be, nu: (i, 0)),
    )
    return pl.pallas_call(
        _moe_kernel,
        grid_spec=grid_spec,
        out_shape=jax.ShapeDtypeStruct((n_rows, d), F32),
        compiler_params=_params("arbitrary"),
        name="moe_experts",
    )(blk_e, n_used, buf, w1, w3, w2)


def _top1(p):
    m = jnp.max(p, axis=-1, keepdims=True)
    idx = lax.broadcasted_iota(jnp.int32, p.shape, 1)
    return m, jnp.min(jnp.where(p == m, idx, p.shape[-1]), axis=-1, keepdims=True)


def _hier_moe(tokens, logits, b_group, b_expert, w1, w3, w2):
    t, d = tokens.shape
    pg = jax.nn.softmax(logits[:, :N_GROUPS] + b_group.astype(F32), axis=-1)
    g_prob, g_idx = _top1(pg)
    le = (logits[:, N_GROUPS:N_GROUPS + N_EXPERTS] + b_expert.astype(F32)).reshape(t, N_GROUPS, EXPERTS_PER_GROUP)
    le = jnp.take_along_axis(le, g_idx[:, :, None], axis=1)[:, 0]
    pe = jax.nn.softmax(le, axis=-1)
    p1, i1 = _top1(pe)
    p2, i2 = _top1(jnp.where(lax.broadcasted_iota(jnp.int32, pe.shape, 1) == i1, -1.0, pe))
    e_prob, e_idx = jnp.concatenate([p1, p2], axis=-1), jnp.concatenate([i1, i2], axis=-1)
    weights = g_prob * e_prob / jnp.sum(e_prob, axis=-1, keepdims=True)
    flat_e = (g_idx * EXPERTS_PER_GROUP + e_idx).reshape(-1)
    onehot = (flat_e[:, None] == jnp.arange(N_EXPERTS)[None, :]).astype(jnp.int32)
    rank = jnp.sum((jnp.cumsum(onehot, axis=0) - onehot) * onehot, axis=1)
    counts = jnp.sum(onehot, axis=0)
    padded = (counts + EXPERT_BLOCK - 1) // EXPERT_BLOCK * EXPERT_BLOCK
    pend = jnp.cumsum(padded)
    dest = (pend - padded)[flat_e] + rank
    n_rows = (t * TOP_K + EXPERT_BLOCK - 1) // EXPERT_BLOCK * EXPERT_BLOCK + N_EXPERTS * EXPERT_BLOCK
    n_blk = n_rows // EXPERT_BLOCK
    src = jnp.full((n_rows,), t, jnp.int32).at[dest].set(jnp.arange(t * TOP_K, dtype=jnp.int32) // TOP_K)
    buf = jnp.take(tokens, src, axis=0, mode="fill", fill_value=0)
    blk_e = jnp.minimum(jnp.searchsorted(pend, jnp.arange(n_blk) * EXPERT_BLOCK, side="right"), N_EXPERTS - 1).astype(jnp.int32)
    n_used = (pend[-1] // EXPERT_BLOCK).astype(jnp.int32).reshape(1)
    yb = _moe_experts(buf, blk_e, n_used, w1, w3, w2)
    dest2 = dest.reshape(t, TOP_K)
    return yb[dest2[:, 0]] * weights[:, 0:1] + yb[dest2[:, 1]] * weights[:, 1:2]


def kernel(x, c, ctx, c_ctx, w_mod, b_mod, norm_mix, norm_ffn, w_in_ab, w_out_ab, qn_a, kn_a, sink_a, qn_b, kn_b, w_in_c, w_out_c, qn_c, kn_c, lam_c, subln_c, w_group, b_group, w_expert, b_expert, w1, w3, w2):
    b, s_lat, d = x.shape
    assert b == 1
    c_len = ctx.shape[1]
    depth = w_mod.shape[0]
    cos, sin = _rope_tables(s_lat // GRID_W)
    mods = _mod_vectors(c, c_ctx, w_mod, b_mod)
    xs = jnp.concatenate([ctx[0], x[0]], axis=0)
    n_ctx = c_len
    for l in range(depth):
        last = l == depth - 1
        i = l // 2
        sh1, sc1, gt1, sh2, sc2, gt2 = [mods[l, :2, j * d:(j + 1) * d].reshape(2, 1, d) for j in range(6)]
        gain1 = norm_mix[l].astype(F32) * (1 + sc1)
        if l % 2 == 0:
            proj = _mod_matmul(xs, gain1, sh1, w_in_ab[i].astype(BF16), n_ctx)
            o = _mixer_ab(proj, n_ctx, cos, sin, qn_a[i], kn_a[i], sink_a[i], qn_b[i], kn_b[i], not last)
            w_out = w_out_ab[i]
        else:
            lam_init = 0.8 - 0.6 * math.exp(-0.3 * l)
            proj = _mod_matmul(xs, gain1, sh1, w_in_c[i].astype(BF16), n_ctx)
            o = _mixer_c(proj, n_ctx, cos, sin, qn_c[i], kn_c[i], lam_c[i], subln_c[i], lam_init, not last)
            w_out = w_out_c[i]
        if last:
            xs, n_ctx = xs[n_ctx:], 0
        xs = _out_proj(o, w_out.astype(BF16), xs, gt1, n_ctx)
        gain2 = norm_ffn[l].astype(F32) * (1 + sc2)
        w_router = jnp.zeros((d, ROUTER_LANES), F32).at[:, :N_GROUPS].set(w_group[l]).at[:, N_GROUPS:N_GROUPS + N_EXPERTS].set(w_expert[l])
        tokens, logits = _mod_router(xs, gain2, sh2, w_router, n_ctx)
        f = _hier_moe(tokens, logits, b_group[l], b_expert[l], w1[l].astype(BF16), w3[l].astype(BF16), w2[l].astype(BF16))
        gate = jnp.where((jnp.arange(xs.shape[0]) < n_ctx)[:, None], gt2[0], gt2[1]) if n_ctx else gt2[1]
        xs = xs + gate * f
    return xs[n_ctx:].reshape(b, s_lat, d)
```

```python
import functools
import math

import jax
import jax.numpy as jnp
from jax import lax
from jax.experimental import pallas as pl
from jax.experimental.pallas import tpu as pltpu

F32 = jnp.float32
BF16 = jnp.bfloat16

GRID_W = 64
HEAD_DIM = 64
WINDOW = 128
ROPE_THETA = 10000.0
EPS = 1e-6
NEG = -1e30
N_HEADS_A, N_KV_A = 8, 2
N_HEADS_B, N_KV_B = 8, 2
GROUP = N_HEADS_A // N_KV_A
QA_W, KVA_W = N_HEADS_A * HEAD_DIM, N_KV_A * HEAD_DIM
QB_W, KVB_W = N_HEADS_B * HEAD_DIM, N_KV_B * HEAD_DIM
N_HEADS_C = 8
DV_C = 2 * HEAD_DIM
N_GROUPS, EXPERTS_PER_GROUP, TOP_K = 4, 8, 2
N_EXPERTS = N_GROUPS * EXPERTS_PER_GROUP
LANES = 128
ROW_BLOCK = 256
EXPERT_BLOCK = 256
KEY_BLOCK = 2048
VMEM_LIMIT = 48 * 1024 * 1024
LOG2E = math.log2(math.e)
Q_SCALE = HEAD_DIM ** -0.5 * LOG2E
LOG2_LOGIT_BOUND = 60.0


def _params(*sem):
    return pltpu.CompilerParams(dimension_semantics=sem, vmem_limit_bytes=VMEM_LIMIT)


def _lane(shape):
    return lax.broadcasted_iota(jnp.int32, shape, 1)


def _mod_vec_kernel(a_ref, w_ref, b_ref, o_ref):
    a = a_ref[...]
    a = a * jax.nn.sigmoid(a)
    o_ref[0] = jnp.dot(a, w_ref[0], preferred_element_type=F32, precision=lax.Precision.HIGHEST) + b_ref[0]


def _mod_vectors(c, c_ctx, w_mod, b_mod):
    depth, d, n = w_mod.shape
    a = jnp.zeros((8, d), F32).at[0].set(c_ctx).at[1].set(c[0])
    bn = 1024
    return pl.pallas_call(
        _mod_vec_kernel,
        grid=(depth, n // bn),
        in_specs=[pl.BlockSpec((8, d), lambda l, j: (0, 0)),
                  pl.BlockSpec((1, d, bn), lambda l, j: (l, 0, j)),
                  pl.BlockSpec((1, 1, bn), lambda l, j: (l, 0, j))],
        out_specs=pl.BlockSpec((1, 8, bn), lambda l, j: (l, 0, j)),
        out_shape=jax.ShapeDtypeStruct((depth, 8, n), F32),
        compiler_params=_params("parallel", "parallel"),
        name="mod_vectors",
    )(a, w_mod, b_mod.reshape(depth, 1, n))


def _modulated(x, g_ref, s_ref):
    y = x * lax.rsqrt(jnp.mean(x * x, axis=-1, keepdims=True) + EPS)
    row = jnp.minimum(pl.program_id(0), 1)
    return y * g_ref[row] + s_ref[row]


def _head_norm_rope(x, gain, cos, sin, seg_mean):
    x2 = x * x
    hi = x2.astype(BF16)
    lo = (x2 - hi.astype(F32)).astype(BF16)
    ms = jnp.dot(hi, seg_mean, preferred_element_type=F32) + jnp.dot(lo, seg_mean, preferred_element_type=F32)
    y = x * lax.rsqrt(ms + EPS) * gain
    first_half = (_lane(y.shape) & (HEAD_DIM - 1)) < HEAD_DIM // 2
    partner = jnp.where(first_half, pltpu.roll(y, LANES - HEAD_DIM // 2, 1), pltpu.roll(y, HEAD_DIM // 2, 1))
    return y * cos + partner * sin


def _store_value_heads(v_ref, v):
    lane = _lane(v.shape)
    ones_col = jnp.where(lane == HEAD_DIM, 1.0, 0.0)
    v_ref[0] = jnp.where(lane < HEAD_DIM, v, ones_col).astype(BF16)
    v_ref[1] = jnp.where(lane < HEAD_DIM, pltpu.roll(v, HEAD_DIM, 1), ones_col).astype(BF16)


def _inproj_ab_kernel(x_ref, g_ref, s_ref, w_ref, hg_ref, cos_ref, sin_ref, seg_ref,
                      qa_ref, qb_ref, ka_ref, kb_ref, va_ref, vb_ref):
    bm = x_ref.shape[0]
    h = _modulated(x_ref[...], g_ref, s_ref).astype(BF16)
    proj = jnp.dot(h, w_ref[...], preferred_element_type=F32)
    cos, sin, seg = cos_ref[...], sin_ref[...], seg_ref[...]
    low = _lane((bm, LANES)) < HEAD_DIM

    def normed(j):
        cols = slice(j * LANES, (j + 1) * LANES)
        return _head_norm_rope(proj[:, cols], hg_ref[:, cols], cos, sin, seg)

    def store_queries(q_ref, tile0):
        for j in range(N_HEADS_A // 2):
            y = normed(tile0 + j)
            swapped = pltpu.roll(y, HEAD_DIM, 1)
            kv, g0 = (2 * j) // GROUP, (2 * j) % GROUP
            if kv == 0:
                even, odd = jnp.where(low, y, 0.0), jnp.where(low, swapped, 0.0)
            else:
                even, odd = jnp.where(low, 0.0, swapped), jnp.where(low, 0.0, y)
            q_ref[kv, 0, g0 * bm:(g0 + 1) * bm, :] = even.astype(BF16)
            q_ref[kv, 0, (g0 + 1) * bm:(g0 + 2) * bm, :] = odd.astype(BF16)

    tq = QA_W // LANES
    store_queries(qa_ref, 0)
    ka_ref[0] = normed(tq).astype(BF16)
    _store_value_heads(va_ref, proj[:, (tq + 1) * LANES:(tq + 2) * LANES])
    store_queries(qb_ref, tq + 2)
    kb_ref[0] = normed(2 * tq + 2).astype(BF16)
    _store_value_heads(vb_ref, proj[:, (2 * tq + 3) * LANES:(2 * tq + 4) * LANES])


def _inproj_c_kernel(x_ref, g_ref, s_ref, w_ref, hg_ref, cos_ref, sin_ref, seg_ref, q_ref, k_ref, v_ref):
    bm = x_ref.shape[0]
    h = _modulated(x_ref[...], g_ref, s_ref).astype(BF16)
    proj = jnp.dot(h, w_ref[...], preferred_element_type=F32)
    cos, sin, seg = cos_ref[...], sin_ref[...], seg_ref[...]
    lane = _lane((bm, LANES))
    low = lane < HEAD_DIM
    ones_tile = jnp.where(lane == 0, 1.0, 0.0).astype(BF16)

    def normed(j):
        cols = slice(j * LANES, (j + 1) * LANES)
        return _head_norm_rope(proj[:, cols], hg_ref[:, cols], cos, sin, seg)

    for j in range(N_HEADS_C):
        y = normed(j)
        q_ref[j, 0, 0:bm, :] = jnp.where(low, y, 0.0).astype(BF16)
        q_ref[j, 0, bm:2 * bm, :] = jnp.where(low, 0.0, y).astype(BF16)
        k_ref[j] = normed(N_HEADS_C + j).astype(BF16)
        v = proj[:, (2 * N_HEADS_C + j) * LANES:(2 * N_HEADS_C + j + 1) * LANES]
        v_ref[j] = jnp.concatenate([v.astype(BF16), ones_tile], axis=1)


def _rope_tables(seq, n_ctx):
    rows_n = seq // GRID_W
    rows = jnp.broadcast_to(jnp.arange(rows_n, dtype=F32)[:, None], (rows_n, GRID_W)).reshape(-1)
    cols = jnp.broadcast_to(jnp.arange(GRID_W, dtype=F32)[None, :], (rows_n, GRID_W)).reshape(-1)
    half = HEAD_DIM // 2
    inv = ROPE_THETA ** (-jnp.arange(0, half, 2, dtype=F32) / half)
    ang = jnp.concatenate([rows[:, None] * inv, cols[:, None] * inv], axis=-1)
    reps = LANES // half
    sign = jnp.tile(jnp.concatenate([-jnp.ones((half,), F32), jnp.ones((half,), F32)]), LANES // HEAD_DIM)
    cos = jnp.pad(jnp.tile(jnp.cos(ang), (1, reps)), ((n_ctx, 0), (0, 0)), constant_values=1.0)
    sin = jnp.pad(jnp.tile(jnp.sin(ang), (1, reps)) * sign, ((n_ctx, 0), (0, 0)))
    return cos, sin


def _segment_mean_matrix():
    idx = jnp.arange(LANES) // HEAD_DIM
    return jnp.where(idx[:, None] == idx[None, :], 1.0 / HEAD_DIM, 0.0).astype(BF16)


def _q_slot(i, nblk):
    return (i + nblk - 1) % nblk


def _inproj_common_specs(t, d, n):
    bm = ROW_BLOCK
    return [pl.BlockSpec((bm, d), lambda i: (i, 0)),
            pl.BlockSpec((2, 1, d), lambda i: (0, 0, 0)),
            pl.BlockSpec((2, 1, d), lambda i: (0, 0, 0)),
            pl.BlockSpec((d, n), lambda i: (0, 0)),
            pl.BlockSpec((1, n), lambda i: (0, 0)),
            pl.BlockSpec((bm, LANES), lambda i: (i, 0)),
            pl.BlockSpec((bm, LANES), lambda i: (i, 0)),
            pl.BlockSpec((LANES, LANES), lambda i: (0, 0))]


def _inproj_ab(x, gain2, shift2, w, qn_a, kn_a, qn_b, kn_b, cos, sin):
    t, d = x.shape
    n = w.shape[1]
    bm, nblk = ROW_BLOCK, t // ROW_BLOCK
    tile = lambda g, reps: jnp.tile(g.astype(F32), reps)
    ones_v = jnp.ones((KVA_W,), F32)
    hg = jnp.concatenate([tile(qn_a, N_HEADS_A) * Q_SCALE, tile(kn_a, N_KV_A), ones_v,
                          tile(qn_b, N_HEADS_B) * Q_SCALE, tile(kn_b, N_KV_B), ones_v]).reshape(1, n)
    q_shape = jax.ShapeDtypeStruct((N_KV_A, nblk, GROUP * bm, LANES), BF16)
    k_shape = jax.ShapeDtypeStruct((1, t, LANES), BF16)
    v_shape = jax.ShapeDtypeStruct((N_KV_A, t, LANES), BF16)
    q_spec = pl.BlockSpec((N_KV_A, 1, GROUP * bm, LANES), lambda i: (0, _q_slot(i, nblk), 0, 0))
    k_spec = pl.BlockSpec((1, bm, LANES), lambda i: (0, i, 0))
    v_spec = pl.BlockSpec((N_KV_A, bm, LANES), lambda i: (0, i, 0))
    return pl.pallas_call(
        _inproj_ab_kernel,
        grid=(nblk,),
        in_specs=_inproj_common_specs(t, d, n),
        out_specs=[q_spec, q_spec, k_spec, k_spec, v_spec, v_spec],
        out_shape=[q_shape, q_shape, k_shape, k_shape, v_shape, v_shape],
        compiler_params=_params("parallel"),
        name="inproj_ab",
    )(x, gain2, shift2, w, hg, cos, sin, _segment_mean_matrix())


def _inproj_c(x, gain2, shift2, w, qn, kn, cos, sin):
    t, d = x.shape
    n = w.shape[1]
    bm, nblk = ROW_BLOCK, t // ROW_BLOCK
    h = N_HEADS_C
    tile = lambda g: jnp.tile(g.astype(F32), 2 * h)
    hg = jnp.concatenate([tile(qn) * Q_SCALE, tile(kn), jnp.ones((h * DV_C,), F32)]).reshape(1, n)
    return pl.pallas_call(
        _inproj_c_kernel,
        grid=(nblk,),
        in_specs=_inproj_common_specs(t, d, n),
        out_specs=[pl.BlockSpec((h, 1, 2 * bm, LANES), lambda i: (0, _q_slot(i, nblk), 0, 0)),
                   pl.BlockSpec((h, bm, LANES), lambda i: (0, i, 0)),
                   pl.BlockSpec((h, bm, DV_C + LANES), lambda i: (0, i, 0))],
        out_shape=[jax.ShapeDtypeStruct((h, nblk, 2 * bm, LANES), BF16),
                   jax.ShapeDtypeStruct((h, t, LANES), BF16),
                   jax.ShapeDtypeStruct((h, t, DV_C + LANES), BF16)],
        compiler_params=_params("parallel"),
        name="inproj_c",
    )(x, gain2, shift2, w, hg, cos, sin, _segment_mean_matrix())


def _merge_gqa_heads(o, bq):
    low = _lane((bq, LANES)) < HEAD_DIM
    pairs = [jnp.where(low, o[g * bq:(g + 1) * bq], pltpu.roll(o[(g + 1) * bq:(g + 2) * bq], HEAD_DIM, 1))
             for g in range(0, GROUP, 2)]
    return jnp.concatenate(pairs, axis=1)


def _flash_kernel(*refs, mode, online, n_keys, bk, dv, l0):
    refs = list(refs)
    q_ref, k_ref, v_ref = refs[:3]
    pos = 3
    if online:
        m0_ref = refs[pos]
        pos += 1
    if mode == "diff":
        lam_ref, sub_ref = refs[pos:pos + 2]
        pos += 2
    o_ref = refs[pos]
    acc_sc = refs[pos + 1]
    m_sc = refs[pos + 2] if online else None

    nsub, rb = q_ref.shape[1], q_ref.shape[2]
    q = q_ref[0].reshape(nsub * rb, LANES)
    acc_sc[...] = jnp.where(_lane(acc_sc.shape) == dv, l0, 0.0).astype(F32)
    if online:
        m_sc[...] = m0_ref[0]

    def block(start, size):
        kb = k_ref[0, pl.ds(start, size), :]
        vb = v_ref[0, pl.ds(start, size), :]
        s = lax.dot_general(q, kb, (((1,), (1,)), ((), ())), preferred_element_type=F32)
        if online:
            m_prev = m_sc[...]
            m_new = jnp.maximum(m_prev, jnp.max(s, axis=-1, keepdims=True))
            p = jnp.exp2(s - m_new).astype(BF16)
            acc_sc[...] = jnp.exp2(m_prev - m_new) * acc_sc[...] + jnp.dot(p, vb, preferred_element_type=F32)
            m_sc[...] = m_new
        else:
            acc_sc[...] += jnp.dot(jnp.exp2(s).astype(BF16), vb, preferred_element_type=F32)

    n_full, tail = n_keys // bk, n_keys % bk
    if n_full:
        def body(i, carry):
            block(pl.multiple_of(i * bk, bk), bk)
            return carry
        lax.fori_loop(0, n_full, body, 0)
    if tail:
        block(n_full * bk, tail)

    acc = acc_sc[...]
    o = acc[:, :LANES] / acc[:, dv:dv + 1]
    if mode == "gqa":
        o_ref[...] = _merge_gqa_heads(o, rb // GROUP).astype(o_ref.dtype)
    else:
        bq = rb // 2
        for b in range(nsub):
            d = o[b * rb:b * rb + bq] - lam_ref[...] * o[b * rb + bq:(b + 1) * rb]
            y = d * lax.rsqrt(jnp.mean(d * d, axis=-1, keepdims=True) + EPS) * sub_ref[...]
            o_ref[b * bq:(b + 1) * bq, :] = y.astype(o_ref.dtype)


def _flash_call(q, k, v, extra, *, mode, online, n_keys, slot0, nsub, n_steps, l0=0.0):
    hkv, _, rb, _ = q.shape
    hk, dva = k.shape[0], v.shape[2]
    dv = HEAD_DIM if mode == "gqa" else DV_C
    bq = rb // (GROUP if mode == "gqa" else 2)
    ocols = GROUP * HEAD_DIM if mode == "gqa" else DV_C
    r = nsub * rb
    bk = min(KEY_BLOCK, n_keys)
    in_specs = [pl.BlockSpec((1, nsub, rb, LANES), lambda h, i: (h, slot0 // nsub + i, 0, 0)),
                pl.BlockSpec((1, n_keys, LANES), (lambda h, i: (h, 0, 0)) if hk > 1 else (lambda h, i: (0, 0, 0))),
                pl.BlockSpec((1, n_keys, dva), lambda h, i: (h, 0, 0))]
    args = [q, k, v]
    scratch = [pltpu.VMEM((r, dva), F32)]
    if online:
        m0 = extra.pop(0)
        in_specs.append(pl.BlockSpec((1, r, 1), lambda h, i: (h, 0, 0)))
        args.append(m0)
        scratch.append(pltpu.VMEM((r, 1), F32))
    for a in extra:
        in_specs.append(pl.BlockSpec(a.shape, lambda h, i: (0, 0)))
        args.append(a)
    return pl.pallas_call(
        functools.partial(_flash_kernel, mode=mode, online=online, n_keys=n_keys, bk=bk, dv=dv, l0=l0),
        grid=(hkv, n_steps),
        in_specs=in_specs,
        out_specs=pl.BlockSpec((nsub * bq, ocols), lambda h, i: (i, h)),
        out_shape=jax.ShapeDtypeStruct((n_steps * nsub * bq, hkv * ocols), BF16),
        scratch_shapes=scratch,
        compiler_params=_params("parallel", "parallel"),
        name="flash_online" if online else "flash_bounded",
    )(*args)


def _logits_bounded(q, k):
    qn = jnp.max(jnp.sum(jnp.square(q.astype(F32)), axis=-1))
    kn = jnp.max(jnp.sum(jnp.square(k.astype(F32)), axis=-1))
    return qn * kn <= LOG2_LOGIT_BOUND ** 2


def _attend(q, k, v, extra, bounded, **kw):
    hkv, r = q.shape[0], kw["nsub"] * q.shape[2]
    fast = lambda q_, k_, v_, *e: _flash_call(q_, k_, v_, list(e), online=False, **kw)
    safe = lambda q_, k_, v_, *e: _flash_call(q_, k_, v_, [jnp.full((hkv, r, 1), NEG, F32)] + list(e), online=True, **kw)
    return lax.cond(bounded, fast, safe, q, k, v, *extra)


def _window_kernel(q_ref, k_ref, v_ref, sink_ref, o_ref, *, bq, n_ctx):
    q = q_ref[0, 0]
    r = q.shape[0]
    t = k_ref.shape[1]
    w = bq + 2 * WINDOW
    q0 = pl.program_id(1) * bq
    ws = pl.multiple_of(jnp.clip(n_ctx + q0 - WINDOW, 0, t - w), WINDOW)
    kw = k_ref[0, pl.ds(ws, w), :]
    vw = v_ref[0, pl.ds(ws, w), :]
    contract_last = (((1,), (1,)), ((), ()))
    s_loc = lax.dot_general(q, kw, contract_last, preferred_element_type=F32)
    qpos = q0 + (lax.broadcasted_iota(jnp.int32, (r, w), 0) & (bq - 1))
    kpos = ws - n_ctx + lax.broadcasted_iota(jnp.int32, (r, w), 1)
    mask = (kpos >= 0) & (kpos - qpos <= WINDOW) & (qpos - kpos <= WINDOW)
    s_loc = jnp.where(mask, s_loc, NEG)
    s_ctx = lax.dot_general(q, k_ref[0, 0:n_ctx, :], contract_last, preferred_element_type=F32)
    sink = sink_ref[0]
    m = jnp.maximum(sink, jnp.maximum(jnp.max(s_loc, axis=-1, keepdims=True), jnp.max(s_ctx, axis=-1, keepdims=True)))
    p_loc = jnp.exp2(s_loc - m)
    p_ctx = jnp.exp2(s_ctx - m)
    l = jnp.exp2(sink - m) + jnp.sum(p_loc, axis=-1, keepdims=True) + jnp.sum(p_ctx, axis=-1, keepdims=True)
    o = (jnp.dot(p_loc.astype(BF16), vw, preferred_element_type=F32)
         + jnp.dot(p_ctx.astype(BF16), v_ref[0, 0:n_ctx, :], preferred_element_type=F32))
    o_ref[...] = _merge_gqa_heads(o / l, bq).astype(o_ref.dtype)


def _window_attention(q, k, v, sink_rows, n_ctx):
    hkv, slots, r, _ = q.shape
    t = k.shape[1]
    bq = r // GROUP
    nq = slots - n_ctx // bq
    return pl.pallas_call(
        functools.partial(_window_kernel, bq=bq, n_ctx=n_ctx),
        grid=(hkv, nq),
        in_specs=[pl.BlockSpec((1, 1, r, LANES), lambda h, i: (h, i, 0, 0)),
                  pl.BlockSpec((1, t, LANES), lambda h, i: (0, 0, 0)),
                  pl.BlockSpec((1, t, LANES), lambda h, i: (h, 0, 0)),
                  pl.BlockSpec((1, r, 1), lambda h, i: (h, 0, 0))],
        out_specs=pl.BlockSpec((bq, GROUP * HEAD_DIM), lambda h, i: (i, h)),
        out_shape=jax.ShapeDtypeStruct((nq * bq, hkv * GROUP * HEAD_DIM), BF16),
        compiler_params=_params("parallel", "parallel"),
        name="window_attention",
    )(q, k, v, sink_rows)


def _sink_rows(sink, bq):
    hkv, g = sink.shape
    return jnp.broadcast_to(sink.astype(F32)[:, :, None] * LOG2E, (hkv, g, bq)).reshape(hkv, g * bq, 1)


def _mixer_ab(heads, n_ctx, sink_a, with_ctx):
    qa, qb, ka, kb, va, vb = heads
    t = ka.shape[1]
    bq = ROW_BLOCK
    n_lat = (t - n_ctx) // bq
    sink = _sink_rows(sink_a.reshape(N_KV_A, GROUP), bq)
    bounded = _logits_bounded(qb, kb)
    oa = _window_attention(qa, ka, va, sink, n_ctx)
    ob = _attend(qb, kb, vb, [], bounded, mode="gqa", n_keys=t, slot0=0, nsub=1, n_steps=n_lat)
    if not with_ctx:
        return (oa, ob), None
    oca = _flash_call(qa, ka, va, [sink], mode="gqa", online=True, n_keys=n_ctx, slot0=n_lat, nsub=1, n_steps=1, l0=1.0)
    ocb = _attend(qb, kb, vb, [], bounded, mode="gqa", n_keys=n_ctx, slot0=n_lat, nsub=1, n_steps=1)
    return (oa, ob), (oca, ocb)


def _mixer_c(heads, n_ctx, lam_p, subln, lam_init, with_ctx):
    q, k, v = heads
    t = k.shape[1]
    n_lat = (t - n_ctx) // ROW_BLOCK
    lp = lam_p.astype(F32)
    lam = jnp.exp(jnp.sum(lp[0] * lp[1])) - jnp.exp(jnp.sum(lp[2] * lp[3])) + lam_init
    extra = [jnp.full((1, DV_C), lam, F32), (subln.astype(F32) * (1 - lam_init)).reshape(1, DV_C)]
    bounded = _logits_bounded(q, k)
    nsub = 2 if n_lat % 2 == 0 else 1
    o_lat = _attend(q, k, v, extra, bounded, mode="diff", n_keys=t, slot0=0, nsub=nsub, n_steps=n_lat // nsub)
    if not with_ctx:
        return (o_lat,), None
    o_ctx = _attend(q, k, v, extra, bounded, mode="diff", n_keys=n_ctx, slot0=n_lat, nsub=1, n_steps=1)
    return (o_lat,), (o_ctx,)


def _out_proj_kernel(*refs, n_parts, has_ctx):
    lat = refs[:n_parts]
    ctx = refs[n_parts:2 * n_parts] if has_ctx else None
    w_ref, x_ref, gate_ref, y_ref = refs[-4:]
    is_ctx = pl.program_id(0) == 0 if has_ctx else False
    acc = None
    col = 0
    for p in range(n_parts):
        o = lat[p][...]
        if has_ctx:
            o = jnp.where(is_ctx, ctx[p][...], o)
        width = o.shape[1]
        part = jnp.dot(o, w_ref[col:col + width, :], preferred_element_type=F32)
        acc = part if acc is None else acc + part
        col += width
    gate = gate_ref[0] if not has_ctx else gate_ref[jnp.where(is_ctx, 0, 1)]
    y_ref[...] = x_ref[...] + gate * acc


def _out_proj(o_lat, o_ctx, w, x, gate2):
    t, d = x.shape
    bm = ROW_BLOCK
    has_ctx = o_ctx is not None
    n_parts = len(o_lat)
    lat_map = (lambda i: (jnp.maximum(i - 1, 0), 0)) if has_ctx else (lambda i: (i, 0))
    in_specs = [pl.BlockSpec((bm, o.shape[1]), lat_map) for o in o_lat]
    args = list(o_lat)
    if has_ctx:
        in_specs += [pl.BlockSpec((bm, o.shape[1]), lambda i: (0, 0)) for o in o_ctx]
        args += list(o_ctx)
        gate = gate2
    else:
        gate = gate2[1:2]
    in_specs += [pl.BlockSpec(w.shape, lambda i: (0, 0)),
                 pl.BlockSpec((bm, d), lambda i: (i, 0)),
                 pl.BlockSpec(gate.shape, lambda i: (0, 0, 0))]
    return pl.pallas_call(
        functools.partial(_out_proj_kernel, n_parts=n_parts, has_ctx=has_ctx),
        grid=(t // bm,),
        in_specs=in_specs,
        out_specs=pl.BlockSpec((bm, d), lambda i: (i, 0)),
        out_shape=jax.ShapeDtypeStruct((t, d), F32),
        compiler_params=_params("parallel"),
        name="out_proj",
    )(*args, w, x, gate)


def _mod_router_kernel(x_ref, g_ref, s_ref, wr_ref, tok_ref, logit_ref, *, row0):
    y = x_ref[...]
    y = y * lax.rsqrt(jnp.mean(y * y, axis=-1, keepdims=True) + EPS)
    row = jnp.minimum(pl.program_id(0) + row0, 1)
    h = y * g_ref[row] + s_ref[row]
    tok_ref[...] = h.astype(BF16)
    logit_ref[...] = jnp.dot(h, wr_ref[...], preferred_element_type=F32, precision=lax.Precision.HIGHEST)


def _mod_router(x, gain2, shift2, w_router, has_ctx):
    t, d = x.shape
    bm = ROW_BLOCK
    return pl.pallas_call(
        functools.partial(_mod_router_kernel, row0=0 if has_ctx else 1),
        grid=(t // bm,),
        in_specs=[pl.BlockSpec((bm, d), lambda i: (i, 0)),
                  pl.BlockSpec((2, 1, d), lambda i: (0, 0, 0)),
                  pl.BlockSpec((2, 1, d), lambda i: (0, 0, 0)),
                  pl.BlockSpec((d, LANES), lambda i: (0, 0))],
        out_specs=[pl.BlockSpec((bm, d), lambda i: (i, 0)),
                   pl.BlockSpec((bm, LANES), lambda i: (i, 0))],
        out_shape=[jax.ShapeDtypeStruct((t, d), BF16), jax.ShapeDtypeStruct((t, LANES), F32)],
        compiler_params=_params("parallel"),
        name="mod_router",
    )(x, gain2, shift2, w_router)


def _moe_kernel(be_ref, nu_ref, x_ref, w1_ref, w3_ref, w2_ref, y_ref):
    i = pl.program_id(0)

    @pl.when(i < nu_ref[0])
    def _():
        x = x_ref[...]
        a = jnp.dot(x, w1_ref[0], preferred_element_type=F32)
        b = jnp.dot(x, w3_ref[0], preferred_element_type=F32)
        hidden = (a * jax.nn.sigmoid(a)) * b
        y_ref[...] = jnp.dot(hidden.astype(BF16), w2_ref[0], preferred_element_type=F32)

    @pl.when(i >= nu_ref[0])
    def _():
        y_ref[...] = jnp.zeros(y_ref.shape, y_ref.dtype)


def _moe_experts(buf, blk_e, n_used, w1, w3, w2):
    n_rows, d = buf.shape
    de = w1.shape[2]
    n_blk = n_rows // EXPERT_BLOCK
    grid_spec = pltpu.PrefetchScalarGridSpec(
        num_scalar_prefetch=2,
        grid=(n_blk,),
        in_specs=[pl.BlockSpec((EXPERT_BLOCK, d), lambda i, be, nu: (i, 0)),
                  pl.BlockSpec((1, d, de), lambda i, be, nu: (be[i], 0, 0)),
                  pl.BlockSpec((1, d, de), lambda i, be, nu: (be[i], 0, 0)),
                  pl.BlockSpec((1, de, d), lambda i, be, nu: (be[i], 0, 0))],
        out_specs=pl.BlockSpec((EXPERT_BLOCK, d), lambda i, be, nu: (i, 0)),
    )
    return pl.pallas_call(
        _moe_kernel,
        grid_spec=grid_spec,
        out_shape=jax.ShapeDtypeStruct((n_rows, d), F32),
        compiler_params=_params("arbitrary"),
        name="moe_experts",
    )(blk_e, n_used, buf, w1, w3, w2)


def _top1(p):
    m = jnp.max(p, axis=-1, keepdims=True)
    idx = lax.broadcasted_iota(jnp.int32, p.shape, 1)
    return m, jnp.min(jnp.where(p == m, idx, p.shape[-1]), axis=-1, keepdims=True)


def _hier_moe(tokens, logits, b_group, b_expert, w1, w3, w2):
    t, d = tokens.shape
    pg = jax.nn.softmax(logits[:, :N_GROUPS] + b_group.astype(F32), axis=-1)
    g_prob, g_idx = _top1(pg)
    le = (logits[:, N_GROUPS:N_GROUPS + N_EXPERTS] + b_expert.astype(F32)).reshape(t, N_GROUPS, EXPERTS_PER_GROUP)
    le = jnp.take_along_axis(le, g_idx[:, :, None], axis=1)[:, 0]
    pe = jax.nn.softmax(le, axis=-1)
    p1, i1 = _top1(pe)
    p2, i2 = _top1(jnp.where(lax.broadcasted_iota(jnp.int32, pe.shape, 1) == i1, -1.0, pe))
    e_prob, e_idx = jnp.concatenate([p1, p2], axis=-1), jnp.concatenate([i1, i2], axis=-1)
    weights = g_prob * e_prob / jnp.sum(e_prob, axis=-1, keepdims=True)
    flat_e = (g_idx * EXPERTS_PER_GROUP + e_idx).reshape(-1)
    onehot = (flat_e[:, None] == jnp.arange(N_EXPERTS)[None, :]).astype(jnp.int32)
    rank = jnp.sum((jnp.cumsum(onehot, axis=0) - onehot) * onehot, axis=1)
    counts = jnp.sum(onehot, axis=0)
    padded = (counts + EXPERT_BLOCK - 1) // EXPERT_BLOCK * EXPERT_BLOCK
    pend = jnp.cumsum(padded)
    dest = (pend - padded)[flat_e] + rank
    n_rows = (t * TOP_K + EXPERT_BLOCK - 1) // EXPERT_BLOCK * EXPERT_BLOCK + N_EXPERTS * EXPERT_BLOCK
    n_blk = n_rows // EXPERT_BLOCK
    src = jnp.full((n_rows,), t, jnp.int32).at[dest].set(jnp.arange(t * TOP_K, dtype=jnp.int32) // TOP_K)
    buf = jnp.take(tokens, src, axis=0, mode="fill", fill_value=0)
    blk_e = jnp.minimum(jnp.searchsorted(pend, jnp.arange(n_blk) * EXPERT_BLOCK, side="right"), N_EXPERTS - 1).astype(jnp.int32)
    n_used = (pend[-1] // EXPERT_BLOCK).astype(jnp.int32).reshape(1)
    yb = _moe_experts(buf, blk_e, n_used, w1, w3, w2)
    dest2 = dest.reshape(t, TOP_K)
    return yb[dest2[:, 0]] * weights[:, 0:1] + yb[dest2[:, 1]] * weights[:, 1:2]


def kernel(x, c, ctx, c_ctx, w_mod, b_mod, norm_mix, norm_ffn, w_in_ab, w_out_ab, qn_a, kn_a, sink_a, qn_b, kn_b, w_in_c, w_out_c, qn_c, kn_c, lam_c, subln_c, w_group, b_group, w_expert, b_expert, w1, w3, w2):
    b, s_lat, d = x.shape
    n_ctx = ctx.shape[1]
    assert b == 1 and n_ctx == ROW_BLOCK and s_lat % (2 * ROW_BLOCK) == 0
    depth = w_mod.shape[0]
    cos, sin = _rope_tables(s_lat, n_ctx)
    mods = _mod_vectors(c, c_ctx, w_mod, b_mod)
    xs = jnp.concatenate([ctx[0], x[0]], axis=0)
    has_ctx = True
    for l in range(depth):
        last = l == depth - 1
        i = l // 2
        sh1, sc1, gt1, sh2, sc2, gt2 = [mods[l, :2, j * d:(j + 1) * d].reshape(2, 1, d) for j in range(6)]
        gain1 = norm_mix[l].astype(F32) * (1 + sc1)
        if l % 2 == 0:
            heads = _inproj_ab(xs, gain1, sh1, w_in_ab[i].astype(BF16), qn_a[i], kn_a[i], qn_b[i], kn_b[i], cos, sin)
            o_lat, o_ctx = _mixer_ab(heads, n_ctx, sink_a[i], not last)
            w_out = w_out_ab[i]
        else:
            lam_init = 0.8 - 0.6 * math.exp(-0.3 * l)
            heads = _inproj_c(xs, gain1, sh1, w_in_c[i].astype(BF16), qn_c[i], kn_c[i], cos, sin)
            o_lat, o_ctx = _mixer_c(heads, n_ctx, lam_c[i], subln_c[i], lam_init, not last)
            w_out = w_out_c[i]
        if last:
            xs, has_ctx = xs[n_ctx:], False
        xs = _out_proj(o_lat, o_ctx, w_out.astype(BF16), xs, gt1)
        gain2 = norm_ffn[l].astype(F32) * (1 + sc2)
        w_router = jnp.zeros((d, LANES), F32).at[:, :N_GROUPS].set(w_group[l]).at[:, N_GROUPS:N_GROUPS + N_EXPERTS].set(w_expert[l])
        tokens, logits = _mod_router(xs, gain2, sh2, w_router, has_ctx)
        f = _hier_moe(tokens, logits, b_group[l], b_expert[l], w1[l].astype(BF16), w3[l].astype(BF16), w2[l].astype(BF16))
        gate = jnp.where((jnp.arange(xs.shape[0]) < n_ctx)[:, None], gt2[0], gt2[1]) if has_ctx else gt2[1]
        xs = xs + gate * f
    return xs.reshape(b, s_lat, d)
```

```python
import functools
import math

import jax
import jax.numpy as jnp
from jax import lax
from jax.experimental import pallas as pl
from jax.experimental.pallas import tpu as pltpu
from jax.experimental.pallas import tpu_sc as plsc

F32 = jnp.float32
BF16 = jnp.bfloat16

GRID_W = 64
HEAD_DIM = 64
WINDOW = 128
ROPE_THETA = 10000.0
EPS = 1e-6
NEG = -1e30
N_HEADS_A, N_KV_A = 8, 2
N_HEADS_B, N_KV_B = 8, 2
GROUP = N_HEADS_A // N_KV_A
QA_W, KVA_W = N_HEADS_A * HEAD_DIM, N_KV_A * HEAD_DIM
QB_W, KVB_W = N_HEADS_B * HEAD_DIM, N_KV_B * HEAD_DIM
N_HEADS_C = 8
DV_C = 2 * HEAD_DIM
N_GROUPS, EXPERTS_PER_GROUP, TOP_K = 4, 8, 2
N_EXPERTS = N_GROUPS * EXPERTS_PER_GROUP
LANES = 128
ROW_BLOCK = 256
EXPERT_BLOCK = 256
SC_GATHER_WINDOW = 128
SC_ROW_SPLIT = 4
KEY_BLOCK = 4096
VMEM_LIMIT = 48 * 1024 * 1024
LOG2E = math.log2(math.e)
Q_SCALE = HEAD_DIM ** -0.5 * LOG2E
LOG2_LOGIT_BOUND = 60.0


def _params(*sem):
    return pltpu.CompilerParams(dimension_semantics=sem, vmem_limit_bytes=VMEM_LIMIT)


def _lane(shape):
    return lax.broadcasted_iota(jnp.int32, shape, 1)


def _mod_vec_kernel(a_ref, w_ref, b_ref, o_ref):
    a = a_ref[...]
    a = a * jax.nn.sigmoid(a)
    o_ref[0] = jnp.dot(a, w_ref[0], preferred_element_type=F32, precision=lax.Precision.HIGHEST) + b_ref[0]


def _mod_vectors(c, c_ctx, w_mod, b_mod):
    depth, d, n = w_mod.shape
    a = jnp.zeros((8, d), F32).at[0].set(c_ctx).at[1].set(c[0])
    bn = 1024
    return pl.pallas_call(
        _mod_vec_kernel,
        grid=(depth, n // bn),
        in_specs=[pl.BlockSpec((8, d), lambda l, j: (0, 0)),
                  pl.BlockSpec((1, d, bn), lambda l, j: (l, 0, j)),
                  pl.BlockSpec((1, 1, bn), lambda l, j: (l, 0, j))],
        out_specs=pl.BlockSpec((1, 8, bn), lambda l, j: (l, 0, j)),
        out_shape=jax.ShapeDtypeStruct((depth, 8, n), F32),
        compiler_params=_params("parallel", "parallel"),
        name="mod_vectors",
    )(a, w_mod, b_mod.reshape(depth, 1, n))


def _modulated(x, g_ref, s_ref):
    y = x * lax.rsqrt(jnp.mean(x * x, axis=-1, keepdims=True) + EPS)
    row = jnp.minimum(pl.program_id(0), 1)
    return y * g_ref[row] + s_ref[row]


def _head_norm_rope(x, gain, cos, sin, seg_mean):
    x2 = x * x
    hi = x2.astype(BF16)
    lo = (x2 - hi.astype(F32)).astype(BF16)
    ms = jnp.dot(hi, seg_mean, preferred_element_type=F32) + jnp.dot(lo, seg_mean, preferred_element_type=F32)
    y = x * lax.rsqrt(ms + EPS) * gain
    first_half = (_lane(y.shape) & (HEAD_DIM - 1)) < HEAD_DIM // 2
    partner = jnp.where(first_half, pltpu.roll(y, LANES - HEAD_DIM // 2, 1), pltpu.roll(y, HEAD_DIM // 2, 1))
    return y * cos + partner * sin


def _store_value_heads(v_ref, v):
    lane = _lane(v.shape)
    ones_col = jnp.where(lane == HEAD_DIM, 1.0, 0.0)
    v_ref[0] = jnp.where(lane < HEAD_DIM, v, ones_col).astype(BF16)
    v_ref[1] = jnp.where(lane < HEAD_DIM, pltpu.roll(v, HEAD_DIM, 1), ones_col).astype(BF16)


def _inproj_ab_kernel(x_ref, g_ref, s_ref, w_ref, hg_ref, cos_ref, sin_ref, seg_ref,
                      qa_ref, qb_ref, ka_ref, kb_ref, va_ref, vb_ref):
    bm = x_ref.shape[0]
    h = _modulated(x_ref[...], g_ref, s_ref).astype(BF16)
    proj = jnp.dot(h, w_ref[...], preferred_element_type=F32)
    cos, sin, seg = cos_ref[...], sin_ref[...], seg_ref[...]
    low = _lane((bm, LANES)) < HEAD_DIM

    def normed(j):
        cols = slice(j * LANES, (j + 1) * LANES)
        return _head_norm_rope(proj[:, cols], hg_ref[:, cols], cos, sin, seg)

    def store_queries(q_ref, tile0):
        for j in range(N_HEADS_A // 2):
            y = normed(tile0 + j)
            swapped = pltpu.roll(y, HEAD_DIM, 1)
            kv, g0 = (2 * j) // GROUP, (2 * j) % GROUP
            if kv == 0:
                even, odd = jnp.where(low, y, 0.0), jnp.where(low, swapped, 0.0)
            else:
                even, odd = jnp.where(low, 0.0, swapped), jnp.where(low, 0.0, y)
            q_ref[kv, 0, g0 * bm:(g0 + 1) * bm, :] = even.astype(BF16)
            q_ref[kv, 0, (g0 + 1) * bm:(g0 + 2) * bm, :] = odd.astype(BF16)

    tq = QA_W // LANES
    store_queries(qa_ref, 0)
    ka_ref[0] = normed(tq).astype(BF16)
    _store_value_heads(va_ref, proj[:, (tq + 1) * LANES:(tq + 2) * LANES])
    store_queries(qb_ref, tq + 2)
    kb_ref[0] = normed(2 * tq + 2).astype(BF16)
    _store_value_heads(vb_ref, proj[:, (2 * tq + 3) * LANES:(2 * tq + 4) * LANES])


def _inproj_c_kernel(x_ref, g_ref, s_ref, w_ref, hg_ref, cos_ref, sin_ref, seg_ref, q_ref, k_ref, v_ref):
    bm = x_ref.shape[0]
    h = _modulated(x_ref[...], g_ref, s_ref).astype(BF16)
    proj = jnp.dot(h, w_ref[...], preferred_element_type=F32)
    cos, sin, seg = cos_ref[...], sin_ref[...], seg_ref[...]
    lane = _lane((bm, LANES))
    low = lane < HEAD_DIM
    ones_tile = jnp.where(lane == 0, 1.0, 0.0).astype(BF16)

    def normed(j):
        cols = slice(j * LANES, (j + 1) * LANES)
        return _head_norm_rope(proj[:, cols], hg_ref[:, cols], cos, sin, seg)

    for j in range(N_HEADS_C):
        y = normed(j)
        q_ref[j, 0, 0:bm, :] = jnp.where(low, y, 0.0).astype(BF16)
        q_ref[j, 0, bm:2 * bm, :] = jnp.where(low, 0.0, y).astype(BF16)
        k_ref[j] = normed(N_HEADS_C + j).astype(BF16)
        v = proj[:, (2 * N_HEADS_C + j) * LANES:(2 * N_HEADS_C + j + 1) * LANES]
        v_ref[j] = jnp.concatenate([v.astype(BF16), ones_tile], axis=1)


def _rope_tables(seq, n_ctx):
    rows_n = seq // GRID_W
    rows = jnp.broadcast_to(jnp.arange(rows_n, dtype=F32)[:, None], (rows_n, GRID_W)).reshape(-1)
    cols = jnp.broadcast_to(jnp.arange(GRID_W, dtype=F32)[None, :], (rows_n, GRID_W)).reshape(-1)
    half = HEAD_DIM // 2
    inv = ROPE_THETA ** (-jnp.arange(0, half, 2, dtype=F32) / half)
    ang = jnp.concatenate([rows[:, None] * inv, cols[:, None] * inv], axis=-1)
    reps = LANES // half
    sign = jnp.tile(jnp.concatenate([-jnp.ones((half,), F32), jnp.ones((half,), F32)]), LANES // HEAD_DIM)
    cos = jnp.pad(jnp.tile(jnp.cos(ang), (1, reps)), ((n_ctx, 0), (0, 0)), constant_values=1.0)
    sin = jnp.pad(jnp.tile(jnp.sin(ang), (1, reps)) * sign, ((n_ctx, 0), (0, 0)))
    return cos, sin


def _segment_mean_matrix():
    idx = jnp.arange(LANES) // HEAD_DIM
    return jnp.where(idx[:, None] == idx[None, :], 1.0 / HEAD_DIM, 0.0).astype(BF16)


def _q_slot(i, nblk):
    return (i + nblk - 1) % nblk


def _inproj_common_specs(t, d, n):
    bm = ROW_BLOCK
    return [pl.BlockSpec((bm, d), lambda i: (i, 0)),
            pl.BlockSpec((2, 1, d), lambda i: (0, 0, 0)),
            pl.BlockSpec((2, 1, d), lambda i: (0, 0, 0)),
            pl.BlockSpec((d, n), lambda i: (0, 0)),
            pl.BlockSpec((1, n), lambda i: (0, 0)),
            pl.BlockSpec((bm, LANES), lambda i: (i, 0)),
            pl.BlockSpec((bm, LANES), lambda i: (i, 0)),
            pl.BlockSpec((LANES, LANES), lambda i: (0, 0))]


def _inproj_ab(x, gain2, shift2, w, qn_a, kn_a, qn_b, kn_b, cos, sin):
    t, d = x.shape
    n = w.shape[1]
    bm, nblk = ROW_BLOCK, t // ROW_BLOCK
    tile = lambda g, reps: jnp.tile(g.astype(F32), reps)
    ones_v = jnp.ones((KVA_W,), F32)
    hg = jnp.concatenate([tile(qn_a, N_HEADS_A) * Q_SCALE, tile(kn_a, N_KV_A), ones_v,
                          tile(qn_b, N_HEADS_B) * Q_SCALE, tile(kn_b, N_KV_B), ones_v]).reshape(1, n)
    q_shape = jax.ShapeDtypeStruct((N_KV_A, nblk, GROUP * bm, LANES), BF16)
    k_shape = jax.ShapeDtypeStruct((1, t, LANES), BF16)
    v_shape = jax.ShapeDtypeStruct((N_KV_A, t, LANES), BF16)
    q_spec = pl.BlockSpec((N_KV_A, 1, GROUP * bm, LANES), lambda i: (0, _q_slot(i, nblk), 0, 0))
    k_spec = pl.BlockSpec((1, bm, LANES), lambda i: (0, i, 0))
    v_spec = pl.BlockSpec((N_KV_A, bm, LANES), lambda i: (0, i, 0))
    return pl.pallas_call(
        _inproj_ab_kernel,
        grid=(nblk,),
        in_specs=_inproj_common_specs(t, d, n),
        out_specs=[q_spec, q_spec, k_spec, k_spec, v_spec, v_spec],
        out_shape=[q_shape, q_shape, k_shape, k_shape, v_shape, v_shape],
        compiler_params=_params("parallel"),
        name="inproj_ab",
    )(x, gain2, shift2, w, hg, cos, sin, _segment_mean_matrix())


def _inproj_c(x, gain2, shift2, w, qn, kn, cos, sin):
    t, d = x.shape
    n = w.shape[1]
    bm, nblk = ROW_BLOCK, t // ROW_BLOCK
    h = N_HEADS_C
    tile = lambda g: jnp.tile(g.astype(F32), 2 * h)
    hg = jnp.concatenate([tile(qn) * Q_SCALE, tile(kn), jnp.ones((h * DV_C,), F32)]).reshape(1, n)
    return pl.pallas_call(
        _inproj_c_kernel,
        grid=(nblk,),
        in_specs=_inproj_common_specs(t, d, n),
        out_specs=[pl.BlockSpec((h, 1, 2 * bm, LANES), lambda i: (0, _q_slot(i, nblk), 0, 0)),
                   pl.BlockSpec((h, bm, LANES), lambda i: (0, i, 0)),
                   pl.BlockSpec((h, bm, DV_C + LANES), lambda i: (0, i, 0))],
        out_shape=[jax.ShapeDtypeStruct((h, nblk, 2 * bm, LANES), BF16),
                   jax.ShapeDtypeStruct((h, t, LANES), BF16),
                   jax.ShapeDtypeStruct((h, t, DV_C + LANES), BF16)],
        compiler_params=_params("parallel"),
        name="inproj_c",
    )(x, gain2, shift2, w, hg, cos, sin, _segment_mean_matrix())


def _merge_gqa_heads(o, bq):
    low = _lane((bq, LANES)) < HEAD_DIM
    pairs = [jnp.where(low, o[g * bq:(g + 1) * bq], pltpu.roll(o[(g + 1) * bq:(g + 2) * bq], HEAD_DIM, 1))
             for g in range(0, GROUP, 2)]
    return jnp.concatenate(pairs, axis=1)


def _flash_kernel(*refs, mode, online, n_keys, bk, dv, l0):
    refs = list(refs)
    q_ref, k_ref, v_ref = refs[:3]
    pos = 3
    if online:
        m0_ref = refs[pos]
        pos += 1
    if mode == "diff":
        lam_ref, sub_ref = refs[pos:pos + 2]
        pos += 2
    o_ref = refs[pos]
    acc_sc = refs[pos + 1]
    m_sc = refs[pos + 2] if online else None

    nsub, rb = q_ref.shape[1], q_ref.shape[2]
    q = q_ref[0].reshape(nsub * rb, LANES)
    acc_sc[...] = jnp.where(_lane(acc_sc.shape) == dv, l0, 0.0).astype(F32)
    if online:
        m_sc[...] = m0_ref[0]

    def block(start, size):
        kb = k_ref[0, pl.ds(start, size), :]
        vb = v_ref[0, pl.ds(start, size), :]
        s = lax.dot_general(q, kb, (((1,), (1,)), ((), ())), preferred_element_type=F32)
        if online:
            m_prev = m_sc[...]
            m_new = jnp.maximum(m_prev, jnp.max(s, axis=-1, keepdims=True))
            p = jnp.exp2(s - m_new).astype(BF16)
            acc_sc[...] = jnp.exp2(m_prev - m_new) * acc_sc[...] + jnp.dot(p, vb, preferred_element_type=F32)
            m_sc[...] = m_new
        else:
            acc_sc[...] += jnp.dot(jnp.exp2(s).astype(BF16), vb, preferred_element_type=F32)

    n_full, tail = n_keys // bk, n_keys % bk
    if n_full:
        def body(i, carry):
            block(pl.multiple_of(i * bk, bk), bk)
            return carry
        lax.fori_loop(0, n_full, body, 0)
    if tail:
        block(n_full * bk, tail)

    acc = acc_sc[...]
    o = acc[:, :LANES] / acc[:, dv:dv + 1]
    if mode == "gqa":
        o_ref[...] = _merge_gqa_heads(o, rb // GROUP).astype(o_ref.dtype)
    else:
        bq = rb // 2
        for b in range(nsub):
            d = o[b * rb:b * rb + bq] - lam_ref[...] * o[b * rb + bq:(b + 1) * rb]
            y = d * lax.rsqrt(jnp.mean(d * d, axis=-1, keepdims=True) + EPS) * sub_ref[...]
            o_ref[b * bq:(b + 1) * bq, :] = y.astype(o_ref.dtype)


def _flash_call(q, k, v, extra, *, mode, online, n_keys, slot0, nsub, n_steps, l0=0.0):
    hkv, _, rb, _ = q.shape
    hk, dva = k.shape[0], v.shape[2]
    dv = HEAD_DIM if mode == "gqa" else DV_C
    bq = rb // (GROUP if mode == "gqa" else 2)
    ocols = GROUP * HEAD_DIM if mode == "gqa" else DV_C
    r = nsub * rb
    bk = min(KEY_BLOCK, n_keys)
    in_specs = [pl.BlockSpec((1, nsub, rb, LANES), lambda h, i: (h, slot0 // nsub + i, 0, 0)),
                pl.BlockSpec((1, n_keys, LANES), (lambda h, i: (h, 0, 0)) if hk > 1 else (lambda h, i: (0, 0, 0))),
                pl.BlockSpec((1, n_keys, dva), lambda h, i: (h, 0, 0))]
    args = [q, k, v]
    scratch = [pltpu.VMEM((r, dva), F32)]
    if online:
        m0 = extra.pop(0)
        in_specs.append(pl.BlockSpec((1, r, 1), lambda h, i: (h, 0, 0)))
        args.append(m0)
        scratch.append(pltpu.VMEM((r, 1), F32))
    for a in extra:
        in_specs.append(pl.BlockSpec(a.shape, lambda h, i: (0, 0)))
        args.append(a)
    return pl.pallas_call(
        functools.partial(_flash_kernel, mode=mode, online=online, n_keys=n_keys, bk=bk, dv=dv, l0=l0),
        grid=(hkv, n_steps),
        in_specs=in_specs,
        out_specs=pl.BlockSpec((nsub * bq, ocols), lambda h, i: (i, h)),
        out_shape=jax.ShapeDtypeStruct((n_steps * nsub * bq, hkv * ocols), BF16),
        scratch_shapes=scratch,
        compiler_params=_params("parallel", "parallel"),
        name="flash_online" if online else "flash_bounded",
    )(*args)


def _logits_bounded(q, k):
    qn = jnp.max(jnp.sum(jnp.square(q.astype(F32)), axis=-1))
    kn = jnp.max(jnp.sum(jnp.square(k.astype(F32)), axis=-1))
    return qn * kn <= LOG2_LOGIT_BOUND ** 2


def _attend(q, k, v, extra, bounded, **kw):
    hkv, r = q.shape[0], kw["nsub"] * q.shape[2]
    fast = lambda q_, k_, v_, *e: _flash_call(q_, k_, v_, list(e), online=False, **kw)
    safe = lambda q_, k_, v_, *e: _flash_call(q_, k_, v_, [jnp.full((hkv, r, 1), NEG, F32)] + list(e), online=True, **kw)
    return lax.cond(bounded, fast, safe, q, k, v, *extra)


def _window_kernel(q_ref, k_ref, v_ref, sink_ref, o_ref, *, bq, n_ctx):
    q = q_ref[0, 0]
    r = q.shape[0]
    t = k_ref.shape[1]
    w = bq + 2 * WINDOW
    q0 = pl.program_id(1) * bq
    ws = pl.multiple_of(jnp.clip(n_ctx + q0 - WINDOW, 0, t - w), WINDOW)
    kw = k_ref[0, pl.ds(ws, w), :]
    vw = v_ref[0, pl.ds(ws, w), :]
    contract_last = (((1,), (1,)), ((), ()))
    s_loc = lax.dot_general(q, kw, contract_last, preferred_element_type=F32)
    qpos = q0 + (lax.broadcasted_iota(jnp.int32, (r, w), 0) & (bq - 1))
    kpos = ws - n_ctx + lax.broadcasted_iota(jnp.int32, (r, w), 1)
    mask = (kpos >= 0) & (kpos - qpos <= WINDOW) & (qpos - kpos <= WINDOW)
    s_loc = jnp.where(mask, s_loc, NEG)
    s_ctx = lax.dot_general(q, k_ref[0, 0:n_ctx, :], contract_last, preferred_element_type=F32)
    sink = sink_ref[0]
    m = jnp.maximum(sink, jnp.maximum(jnp.max(s_loc, axis=-1, keepdims=True), jnp.max(s_ctx, axis=-1, keepdims=True)))
    p_loc = jnp.exp2(s_loc - m)
    p_ctx = jnp.exp2(s_ctx - m)
    l = jnp.exp2(sink - m) + jnp.sum(p_loc, axis=-1, keepdims=True) + jnp.sum(p_ctx, axis=-1, keepdims=True)
    o = (jnp.dot(p_loc.astype(BF16), vw, preferred_element_type=F32)
         + jnp.dot(p_ctx.astype(BF16), v_ref[0, 0:n_ctx, :], preferred_element_type=F32))
    o_ref[...] = _merge_gqa_heads(o / l, bq).astype(o_ref.dtype)


def _window_attention(q, k, v, sink_rows, n_ctx):
    hkv, slots, r, _ = q.shape
    t = k.shape[1]
    bq = r // GROUP
    nq = slots - n_ctx // bq
    return pl.pallas_call(
        functools.partial(_window_kernel, bq=bq, n_ctx=n_ctx),
        grid=(hkv, nq),
        in_specs=[pl.BlockSpec((1, 1, r, LANES), lambda h, i: (h, i, 0, 0)),
                  pl.BlockSpec((1, t, LANES), lambda h, i: (0, 0, 0)),
                  pl.BlockSpec((1, t, LANES), lambda h, i: (h, 0, 0)),
                  pl.BlockSpec((1, r, 1), lambda h, i: (h, 0, 0))],
        out_specs=pl.BlockSpec((bq, GROUP * HEAD_DIM), lambda h, i: (i, h)),
        out_shape=jax.ShapeDtypeStruct((nq * bq, hkv * GROUP * HEAD_DIM), BF16),
        compiler_params=_params("parallel", "parallel"),
        name="window_attention",
    )(q, k, v, sink_rows)


def _sink_rows(sink, bq):
    hkv, g = sink.shape
    return jnp.broadcast_to(sink.astype(F32)[:, :, None] * LOG2E, (hkv, g, bq)).reshape(hkv, g * bq, 1)


def _mixer_ab(heads, n_ctx, sink_a, with_ctx):
    qa, qb, ka, kb, va, vb = heads
    t = ka.shape[1]
    bq = ROW_BLOCK
    n_lat = (t - n_ctx) // bq
    sink = _sink_rows(sink_a.reshape(N_KV_A, GROUP), bq)
    bounded = _logits_bounded(qb, kb)
    oa = _window_attention(qa, ka, va, sink, n_ctx)
    ob = _attend(qb, kb, vb, [], bounded, mode="gqa", n_keys=t, slot0=0, nsub=1, n_steps=n_lat)
    if not with_ctx:
        return (oa, ob), None
    oca = _flash_call(qa, ka, va, [sink], mode="gqa", online=True, n_keys=n_ctx, slot0=n_lat, nsub=1, n_steps=1, l0=1.0)
    ocb = _attend(qb, kb, vb, [], bounded, mode="gqa", n_keys=n_ctx, slot0=n_lat, nsub=1, n_steps=1)
    return (oa, ob), (oca, ocb)


def _mixer_c(heads, n_ctx, lam_p, subln, lam_init, with_ctx):
    q, k, v = heads
    t = k.shape[1]
    n_lat = (t - n_ctx) // ROW_BLOCK
    lp = lam_p.astype(F32)
    lam = jnp.exp(jnp.sum(lp[0] * lp[1])) - jnp.exp(jnp.sum(lp[2] * lp[3])) + lam_init
    extra = [jnp.full((1, DV_C), lam, F32), (subln.astype(F32) * (1 - lam_init)).reshape(1, DV_C)]
    bounded = _logits_bounded(q, k)
    nsub = 2 if n_lat % 2 == 0 else 1
    o_lat = _attend(q, k, v, extra, bounded, mode="diff", n_keys=t, slot0=0, nsub=nsub, n_steps=n_lat // nsub)
    if not with_ctx:
        return (o_lat,), None
    o_ctx = _attend(q, k, v, extra, bounded, mode="diff", n_keys=n_ctx, slot0=n_lat, nsub=1, n_steps=1)
    return (o_lat,), (o_ctx,)


def _out_proj_kernel(*refs, n_parts, has_ctx):
    lat = refs[:n_parts]
    ctx = refs[n_parts:2 * n_parts] if has_ctx else None
    w_ref, x_ref, gate_ref, y_ref = refs[-4:]
    is_ctx = pl.program_id(0) == 0 if has_ctx else False
    acc = None
    col = 0
    for p in range(n_parts):
        o = lat[p][...]
        if has_ctx:
            o = jnp.where(is_ctx, ctx[p][...], o)
        width = o.shape[1]
        part = jnp.dot(o, w_ref[col:col + width, :], preferred_element_type=F32)
        acc = part if acc is None else acc + part
        col += width
    gate = gate_ref[0] if not has_ctx else gate_ref[jnp.where(is_ctx, 0, 1)]
    y_ref[...] = x_ref[...] + gate * acc


def _out_proj(o_lat, o_ctx, w, x, gate2):
    t, d = x.shape
    bm = ROW_BLOCK
    has_ctx = o_ctx is not None
    n_parts = len(o_lat)
    lat_map = (lambda i: (jnp.maximum(i - 1, 0), 0)) if has_ctx else (lambda i: (i, 0))
    in_specs = [pl.BlockSpec((bm, o.shape[1]), lat_map) for o in o_lat]
    args = list(o_lat)
    if has_ctx:
        in_specs += [pl.BlockSpec((bm, o.shape[1]), lambda i: (0, 0)) for o in o_ctx]
        args += list(o_ctx)
        gate = gate2
    else:
        gate = gate2[1:2]
    in_specs += [pl.BlockSpec(w.shape, lambda i: (0, 0)),
                 pl.BlockSpec((bm, d), lambda i: (i, 0)),
                 pl.BlockSpec(gate.shape, lambda i: (0, 0, 0))]
    return pl.pallas_call(
        functools.partial(_out_proj_kernel, n_parts=n_parts, has_ctx=has_ctx),
        grid=(t // bm,),
        in_specs=in_specs,
        out_specs=pl.BlockSpec((bm, d), lambda i: (i, 0)),
        out_shape=jax.ShapeDtypeStruct((t, d), F32),
        compiler_params=_params("parallel"),
        name="out_proj",
    )(*args, w, x, gate)


def _mod_router_kernel(x_ref, g_ref, s_ref, wr_ref, tok_ref, logit_ref, *, row0):
    y = x_ref[...]
    y = y * lax.rsqrt(jnp.mean(y * y, axis=-1, keepdims=True) + EPS)
    row = jnp.minimum(pl.program_id(0) + row0, 1)
    h = y * g_ref[row] + s_ref[row]
    tok_ref[...] = h
    logit_ref[...] = jnp.dot(h, wr_ref[...], preferred_element_type=F32, precision=lax.Precision.HIGHEST)


def _mod_router(x, gain2, shift2, w_router, has_ctx):
    t, d = x.shape
    bm = ROW_BLOCK
    return pl.pallas_call(
        functools.partial(_mod_router_kernel, row0=0 if has_ctx else 1),
        grid=(t // bm,),
        in_specs=[pl.BlockSpec((bm, d), lambda i: (i, 0)),
                  pl.BlockSpec((2, 1, d), lambda i: (0, 0, 0)),
                  pl.BlockSpec((2, 1, d), lambda i: (0, 0, 0)),
                  pl.BlockSpec((d, LANES), lambda i: (0, 0))],
        out_specs=[pl.BlockSpec((bm, d), lambda i: (i, 0)),
                   pl.BlockSpec((bm, LANES), lambda i: (i, 0))],
        out_shape=[jax.ShapeDtypeStruct((t, d), F32), jax.ShapeDtypeStruct((t, LANES), F32)],
        compiler_params=_params("parallel"),
        name="mod_router",
    )(x, gain2, shift2, w_router)


def _moe_kernel(be_ref, nu_ref, x_ref, w1_ref, w3_ref, w2_ref, y_ref):
    i = pl.program_id(0)

    @pl.when(i < nu_ref[0])
    def _():
        x = x_ref[...].astype(BF16)
        a = jnp.dot(x, w1_ref[0].astype(BF16), preferred_element_type=F32)
        b = jnp.dot(x, w3_ref[0].astype(BF16), preferred_element_type=F32)
        hidden = (a * jax.nn.sigmoid(a)) * b
        y_ref[...] = jnp.dot(hidden.astype(BF16), w2_ref[0].astype(BF16), preferred_element_type=F32)

    @pl.when(i >= nu_ref[0])
    def _():
        y_ref[...] = jnp.zeros(y_ref.shape, y_ref.dtype)


def _moe_experts(buf, blk_e, n_used, w1, w3, w2):
    n_rows, d = buf.shape
    de = w1.shape[2]
    n_blk = n_rows // EXPERT_BLOCK
    grid_spec = pltpu.PrefetchScalarGridSpec(
        num_scalar_prefetch=2,
        grid=(n_blk,),
        in_specs=[pl.BlockSpec((EXPERT_BLOCK, d), lambda i, be, nu: (i, 0)),
                  pl.BlockSpec((1, d, de), lambda i, be, nu: (be[i], 0, 0)),
                  pl.BlockSpec((1, d, de), lambda i, be, nu: (be[i], 0, 0)),
                  pl.BlockSpec((1, de, d), lambda i, be, nu: (be[i], 0, 0))],
        out_specs=pl.BlockSpec((EXPERT_BLOCK, d), lambda i, be, nu: (i, 0)),
    )
    return pl.pallas_call(
        _moe_kernel,
        grid_spec=grid_spec,
        out_shape=jax.ShapeDtypeStruct((n_rows, d), F32),
        compiler_params=_params("arbitrary"),
        name="moe_experts",
    )(blk_e, n_used, buf, w1, w3, w2)


def _top1(p):
    m = jnp.max(p, axis=-1, keepdims=True)
    idx = lax.broadcasted_iota(jnp.int32, p.shape, 1)
    return m, jnp.min(jnp.where(p == m, idx, p.shape[-1]), axis=-1, keepdims=True)


def _sc_gather(x, idx):
    t, d = x.shape
    n = idx.shape[0] * SC_ROW_SPLIT
    dp = d // SC_ROW_SPLIT
    pieces = (idx[:, None] * SC_ROW_SPLIT + jnp.arange(SC_ROW_SPLIT, dtype=jnp.int32)[None, :]).reshape(1, n)
    mesh = plsc.VectorSubcoreMesh(core_axis_name="core", subcore_axis_name="subcore")

    @pl.kernel(out_type=jax.ShapeDtypeStruct((n, dp), x.dtype), mesh=mesh, scratch_types=[])
    def gather(x_hbm, i_hbm, o_hbm):
        def body(i_vmem, o_vmem):
            pltpu.sync_copy(x_hbm.at[i_vmem.at[0]], o_vmem)

        pltpu.emit_pipeline(
            body,
            grid=(n // SC_GATHER_WINDOW,),
            in_specs=[pl.BlockSpec((1, SC_GATHER_WINDOW), lambda i: (0, i))],
            out_specs=[pl.BlockSpec((SC_GATHER_WINDOW, dp), lambda i: (i, 0))],
            core_axis_name=("core", "subcore"),
            dimension_semantics=(pltpu.PARALLEL,),
        )(i_hbm, o_hbm)

    return gather(x.reshape(t * SC_ROW_SPLIT, dp), pieces).reshape(idx.shape[0], d)


def _combine_kernel(y_ref, w_ref, x_ref, gate_ref, o_ref, *, row0):
    d = x_ref.shape[1]
    y, w = y_ref[...], w_ref[...]
    f = w[:, 0:1] * y[:, :d] + w[:, 1:2] * y[:, d:]
    row = jnp.minimum(pl.program_id(0) + row0, 1)
    o_ref[...] = x_ref[...] + gate_ref[row] * f


def _combine(y2, weights, x, gate2, has_ctx):
    t, d = x.shape
    bm = ROW_BLOCK
    return pl.pallas_call(
        functools.partial(_combine_kernel, row0=0 if has_ctx else 1),
        grid=(t // bm,),
        in_specs=[pl.BlockSpec((bm, TOP_K * d), lambda i: (i, 0)),
                  pl.BlockSpec((bm, TOP_K), lambda i: (i, 0)),
                  pl.BlockSpec((bm, d), lambda i: (i, 0)),
                  pl.BlockSpec((2, 1, d), lambda i: (0, 0, 0))],
        out_specs=pl.BlockSpec((bm, d), lambda i: (i, 0)),
        out_shape=jax.ShapeDtypeStruct((t, d), F32),
        compiler_params=_params("parallel"),
        name="moe_combine",
    )(y2, weights, x, gate2)


def _prefix_rank(onehot):
    rows, e = onehot.shape
    blk = EXPERT_BLOCK
    oh = onehot.astype(F32).reshape(rows // blk, blk, e)
    tri = (jnp.arange(blk)[:, None] > jnp.arange(blk)[None, :]).astype(F32)
    within = jnp.einsum("ij,bjk->bik", tri, oh)
    tot = jnp.sum(oh, axis=1)
    before = jnp.cumsum(tot, axis=0) - tot
    return jnp.sum((within + before[:, None, :]) * oh, axis=-1).reshape(rows).astype(jnp.int32)


def _hier_moe(tokens, logits, b_group, b_expert, w1, w3, w2):
    t, d = tokens.shape
    pg = jax.nn.softmax(logits[:, :N_GROUPS] + b_group.astype(F32), axis=-1)
    g_prob, g_idx = _top1(pg)
    le = (logits[:, N_GROUPS:N_GROUPS + N_EXPERTS] + b_expert.astype(F32)).reshape(t, N_GROUPS, EXPERTS_PER_GROUP)
    le = jnp.take_along_axis(le, g_idx[:, :, None], axis=1)[:, 0]
    pe = jax.nn.softmax(le, axis=-1)
    p1, i1 = _top1(pe)
    p2, i2 = _top1(jnp.where(lax.broadcasted_iota(jnp.int32, pe.shape, 1) == i1, -1.0, pe))
    e_prob, e_idx = jnp.concatenate([p1, p2], axis=-1), jnp.concatenate([i1, i2], axis=-1)
    weights = g_prob * e_prob / jnp.sum(e_prob, axis=-1, keepdims=True)
    flat_e = (g_idx * EXPERTS_PER_GROUP + e_idx).reshape(-1)
    onehot = (flat_e[:, None] == jnp.arange(N_EXPERTS)[None, :]).astype(jnp.int32)
    rank = _prefix_rank(onehot)
    counts = jnp.sum(onehot, axis=0)
    padded = (counts + EXPERT_BLOCK - 1) // EXPERT_BLOCK * EXPERT_BLOCK
    pend = jnp.cumsum(padded)
    dest = (pend - padded)[flat_e] + rank
    n_rows = (t * TOP_K + EXPERT_BLOCK - 1) // EXPERT_BLOCK * EXPERT_BLOCK + N_EXPERTS * EXPERT_BLOCK
    n_blk = n_rows // EXPERT_BLOCK
    src = jnp.zeros((n_rows,), jnp.int32).at[dest].set(jnp.arange(t * TOP_K, dtype=jnp.int32) // TOP_K)
    buf = _sc_gather(tokens, src)
    blk_e = jnp.minimum(jnp.searchsorted(pend, jnp.arange(n_blk) * EXPERT_BLOCK, side="right"), N_EXPERTS - 1).astype(jnp.int32)
    n_used = (pend[-1] // EXPERT_BLOCK).astype(jnp.int32).reshape(1)
    yb = _moe_experts(buf, blk_e, n_used, w1, w3, w2)
    return _sc_gather(yb, dest).reshape(t, TOP_K * d), weights


def kernel(x, c, ctx, c_ctx, w_mod, b_mod, norm_mix, norm_ffn, w_in_ab, w_out_ab, qn_a, kn_a, sink_a, qn_b, kn_b, w_in_c, w_out_c, qn_c, kn_c, lam_c, subln_c, w_group, b_group, w_expert, b_expert, w1, w3, w2):
    b, s_lat, d = x.shape
    n_ctx = ctx.shape[1]
    assert b == 1 and n_ctx == ROW_BLOCK and s_lat % (2 * ROW_BLOCK) == 0
    depth = w_mod.shape[0]
    cos, sin = _rope_tables(s_lat, n_ctx)
    mods = _mod_vectors(c, c_ctx, w_mod, b_mod)
    xs = jnp.concatenate([ctx[0], x[0]], axis=0)
    has_ctx = True
    for l in range(depth):
        last = l == depth - 1
        i = l // 2
        sh1, sc1, gt1, sh2, sc2, gt2 = [mods[l, :2, j * d:(j + 1) * d].reshape(2, 1, d) for j in range(6)]
        gain1 = norm_mix[l].astype(F32) * (1 + sc1)
        if l % 2 == 0:
            heads = _inproj_ab(xs, gain1, sh1, w_in_ab[i].astype(BF16), qn_a[i], kn_a[i], qn_b[i], kn_b[i], cos, sin)
            o_lat, o_ctx = _mixer_ab(heads, n_ctx, sink_a[i], not last)
            w_out = w_out_ab[i]
        else:
            lam_init = 0.8 - 0.6 * math.exp(-0.3 * l)
            heads = _inproj_c(xs, gain1, sh1, w_in_c[i].astype(BF16), qn_c[i], kn_c[i], cos, sin)
            o_lat, o_ctx = _mixer_c(heads, n_ctx, lam_c[i], subln_c[i], lam_init, not last)
            w_out = w_out_c[i]
        if last:
            xs, has_ctx = xs[n_ctx:], False
        xs = _out_proj(o_lat, o_ctx, w_out.astype(BF16), xs, gt1)
        gain2 = norm_ffn[l].astype(F32) * (1 + sc2)
        w_router = jnp.zeros((d, LANES), F32).at[:, :N_GROUPS].set(w_group[l]).at[:, N_GROUPS:N_GROUPS + N_EXPERTS].set(w_expert[l])
        tokens, logits = _mod_router(xs, gain2, sh2, w_router, has_ctx)
        y2, weights = _hier_moe(tokens, logits, b_group[l], b_expert[l], w1[l], w3[l], w2[l])
        xs = _combine(y2, weights, xs, gt2, has_ctx)
    return xs.reshape(b, s_lat, d)
```

```python
import functools
import math

import jax
import jax.numpy as jnp
from jax import lax
from jax.experimental import pallas as pl
from jax.experimental.pallas import tpu as pltpu
from jax.experimental.pallas import tpu_sc as plsc

F32 = jnp.float32
BF16 = jnp.bfloat16

GRID_W = 64
HEAD_DIM = 64
WINDOW = 128
ROPE_THETA = 10000.0
EPS = 1e-6
NEG = -1e30
N_HEADS_A, N_KV_A = 8, 2
N_HEADS_B, N_KV_B = 8, 2
GROUP = N_HEADS_A // N_KV_A
QA_W, KVA_W = N_HEADS_A * HEAD_DIM, N_KV_A * HEAD_DIM
QB_W, KVB_W = N_HEADS_B * HEAD_DIM, N_KV_B * HEAD_DIM
N_HEADS_C = 8
DV_C = 2 * HEAD_DIM
N_GROUPS, EXPERTS_PER_GROUP, TOP_K = 4, 8, 2
N_EXPERTS = N_GROUPS * EXPERTS_PER_GROUP
LANES = 128
ROW_BLOCK = 256
EXPERT_BLOCK = 256
SC_GATHER_WINDOW = 128
SC_ROW_SPLIT = 4
KEY_BLOCK = 4096
VMEM_LIMIT = 48 * 1024 * 1024
LOG2E = math.log2(math.e)
Q_SCALE = HEAD_DIM ** -0.5 * LOG2E
LOG2_LOGIT_BOUND = 60.0


def _params(*sem):
    return pltpu.CompilerParams(dimension_semantics=sem, vmem_limit_bytes=VMEM_LIMIT)


def _lane(shape):
    return lax.broadcasted_iota(jnp.int32, shape, 1)


def _mod_vec_kernel(a_ref, w_ref, b_ref, o_ref):
    a = a_ref[...]
    a = a * jax.nn.sigmoid(a)
    o_ref[0] = jnp.dot(a, w_ref[0], preferred_element_type=F32, precision=lax.Precision.HIGHEST) + b_ref[0]


def _mod_vectors(c, c_ctx, w_mod, b_mod):
    depth, d, n = w_mod.shape
    a = jnp.zeros((8, d), F32).at[0].set(c_ctx).at[1].set(c[0])
    bn = 1024
    return pl.pallas_call(
        _mod_vec_kernel,
        grid=(depth, n // bn),
        in_specs=[pl.BlockSpec((8, d), lambda l, j: (0, 0)),
                  pl.BlockSpec((1, d, bn), lambda l, j: (l, 0, j)),
                  pl.BlockSpec((1, 1, bn), lambda l, j: (l, 0, j))],
        out_specs=pl.BlockSpec((1, 8, bn), lambda l, j: (l, 0, j)),
        out_shape=jax.ShapeDtypeStruct((depth, 8, n), F32),
        compiler_params=_params("parallel", "parallel"),
        name="mod_vectors",
    )(a, w_mod, b_mod.reshape(depth, 1, n))


def _modulated(x, g_ref, s_ref):
    y = x * lax.rsqrt(jnp.mean(x * x, axis=-1, keepdims=True) + EPS)
    row = jnp.minimum(pl.program_id(0), 1)
    return y * g_ref[row] + s_ref[row]


def _head_norm_rope(x, gain, cos, sin, seg_mean):
    x2 = x * x
    hi = x2.astype(BF16)
    lo = (x2 - hi.astype(F32)).astype(BF16)
    ms = jnp.dot(hi, seg_mean, preferred_element_type=F32) + jnp.dot(lo, seg_mean, preferred_element_type=F32)
    y = x * lax.rsqrt(ms + EPS) * gain
    first_half = (_lane(y.shape) & (HEAD_DIM - 1)) < HEAD_DIM // 2
    partner = jnp.where(first_half, pltpu.roll(y, LANES - HEAD_DIM // 2, 1), pltpu.roll(y, HEAD_DIM // 2, 1))
    return y * cos + partner * sin


def _store_value_heads(v_ref, v):
    lane = _lane(v.shape)
    ones_col = jnp.where(lane == HEAD_DIM, 1.0, 0.0)
    v_ref[0] = jnp.where(lane < HEAD_DIM, v, ones_col).astype(BF16)
    v_ref[1] = jnp.where(lane < HEAD_DIM, pltpu.roll(v, HEAD_DIM, 1), ones_col).astype(BF16)


def _inproj_ab_kernel(x_ref, g_ref, s_ref, w_ref, hg_ref, cos_ref, sin_ref, seg_ref,
                      qa_ref, qb_ref, ka_ref, kb_ref, va_ref, vb_ref):
    bm = x_ref.shape[0]
    h = _modulated(x_ref[...], g_ref, s_ref).astype(BF16)
    proj = jnp.dot(h, w_ref[...], preferred_element_type=F32)
    cos, sin, seg = cos_ref[...], sin_ref[...], seg_ref[...]
    low = _lane((bm, LANES)) < HEAD_DIM

    def normed(j):
        cols = slice(j * LANES, (j + 1) * LANES)
        return _head_norm_rope(proj[:, cols], hg_ref[:, cols], cos, sin, seg)

    def store_queries(q_ref, tile0):
        for j in range(N_HEADS_A // 2):
            y = normed(tile0 + j)
            swapped = pltpu.roll(y, HEAD_DIM, 1)
            kv, g0 = (2 * j) // GROUP, (2 * j) % GROUP
            if kv == 0:
                even, odd = jnp.where(low, y, 0.0), jnp.where(low, swapped, 0.0)
            else:
                even, odd = jnp.where(low, 0.0, swapped), jnp.where(low, 0.0, y)
            q_ref[kv, 0, g0 * bm:(g0 + 1) * bm, :] = even.astype(BF16)
            q_ref[kv, 0, (g0 + 1) * bm:(g0 + 2) * bm, :] = odd.astype(BF16)

    tq = QA_W // LANES
    store_queries(qa_ref, 0)
    ka_ref[0] = normed(tq).astype(BF16)
    _store_value_heads(va_ref, proj[:, (tq + 1) * LANES:(tq + 2) * LANES])
    store_queries(qb_ref, tq + 2)
    kb_ref[0] = normed(2 * tq + 2).astype(BF16)
    _store_value_heads(vb_ref, proj[:, (2 * tq + 3) * LANES:(2 * tq + 4) * LANES])


def _inproj_c_kernel(x_ref, g_ref, s_ref, w_ref, hg_ref, cos_ref, sin_ref, seg_ref, q_ref, k_ref, v_ref):
    bm = x_ref.shape[0]
    h = _modulated(x_ref[...], g_ref, s_ref).astype(BF16)
    proj = jnp.dot(h, w_ref[...], preferred_element_type=F32)
    cos, sin, seg = cos_ref[...], sin_ref[...], seg_ref[...]
    lane = _lane((bm, LANES))
    low = lane < HEAD_DIM
    ones_tile = jnp.where(lane == 0, 1.0, 0.0).astype(BF16)

    def normed(j):
        cols = slice(j * LANES, (j + 1) * LANES)
        return _head_norm_rope(proj[:, cols], hg_ref[:, cols], cos, sin, seg)

    for j in range(N_HEADS_C):
        y = normed(j)
        q_ref[j, 0, 0:bm, :] = jnp.where(low, y, 0.0).astype(BF16)
        q_ref[j, 0, bm:2 * bm, :] = jnp.where(low, 0.0, y).astype(BF16)
        k_ref[j] = normed(N_HEADS_C + j).astype(BF16)
        v = proj[:, (2 * N_HEADS_C + j) * LANES:(2 * N_HEADS_C + j + 1) * LANES]
        v_ref[j] = jnp.concatenate([v.astype(BF16), ones_tile], axis=1)


def _rope_tables(seq, n_ctx):
    rows_n = seq // GRID_W
    rows = jnp.broadcast_to(jnp.arange(rows_n, dtype=F32)[:, None], (rows_n, GRID_W)).reshape(-1)
    cols = jnp.broadcast_to(jnp.arange(GRID_W, dtype=F32)[None, :], (rows_n, GRID_W)).reshape(-1)
    half = HEAD_DIM // 2
    inv = ROPE_THETA ** (-jnp.arange(0, half, 2, dtype=F32) / half)
    ang = jnp.concatenate([rows[:, None] * inv, cols[:, None] * inv], axis=-1)
    reps = LANES // half
    sign = jnp.tile(jnp.concatenate([-jnp.ones((half,), F32), jnp.ones((half,), F32)]), LANES // HEAD_DIM)
    cos = jnp.pad(jnp.tile(jnp.cos(ang), (1, reps)), ((n_ctx, 0), (0, 0)), constant_values=1.0)
    sin = jnp.pad(jnp.tile(jnp.sin(ang), (1, reps)) * sign, ((n_ctx, 0), (0, 0)))
    return cos, sin


def _segment_mean_matrix():
    idx = jnp.arange(LANES) // HEAD_DIM
    return jnp.where(idx[:, None] == idx[None, :], 1.0 / HEAD_DIM, 0.0).astype(BF16)


def _q_slot(i, nblk):
    return (i + nblk - 1) % nblk


def _inproj_common_specs(t, d, n):
    bm = ROW_BLOCK
    return [pl.BlockSpec((bm, d), lambda i: (i, 0)),
            pl.BlockSpec((2, 1, d), lambda i: (0, 0, 0)),
            pl.BlockSpec((2, 1, d), lambda i: (0, 0, 0)),
            pl.BlockSpec((d, n), lambda i: (0, 0)),
            pl.BlockSpec((1, n), lambda i: (0, 0)),
            pl.BlockSpec((bm, LANES), lambda i: (i, 0)),
            pl.BlockSpec((bm, LANES), lambda i: (i, 0)),
            pl.BlockSpec((LANES, LANES), lambda i: (0, 0))]


def _inproj_ab(x, gain2, shift2, w, qn_a, kn_a, qn_b, kn_b, cos, sin):
    t, d = x.shape
    n = w.shape[1]
    bm, nblk = ROW_BLOCK, t // ROW_BLOCK
    tile = lambda g, reps: jnp.tile(g.astype(F32), reps)
    ones_v = jnp.ones((KVA_W,), F32)
    hg = jnp.concatenate([tile(qn_a, N_HEADS_A) * Q_SCALE, tile(kn_a, N_KV_A), ones_v,
                          tile(qn_b, N_HEADS_B) * Q_SCALE, tile(kn_b, N_KV_B), ones_v]).reshape(1, n)
    q_shape = jax.ShapeDtypeStruct((N_KV_A, nblk, GROUP * bm, LANES), BF16)
    k_shape = jax.ShapeDtypeStruct((1, t, LANES), BF16)
    v_shape = jax.ShapeDtypeStruct((N_KV_A, t, LANES), BF16)
    q_spec = pl.BlockSpec((N_KV_A, 1, GROUP * bm, LANES), lambda i: (0, _q_slot(i, nblk), 0, 0))
    k_spec = pl.BlockSpec((1, bm, LANES), lambda i: (0, i, 0))
    v_spec = pl.BlockSpec((N_KV_A, bm, LANES), lambda i: (0, i, 0))
    return pl.pallas_call(
        _inproj_ab_kernel,
        grid=(nblk,),
        in_specs=_inproj_common_specs(t, d, n),
        out_specs=[q_spec, q_spec, k_spec, k_spec, v_spec, v_spec],
        out_shape=[q_shape, q_shape, k_shape, k_shape, v_shape, v_shape],
        compiler_params=_params("parallel"),
        name="inproj_ab",
    )(x, gain2, shift2, w, hg, cos, sin, _segment_mean_matrix())


def _inproj_c(x, gain2, shift2, w, qn, kn, cos, sin):
    t, d = x.shape
    n = w.shape[1]
    bm, nblk = ROW_BLOCK, t // ROW_BLOCK
    h = N_HEADS_C
    tile = lambda g: jnp.tile(g.astype(F32), 2 * h)
    hg = jnp.concatenate([tile(qn) * Q_SCALE, tile(kn), jnp.ones((h * DV_C,), F32)]).reshape(1, n)
    return pl.pallas_call(
        _inproj_c_kernel,
        grid=(nblk,),
        in_specs=_inproj_common_specs(t, d, n),
        out_specs=[pl.BlockSpec((h, 1, 2 * bm, LANES), lambda i: (0, _q_slot(i, nblk), 0, 0)),
                   pl.BlockSpec((h, bm, LANES), lambda i: (0, i, 0)),
                   pl.BlockSpec((h, bm, DV_C + LANES), lambda i: (0, i, 0))],
        out_shape=[jax.ShapeDtypeStruct((h, nblk, 2 * bm, LANES), BF16),
                   jax.ShapeDtypeStruct((h, t, LANES), BF16),
                   jax.ShapeDtypeStruct((h, t, DV_C + LANES), BF16)],
        compiler_params=_params("parallel"),
        name="inproj_c",
    )(x, gain2, shift2, w, hg, cos, sin, _segment_mean_matrix())


def _merge_gqa_heads(o, bq):
    low = _lane((bq, LANES)) < HEAD_DIM
    pairs = [jnp.where(low, o[g * bq:(g + 1) * bq], pltpu.roll(o[(g + 1) * bq:(g + 2) * bq], HEAD_DIM, 1))
             for g in range(0, GROUP, 2)]
    return jnp.concatenate(pairs, axis=1)


def _flash_kernel(*refs, mode, online, n_keys, bk, dv, l0):
    refs = list(refs)
    q_ref, k_ref, v_ref = refs[:3]
    pos = 3
    if online:
        m0_ref = refs[pos]
        pos += 1
    if mode == "diff":
        lam_ref, sub_ref = refs[pos:pos + 2]
        pos += 2
    o_ref = refs[pos]
    acc_sc = refs[pos + 1]
    m_sc = refs[pos + 2] if online else None

    nsub, rb = q_ref.shape[1], q_ref.shape[2]
    q = q_ref[0].reshape(nsub * rb, LANES)
    acc_sc[...] = jnp.where(_lane(acc_sc.shape) == dv, l0, 0.0).astype(F32)
    if online:
        m_sc[...] = m0_ref[0]

    def block(start, size):
        kb = k_ref[0, pl.ds(start, size), :]
        vb = v_ref[0, pl.ds(start, size), :]
        s = lax.dot_general(q, kb, (((1,), (1,)), ((), ())), preferred_element_type=F32)
        if online:
            m_prev = m_sc[...]
            m_new = jnp.maximum(m_prev, jnp.max(s, axis=-1, keepdims=True))
            p = jnp.exp2(s - m_new).astype(BF16)
            acc_sc[...] = jnp.exp2(m_prev - m_new) * acc_sc[...] + jnp.dot(p, vb, preferred_element_type=F32)
            m_sc[...] = m_new
        else:
            acc_sc[...] += jnp.dot(jnp.exp2(s).astype(BF16), vb, preferred_element_type=F32)

    n_full, tail = n_keys // bk, n_keys % bk
    if n_full:
        def body(i, carry):
            block(pl.multiple_of(i * bk, bk), bk)
            return carry
        lax.fori_loop(0, n_full, body, 0)
    if tail:
        block(n_full * bk, tail)

    acc = acc_sc[...]
    o = acc[:, :LANES] / acc[:, dv:dv + 1]
    if mode == "gqa":
        o_ref[...] = _merge_gqa_heads(o, rb // GROUP).astype(o_ref.dtype)
    else:
        bq = rb // 2
        for b in range(nsub):
            d = o[b * rb:b * rb + bq] - lam_ref[...] * o[b * rb + bq:(b + 1) * rb]
            y = d * lax.rsqrt(jnp.mean(d * d, axis=-1, keepdims=True) + EPS) * sub_ref[...]
            o_ref[b * bq:(b + 1) * bq, :] = y.astype(o_ref.dtype)


def _flash_call(q, k, v, extra, *, mode, online, n_keys, slot0, nsub, n_steps, l0=0.0):
    hkv, _, rb, _ = q.shape
    hk, dva = k.shape[0], v.shape[2]
    dv = HEAD_DIM if mode == "gqa" else DV_C
    bq = rb // (GROUP if mode == "gqa" else 2)
    ocols = GROUP * HEAD_DIM if mode == "gqa" else DV_C
    r = nsub * rb
    bk = min(KEY_BLOCK, n_keys)
    in_specs = [pl.BlockSpec((1, nsub, rb, LANES), lambda h, i: (h, slot0 // nsub + i, 0, 0)),
                pl.BlockSpec((1, n_keys, LANES), (lambda h, i: (h, 0, 0)) if hk > 1 else (lambda h, i: (0, 0, 0))),
                pl.BlockSpec((1, n_keys, dva), lambda h, i: (h, 0, 0))]
    args = [q, k, v]
    scratch = [pltpu.VMEM((r, dva), F32)]
    if online:
        m0 = extra.pop(0)
        in_specs.append(pl.BlockSpec((1, r, 1), lambda h, i: (h, 0, 0)))
        args.append(m0)
        scratch.append(pltpu.VMEM((r, 1), F32))
    for a in extra:
        in_specs.append(pl.BlockSpec(a.shape, lambda h, i: (0, 0)))
        args.append(a)
    return pl.pallas_call(
        functools.partial(_flash_kernel, mode=mode, online=online, n_keys=n_keys, bk=bk, dv=dv, l0=l0),
        grid=(hkv, n_steps),
        in_specs=in_specs,
        out_specs=pl.BlockSpec((nsub * bq, ocols), lambda h, i: (i, h)),
        out_shape=jax.ShapeDtypeStruct((n_steps * nsub * bq, hkv * ocols), BF16),
        scratch_shapes=scratch,
        compiler_params=_params("parallel", "parallel"),
        name="flash_online" if online else "flash_bounded",
    )(*args)


def _logits_bounded(q, k):
    qn = jnp.max(jnp.sum(jnp.square(q.astype(F32)), axis=-1))
    kn = jnp.max(jnp.sum(jnp.square(k.astype(F32)), axis=-1))
    return qn * kn <= LOG2_LOGIT_BOUND ** 2


def _attend(q, k, v, extra, bounded, **kw):
    hkv, r = q.shape[0], kw["nsub"] * q.shape[2]
    fast = lambda q_, k_, v_, *e: _flash_call(q_, k_, v_, list(e), online=False, **kw)
    safe = lambda q_, k_, v_, *e: _flash_call(q_, k_, v_, [jnp.full((hkv, r, 1), NEG, F32)] + list(e), online=True, **kw)
    return lax.cond(bounded, fast, safe, q, k, v, *extra)


def _window_kernel(q_ref, k_ref, v_ref, sink_ref, o_ref, *, bq, n_ctx):
    q = q_ref[0, 0]
    r = q.shape[0]
    t = k_ref.shape[1]
    w = bq + 2 * WINDOW
    q0 = pl.program_id(1) * bq
    ws = pl.multiple_of(jnp.clip(n_ctx + q0 - WINDOW, 0, t - w), WINDOW)
    kw = k_ref[0, pl.ds(ws, w), :]
    vw = v_ref[0, pl.ds(ws, w), :]
    contract_last = (((1,), (1,)), ((), ()))
    s_loc = lax.dot_general(q, kw, contract_last, preferred_element_type=F32)
    qpos = q0 + (lax.broadcasted_iota(jnp.int32, (r, w), 0) & (bq - 1))
    kpos = ws - n_ctx + lax.broadcasted_iota(jnp.int32, (r, w), 1)
    mask = (kpos >= 0) & (kpos - qpos <= WINDOW) & (qpos - kpos <= WINDOW)
    s_loc = jnp.where(mask, s_loc, NEG)
    s_ctx = lax.dot_general(q, k_ref[0, 0:n_ctx, :], contract_last, preferred_element_type=F32)
    sink = sink_ref[0]
    m = jnp.maximum(sink, jnp.maximum(jnp.max(s_loc, axis=-1, keepdims=True), jnp.max(s_ctx, axis=-1, keepdims=True)))
    p_loc = jnp.exp2(s_loc - m)
    p_ctx = jnp.exp2(s_ctx - m)
    l = jnp.exp2(sink - m) + jnp.sum(p_loc, axis=-1, keepdims=True) + jnp.sum(p_ctx, axis=-1, keepdims=True)
    o = (jnp.dot(p_loc.astype(BF16), vw, preferred_element_type=F32)
         + jnp.dot(p_ctx.astype(BF16), v_ref[0, 0:n_ctx, :], preferred_element_type=F32))
    o_ref[...] = _merge_gqa_heads(o / l, bq).astype(o_ref.dtype)


def _window_attention(q, k, v, sink_rows, n_ctx):
    hkv, slots, r, _ = q.shape
    t = k.shape[1]
    bq = r // GROUP
    nq = slots - n_ctx // bq
    return pl.pallas_call(
        functools.partial(_window_kernel, bq=bq, n_ctx=n_ctx),
        grid=(hkv, nq),
        in_specs=[pl.BlockSpec((1, 1, r, LANES), lambda h, i: (h, i, 0, 0)),
                  pl.BlockSpec((1, t, LANES), lambda h, i: (0, 0, 0)),
                  pl.BlockSpec((1, t, LANES), lambda h, i: (h, 0, 0)),
                  pl.BlockSpec((1, r, 1), lambda h, i: (h, 0, 0))],
        out_specs=pl.BlockSpec((bq, GROUP * HEAD_DIM), lambda h, i: (i, h)),
        out_shape=jax.ShapeDtypeStruct((nq * bq, hkv * GROUP * HEAD_DIM), BF16),
        compiler_params=_params("parallel", "parallel"),
        name="window_attention",
    )(q, k, v, sink_rows)


def _sink_rows(sink, bq):
    hkv, g = sink.shape
    return jnp.broadcast_to(sink.astype(F32)[:, :, None] * LOG2E, (hkv, g, bq)).reshape(hkv, g * bq, 1)


def _mixer_ab(heads, n_ctx, sink_a, with_ctx):
    qa, qb, ka, kb, va, vb = heads
    t = ka.shape[1]
    bq = ROW_BLOCK
    n_lat = (t - n_ctx) // bq
    sink = _sink_rows(sink_a.reshape(N_KV_A, GROUP), bq)
    bounded = _logits_bounded(qb, kb)
    oa = _window_attention(qa, ka, va, sink, n_ctx)
    ob = _attend(qb, kb, vb, [], bounded, mode="gqa", n_keys=t, slot0=0, nsub=1, n_steps=n_lat)
    if not with_ctx:
        return (oa, ob), None
    oca = _flash_call(qa, ka, va, [sink], mode="gqa", online=True, n_keys=n_ctx, slot0=n_lat, nsub=1, n_steps=1, l0=1.0)
    ocb = _attend(qb, kb, vb, [], bounded, mode="gqa", n_keys=n_ctx, slot0=n_lat, nsub=1, n_steps=1)
    return (oa, ob), (oca, ocb)


def _mixer_c(heads, n_ctx, lam_p, subln, lam_init, with_ctx):
    q, k, v = heads
    t = k.shape[1]
    n_lat = (t - n_ctx) // ROW_BLOCK
    lp = lam_p.astype(F32)
    lam = jnp.exp(jnp.sum(lp[0] * lp[1])) - jnp.exp(jnp.sum(lp[2] * lp[3])) + lam_init
    extra = [jnp.full((1, DV_C), lam, F32), (subln.astype(F32) * (1 - lam_init)).reshape(1, DV_C)]
    bounded = _logits_bounded(q, k)
    nsub = 2 if n_lat % 2 == 0 else 1
    o_lat = _attend(q, k, v, extra, bounded, mode="diff", n_keys=t, slot0=0, nsub=nsub, n_steps=n_lat // nsub)
    if not with_ctx:
        return (o_lat,), None
    o_ctx = _attend(q, k, v, extra, bounded, mode="diff", n_keys=n_ctx, slot0=n_lat, nsub=1, n_steps=1)
    return (o_lat,), (o_ctx,)


def _out_proj_kernel(*refs, n_parts, has_ctx):
    lat = refs[:n_parts]
    ctx = refs[n_parts:2 * n_parts] if has_ctx else None
    w_ref, x_ref, gate_ref, y_ref = refs[-4:]
    is_ctx = pl.program_id(0) == 0 if has_ctx else False
    acc = None
    col = 0
    for p in range(n_parts):
        o = lat[p][...]
        if has_ctx:
            o = jnp.where(is_ctx, ctx[p][...], o)
        width = o.shape[1]
        part = jnp.dot(o, w_ref[col:col + width, :], preferred_element_type=F32)
        acc = part if acc is None else acc + part
        col += width
    gate = gate_ref[0] if not has_ctx else gate_ref[jnp.where(is_ctx, 0, 1)]
    y_ref[...] = x_ref[...] + gate * acc


def _out_proj(o_lat, o_ctx, w, x, gate2):
    t, d = x.shape
    bm = ROW_BLOCK
    has_ctx = o_ctx is not None
    n_parts = len(o_lat)
    lat_map = (lambda i: (jnp.maximum(i - 1, 0), 0)) if has_ctx else (lambda i: (i, 0))
    in_specs = [pl.BlockSpec((bm, o.shape[1]), lat_map) for o in o_lat]
    args = list(o_lat)
    if has_ctx:
        in_specs += [pl.BlockSpec((bm, o.shape[1]), lambda i: (0, 0)) for o in o_ctx]
        args += list(o_ctx)
        gate = gate2
    else:
        gate = gate2[1:2]
    in_specs += [pl.BlockSpec(w.shape, lambda i: (0, 0)),
                 pl.BlockSpec((bm, d), lambda i: (i, 0)),
                 pl.BlockSpec(gate.shape, lambda i: (0, 0, 0))]
    return pl.pallas_call(
        functools.partial(_out_proj_kernel, n_parts=n_parts, has_ctx=has_ctx),
        grid=(t // bm,),
        in_specs=in_specs,
        out_specs=pl.BlockSpec((bm, d), lambda i: (i, 0)),
        out_shape=jax.ShapeDtypeStruct((t, d), F32),
        compiler_params=_params("parallel"),
        name="out_proj",
    )(*args, w, x, gate)


def _mod_router_kernel(x_ref, g_ref, s_ref, wr_ref, tok_ref, logit_ref, *, row0):
    y = x_ref[...]
    y = y * lax.rsqrt(jnp.mean(y * y, axis=-1, keepdims=True) + EPS)
    row = jnp.minimum(pl.program_id(0) + row0, 1)
    h = y * g_ref[row] + s_ref[row]
    _store_pieces(tok_ref, h)
    logit_ref[...] = jnp.dot(h, wr_ref[...], preferred_element_type=F32, precision=lax.Precision.HIGHEST)


def _store_pieces(ref, rows):
    dp = ref.shape[2]
    for j in range(ref.shape[0]):
        ref[j] = rows[:, j * dp:(j + 1) * dp]


def _load_pieces(planes):
    return jnp.concatenate([planes[j] for j in range(planes.shape[0])], axis=1)


def _mod_router(x, gain2, shift2, w_router, has_ctx):
    t, d = x.shape
    bm = ROW_BLOCK
    dp = d // SC_ROW_SPLIT
    return pl.pallas_call(
        functools.partial(_mod_router_kernel, row0=0 if has_ctx else 1),
        grid=(t // bm,),
        in_specs=[pl.BlockSpec((bm, d), lambda i: (i, 0)),
                  pl.BlockSpec((2, 1, d), lambda i: (0, 0, 0)),
                  pl.BlockSpec((2, 1, d), lambda i: (0, 0, 0)),
                  pl.BlockSpec((d, LANES), lambda i: (0, 0))],
        out_specs=[pl.BlockSpec((SC_ROW_SPLIT, bm, dp), lambda i: (0, i, 0)),
                   pl.BlockSpec((bm, LANES), lambda i: (i, 0))],
        out_shape=[jax.ShapeDtypeStruct((SC_ROW_SPLIT, t, dp), F32), jax.ShapeDtypeStruct((t, LANES), F32)],
        compiler_params=_params("parallel"),
        name="mod_router",
    )(x, gain2, shift2, w_router)


def _moe_kernel(be_ref, nu_ref, x_ref, w1_ref, w3_ref, w2_ref, y_ref):
    i = pl.program_id(0)

    @pl.when(i < nu_ref[0])
    def _():
        x = _load_pieces(x_ref[...]).astype(BF16)
        a = jnp.dot(x, w1_ref[0, 0].astype(BF16), preferred_element_type=F32)
        b = jnp.dot(x, w3_ref[0, 0].astype(BF16), preferred_element_type=F32)
        hidden = (a * jax.nn.sigmoid(a)) * b
        _store_pieces(y_ref, jnp.dot(hidden.astype(BF16), w2_ref[0, 0].astype(BF16), preferred_element_type=F32))

    @pl.when(i >= nu_ref[0])
    def _():
        y_ref[...] = jnp.zeros(y_ref.shape, y_ref.dtype)


def _moe_experts(buf, blk_e, n_used, w1, w3, w2, layer):
    pieces, n_rows, dp = buf.shape
    d, de = w1.shape[2], w1.shape[3]
    n_blk = n_rows // EXPERT_BLOCK
    row_spec = pl.BlockSpec((pieces, EXPERT_BLOCK, dp), lambda i, be, nu: (0, i, 0))
    grid_spec = pltpu.PrefetchScalarGridSpec(
        num_scalar_prefetch=2,
        grid=(n_blk,),
        in_specs=[row_spec,
                  pl.BlockSpec((1, 1, d, de), lambda i, be, nu: (layer, be[i], 0, 0)),
                  pl.BlockSpec((1, 1, d, de), lambda i, be, nu: (layer, be[i], 0, 0)),
                  pl.BlockSpec((1, 1, de, d), lambda i, be, nu: (layer, be[i], 0, 0))],
        out_specs=row_spec,
    )
    return pl.pallas_call(
        _moe_kernel,
        grid_spec=grid_spec,
        out_shape=jax.ShapeDtypeStruct((pieces, n_rows, dp), F32),
        compiler_params=_params("arbitrary"),
        name="moe_experts",
    )(blk_e, n_used, buf, w1, w3, w2)


def _top1(p):
    m = jnp.max(p, axis=-1, keepdims=True)
    idx = lax.broadcasted_iota(jnp.int32, p.shape, 1)
    return m, jnp.min(jnp.where(p == m, idx, p.shape[-1]), axis=-1, keepdims=True)


def _sc_gather(x, idx):
    pieces, t, dp = x.shape
    n = idx.shape[0]
    flat = (jnp.arange(pieces, dtype=jnp.int32)[:, None] * t + idx[None, :]).reshape(1, pieces * n)
    mesh = plsc.VectorSubcoreMesh(core_axis_name="core", subcore_axis_name="subcore")

    @pl.kernel(out_type=jax.ShapeDtypeStruct((pieces * n, dp), x.dtype), mesh=mesh, scratch_types=[])
    def gather(x_hbm, i_hbm, o_hbm):
        def body(i_vmem, o_vmem):
            pltpu.sync_copy(x_hbm.at[i_vmem.at[0]], o_vmem)

        pltpu.emit_pipeline(
            body,
            grid=(pieces * n // SC_GATHER_WINDOW,),
            in_specs=[pl.BlockSpec((1, SC_GATHER_WINDOW), lambda i: (0, i))],
            out_specs=[pl.BlockSpec((SC_GATHER_WINDOW, dp), lambda i: (i, 0))],
            core_axis_name=("core", "subcore"),
            dimension_semantics=(pltpu.PARALLEL,),
        )(i_hbm, o_hbm)

    return gather(x.reshape(pieces * t, dp), flat).reshape(pieces, n, dp)


def _combine_kernel(y_ref, w_ref, x_ref, gate_ref, o_ref, *, row0):
    w = w_ref[...]
    f = w[:, 0:1] * _load_pieces(y_ref[:, 0]) + w[:, 1:2] * _load_pieces(y_ref[:, 1])
    row = jnp.minimum(pl.program_id(0) + row0, 1)
    o_ref[...] = x_ref[...] + gate_ref[row] * f


def _combine(y2, weights, x, gate2, has_ctx):
    t, d = x.shape
    bm = ROW_BLOCK
    pieces, _, _, dp = y2.shape
    return pl.pallas_call(
        functools.partial(_combine_kernel, row0=0 if has_ctx else 1),
        grid=(t // bm,),
        in_specs=[pl.BlockSpec((pieces, TOP_K, bm, dp), lambda i: (0, 0, i, 0)),
                  pl.BlockSpec((bm, TOP_K), lambda i: (i, 0)),
                  pl.BlockSpec((bm, d), lambda i: (i, 0)),
                  pl.BlockSpec((2, 1, d), lambda i: (0, 0, 0))],
        out_specs=pl.BlockSpec((bm, d), lambda i: (i, 0)),
        out_shape=jax.ShapeDtypeStruct((t, d), F32),
        compiler_params=_params("parallel"),
        name="moe_combine",
    )(y2, weights, x, gate2)


def _prefix_rank(onehot):
    rows, e = onehot.shape
    blk = EXPERT_BLOCK
    oh = onehot.astype(F32).reshape(rows // blk, blk, e)
    tri = (jnp.arange(blk)[:, None] > jnp.arange(blk)[None, :]).astype(F32)
    within = jnp.einsum("ij,bjk->bik", tri, oh)
    tot = jnp.sum(oh, axis=1)
    before = jnp.cumsum(tot, axis=0) - tot
    return jnp.sum((within + before[:, None, :]) * oh, axis=-1).reshape(rows).astype(jnp.int32)


def _hier_moe(tokens, logits, b_group, b_expert, w1, w3, w2, layer):
    pieces, t, dp = tokens.shape
    pg = jax.nn.softmax(logits[:, :N_GROUPS] + b_group.astype(F32), axis=-1)
    g_prob, g_idx = _top1(pg)
    le = (logits[:, N_GROUPS:N_GROUPS + N_EXPERTS] + b_expert.astype(F32)).reshape(t, N_GROUPS, EXPERTS_PER_GROUP)
    le = jnp.take_along_axis(le, g_idx[:, :, None], axis=1)[:, 0]
    pe = jax.nn.softmax(le, axis=-1)
    p1, i1 = _top1(pe)
    p2, i2 = _top1(jnp.where(lax.broadcasted_iota(jnp.int32, pe.shape, 1) == i1, -1.0, pe))
    e_prob, e_idx = jnp.concatenate([p1, p2], axis=-1), jnp.concatenate([i1, i2], axis=-1)
    weights = g_prob * e_prob / jnp.sum(e_prob, axis=-1, keepdims=True)
    flat_e = (g_idx * EXPERTS_PER_GROUP + e_idx).reshape(-1)
    onehot = (flat_e[:, None] == jnp.arange(N_EXPERTS)[None, :]).astype(jnp.int32)
    rank = _prefix_rank(onehot)
    counts = jnp.sum(onehot, axis=0)
    padded = (counts + EXPERT_BLOCK - 1) // EXPERT_BLOCK * EXPERT_BLOCK
    pend = jnp.cumsum(padded)
    dest = (pend - padded)[flat_e] + rank
    n_rows = (t * TOP_K + EXPERT_BLOCK - 1) // EXPERT_BLOCK * EXPERT_BLOCK + N_EXPERTS * EXPERT_BLOCK
    n_blk = n_rows // EXPERT_BLOCK
    src = (jnp.arange(n_rows, dtype=jnp.int32) % t).at[dest].set(jnp.arange(t * TOP_K, dtype=jnp.int32) // TOP_K)
    buf = _sc_gather(tokens, src)
    starts = jnp.arange(n_blk, dtype=jnp.int32) * EXPERT_BLOCK
    blk_e = jnp.minimum(jnp.sum(pend[None, :] <= starts[:, None], axis=1), N_EXPERTS - 1).astype(jnp.int32)
    n_used = (pend[-1] // EXPERT_BLOCK).astype(jnp.int32).reshape(1)
    yb = _moe_experts(buf, blk_e, n_used, w1, w3, w2, layer)
    dest_k_major = dest.reshape(t, TOP_K).T.reshape(-1)
    return _sc_gather(yb, dest_k_major).reshape(pieces, TOP_K, t, dp), weights


def kernel(x, c, ctx, c_ctx, w_mod, b_mod, norm_mix, norm_ffn, w_in_ab, w_out_ab, qn_a, kn_a, sink_a, qn_b, kn_b, w_in_c, w_out_c, qn_c, kn_c, lam_c, subln_c, w_group, b_group, w_expert, b_expert, w1, w3, w2):
    b, s_lat, d = x.shape
    n_ctx = ctx.shape[1]
    assert b == 1 and n_ctx == ROW_BLOCK and s_lat % (2 * ROW_BLOCK) == 0
    depth = w_mod.shape[0]
    cos, sin = _rope_tables(s_lat, n_ctx)
    mods = _mod_vectors(c, c_ctx, w_mod, b_mod)
    xs = jnp.concatenate([ctx[0], x[0]], axis=0)
    has_ctx = True
    for l in range(depth):
        last = l == depth - 1
        i = l // 2
        sh1, sc1, gt1, sh2, sc2, gt2 = [mods[l, :2, j * d:(j + 1) * d].reshape(2, 1, d) for j in range(6)]
        gain1 = norm_mix[l].astype(F32) * (1 + sc1)
        if l % 2 == 0:
            heads = _inproj_ab(xs, gain1, sh1, w_in_ab[i].astype(BF16), qn_a[i], kn_a[i], qn_b[i], kn_b[i], cos, sin)
            o_lat, o_ctx = _mixer_ab(heads, n_ctx, sink_a[i], not last)
            w_out = w_out_ab[i]
        else:
            lam_init = 0.8 - 0.6 * math.exp(-0.3 * l)
            heads = _inproj_c(xs, gain1, sh1, w_in_c[i].astype(BF16), qn_c[i], kn_c[i], cos, sin)
            o_lat, o_ctx = _mixer_c(heads, n_ctx, lam_c[i], subln_c[i], lam_init, not last)
            w_out = w_out_c[i]
        if last:
            xs, has_ctx = xs[n_ctx:], False
        xs = _out_proj(o_lat, o_ctx, w_out.astype(BF16), xs, gt1)
        gain2 = norm_ffn[l].astype(F32) * (1 + sc2)
        w_router = jnp.zeros((d, LANES), F32).at[:, :N_GROUPS].set(w_group[l]).at[:, N_GROUPS:N_GROUPS + N_EXPERTS].set(w_expert[l])
        tokens, logits = _mod_router(xs, gain2, sh2, w_router, has_ctx)
        y2, weights = _hier_moe(tokens, logits, b_group[l], b_expert[l], w1, w3, w2, l)
        xs = _combine(y2, weights, xs, gt2, has_ctx)
    return xs.reshape(b, s_lat, d)
```

```python
import functools
import math

import jax
import jax.numpy as jnp
from jax import lax
from jax.experimental import pallas as pl
from jax.experimental.pallas import tpu as pltpu
from jax.experimental.pallas import tpu_sc as plsc

F32 = jnp.float32
BF16 = jnp.bfloat16

GRID_W = 64
HEAD_DIM = 64
WINDOW = 128
ROPE_THETA = 10000.0
EPS = 1e-6
NEG = -1e30
N_HEADS_A, N_KV_A = 8, 2
N_HEADS_B, N_KV_B = 8, 2
GROUP = N_HEADS_A // N_KV_A
QA_W, KVA_W = N_HEADS_A * HEAD_DIM, N_KV_A * HEAD_DIM
QB_W, KVB_W = N_HEADS_B * HEAD_DIM, N_KV_B * HEAD_DIM
N_HEADS_C = 8
DV_C = 2 * HEAD_DIM
N_GROUPS, EXPERTS_PER_GROUP, TOP_K = 4, 8, 2
N_EXPERTS = N_GROUPS * EXPERTS_PER_GROUP
LANES = 128
ROW_BLOCK = 256
EXPERT_BLOCK = 256
SC_GATHER_WINDOW = 128
SC_ROW_SPLIT = 4
KEY_BLOCK = 4096
GQA_V_ROWS = 128
DIFF_V_ROWS = 144
VMEM_LIMIT = 48 * 1024 * 1024
LOG2E = math.log2(math.e)
Q_SCALE = HEAD_DIM ** -0.5 * LOG2E
LOG2_LOGIT_BOUND = 60.0


def _params(*sem):
    return pltpu.CompilerParams(dimension_semantics=sem, vmem_limit_bytes=VMEM_LIMIT)


def _lane(shape):
    return lax.broadcasted_iota(jnp.int32, shape, 1)


def _mod_vec_kernel(a_ref, w_ref, b_ref, o_ref):
    a = a_ref[...]
    a = a * jax.nn.sigmoid(a)
    o_ref[0] = jnp.dot(a, w_ref[0], preferred_element_type=F32, precision=lax.Precision.HIGHEST) + b_ref[0]


def _mod_vectors(c, c_ctx, w_mod, b_mod):
    depth, d, n = w_mod.shape
    a = jnp.zeros((8, d), F32).at[0].set(c_ctx).at[1].set(c[0])
    bn = 1024
    return pl.pallas_call(
        _mod_vec_kernel,
        grid=(depth, n // bn),
        in_specs=[pl.BlockSpec((8, d), lambda l, j: (0, 0)),
                  pl.BlockSpec((1, d, bn), lambda l, j: (l, 0, j)),
                  pl.BlockSpec((1, 1, bn), lambda l, j: (l, 0, j))],
        out_specs=pl.BlockSpec((1, 8, bn), lambda l, j: (l, 0, j)),
        out_shape=jax.ShapeDtypeStruct((depth, 8, n), F32),
        compiler_params=_params("parallel", "parallel"),
        name="mod_vectors",
    )(a, w_mod, b_mod.reshape(depth, 1, n))


def _modulated(x, g_ref, s_ref):
    y = x * lax.rsqrt(jnp.mean(x * x, axis=-1, keepdims=True) + EPS)
    row = jnp.minimum(pl.program_id(0), 1)
    return y * g_ref[row] + s_ref[row]


def _head_norm_rope(x, gain, cos, sin, seg_mean):
    x2 = x * x
    hi = x2.astype(BF16)
    lo = (x2 - hi.astype(F32)).astype(BF16)
    ms = jnp.dot(hi, seg_mean, preferred_element_type=F32) + jnp.dot(lo, seg_mean, preferred_element_type=F32)
    y = x * lax.rsqrt(ms + EPS) * gain
    first_half = (_lane(y.shape) & (HEAD_DIM - 1)) < HEAD_DIM // 2
    partner = jnp.where(first_half, pltpu.roll(y, LANES - HEAD_DIM // 2, 1), pltpu.roll(y, HEAD_DIM // 2, 1))
    return y * cos + partner * sin


def _store_value_heads(v_ref, v):
    lane = _lane(v.shape)
    ones_col = jnp.where(lane == HEAD_DIM, 1.0, 0.0)
    v_ref[0] = jnp.where(lane < HEAD_DIM, v, ones_col).astype(BF16)
    v_ref[1] = jnp.where(lane < HEAD_DIM, pltpu.roll(v, HEAD_DIM, 1), ones_col).astype(BF16)


def _ones_row_block(rows, cols):
    return jnp.where(lax.broadcasted_iota(jnp.int32, (rows, cols), 0) == 0, 1.0, 0.0)


def _store_value_heads_t(vt_ref, v):
    vt = v.T
    tail = _ones_row_block(GQA_V_ROWS - HEAD_DIM, v.shape[0])
    vt_ref[0, 0] = jnp.concatenate([vt[:HEAD_DIM], tail], axis=0).astype(BF16)
    vt_ref[1, 0] = jnp.concatenate([vt[HEAD_DIM:], tail], axis=0).astype(BF16)


def _inproj_ab_kernel(x_ref, g_ref, s_ref, w_ref, hg_ref, cos_ref, sin_ref, seg_ref,
                      qa_ref, qb_ref, ka_ref, kb_ref, va_ref, vat_ref, vbt_ref):
    bm = x_ref.shape[0]
    h = _modulated(x_ref[...], g_ref, s_ref).astype(BF16)
    proj = jnp.dot(h, w_ref[...], preferred_element_type=F32)
    cos, sin, seg = cos_ref[...], sin_ref[...], seg_ref[...]
    low = _lane((bm, LANES)) < HEAD_DIM

    def normed(j):
        cols = slice(j * LANES, (j + 1) * LANES)
        return _head_norm_rope(proj[:, cols], hg_ref[:, cols], cos, sin, seg)

    def store_queries(q_ref, tile0):
        for j in range(N_HEADS_A // 2):
            y = normed(tile0 + j)
            swapped = pltpu.roll(y, HEAD_DIM, 1)
            kv, g0 = (2 * j) // GROUP, (2 * j) % GROUP
            if kv == 0:
                even, odd = jnp.where(low, y, 0.0), jnp.where(low, swapped, 0.0)
            else:
                even, odd = jnp.where(low, 0.0, swapped), jnp.where(low, 0.0, y)
            q_ref[kv, 0, g0 * bm:(g0 + 1) * bm, :] = even.astype(BF16)
            q_ref[kv, 0, (g0 + 1) * bm:(g0 + 2) * bm, :] = odd.astype(BF16)

    tq = QA_W // LANES
    store_queries(qa_ref, 0)
    ka_ref[0] = normed(tq).astype(BF16)
    va = proj[:, (tq + 1) * LANES:(tq + 2) * LANES]
    _store_value_heads(va_ref, va)
    _store_value_heads_t(vat_ref, va)
    store_queries(qb_ref, tq + 2)
    kb_ref[0] = normed(2 * tq + 2).astype(BF16)
    _store_value_heads_t(vbt_ref, proj[:, (2 * tq + 3) * LANES:(2 * tq + 4) * LANES])


def _inproj_c_kernel(x_ref, g_ref, s_ref, w_ref, hg_ref, cos_ref, sin_ref, seg_ref, q_ref, k_ref, v_ref):
    bm = x_ref.shape[0]
    h = _modulated(x_ref[...], g_ref, s_ref).astype(BF16)
    proj = jnp.dot(h, w_ref[...], preferred_element_type=F32)
    cos, sin, seg = cos_ref[...], sin_ref[...], seg_ref[...]
    low = _lane((bm, LANES)) < HEAD_DIM
    tail = _ones_row_block(DIFF_V_ROWS - DV_C, bm)

    def normed(j):
        cols = slice(j * LANES, (j + 1) * LANES)
        return _head_norm_rope(proj[:, cols], hg_ref[:, cols], cos, sin, seg)

    for j in range(N_HEADS_C):
        y = normed(j)
        q_ref[j, 0, 0:bm, :] = jnp.where(low, y, 0.0).astype(BF16)
        q_ref[j, 0, bm:2 * bm, :] = jnp.where(low, 0.0, y).astype(BF16)
        k_ref[j] = normed(N_HEADS_C + j).astype(BF16)
        v = proj[:, (2 * N_HEADS_C + j) * LANES:(2 * N_HEADS_C + j + 1) * LANES]
        v_ref[j, 0] = jnp.concatenate([v.T, tail], axis=0).astype(BF16)


def _rope_tables(seq, n_ctx):
    rows_n = seq // GRID_W
    rows = jnp.broadcast_to(jnp.arange(rows_n, dtype=F32)[:, None], (rows_n, GRID_W)).reshape(-1)
    cols = jnp.broadcast_to(jnp.arange(GRID_W, dtype=F32)[None, :], (rows_n, GRID_W)).reshape(-1)
    half = HEAD_DIM // 2
    inv = ROPE_THETA ** (-jnp.arange(0, half, 2, dtype=F32) / half)
    ang = jnp.concatenate([rows[:, None] * inv, cols[:, None] * inv], axis=-1)
    reps = LANES // half
    sign = jnp.tile(jnp.concatenate([-jnp.ones((half,), F32), jnp.ones((half,), F32)]), LANES // HEAD_DIM)
    cos = jnp.pad(jnp.tile(jnp.cos(ang), (1, reps)), ((n_ctx, 0), (0, 0)), constant_values=1.0)
    sin = jnp.pad(jnp.tile(jnp.sin(ang), (1, reps)) * sign, ((n_ctx, 0), (0, 0)))
    return cos, sin


def _segment_mean_matrix():
    idx = jnp.arange(LANES) // HEAD_DIM
    return jnp.where(idx[:, None] == idx[None, :], 1.0 / HEAD_DIM, 0.0).astype(BF16)


def _q_slot(i, nblk):
    return (i + nblk - 1) % nblk


def _inproj_common_specs(t, d, n):
    bm = ROW_BLOCK
    return [pl.BlockSpec((bm, d), lambda i: (i, 0)),
            pl.BlockSpec((2, 1, d), lambda i: (0, 0, 0)),
            pl.BlockSpec((2, 1, d), lambda i: (0, 0, 0)),
            pl.BlockSpec((d, n), lambda i: (0, 0)),
            pl.BlockSpec((1, n), lambda i: (0, 0)),
            pl.BlockSpec((bm, LANES), lambda i: (i, 0)),
            pl.BlockSpec((bm, LANES), lambda i: (i, 0)),
            pl.BlockSpec((LANES, LANES), lambda i: (0, 0))]


def _inproj_ab(x, gain2, shift2, w, qn_a, kn_a, qn_b, kn_b, cos, sin):
    t, d = x.shape
    n = w.shape[1]
    bm, nblk = ROW_BLOCK, t // ROW_BLOCK
    tile = lambda g, reps: jnp.tile(g.astype(F32), reps)
    ones_v = jnp.ones((KVA_W,), F32)
    hg = jnp.concatenate([tile(qn_a, N_HEADS_A) * Q_SCALE, tile(kn_a, N_KV_A), ones_v,
                          tile(qn_b, N_HEADS_B) * Q_SCALE, tile(kn_b, N_KV_B), ones_v]).reshape(1, n)
    q_shape = jax.ShapeDtypeStruct((N_KV_A, nblk, GROUP * bm, LANES), BF16)
    k_shape = jax.ShapeDtypeStruct((1, t, LANES), BF16)
    v_shape = jax.ShapeDtypeStruct((N_KV_A, t, LANES), BF16)
    vt_shape = jax.ShapeDtypeStruct((N_KV_A, nblk, GQA_V_ROWS, bm), BF16)
    q_spec = pl.BlockSpec((N_KV_A, 1, GROUP * bm, LANES), lambda i: (0, _q_slot(i, nblk), 0, 0))
    k_spec = pl.BlockSpec((1, bm, LANES), lambda i: (0, i, 0))
    v_spec = pl.BlockSpec((N_KV_A, bm, LANES), lambda i: (0, i, 0))
    vt_spec = pl.BlockSpec((N_KV_A, 1, GQA_V_ROWS, bm), lambda i: (0, i, 0, 0))
    return pl.pallas_call(
        _inproj_ab_kernel,
        grid=(nblk,),
        in_specs=_inproj_common_specs(t, d, n),
        out_specs=[q_spec, q_spec, k_spec, k_spec, v_spec, vt_spec, vt_spec],
        out_shape=[q_shape, q_shape, k_shape, k_shape, v_shape, vt_shape, vt_shape],
        compiler_params=_params("parallel"),
        name="inproj_ab",
    )(x, gain2, shift2, w, hg, cos, sin, _segment_mean_matrix())


def _inproj_c(x, gain2, shift2, w, qn, kn, cos, sin):
    t, d = x.shape
    n = w.shape[1]
    bm, nblk = ROW_BLOCK, t // ROW_BLOCK
    h = N_HEADS_C
    tile = lambda g: jnp.tile(g.astype(F32), 2 * h)
    hg = jnp.concatenate([tile(qn) * Q_SCALE, tile(kn), jnp.ones((h * DV_C,), F32)]).reshape(1, n)
    return pl.pallas_call(
        _inproj_c_kernel,
        grid=(nblk,),
        in_specs=_inproj_common_specs(t, d, n),
        out_specs=[pl.BlockSpec((h, 1, 2 * bm, LANES), lambda i: (0, _q_slot(i, nblk), 0, 0)),
                   pl.BlockSpec((h, bm, LANES), lambda i: (0, i, 0)),
                   pl.BlockSpec((h, 1, DIFF_V_ROWS, bm), lambda i: (0, i, 0, 0))],
        out_shape=[jax.ShapeDtypeStruct((h, nblk, 2 * bm, LANES), BF16),
                   jax.ShapeDtypeStruct((h, t, LANES), BF16),
                   jax.ShapeDtypeStruct((h, nblk, DIFF_V_ROWS, bm), BF16)],
        compiler_params=_params("parallel"),
        name="inproj_c",
    )(x, gain2, shift2, w, hg, cos, sin, _segment_mean_matrix())


def _merge_gqa_heads(o, bq):
    low = _lane((bq, LANES)) < HEAD_DIM
    pairs = [jnp.where(low, o[g * bq:(g + 1) * bq], pltpu.roll(o[(g + 1) * bq:(g + 2) * bq], HEAD_DIM, 1))
             for g in range(0, GROUP, 2)]
    return jnp.concatenate(pairs, axis=1)


def _flash_kernel(*refs, mode, online, n_keys, bk, dv, l0):
    refs = list(refs)
    q_ref, k_ref, vt_ref = refs[:3]
    pos = 3
    if online:
        m0_ref = refs[pos]
        pos += 1
    if mode == "diff":
        lam_ref, sub_ref = refs[pos:pos + 2]
        pos += 2
    o_ref = refs[pos]
    acc_sc = refs[pos + 1]
    m_sc = refs[pos + 2] if online else None

    nsub, rb = q_ref.shape[1], q_ref.shape[2]
    r = nsub * rb
    ch = vt_ref.shape[3]
    q = q_ref[0].reshape(r, LANES)
    acc_sc[...] = jnp.where(lax.broadcasted_iota(jnp.int32, acc_sc.shape, 0) == dv, l0, 0.0).astype(F32)
    if online:
        m_sc[...] = m0_ref[0]

    def block(start, size):
        kb = k_ref[0, pl.ds(start, size), :]
        st = lax.dot_general(kb, q, (((1,), (1,)), ((), ())), preferred_element_type=F32)
        if online:
            m_prev = m_sc[...]
            m_new = jnp.maximum(m_prev, jnp.max(st, axis=0, keepdims=True))
            pt = jnp.exp2(st - m_new).astype(BF16)
            acc = jnp.exp2(m_prev - m_new) * acc_sc[...]
            m_sc[...] = m_new
        else:
            pt = jnp.exp2(st).astype(BF16)
            acc = acc_sc[...]
        c0 = start // ch
        for c in range(size // ch):
            acc = acc + jnp.dot(vt_ref[0, c0 + c], pt[c * ch:(c + 1) * ch], preferred_element_type=F32)
        acc_sc[...] = acc

    n_full, tail = n_keys // bk, n_keys % bk
    if n_full:
        def body(i, carry):
            block(pl.multiple_of(i * bk, bk), bk)
            return carry
        lax.fori_loop(0, n_full, body, 0)
    if tail:
        block(n_full * bk, tail)

    acc = acc_sc[...]
    ot = acc[:LANES] / acc[dv:dv + 1]
    bq = rb // (GROUP if mode == "gqa" else 2)
    o = jnp.concatenate([ot[:, j * bq:(j + 1) * bq].T for j in range(r // bq)], axis=0)
    if mode == "gqa":
        o_ref[...] = _merge_gqa_heads(o, bq).astype(o_ref.dtype)
    else:
        for b in range(nsub):
            d = o[b * rb:b * rb + bq] - lam_ref[...] * o[b * rb + bq:(b + 1) * rb]
            y = d * lax.rsqrt(jnp.mean(d * d, axis=-1, keepdims=True) + EPS) * sub_ref[...]
            o_ref[b * bq:(b + 1) * bq, :] = y.astype(o_ref.dtype)


def _flash_call(q, k, vt, extra, *, mode, online, n_keys, slot0, nsub, n_steps, l0=0.0):
    hkv, _, rb, _ = q.shape
    hk = k.shape[0]
    _, _, dvr, ch = vt.shape
    dv = HEAD_DIM if mode == "gqa" else DV_C
    bq = rb // (GROUP if mode == "gqa" else 2)
    ocols = GROUP * HEAD_DIM if mode == "gqa" else DV_C
    r = nsub * rb
    bk = min(KEY_BLOCK, n_keys)
    in_specs = [pl.BlockSpec((1, nsub, rb, LANES), lambda h, i: (h, slot0 // nsub + i, 0, 0)),
                pl.BlockSpec((1, n_keys, LANES), (lambda h, i: (h, 0, 0)) if hk > 1 else (lambda h, i: (0, 0, 0))),
                pl.BlockSpec((1, n_keys // ch, dvr, ch), lambda h, i: (h, 0, 0, 0))]
    args = [q, k, vt]
    scratch = [pltpu.VMEM((dvr, r), F32)]
    if online:
        m0 = extra.pop(0)
        in_specs.append(pl.BlockSpec((1, 1, r), lambda h, i: (h, 0, 0)))
        args.append(m0)
        scratch.append(pltpu.VMEM((1, r), F32))
    for a in extra:
        in_specs.append(pl.BlockSpec(a.shape, lambda h, i: (0, 0)))
        args.append(a)
    return pl.pallas_call(
        functools.partial(_flash_kernel, mode=mode, online=online, n_keys=n_keys, bk=bk, dv=dv, l0=l0),
        grid=(hkv, n_steps),
        in_specs=in_specs,
        out_specs=pl.BlockSpec((nsub * bq, ocols), lambda h, i: (i, h)),
        out_shape=jax.ShapeDtypeStruct((n_steps * nsub * bq, hkv * ocols), BF16),
        scratch_shapes=scratch,
        compiler_params=_params("parallel", "parallel"),
        name="flash_online" if online else "flash_bounded",
    )(*args)


def _logits_bounded(q, k):
    qn = jnp.max(jnp.sum(jnp.square(q.astype(F32)), axis=-1))
    kn = jnp.max(jnp.sum(jnp.square(k.astype(F32)), axis=-1))
    return qn * kn <= LOG2_LOGIT_BOUND ** 2


def _attend(q, k, v, extra, bounded, **kw):
    hkv, r = q.shape[0], kw["nsub"] * q.shape[2]
    fast = lambda q_, k_, v_, *e: _flash_call(q_, k_, v_, list(e), online=False, **kw)
    safe = lambda q_, k_, v_, *e: _flash_call(q_, k_, v_, [jnp.full((hkv, 1, r), NEG, F32)] + list(e), online=True, **kw)
    return lax.cond(bounded, fast, safe, q, k, v, *extra)


def _window_kernel(q_ref, k_ref, v_ref, sink_ref, o_ref, *, bq, n_ctx):
    q = q_ref[0, 0]
    r = q.shape[0]
    t = k_ref.shape[1]
    w = bq + 2 * WINDOW
    q0 = pl.program_id(1) * bq
    ws = pl.multiple_of(jnp.clip(n_ctx + q0 - WINDOW, 0, t - w), WINDOW)
    kw = k_ref[0, pl.ds(ws, w), :]
    vw = v_ref[0, pl.ds(ws, w), :]
    contract_last = (((1,), (1,)), ((), ()))
    s_loc = lax.dot_general(q, kw, contract_last, preferred_element_type=F32)
    qpos = q0 + (lax.broadcasted_iota(jnp.int32, (r, w), 0) & (bq - 1))
    kpos = ws - n_ctx + lax.broadcasted_iota(jnp.int32, (r, w), 1)
    mask = (kpos >= 0) & (kpos - qpos <= WINDOW) & (qpos - kpos <= WINDOW)
    s_loc = jnp.where(mask, s_loc, NEG)
    s_ctx = lax.dot_general(q, k_ref[0, 0:n_ctx, :], contract_last, preferred_element_type=F32)
    sink = sink_ref[0]
    m = jnp.maximum(sink, jnp.maximum(jnp.max(s_loc, axis=-1, keepdims=True), jnp.max(s_ctx, axis=-1, keepdims=True)))
    p_loc = jnp.exp2(s_loc - m)
    p_ctx = jnp.exp2(s_ctx - m)
    l = jnp.exp2(sink - m) + jnp.sum(p_loc, axis=-1, keepdims=True) + jnp.sum(p_ctx, axis=-1, keepdims=True)
    o = (jnp.dot(p_loc.astype(BF16), vw, preferred_element_type=F32)
         + jnp.dot(p_ctx.astype(BF16), v_ref[0, 0:n_ctx, :], preferred_element_type=F32))
    o_ref[...] = _merge_gqa_heads(o / l, bq).astype(o_ref.dtype)


def _window_attention(q, k, v, sink_rows, n_ctx):
    hkv, slots, r, _ = q.shape
    t = k.shape[1]
    bq = r // GROUP
    nq = slots - n_ctx // bq
    return pl.pallas_call(
        functools.partial(_window_kernel, bq=bq, n_ctx=n_ctx),
        grid=(hkv, nq),
        in_specs=[pl.BlockSpec((1, 1, r, LANES), lambda h, i: (h, i, 0, 0)),
                  pl.BlockSpec((1, t, LANES), lambda h, i: (0, 0, 0)),
                  pl.BlockSpec((1, t, LANES), lambda h, i: (h, 0, 0)),
                  pl.BlockSpec((1, r, 1), lambda h, i: (h, 0, 0))],
        out_specs=pl.BlockSpec((bq, GROUP * HEAD_DIM), lambda h, i: (i, h)),
        out_shape=jax.ShapeDtypeStruct((nq * bq, hkv * GROUP * HEAD_DIM), BF16),
        compiler_params=_params("parallel", "parallel"),
        name="window_attention",
    )(q, k, v, sink_rows)


def _sink_rows(sink, bq):
    hkv, g = sink.shape
    return jnp.broadcast_to(sink.astype(F32)[:, :, None] * LOG2E, (hkv, g, bq)).reshape(hkv, g * bq, 1)


def _mixer_ab(heads, n_ctx, sink_a, with_ctx):
    qa, qb, ka, kb, va, vat, vbt = heads
    t = ka.shape[1]
    bq = ROW_BLOCK
    n_lat = (t - n_ctx) // bq
    sink = _sink_rows(sink_a.reshape(N_KV_A, GROUP), bq)
    bounded = _logits_bounded(qb, kb)
    oa = _window_attention(qa, ka, va, sink, n_ctx)
    ob = _attend(qb, kb, vbt, [], bounded, mode="gqa", n_keys=t, slot0=0, nsub=1, n_steps=n_lat)
    if not with_ctx:
        return (oa, ob), None
    oca = _flash_call(qa, ka, vat, [sink.reshape(N_KV_A, 1, GROUP * bq)], mode="gqa", online=True, n_keys=n_ctx,
                      slot0=n_lat, nsub=1, n_steps=1, l0=1.0)
    ocb = _attend(qb, kb, vbt, [], bounded, mode="gqa", n_keys=n_ctx, slot0=n_lat, nsub=1, n_steps=1)
    return (oa, ob), (oca, ocb)


def _mixer_c(heads, n_ctx, lam_p, subln, lam_init, with_ctx):
    q, k, v = heads
    t = k.shape[1]
    n_lat = (t - n_ctx) // ROW_BLOCK
    lp = lam_p.astype(F32)
    lam = jnp.exp(jnp.sum(lp[0] * lp[1])) - jnp.exp(jnp.sum(lp[2] * lp[3])) + lam_init
    extra = [jnp.full((1, DV_C), lam, F32), (subln.astype(F32) * (1 - lam_init)).reshape(1, DV_C)]
    bounded = _logits_bounded(q, k)
    nsub = 2 if n_lat % 2 == 0 else 1
    o_lat = _attend(q, k, v, extra, bounded, mode="diff", n_keys=t, slot0=0, nsub=nsub, n_steps=n_lat // nsub)
    if not with_ctx:
        return (o_lat,), None
    o_ctx = _attend(q, k, v, extra, bounded, mode="diff", n_keys=n_ctx, slot0=n_lat, nsub=1, n_steps=1)
    return (o_lat,), (o_ctx,)


def _out_proj_kernel(*refs, n_parts, has_ctx):
    lat = refs[:n_parts]
    ctx = refs[n_parts:2 * n_parts] if has_ctx else None
    w_ref, x_ref, gate_ref, y_ref = refs[-4:]
    is_ctx = pl.program_id(0) == 0 if has_ctx else False
    acc = None
    col = 0
    for p in range(n_parts):
        o = lat[p][...]
        if has_ctx:
            o = jnp.where(is_ctx, ctx[p][...], o)
        width = o.shape[1]
        part = jnp.dot(o, w_ref[col:col + width, :], preferred_element_type=F32)
        acc = part if acc is None else acc + part
        col += width
    gate = gate_ref[0] if not has_ctx else gate_ref[jnp.where(is_ctx, 0, 1)]
    y_ref[...] = x_ref[...] + gate * acc


def _out_proj(o_lat, o_ctx, w, x, gate2):
    t, d = x.shape
    bm = ROW_BLOCK
    has_ctx = o_ctx is not None
    n_parts = len(o_lat)
    lat_map = (lambda i: (jnp.maximum(i - 1, 0), 0)) if has_ctx else (lambda i: (i, 0))
    in_specs = [pl.BlockSpec((bm, o.shape[1]), lat_map) for o in o_lat]
    args = list(o_lat)
    if has_ctx:
        in_specs += [pl.BlockSpec((bm, o.shape[1]), lambda i: (0, 0)) for o in o_ctx]
        args += list(o_ctx)
        gate = gate2
    else:
        gate = gate2[1:2]
    in_specs += [pl.BlockSpec(w.shape, lambda i: (0, 0)),
                 pl.BlockSpec((bm, d), lambda i: (i, 0)),
                 pl.BlockSpec(gate.shape, lambda i: (0, 0, 0))]
    return pl.pallas_call(
        functools.partial(_out_proj_kernel, n_parts=n_parts, has_ctx=has_ctx),
        grid=(t // bm,),
        in_specs=in_specs,
        out_specs=pl.BlockSpec((bm, d), lambda i: (i, 0)),
        out_shape=jax.ShapeDtypeStruct((t, d), F32),
        compiler_params=_params("parallel"),
        name="out_proj",
    )(*args, w, x, gate)


def _mod_router_kernel(x_ref, g_ref, s_ref, wr_ref, tok_ref, logit_ref, *, row0):
    y = x_ref[...]
    y = y * lax.rsqrt(jnp.mean(y * y, axis=-1, keepdims=True) + EPS)
    row = jnp.minimum(pl.program_id(0) + row0, 1)
    h = y * g_ref[row] + s_ref[row]
    _store_pieces(tok_ref, h)
    logit_ref[...] = jnp.dot(h, wr_ref[...], preferred_element_type=F32, precision=lax.Precision.HIGHEST)


def _store_pieces(ref, rows):
    dp = ref.shape[2]
    for j in range(ref.shape[0]):
        ref[j] = rows[:, j * dp:(j + 1) * dp]


def _load_pieces(planes):
    return jnp.concatenate([planes[j] for j in range(planes.shape[0])], axis=1)


def _mod_router(x, gain2, shift2, w_router, has_ctx):
    t, d = x.shape
    bm = ROW_BLOCK
    dp = d // SC_ROW_SPLIT
    return pl.pallas_call(
        functools.partial(_mod_router_kernel, row0=0 if has_ctx else 1),
        grid=(t // bm,),
        in_specs=[pl.BlockSpec((bm, d), lambda i: (i, 0)),
                  pl.BlockSpec((2, 1, d), lambda i: (0, 0, 0)),
                  pl.BlockSpec((2, 1, d), lambda i: (0, 0, 0)),
                  pl.BlockSpec((d, LANES), lambda i: (0, 0))],
        out_specs=[pl.BlockSpec((SC_ROW_SPLIT, bm, dp), lambda i: (0, i, 0)),
                   pl.BlockSpec((bm, LANES), lambda i: (i, 0))],
        out_shape=[jax.ShapeDtypeStruct((SC_ROW_SPLIT, t, dp), F32), jax.ShapeDtypeStruct((t, LANES), F32)],
        compiler_params=_params("parallel"),
        name="mod_router",
    )(x, gain2, shift2, w_router)


def _moe_kernel(be_ref, nu_ref, x_ref, w1_ref, w3_ref, w2_ref, y_ref):
    i = pl.program_id(0)

    @pl.when(i < nu_ref[0])
    def _():
        x = _load_pieces(x_ref[...]).astype(BF16)
        a = jnp.dot(x, w1_ref[0, 0].astype(BF16), preferred_element_type=F32)
        b = jnp.dot(x, w3_ref[0, 0].astype(BF16), preferred_element_type=F32)
        hidden = (a * jax.nn.sigmoid(a)) * b
        _store_pieces(y_ref, jnp.dot(hidden.astype(BF16), w2_ref[0, 0].astype(BF16), preferred_element_type=F32))

    @pl.when(i >= nu_ref[0])
    def _():
        y_ref[...] = jnp.zeros(y_ref.shape, y_ref.dtype)


def _moe_experts(buf, blk_e, n_used, w1, w3, w2, layer):
    pieces, n_rows, dp = buf.shape
    d, de = w1.shape[2], w1.shape[3]
    n_blk = n_rows // EXPERT_BLOCK
    row_spec = pl.BlockSpec((pieces, EXPERT_BLOCK, dp), lambda i, be, nu: (0, i, 0))
    grid_spec = pltpu.PrefetchScalarGridSpec(
        num_scalar_prefetch=2,
        grid=(n_blk,),
        in_specs=[row_spec,
                  pl.BlockSpec((1, 1, d, de), lambda i, be, nu: (layer, be[i], 0, 0)),
                  pl.BlockSpec((1, 1, d, de), lambda i, be, nu: (layer, be[i], 0, 0)),
                  pl.BlockSpec((1, 1, de, d), lambda i, be, nu: (layer, be[i], 0, 0))],
        out_specs=row_spec,
    )
    return pl.pallas_call(
        _moe_kernel,
        grid_spec=grid_spec,
        out_shape=jax.ShapeDtypeStruct((pieces, n_rows, dp), F32),
        compiler_params=_params("arbitrary"),
        name="moe_experts",
    )(blk_e, n_used, buf, w1, w3, w2)


def _top1(p):
    m = jnp.max(p, axis=-1, keepdims=True)
    idx = lax.broadcasted_iota(jnp.int32, p.shape, 1)
    return m, jnp.min(jnp.where(p == m, idx, p.shape[-1]), axis=-1, keepdims=True)


def _sc_gather(x, idx):
    pieces, t, dp = x.shape
    n = idx.shape[0]
    flat = (jnp.arange(pieces, dtype=jnp.int32)[:, None] * t + idx[None, :]).reshape(1, pieces * n)
    mesh = plsc.VectorSubcoreMesh(core_axis_name="core", subcore_axis_name="subcore")

    @pl.kernel(out_type=jax.ShapeDtypeStruct((pieces * n, dp), x.dtype), mesh=mesh, scratch_types=[])
    def gather(x_hbm, i_hbm, o_hbm):
        def body(i_vmem, o_vmem):
            pltpu.sync_copy(x_hbm.at[i_vmem.at[0]], o_vmem)

        pltpu.emit_pipeline(
            body,
            grid=(pieces * n // SC_GATHER_WINDOW,),
            in_specs=[pl.BlockSpec((1, SC_GATHER_WINDOW), lambda i: (0, i))],
            out_specs=[pl.BlockSpec((SC_GATHER_WINDOW, dp), lambda i: (i, 0))],
            core_axis_name=("core", "subcore"),
            dimension_semantics=(pltpu.PARALLEL,),
        )(i_hbm, o_hbm)

    return gather(x.reshape(pieces * t, dp), flat).reshape(pieces, n, dp)


def _combine_kernel(y_ref, w_ref, x_ref, gate_ref, o_ref, *, row0):
    w = w_ref[...]
    f = w[:, 0:1] * _load_pieces(y_ref[:, 0]) + w[:, 1:2] * _load_pieces(y_ref[:, 1])
    row = jnp.minimum(pl.program_id(0) + row0, 1)
    o_ref[...] = x_ref[...] + gate_ref[row] * f


def _combine(y2, weights, x, gate2, has_ctx):
    t, d = x.shape
    bm = ROW_BLOCK
    pieces, _, _, dp = y2.shape
    return pl.pallas_call(
        functools.partial(_combine_kernel, row0=0 if has_ctx else 1),
        grid=(t // bm,),
        in_specs=[pl.BlockSpec((pieces, TOP_K, bm, dp), lambda i: (0, 0, i, 0)),
                  pl.BlockSpec((bm, TOP_K), lambda i: (i, 0)),
                  pl.BlockSpec((bm, d), lambda i: (i, 0)),
                  pl.BlockSpec((2, 1, d), lambda i: (0, 0, 0))],
        out_specs=pl.BlockSpec((bm, d), lambda i: (i, 0)),
        out_shape=jax.ShapeDtypeStruct((t, d), F32),
        compiler_params=_params("parallel"),
        name="moe_combine",
    )(y2, weights, x, gate2)


def _prefix_rank(onehot):
    rows, e = onehot.shape
    blk = EXPERT_BLOCK
    oh = onehot.astype(F32).reshape(rows // blk, blk, e)
    tri = (jnp.arange(blk)[:, None] > jnp.arange(blk)[None, :]).astype(F32)
    within = jnp.einsum("ij,bjk->bik", tri, oh)
    tot = jnp.sum(oh, axis=1)
    before = jnp.cumsum(tot, axis=0) - tot
    return jnp.sum((within + before[:, None, :]) * oh, axis=-1).reshape(rows).astype(jnp.int32)


def _hier_moe(tokens, logits, b_group, b_expert, w1, w3, w2, layer):
    pieces, t, dp = tokens.shape
    pg = jax.nn.softmax(logits[:, :N_GROUPS] + b_group.astype(F32), axis=-1)
    g_prob, g_idx = _top1(pg)
    le = (logits[:, N_GROUPS:N_GROUPS + N_EXPERTS] + b_expert.astype(F32)).reshape(t, N_GROUPS, EXPERTS_PER_GROUP)
    le = jnp.take_along_axis(le, g_idx[:, :, None], axis=1)[:, 0]
    pe = jax.nn.softmax(le, axis=-1)
    p1, i1 = _top1(pe)
    p2, i2 = _top1(jnp.where(lax.broadcasted_iota(jnp.int32, pe.shape, 1) == i1, -1.0, pe))
    e_prob, e_idx = jnp.concatenate([p1, p2], axis=-1), jnp.concatenate([i1, i2], axis=-1)
    weights = g_prob * e_prob / jnp.sum(e_prob, axis=-1, keepdims=True)
    flat_e = (g_idx * EXPERTS_PER_GROUP + e_idx).reshape(-1)
    onehot = (flat_e[:, None] == jnp.arange(N_EXPERTS)[None, :]).astype(jnp.int32)
    rank = _prefix_rank(onehot)
    counts = jnp.sum(onehot, axis=0)
    padded = (counts + EXPERT_BLOCK - 1) // EXPERT_BLOCK * EXPERT_BLOCK
    pend = jnp.cumsum(padded)
    dest = (pend - padded)[flat_e] + rank
    n_rows = (t * TOP_K + EXPERT_BLOCK - 1) // EXPERT_BLOCK * EXPERT_BLOCK + N_EXPERTS * EXPERT_BLOCK
    n_blk = n_rows // EXPERT_BLOCK
    src = (jnp.arange(n_rows, dtype=jnp.int32) % t).at[dest].set(jnp.arange(t * TOP_K, dtype=jnp.int32) // TOP_K)
    buf = _sc_gather(tokens, src)
    starts = jnp.arange(n_blk, dtype=jnp.int32) * EXPERT_BLOCK
    blk_e = jnp.minimum(jnp.sum(pend[None, :] <= starts[:, None], axis=1), N_EXPERTS - 1).astype(jnp.int32)
    n_used = (pend[-1] // EXPERT_BLOCK).astype(jnp.int32).reshape(1)
    yb = _moe_experts(buf, blk_e, n_used, w1, w3, w2, layer)
    dest_k_major = dest.reshape(t, TOP_K).T.reshape(-1)
    return _sc_gather(yb, dest_k_major).reshape(pieces, TOP_K, t, dp), weights


def kernel(x, c, ctx, c_ctx, w_mod, b_mod, norm_mix, norm_ffn, w_in_ab, w_out_ab, qn_a, kn_a, sink_a, qn_b, kn_b, w_in_c, w_out_c, qn_c, kn_c, lam_c, subln_c, w_group, b_group, w_expert, b_expert, w1, w3, w2):
    b, s_lat, d = x.shape
    n_ctx = ctx.shape[1]
    assert b == 1 and n_ctx == ROW_BLOCK and s_lat % (2 * ROW_BLOCK) == 0
    depth = w_mod.shape[0]
    cos, sin = _rope_tables(s_lat, n_ctx)
    mods = _mod_vectors(c, c_ctx, w_mod, b_mod)
    xs = jnp.concatenate([ctx[0], x[0]], axis=0)
    has_ctx = True
    for l in range(depth):
        last = l == depth - 1
        i = l // 2
        sh1, sc1, gt1, sh2, sc2, gt2 = [mods[l, :2, j * d:(j + 1) * d].reshape(2, 1, d) for j in range(6)]
        gain1 = norm_mix[l].astype(F32) * (1 + sc1)
        if l % 2 == 0:
            heads = _inproj_ab(xs, gain1, sh1, w_in_ab[i].astype(BF16), qn_a[i], kn_a[i], qn_b[i], kn_b[i], cos, sin)
            o_lat, o_ctx = _mixer_ab(heads, n_ctx, sink_a[i], not last)
            w_out = w_out_ab[i]
        else:
            lam_init = 0.8 - 0.6 * math.exp(-0.3 * l)
            heads = _inproj_c(xs, gain1, sh1, w_in_c[i].astype(BF16), qn_c[i], kn_c[i], cos, sin)
            o_lat, o_ctx = _mixer_c(heads, n_ctx, lam_c[i], subln_c[i], lam_init, not last)
            w_out = w_out_c[i]
        if last:
            xs, has_ctx = xs[n_ctx:], False
        xs = _out_proj(o_lat, o_ctx, w_out.astype(BF16), xs, gt1)
        gain2 = norm_ffn[l].astype(F32) * (1 + sc2)
        w_router = jnp.zeros((d, LANES), F32).at[:, :N_GROUPS].set(w_group[l]).at[:, N_GROUPS:N_GROUPS + N_EXPERTS].set(w_expert[l])
        tokens, logits = _mod_router(xs, gain2, sh2, w_router, has_ctx)
        y2, weights = _hier_moe(tokens, logits, b_group[l], b_expert[l], w1, w3, w2, l)
        xs = _combine(y2, weights, xs, gt2, has_ctx)
    return xs.reshape(b, s_lat, d)
```

```python
import functools
import math

import jax
import jax.numpy as jnp
from jax import lax
from jax.experimental import pallas as pl
from jax.experimental.pallas import tpu as pltpu
from jax.experimental.pallas import tpu_sc as plsc

F32 = jnp.float32
BF16 = jnp.bfloat16

GRID_W = 64
HEAD_DIM = 64
WINDOW = 128
ROPE_THETA = 10000.0
EPS = 1e-6
NEG = -1e30
N_HEADS_A, N_KV_A = 8, 2
N_HEADS_B, N_KV_B = 8, 2
GROUP = N_HEADS_A // N_KV_A
QA_W, KVA_W = N_HEADS_A * HEAD_DIM, N_KV_A * HEAD_DIM
QB_W, KVB_W = N_HEADS_B * HEAD_DIM, N_KV_B * HEAD_DIM
N_HEADS_C = 8
DV_C = 2 * HEAD_DIM
N_GROUPS, EXPERTS_PER_GROUP, TOP_K = 4, 8, 2
N_EXPERTS = N_GROUPS * EXPERTS_PER_GROUP
LANES = 128
ROW_BLOCK = 256
EXPERT_BLOCK = 512
SC_GATHER_WINDOW = 128
SC_ROW_SPLIT = 4
KEY_BLOCK = 4096
GQA_V_ROWS = 128
DIFF_V_ROWS = 144
VMEM_LIMIT = 48 * 1024 * 1024
LOG2E = math.log2(math.e)
Q_SCALE = HEAD_DIM ** -0.5 * LOG2E
LOG2_LOGIT_BOUND = 60.0


def _params(*sem):
    return pltpu.CompilerParams(dimension_semantics=sem, vmem_limit_bytes=VMEM_LIMIT)


def _lane(shape):
    return lax.broadcasted_iota(jnp.int32, shape, 1)


def _mod_vec_kernel(a_ref, w_ref, b_ref, o_ref):
    a = a_ref[...]
    a = a * jax.nn.sigmoid(a)
    o_ref[0] = jnp.dot(a, w_ref[0], preferred_element_type=F32, precision=lax.Precision.HIGHEST) + b_ref[0]


def _mod_vectors(c, c_ctx, w_mod, b_mod):
    depth, d, n = w_mod.shape
    a = jnp.zeros((8, d), F32).at[0].set(c_ctx).at[1].set(c[0])
    bn = 1024
    return pl.pallas_call(
        _mod_vec_kernel,
        grid=(depth, n // bn),
        in_specs=[pl.BlockSpec((8, d), lambda l, j: (0, 0)),
                  pl.BlockSpec((1, d, bn), lambda l, j: (l, 0, j)),
                  pl.BlockSpec((1, 1, bn), lambda l, j: (l, 0, j))],
        out_specs=pl.BlockSpec((1, 8, bn), lambda l, j: (l, 0, j)),
        out_shape=jax.ShapeDtypeStruct((depth, 8, n), F32),
        compiler_params=_params("parallel", "parallel"),
        name="mod_vectors",
    )(a, w_mod, b_mod.reshape(depth, 1, n))


def _modulated(x, g_ref, s_ref):
    y = x * lax.rsqrt(jnp.mean(x * x, axis=-1, keepdims=True) + EPS)
    row = jnp.minimum(pl.program_id(0), 1)
    return y * g_ref[row] + s_ref[row]


def _split_bf16(x):
    top = lax.bitcast_convert_type(lax.bitcast_convert_type(x, jnp.uint32) & jnp.uint32(0xFFFF0000), F32)
    return top.astype(BF16), (x - top).astype(BF16)


def _head_norm_rope(x, gain, cos, sin, seg_mean):
    hi, lo = _split_bf16(x * x)
    ms = jnp.dot(hi, seg_mean, preferred_element_type=F32) + jnp.dot(lo, seg_mean, preferred_element_type=F32)
    y = x * lax.rsqrt(ms + EPS) * gain
    first_half = (_lane(y.shape) & (HEAD_DIM - 1)) < HEAD_DIM // 2
    partner = jnp.where(first_half, pltpu.roll(y, LANES - HEAD_DIM // 2, 1), pltpu.roll(y, HEAD_DIM // 2, 1))
    return y * cos + partner * sin


def _store_value_heads(v_ref, v):
    lane = _lane(v.shape)
    ones_col = jnp.where(lane == HEAD_DIM, 1.0, 0.0)
    v_ref[0] = jnp.where(lane < HEAD_DIM, v, ones_col).astype(BF16)
    v_ref[1] = jnp.where(lane < HEAD_DIM, pltpu.roll(v, HEAD_DIM, 1), ones_col).astype(BF16)


def _ones_row_block(rows, cols):
    return jnp.where(lax.broadcasted_iota(jnp.int32, (rows, cols), 0) == 0, 1.0, 0.0)


def _store_value_heads_t(vt_ref, v):
    vt = v.T
    tail = _ones_row_block(GQA_V_ROWS - HEAD_DIM, v.shape[0])
    vt_ref[0, 0] = jnp.concatenate([vt[:HEAD_DIM], tail], axis=0).astype(BF16)
    vt_ref[1, 0] = jnp.concatenate([vt[HEAD_DIM:], tail], axis=0).astype(BF16)


N_INPROJ_INPUTS = 8


def _inproj_rows(refs, fused):
    if not fused:
        return refs[0][...], refs[1:N_INPROJ_INPUTS], refs[N_INPROJ_INPUTS:]
    y_ref, wts_ref, gate_ref, x_ref = refs[:4]
    ins, outs = refs[4:N_INPROJ_INPUTS + 3], refs[N_INPROJ_INPUTS + 3:]
    wts = wts_ref[...]
    f = wts[:, 0:1] * _load_pieces(y_ref[:, 0]) + wts[:, 1:2] * _load_pieces(y_ref[:, 1])
    x = x_ref[...] + gate_ref[jnp.minimum(pl.program_id(0), 1)] * f
    outs[0][...] = x
    return x, ins, outs[1:]


def _inproj_ab_kernel(*refs, fused):
    x, (g_ref, s_ref, w_ref, hg_ref, cos_ref, sin_ref, seg_ref), outs = _inproj_rows(refs, fused)
    qa_ref, qb_ref, ka_ref, kb_ref, va_ref, vat_ref, vbt_ref = outs
    bm = x.shape[0]
    h = _modulated(x, g_ref, s_ref).astype(BF16)
    proj = jnp.dot(h, w_ref[...], preferred_element_type=F32)
    cos, sin, seg = cos_ref[...], sin_ref[...], seg_ref[...]
    low = _lane((bm, LANES)) < HEAD_DIM

    def normed(j):
        cols = slice(j * LANES, (j + 1) * LANES)
        return _head_norm_rope(proj[:, cols], hg_ref[:, cols], cos, sin, seg)

    def store_queries(q_ref, tile0):
        for j in range(N_HEADS_A // 2):
            y = normed(tile0 + j)
            swapped = pltpu.roll(y, HEAD_DIM, 1)
            kv, g0 = (2 * j) // GROUP, (2 * j) % GROUP
            if kv == 0:
                even, odd = jnp.where(low, y, 0.0), jnp.where(low, swapped, 0.0)
            else:
                even, odd = jnp.where(low, 0.0, swapped), jnp.where(low, 0.0, y)
            q_ref[kv, 0, g0 * bm:(g0 + 1) * bm, :] = even.astype(BF16)
            q_ref[kv, 0, (g0 + 1) * bm:(g0 + 2) * bm, :] = odd.astype(BF16)

    tq = QA_W // LANES
    store_queries(qa_ref, 0)
    ka_ref[0] = normed(tq).astype(BF16)
    va = proj[:, (tq + 1) * LANES:(tq + 2) * LANES]
    _store_value_heads(va_ref, va)
    _store_value_heads_t(vat_ref, va)
    store_queries(qb_ref, tq + 2)
    kb_ref[0] = normed(2 * tq + 2).astype(BF16)
    _store_value_heads_t(vbt_ref, proj[:, (2 * tq + 3) * LANES:(2 * tq + 4) * LANES])


def _inproj_c_kernel(*refs, fused):
    x, (g_ref, s_ref, w_ref, hg_ref, cos_ref, sin_ref, seg_ref), (q_ref, k_ref, v_ref) = _inproj_rows(refs, fused)
    bm = x.shape[0]
    h = _modulated(x, g_ref, s_ref).astype(BF16)
    proj = jnp.dot(h, w_ref[...], preferred_element_type=F32)
    cos, sin, seg = cos_ref[...], sin_ref[...], seg_ref[...]
    low = _lane((bm, LANES)) < HEAD_DIM
    tail = _ones_row_block(DIFF_V_ROWS - DV_C, bm)

    def normed(j):
        cols = slice(j * LANES, (j + 1) * LANES)
        return _head_norm_rope(proj[:, cols], hg_ref[:, cols], cos, sin, seg)

    for j in range(N_HEADS_C):
        y = normed(j)
        q_ref[j, 0, 0:bm, :] = jnp.where(low, y, 0.0).astype(BF16)
        q_ref[j, 0, bm:2 * bm, :] = jnp.where(low, 0.0, y).astype(BF16)
        k_ref[j] = normed(N_HEADS_C + j).astype(BF16)
        v = proj[:, (2 * N_HEADS_C + j) * LANES:(2 * N_HEADS_C + j + 1) * LANES]
        v_ref[j, 0] = jnp.concatenate([v.T, tail], axis=0).astype(BF16)


def _rope_tables(seq, n_ctx):
    rows_n = seq // GRID_W
    rows = jnp.broadcast_to(jnp.arange(rows_n, dtype=F32)[:, None], (rows_n, GRID_W)).reshape(-1)
    cols = jnp.broadcast_to(jnp.arange(GRID_W, dtype=F32)[None, :], (rows_n, GRID_W)).reshape(-1)
    half = HEAD_DIM // 2
    inv = ROPE_THETA ** (-jnp.arange(0, half, 2, dtype=F32) / half)
    ang = jnp.concatenate([rows[:, None] * inv, cols[:, None] * inv], axis=-1)
    reps = LANES // half
    sign = jnp.tile(jnp.concatenate([-jnp.ones((half,), F32), jnp.ones((half,), F32)]), LANES // HEAD_DIM)
    cos = jnp.pad(jnp.tile(jnp.cos(ang), (1, reps)), ((n_ctx, 0), (0, 0)), constant_values=1.0)
    sin = jnp.pad(jnp.tile(jnp.sin(ang), (1, reps)) * sign, ((n_ctx, 0), (0, 0)))
    return cos, sin


def _segment_mean_matrix():
    idx = jnp.arange(LANES) // HEAD_DIM
    return jnp.where(idx[:, None] == idx[None, :], 1.0 / HEAD_DIM, 0.0).astype(BF16)


def _q_slot(i, nblk):
    return (i + nblk - 1) % nblk


def _inproj_common_specs(t, d, n):
    bm = ROW_BLOCK
    return [pl.BlockSpec((bm, d), lambda i: (i, 0)),
            pl.BlockSpec((2, 1, d), lambda i: (0, 0, 0)),
            pl.BlockSpec((2, 1, d), lambda i: (0, 0, 0)),
            pl.BlockSpec((d, n), lambda i: (0, 0)),
            pl.BlockSpec((1, n), lambda i: (0, 0)),
            pl.BlockSpec((bm, LANES), lambda i: (i, 0)),
            pl.BlockSpec((bm, LANES), lambda i: (i, 0)),
            pl.BlockSpec((LANES, LANES), lambda i: (0, 0))]


def _inproj_call(kernel_fn, name, x, pending, args, n, out_specs, out_shape):
    t, d = x.shape
    bm = ROW_BLOCK
    in_specs = _inproj_common_specs(t, d, n)
    args = [x] + list(args)
    fused = pending is not None
    if fused:
        y2, wts, gate2 = pending
        pieces, _, _, dp = y2.shape
        in_specs = [pl.BlockSpec((pieces, TOP_K, bm, dp), lambda i: (0, 0, i, 0)),
                    pl.BlockSpec((bm, TOP_K), lambda i: (i, 0)),
                    pl.BlockSpec((2, 1, d), lambda i: (0, 0, 0))] + in_specs
        args = [y2, wts, gate2] + args
        out_specs = [pl.BlockSpec((bm, d), lambda i: (i, 0))] + list(out_specs)
        out_shape = [jax.ShapeDtypeStruct((t, d), F32)] + list(out_shape)
    outs = pl.pallas_call(
        functools.partial(kernel_fn, fused=fused),
        grid=(t // bm,),
        in_specs=in_specs,
        out_specs=out_specs,
        out_shape=out_shape,
        compiler_params=_params("parallel"),
        name=name,
    )(*args)
    return (outs[0], outs[1:]) if fused else (x, outs)


def _inproj_ab(x, pending, gain2, shift2, w, qn_a, kn_a, qn_b, kn_b, cos, sin):
    t, d = x.shape
    n = w.shape[1]
    bm, nblk = ROW_BLOCK, t // ROW_BLOCK
    tile = lambda g, reps: jnp.tile(g.astype(F32), reps)
    ones_v = jnp.ones((KVA_W,), F32)
    hg = jnp.concatenate([tile(qn_a, N_HEADS_A) * Q_SCALE, tile(kn_a, N_KV_A), ones_v,
                          tile(qn_b, N_HEADS_B) * Q_SCALE, tile(kn_b, N_KV_B), ones_v]).reshape(1, n)
    q_shape = jax.ShapeDtypeStruct((N_KV_A, nblk, GROUP * bm, LANES), BF16)
    k_shape = jax.ShapeDtypeStruct((1, t, LANES), BF16)
    v_shape = jax.ShapeDtypeStruct((N_KV_A, t, LANES), BF16)
    vt_shape = jax.ShapeDtypeStruct((N_KV_A, nblk, GQA_V_ROWS, bm), BF16)
    q_spec = pl.BlockSpec((N_KV_A, 1, GROUP * bm, LANES), lambda i: (0, _q_slot(i, nblk), 0, 0))
    k_spec = pl.BlockSpec((1, bm, LANES), lambda i: (0, i, 0))
    v_spec = pl.BlockSpec((N_KV_A, bm, LANES), lambda i: (0, i, 0))
    vt_spec = pl.BlockSpec((N_KV_A, 1, GQA_V_ROWS, bm), lambda i: (0, i, 0, 0))
    return _inproj_call(_inproj_ab_kernel, "inproj_ab", x, pending,
                        [gain2, shift2, w, hg, cos, sin, _segment_mean_matrix()], n,
                        [q_spec, q_spec, k_spec, k_spec, v_spec, vt_spec, vt_spec],
                        [q_shape, q_shape, k_shape, k_shape, v_shape, vt_shape, vt_shape])


def _inproj_c(x, pending, gain2, shift2, w, qn, kn, cos, sin):
    t, d = x.shape
    n = w.shape[1]
    bm, nblk = ROW_BLOCK, t // ROW_BLOCK
    h = N_HEADS_C
    tile = lambda g: jnp.tile(g.astype(F32), 2 * h)
    hg = jnp.concatenate([tile(qn) * Q_SCALE, tile(kn), jnp.ones((h * DV_C,), F32)]).reshape(1, n)
    return _inproj_call(_inproj_c_kernel, "inproj_c", x, pending,
                        [gain2, shift2, w, hg, cos, sin, _segment_mean_matrix()], n,
                        [pl.BlockSpec((h, 1, 2 * bm, LANES), lambda i: (0, _q_slot(i, nblk), 0, 0)),
                         pl.BlockSpec((h, bm, LANES), lambda i: (0, i, 0)),
                         pl.BlockSpec((h, 1, DIFF_V_ROWS, bm), lambda i: (0, i, 0, 0))],
                        [jax.ShapeDtypeStruct((h, nblk, 2 * bm, LANES), BF16),
                         jax.ShapeDtypeStruct((h, t, LANES), BF16),
                         jax.ShapeDtypeStruct((h, nblk, DIFF_V_ROWS, bm), BF16)])


def _merge_gqa_heads(o, bq):
    low = _lane((bq, LANES)) < HEAD_DIM
    pairs = [jnp.where(low, o[g * bq:(g + 1) * bq], pltpu.roll(o[(g + 1) * bq:(g + 2) * bq], HEAD_DIM, 1))
             for g in range(0, GROUP, 2)]
    return jnp.concatenate(pairs, axis=1)


def _flash_kernel(*refs, mode, online, n_keys, bk, dv, l0):
    refs = list(refs)
    q_ref, k_ref, vt_ref = refs[:3]
    pos = 3
    if online:
        m0_ref = refs[pos]
        pos += 1
    if mode == "diff":
        lam_ref, sub_ref = refs[pos:pos + 2]
        pos += 2
    o_ref = refs[pos]
    acc_sc = refs[pos + 1]
    m_sc = refs[pos + 2] if online else None

    nsub, rb = q_ref.shape[1], q_ref.shape[2]
    r = nsub * rb
    ch = vt_ref.shape[3]
    q = q_ref[0].reshape(r, LANES)
    acc_sc[...] = jnp.where(lax.broadcasted_iota(jnp.int32, acc_sc.shape, 0) == dv, l0, 0.0).astype(F32)
    if online:
        m_sc[...] = m0_ref[0]

    def block(start, size):
        kb = k_ref[0, pl.ds(start, size), :]
        st = lax.dot_general(kb, q, (((1,), (1,)), ((), ())), preferred_element_type=F32)
        if online:
            m_prev = m_sc[...]
            m_new = jnp.maximum(m_prev, jnp.max(st, axis=0, keepdims=True))
            pt = jnp.exp2(st - m_new).astype(BF16)
            acc = jnp.exp2(m_prev - m_new) * acc_sc[...]
            m_sc[...] = m_new
        else:
            pt = jnp.exp2(st).astype(BF16)
            acc = acc_sc[...]
        c0 = start // ch
        for c in range(size // ch):
            acc = acc + jnp.dot(vt_ref[0, c0 + c], pt[c * ch:(c + 1) * ch], preferred_element_type=F32)
        acc_sc[...] = acc

    n_full, tail = n_keys // bk, n_keys % bk
    if n_full:
        def body(i, carry):
            block(pl.multiple_of(i * bk, bk), bk)
            return carry
        lax.fori_loop(0, n_full, body, 0)
    if tail:
        block(n_full * bk, tail)

    acc = acc_sc[...]
    ot = acc[:LANES] / acc[dv:dv + 1]
    bq = rb // (GROUP if mode == "gqa" else 2)
    o = jnp.concatenate([ot[:, j * bq:(j + 1) * bq].T for j in range(r // bq)], axis=0)
    if mode == "gqa":
        o_ref[...] = _merge_gqa_heads(o, bq).astype(o_ref.dtype)
    else:
        for b in range(nsub):
            d = o[b * rb:b * rb + bq] - lam_ref[...] * o[b * rb + bq:(b + 1) * rb]
            y = d * lax.rsqrt(jnp.mean(d * d, axis=-1, keepdims=True) + EPS) * sub_ref[...]
            o_ref[b * bq:(b + 1) * bq, :] = y.astype(o_ref.dtype)


def _flash_call(q, k, vt, extra, *, mode, online, n_keys, slot0, nsub, n_steps, l0=0.0):
    hkv, _, rb, _ = q.shape
    hk = k.shape[0]
    _, _, dvr, ch = vt.shape
    dv = HEAD_DIM if mode == "gqa" else DV_C
    bq = rb // (GROUP if mode == "gqa" else 2)
    ocols = GROUP * HEAD_DIM if mode == "gqa" else DV_C
    r = nsub * rb
    bk = min(KEY_BLOCK, n_keys)
    in_specs = [pl.BlockSpec((1, nsub, rb, LANES), lambda h, i: (h, slot0 // nsub + i, 0, 0)),
                pl.BlockSpec((1, n_keys, LANES), (lambda h, i: (h, 0, 0)) if hk > 1 else (lambda h, i: (0, 0, 0))),
                pl.BlockSpec((1, n_keys // ch, dvr, ch), lambda h, i: (h, 0, 0, 0))]
    args = [q, k, vt]
    scratch = [pltpu.VMEM((dvr, r), F32)]
    if online:
        m0 = extra.pop(0)
        in_specs.append(pl.BlockSpec((1, 1, r), lambda h, i: (h, 0, 0)))
        args.append(m0)
        scratch.append(pltpu.VMEM((1, r), F32))
    for a in extra:
        in_specs.append(pl.BlockSpec(a.shape, lambda h, i: (0, 0)))
        args.append(a)
    return pl.pallas_call(
        functools.partial(_flash_kernel, mode=mode, online=online, n_keys=n_keys, bk=bk, dv=dv, l0=l0),
        grid=(hkv, n_steps),
        in_specs=in_specs,
        out_specs=pl.BlockSpec((nsub * bq, ocols), lambda h, i: (i, h)),
        out_shape=jax.ShapeDtypeStruct((n_steps * nsub * bq, hkv * ocols), BF16),
        scratch_shapes=scratch,
        compiler_params=_params("parallel", "parallel"),
        name="flash_online" if online else "flash_bounded",
    )(*args)


def _logits_bounded(q, k):
    qn = jnp.max(jnp.sum(jnp.square(q.astype(F32)), axis=-1))
    kn = jnp.max(jnp.sum(jnp.square(k.astype(F32)), axis=-1))
    return qn * kn <= LOG2_LOGIT_BOUND ** 2


def _attend(q, k, v, extra, bounded, **kw):
    hkv, r = q.shape[0], kw["nsub"] * q.shape[2]
    fast = lambda q_, k_, v_, *e: _flash_call(q_, k_, v_, list(e), online=False, **kw)
    safe = lambda q_, k_, v_, *e: _flash_call(q_, k_, v_, [jnp.full((hkv, 1, r), NEG, F32)] + list(e), online=True, **kw)
    return lax.cond(bounded, fast, safe, q, k, v, *extra)


def _window_kernel(q_ref, k_ref, v_ref, sink_ref, o_ref, *, bq, n_ctx):
    q = q_ref[0, 0]
    r = q.shape[0]
    t = k_ref.shape[1]
    w = bq + 2 * WINDOW
    q0 = pl.program_id(1) * bq
    ws = pl.multiple_of(jnp.clip(n_ctx + q0 - WINDOW, 0, t - w), WINDOW)
    kw = k_ref[0, pl.ds(ws, w), :]
    vw = v_ref[0, pl.ds(ws, w), :]
    contract_last = (((1,), (1,)), ((), ()))
    s_loc = lax.dot_general(q, kw, contract_last, preferred_element_type=F32)
    qpos = q0 + (lax.broadcasted_iota(jnp.int32, (r, w), 0) & (bq - 1))
    kpos = ws - n_ctx + lax.broadcasted_iota(jnp.int32, (r, w), 1)
    mask = (kpos >= 0) & (kpos - qpos <= WINDOW) & (qpos - kpos <= WINDOW)
    s_loc = jnp.where(mask, s_loc, NEG)
    s_ctx = lax.dot_general(q, k_ref[0, 0:n_ctx, :], contract_last, preferred_element_type=F32)
    sink = sink_ref[0]
    m = jnp.maximum(sink, jnp.maximum(jnp.max(s_loc, axis=-1, keepdims=True), jnp.max(s_ctx, axis=-1, keepdims=True)))
    p_loc = jnp.exp2(s_loc - m)
    p_ctx = jnp.exp2(s_ctx - m)
    l = jnp.exp2(sink - m) + jnp.sum(p_loc, axis=-1, keepdims=True) + jnp.sum(p_ctx, axis=-1, keepdims=True)
    o = (jnp.dot(p_loc.astype(BF16), vw, preferred_element_type=F32)
         + jnp.dot(p_ctx.astype(BF16), v_ref[0, 0:n_ctx, :], preferred_element_type=F32))
    o_ref[...] = _merge_gqa_heads(o / l, bq).astype(o_ref.dtype)


def _window_attention(q, k, v, sink_rows, n_ctx):
    hkv, slots, r, _ = q.shape
    t = k.shape[1]
    bq = r // GROUP
    nq = slots - n_ctx // bq
    return pl.pallas_call(
        functools.partial(_window_kernel, bq=bq, n_ctx=n_ctx),
        grid=(hkv, nq),
        in_specs=[pl.BlockSpec((1, 1, r, LANES), lambda h, i: (h, i, 0, 0)),
                  pl.BlockSpec((1, t, LANES), lambda h, i: (0, 0, 0)),
                  pl.BlockSpec((1, t, LANES), lambda h, i: (h, 0, 0)),
                  pl.BlockSpec((1, r, 1), lambda h, i: (h, 0, 0))],
        out_specs=pl.BlockSpec((bq, GROUP * HEAD_DIM), lambda h, i: (i, h)),
        out_shape=jax.ShapeDtypeStruct((nq * bq, hkv * GROUP * HEAD_DIM), BF16),
        compiler_params=_params("parallel", "parallel"),
        name="window_attention",
    )(q, k, v, sink_rows)


def _window_bounded_kernel(q_ref, k_ref, vt_ref, mask_ref, sink_ref, o_ref, *, bq):
    q = q_ref[0, 0]
    ch = vt_ref.shape[3]
    cw = jnp.clip(pl.program_id(1), 0, vt_ref.shape[1] - 3)
    contract_last = (((1,), (1,)), ((), ()))
    k_win = k_ref[0, pl.ds(pl.multiple_of(cw * ch, ch), 3 * ch), :]
    st_win = lax.dot_general(k_win, q, contract_last, preferred_element_type=F32)
    st_ctx = lax.dot_general(k_ref[0, 0:ch, :], q, contract_last, preferred_element_type=F32)
    pt_win = jnp.exp2(st_win).astype(BF16) * mask_ref[0]
    acc = jnp.dot(vt_ref[0, 0], jnp.exp2(st_ctx).astype(BF16), preferred_element_type=F32)
    for c in range(3):
        acc = acc + jnp.dot(vt_ref[0, cw + c], pt_win[c * ch:(c + 1) * ch], preferred_element_type=F32)
    ot = acc / (acc[HEAD_DIM:HEAD_DIM + 1] + jnp.exp2(sink_ref[0]))
    o = jnp.concatenate([ot[:, g * bq:(g + 1) * bq].T for g in range(GROUP)], axis=0)
    o_ref[...] = _merge_gqa_heads(o, bq).astype(o_ref.dtype)


def _window_masks(bq, r):
    c = jnp.arange(3 * bq)[:, None]
    off = c - (jnp.arange(r)[None, :] & (bq - 1))
    centred = (off >= bq - WINDOW) & (off <= bq + WINDOW)
    shifted = (off >= 2 * bq - WINDOW) & (off <= 2 * bq + WINDOW)
    return jnp.stack([centred & (c >= bq), centred, shifted]).astype(BF16)


def _window_attention_bounded(q, k, vt, sink_cols, n_ctx):
    hkv, slots, r, _ = q.shape
    t = k.shape[1]
    bq = r // GROUP
    nblk = t // bq
    nq = nblk - 1
    assert n_ctx == bq and nblk >= 3
    return pl.pallas_call(
        functools.partial(_window_bounded_kernel, bq=bq),
        grid=(hkv, nq),
        in_specs=[pl.BlockSpec((1, 1, r, LANES), lambda h, i: (h, i, 0, 0)),
                  pl.BlockSpec((1, t, LANES), lambda h, i: (0, 0, 0)),
                  pl.BlockSpec((1, nblk, GQA_V_ROWS, bq), lambda h, i: (h, 0, 0, 0)),
                  pl.BlockSpec((1, 3 * bq, r), lambda h, i: (jnp.where(i == 0, 0, jnp.where(i >= nblk - 2, 2, 1)), 0, 0)),
                  pl.BlockSpec((1, 1, r), lambda h, i: (h, 0, 0))],
        out_specs=pl.BlockSpec((bq, GROUP * HEAD_DIM), lambda h, i: (i, h)),
        out_shape=jax.ShapeDtypeStruct((nq * bq, hkv * GROUP * HEAD_DIM), BF16),
        compiler_params=_params("parallel", "parallel"),
        name="window_bounded",
    )(q, k, vt, _window_masks(bq, r), sink_cols)


def _sink_rows(sink, bq):
    hkv, g = sink.shape
    return jnp.broadcast_to(sink.astype(F32)[:, :, None] * LOG2E, (hkv, g, bq)).reshape(hkv, g * bq, 1)


def _mixer_ab(heads, n_ctx, sink_a, with_ctx):
    qa, qb, ka, kb, va, vat, vbt = heads
    t = ka.shape[1]
    bq = ROW_BLOCK
    n_lat = (t - n_ctx) // bq
    sink = _sink_rows(sink_a.reshape(N_KV_A, GROUP), bq)
    sink_cols = sink.reshape(N_KV_A, 1, GROUP * bq)
    bounded = _logits_bounded(qb, kb)
    bounded_a = _logits_bounded(qa, ka) & (jnp.max(jnp.abs(sink)) <= LOG2_LOGIT_BOUND)
    oa = lax.cond(bounded_a,
                  lambda: _window_attention_bounded(qa, ka, vat, sink_cols, n_ctx),
                  lambda: _window_attention(qa, ka, va, sink, n_ctx))
    ob = _attend(qb, kb, vbt, [], bounded, mode="gqa", n_keys=t, slot0=0, nsub=1, n_steps=n_lat)
    if not with_ctx:
        return (oa, ob), None
    oca = _flash_call(qa, ka, vat, [sink_cols], mode="gqa", online=True, n_keys=n_ctx,
                      slot0=n_lat, nsub=1, n_steps=1, l0=1.0)
    ocb = _attend(qb, kb, vbt, [], bounded, mode="gqa", n_keys=n_ctx, slot0=n_lat, nsub=1, n_steps=1)
    return (oa, ob), (oca, ocb)


def _mixer_c(heads, n_ctx, lam_p, subln, lam_init, with_ctx):
    q, k, v = heads
    t = k.shape[1]
    n_lat = (t - n_ctx) // ROW_BLOCK
    lp = lam_p.astype(F32)
    lam = jnp.exp(jnp.sum(lp[0] * lp[1])) - jnp.exp(jnp.sum(lp[2] * lp[3])) + lam_init
    extra = [jnp.full((1, DV_C), lam, F32), (subln.astype(F32) * (1 - lam_init)).reshape(1, DV_C)]
    bounded = _logits_bounded(q, k)
    nsub = 2 if n_lat % 2 == 0 else 1
    o_lat = _attend(q, k, v, extra, bounded, mode="diff", n_keys=t, slot0=0, nsub=nsub, n_steps=n_lat // nsub)
    if not with_ctx:
        return (o_lat,), None
    o_ctx = _attend(q, k, v, extra, bounded, mode="diff", n_keys=n_ctx, slot0=n_lat, nsub=1, n_steps=1)
    return (o_lat,), (o_ctx,)


def _out_proj_kernel(*refs, n_parts, has_ctx):
    lat = refs[:n_parts]
    ctx = refs[n_parts:2 * n_parts] if has_ctx else None
    w_ref, x_ref, gate_ref, g_ref, s_ref, wr_ref, y_ref, tok_ref, logit_ref = refs[-9:]
    is_ctx = pl.program_id(0) == 0 if has_ctx else False
    row = jnp.where(is_ctx, 0, 1) if has_ctx else 1
    acc = None
    col = 0
    for p in range(n_parts):
        o = lat[p][...]
        if has_ctx:
            o = jnp.where(is_ctx, ctx[p][...], o)
        width = o.shape[1]
        part = jnp.dot(o, w_ref[col:col + width, :], preferred_element_type=F32)
        acc = part if acc is None else acc + part
        col += width
    x1 = x_ref[...] + gate_ref[row] * acc
    y_ref[...] = x1
    h = x1 * lax.rsqrt(jnp.mean(x1 * x1, axis=-1, keepdims=True) + EPS) * g_ref[row] + s_ref[row]
    _store_pieces(tok_ref, h)
    hi, lo = _split_bf16(h)
    logit_ref[...] = (jnp.dot(hi, wr_ref[0], preferred_element_type=F32) + jnp.dot(lo, wr_ref[0], preferred_element_type=F32)
                      + jnp.dot(hi, wr_ref[1], preferred_element_type=F32))


def _out_proj(o_lat, o_ctx, w, x, gate2, gain2, shift2, w_router):
    t, d = x.shape
    bm = ROW_BLOCK
    dp = d // SC_ROW_SPLIT
    has_ctx = o_ctx is not None
    n_parts = len(o_lat)
    lat_map = (lambda i: (jnp.maximum(i - 1, 0), 0)) if has_ctx else (lambda i: (i, 0))
    in_specs = [pl.BlockSpec((bm, o.shape[1]), lat_map) for o in o_lat]
    args = list(o_lat)
    if has_ctx:
        in_specs += [pl.BlockSpec((bm, o.shape[1]), lambda i: (0, 0)) for o in o_ctx]
        args += list(o_ctx)
    vec_spec = pl.BlockSpec((2, 1, d), lambda i: (0, 0, 0))
    in_specs += [pl.BlockSpec(w.shape, lambda i: (0, 0)),
                 pl.BlockSpec((bm, d), lambda i: (i, 0)),
                 vec_spec, vec_spec, vec_spec,
                 pl.BlockSpec((2, d, LANES), lambda i: (0, 0, 0))]
    return pl.pallas_call(
        functools.partial(_out_proj_kernel, n_parts=n_parts, has_ctx=has_ctx),
        grid=(t // bm,),
        in_specs=in_specs,
        out_specs=[pl.BlockSpec((bm, d), lambda i: (i, 0)),
                   pl.BlockSpec((SC_ROW_SPLIT, bm, dp), lambda i: (0, i, 0)),
                   pl.BlockSpec((bm, LANES), lambda i: (i, 0))],
        out_shape=[jax.ShapeDtypeStruct((t, d), F32),
                   jax.ShapeDtypeStruct((SC_ROW_SPLIT, t, dp), F32),
                   jax.ShapeDtypeStruct((t, LANES), F32)],
        compiler_params=_params("parallel"),
        name="out_proj_router",
    )(*args, w, x, gate2, gain2, shift2, w_router)


def _store_pieces(ref, rows):
    dp = ref.shape[2]
    for j in range(ref.shape[0]):
        ref[j] = rows[:, j * dp:(j + 1) * dp]


def _load_pieces(planes):
    return jnp.concatenate([planes[j] for j in range(planes.shape[0])], axis=1)


def _moe_kernel(be_ref, nu_ref, x_ref, w1_ref, w3_ref, w2_ref, y_ref):
    i = pl.program_id(0)

    @pl.when(i < nu_ref[0])
    def _():
        x = _load_pieces(x_ref[...]).astype(BF16)
        a = jnp.dot(x, w1_ref[0, 0].astype(BF16), preferred_element_type=F32)
        b = jnp.dot(x, w3_ref[0, 0].astype(BF16), preferred_element_type=F32)
        hidden = (a * jax.nn.sigmoid(a)) * b
        _store_pieces(y_ref, jnp.dot(hidden.astype(BF16), w2_ref[0, 0].astype(BF16), preferred_element_type=F32))

    @pl.when(i >= nu_ref[0])
    def _():
        y_ref[...] = jnp.zeros(y_ref.shape, y_ref.dtype)


def _moe_experts(buf, blk_e, n_used, w1, w3, w2, layer):
    pieces, n_rows, dp = buf.shape
    d, de = w1.shape[2], w1.shape[3]
    n_blk = n_rows // EXPERT_BLOCK
    row_spec = pl.BlockSpec((pieces, EXPERT_BLOCK, dp), lambda i, be, nu: (0, i, 0))
    grid_spec = pltpu.PrefetchScalarGridSpec(
        num_scalar_prefetch=2,
        grid=(n_blk,),
        in_specs=[row_spec,
                  pl.BlockSpec((1, 1, d, de), lambda i, be, nu: (layer, be[i], 0, 0)),
                  pl.BlockSpec((1, 1, d, de), lambda i, be, nu: (layer, be[i], 0, 0)),
                  pl.BlockSpec((1, 1, de, d), lambda i, be, nu: (layer, be[i], 0, 0))],
        out_specs=row_spec,
    )
    return pl.pallas_call(
        _moe_kernel,
        grid_spec=grid_spec,
        out_shape=jax.ShapeDtypeStruct((pieces, n_rows, dp), F32),
        compiler_params=_params("arbitrary"),
        name="moe_experts",
    )(blk_e, n_used, buf, w1, w3, w2)


def _top1(p):
    m = jnp.max(p, axis=-1, keepdims=True)
    idx = lax.broadcasted_iota(jnp.int32, p.shape, 1)
    return m, jnp.min(jnp.where(p == m, idx, p.shape[-1]), axis=-1, keepdims=True)


def _sc_gather(x, idx):
    pieces, t, dp = x.shape
    n = idx.shape[0]
    flat = (jnp.arange(pieces, dtype=jnp.int32)[:, None] * t + idx[None, :]).reshape(1, pieces * n)
    mesh = plsc.VectorSubcoreMesh(core_axis_name="core", subcore_axis_name="subcore")

    @pl.kernel(out_type=jax.ShapeDtypeStruct((pieces * n, dp), x.dtype), mesh=mesh, scratch_types=[])
    def gather(x_hbm, i_hbm, o_hbm):
        def body(i_vmem, o_vmem):
            pltpu.sync_copy(x_hbm.at[i_vmem.at[0]], o_vmem)

        pltpu.emit_pipeline(
            body,
            grid=(pieces * n // SC_GATHER_WINDOW,),
            in_specs=[pl.BlockSpec((1, SC_GATHER_WINDOW), lambda i: (0, i))],
            out_specs=[pl.BlockSpec((SC_GATHER_WINDOW, dp), lambda i: (i, 0))],
            core_axis_name=("core", "subcore"),
            dimension_semantics=(pltpu.PARALLEL,),
        )(i_hbm, o_hbm)

    return gather(x.reshape(pieces * t, dp), flat).reshape(pieces, n, dp)


def _combine_kernel(y_ref, w_ref, x_ref, gate_ref, o_ref, *, row0):
    w = w_ref[...]
    f = w[:, 0:1] * _load_pieces(y_ref[:, 0]) + w[:, 1:2] * _load_pieces(y_ref[:, 1])
    row = jnp.minimum(pl.program_id(0) + row0, 1)
    o_ref[...] = x_ref[...] + gate_ref[row] * f


def _combine(y2, weights, x, gate2, has_ctx):
    t, d = x.shape
    bm = ROW_BLOCK
    pieces, _, _, dp = y2.shape
    return pl.pallas_call(
        functools.partial(_combine_kernel, row0=0 if has_ctx else 1),
        grid=(t // bm,),
        in_specs=[pl.BlockSpec((pieces, TOP_K, bm, dp), lambda i: (0, 0, i, 0)),
                  pl.BlockSpec((bm, TOP_K), lambda i: (i, 0)),
                  pl.BlockSpec((bm, d), lambda i: (i, 0)),
                  pl.BlockSpec((2, 1, d), lambda i: (0, 0, 0))],
        out_specs=pl.BlockSpec((bm, d), lambda i: (i, 0)),
        out_shape=jax.ShapeDtypeStruct((t, d), F32),
        compiler_params=_params("parallel"),
        name="moe_combine",
    )(y2, weights, x, gate2)


def _prefix_rank(onehot):
    rows, e = onehot.shape
    blk = EXPERT_BLOCK
    oh = onehot.astype(F32).reshape(rows // blk, blk, e)
    tri = (jnp.arange(blk)[:, None] > jnp.arange(blk)[None, :]).astype(F32)
    within = jnp.einsum("ij,bjk->bik", tri, oh)
    tot = jnp.sum(oh, axis=1)
    before = jnp.cumsum(tot, axis=0) - tot
    return jnp.sum((within + before[:, None, :]) * oh, axis=-1).reshape(rows).astype(jnp.int32)


def _hier_moe(tokens, logits, b_group, b_expert, w1, w3, w2, layer):
    pieces, t, dp = tokens.shape
    pg = jax.nn.softmax(logits[:, :N_GROUPS] + b_group.astype(F32), axis=-1)
    g_prob, g_idx = _top1(pg)
    le = (logits[:, N_GROUPS:N_GROUPS + N_EXPERTS] + b_expert.astype(F32)).reshape(t, N_GROUPS, EXPERTS_PER_GROUP)
    le = jnp.take_along_axis(le, g_idx[:, :, None], axis=1)[:, 0]
    pe = jax.nn.softmax(le, axis=-1)
    p1, i1 = _top1(pe)
    p2, i2 = _top1(jnp.where(lax.broadcasted_iota(jnp.int32, pe.shape, 1) == i1, -1.0, pe))
    e_prob, e_idx = jnp.concatenate([p1, p2], axis=-1), jnp.concatenate([i1, i2], axis=-1)
    weights = g_prob * e_prob / jnp.sum(e_prob, axis=-1, keepdims=True)
    flat_e = (g_idx * EXPERTS_PER_GROUP + e_idx).reshape(-1)
    onehot = (flat_e[:, None] == jnp.arange(N_EXPERTS)[None, :]).astype(jnp.int32)
    rank = _prefix_rank(onehot)
    counts = jnp.sum(onehot, axis=0)
    padded = (counts + EXPERT_BLOCK - 1) // EXPERT_BLOCK * EXPERT_BLOCK
    pend = jnp.cumsum(padded)
    dest = (pend - padded)[flat_e] + rank
    n_rows = (t * TOP_K + EXPERT_BLOCK - 1) // EXPERT_BLOCK * EXPERT_BLOCK + N_EXPERTS * EXPERT_BLOCK
    n_blk = n_rows // EXPERT_BLOCK
    src = (jnp.arange(n_rows, dtype=jnp.int32) % t).at[dest].set(jnp.arange(t * TOP_K, dtype=jnp.int32) // TOP_K)
    buf = _sc_gather(tokens, src)
    starts = jnp.arange(n_blk, dtype=jnp.int32) * EXPERT_BLOCK
    blk_e = jnp.minimum(jnp.sum(pend[None, :] <= starts[:, None], axis=1), N_EXPERTS - 1).astype(jnp.int32)
    n_used = (pend[-1] // EXPERT_BLOCK).astype(jnp.int32).reshape(1)
    yb = _moe_experts(buf, blk_e, n_used, w1, w3, w2, layer)
    dest_k_major = dest.reshape(t, TOP_K).T.reshape(-1)
    return _sc_gather(yb, dest_k_major).reshape(pieces, TOP_K, t, dp), weights


def kernel(x, c, ctx, c_ctx, w_mod, b_mod, norm_mix, norm_ffn, w_in_ab, w_out_ab, qn_a, kn_a, sink_a, qn_b, kn_b, w_in_c, w_out_c, qn_c, kn_c, lam_c, subln_c, w_group, b_group, w_expert, b_expert, w1, w3, w2):
    b, s_lat, d = x.shape
    n_ctx = ctx.shape[1]
    assert b == 1 and n_ctx == ROW_BLOCK and s_lat % (2 * ROW_BLOCK) == 0
    depth = w_mod.shape[0]
    cos, sin = _rope_tables(s_lat, n_ctx)
    mods = _mod_vectors(c, c_ctx, w_mod, b_mod)
    xs = jnp.concatenate([ctx[0], x[0]], axis=0)
    has_ctx = True
    pending = None
    for l in range(depth):
        last = l == depth - 1
        i = l // 2
        sh1, sc1, gt1, sh2, sc2, gt2 = [mods[l, :2, j * d:(j + 1) * d].reshape(2, 1, d) for j in range(6)]
        gain1 = norm_mix[l].astype(F32) * (1 + sc1)
        if l % 2 == 0:
            xs, heads = _inproj_ab(xs, pending, gain1, sh1, w_in_ab[i].astype(BF16), qn_a[i], kn_a[i], qn_b[i], kn_b[i], cos, sin)
            o_lat, o_ctx = _mixer_ab(heads, n_ctx, sink_a[i], not last)
            w_out = w_out_ab[i]
        else:
            lam_init = 0.8 - 0.6 * math.exp(-0.3 * l)
            xs, heads = _inproj_c(xs, pending, gain1, sh1, w_in_c[i].astype(BF16), qn_c[i], kn_c[i], cos, sin)
            o_lat, o_ctx = _mixer_c(heads, n_ctx, lam_c[i], subln_c[i], lam_init, not last)
            w_out = w_out_c[i]
        if last:
            xs, has_ctx = xs[n_ctx:], False
        gain2 = norm_ffn[l].astype(F32) * (1 + sc2)
        w_router = jnp.zeros((d, LANES), F32).at[:, :N_GROUPS].set(w_group[l]).at[:, N_GROUPS:N_GROUPS + N_EXPERTS].set(w_expert[l])
        w_router = jnp.stack(_split_bf16(w_router))
        xs, tokens, logits = _out_proj(o_lat, o_ctx, w_out.astype(BF16), xs, gt1, gain2, sh2, w_router)
        y2, weights = _hier_moe(tokens, logits, b_group[l], b_expert[l], w1, w3, w2, l)
        pending = (y2, weights, gt2)
    return _combine(y2, weights, xs, gt2, has_ctx).reshape(b, s_lat, d)
```

```python
import functools
import math

import jax
import jax.numpy as jnp
from jax import lax
from jax.experimental import pallas as pl
from jax.experimental.pallas import tpu as pltpu
from jax.experimental.pallas import tpu_sc as plsc

F32 = jnp.float32
BF16 = jnp.bfloat16

GRID_W = 64
HEAD_DIM = 64
WINDOW = 128
ROPE_THETA = 10000.0
EPS = 1e-6
NEG = -1e30
N_HEADS_A, N_KV_A = 8, 2
N_HEADS_B, N_KV_B = 8, 2
GROUP = N_HEADS_A // N_KV_A
QA_W, KVA_W = N_HEADS_A * HEAD_DIM, N_KV_A * HEAD_DIM
QB_W, KVB_W = N_HEADS_B * HEAD_DIM, N_KV_B * HEAD_DIM
N_HEADS_C = 8
DV_C = 2 * HEAD_DIM
N_GROUPS, EXPERTS_PER_GROUP, TOP_K = 4, 8, 2
N_EXPERTS = N_GROUPS * EXPERTS_PER_GROUP
LANES = 128
ROW_BLOCK = 256
EXPERT_BLOCK = 512
SC_GATHER_WINDOW = 128
SC_ROW_SPLIT = 4
KEY_BLOCK = 3328
GQA_V_ROWS = 128
DIFF_V_ROWS = 144
VMEM_LIMIT = 48 * 1024 * 1024
LOG2E = math.log2(math.e)
Q_SCALE = HEAD_DIM ** -0.5 * LOG2E
LOG2_LOGIT_BOUND = 60.0


def _params(*sem):
    return pltpu.CompilerParams(dimension_semantics=sem, vmem_limit_bytes=VMEM_LIMIT)


def _lane(shape):
    return lax.broadcasted_iota(jnp.int32, shape, 1)


def _mod_vec_kernel(a_ref, w_ref, b_ref, o_ref):
    a = a_ref[...]
    a = a * jax.nn.sigmoid(a)
    o_ref[0] = jnp.dot(a, w_ref[0], preferred_element_type=F32, precision=lax.Precision.HIGHEST) + b_ref[0]


def _mod_vectors(c, c_ctx, w_mod, b_mod):
    depth, d, n = w_mod.shape
    a = jnp.zeros((8, d), F32).at[0].set(c_ctx).at[1].set(c[0])
    bn = 1024
    return pl.pallas_call(
        _mod_vec_kernel,
        grid=(depth, n // bn),
        in_specs=[pl.BlockSpec((8, d), lambda l, j: (0, 0)),
                  pl.BlockSpec((1, d, bn), lambda l, j: (l, 0, j)),
                  pl.BlockSpec((1, 1, bn), lambda l, j: (l, 0, j))],
        out_specs=pl.BlockSpec((1, 8, bn), lambda l, j: (l, 0, j)),
        out_shape=jax.ShapeDtypeStruct((depth, 8, n), F32),
        compiler_params=_params("parallel", "parallel"),
        name="mod_vectors",
    )(a, w_mod, b_mod.reshape(depth, 1, n))


def _modulated(x, g_ref, s_ref):
    y = x * lax.rsqrt(jnp.mean(x * x, axis=-1, keepdims=True) + EPS)
    row = jnp.minimum(pl.program_id(0), 1)
    return y * g_ref[row] + s_ref[row]


def _split_bf16(x):
    top = lax.bitcast_convert_type(lax.bitcast_convert_type(x, jnp.uint32) & jnp.uint32(0xFFFF0000), F32)
    return top.astype(BF16), (x - top).astype(BF16)


def _head_norm_rope(x, gain, cos, sin, seg_mean):
    hi, lo = _split_bf16(x * x)
    ms = jnp.dot(hi, seg_mean, preferred_element_type=F32) + jnp.dot(lo, seg_mean, preferred_element_type=F32)
    y = x * lax.rsqrt(ms + EPS) * gain
    first_half = (_lane(y.shape) & (HEAD_DIM - 1)) < HEAD_DIM // 2
    partner = jnp.where(first_half, pltpu.roll(y, LANES - HEAD_DIM // 2, 1), pltpu.roll(y, HEAD_DIM // 2, 1))
    return y * cos + partner * sin


def _store_value_heads(v_ref, v):
    lane = _lane(v.shape)
    ones_col = jnp.where(lane == HEAD_DIM, 1.0, 0.0)
    v_ref[0] = jnp.where(lane < HEAD_DIM, v, ones_col).astype(BF16)
    v_ref[1] = jnp.where(lane < HEAD_DIM, pltpu.roll(v, HEAD_DIM, 1), ones_col).astype(BF16)


def _ones_row_block(rows, cols):
    return jnp.where(lax.broadcasted_iota(jnp.int32, (rows, cols), 0) == 0, 1.0, 0.0)


def _store_value_heads_t(vt_ref, v):
    vt = v.T
    tail = _ones_row_block(GQA_V_ROWS - HEAD_DIM, v.shape[0])
    vt_ref[0, 0] = jnp.concatenate([vt[:HEAD_DIM], tail], axis=0).astype(BF16)
    vt_ref[1, 0] = jnp.concatenate([vt[HEAD_DIM:], tail], axis=0).astype(BF16)


N_INPROJ_INPUTS = 8


def _inproj_rows(refs, fused):
    if not fused:
        return refs[0][...], refs[1:N_INPROJ_INPUTS], refs[N_INPROJ_INPUTS:]
    y_ref, wts_ref, gate_ref, x_ref = refs[:4]
    ins, outs = refs[4:N_INPROJ_INPUTS + 3], refs[N_INPROJ_INPUTS + 3:]
    wts = wts_ref[...]
    f = wts[:, 0:1] * _load_pieces(y_ref[:, 0]) + wts[:, 1:2] * _load_pieces(y_ref[:, 1])
    x = x_ref[...] + gate_ref[jnp.minimum(pl.program_id(0), 1)] * f
    outs[0][...] = x
    return x, ins, outs[1:]


def _inproj_ab_kernel(*refs, fused):
    x, (g_ref, s_ref, w_ref, hg_ref, cos_ref, sin_ref, seg_ref), outs = _inproj_rows(refs, fused)
    qa_ref, qb_ref, ka_ref, kb_ref, va_ref, vat_ref, vbt_ref = outs
    bm = x.shape[0]
    h = _modulated(x, g_ref, s_ref).astype(BF16)
    proj = jnp.dot(h, w_ref[...], preferred_element_type=F32)
    cos, sin, seg = cos_ref[...], sin_ref[...], seg_ref[...]
    low = _lane((bm, LANES)) < HEAD_DIM

    def normed(j):
        cols = slice(j * LANES, (j + 1) * LANES)
        return _head_norm_rope(proj[:, cols], hg_ref[:, cols], cos, sin, seg)

    def store_queries(q_ref, tile0):
        for j in range(N_HEADS_A // 2):
            y = normed(tile0 + j)
            swapped = pltpu.roll(y, HEAD_DIM, 1)
            kv, g0 = (2 * j) // GROUP, (2 * j) % GROUP
            if kv == 0:
                even, odd = jnp.where(low, y, 0.0), jnp.where(low, swapped, 0.0)
            else:
                even, odd = jnp.where(low, 0.0, swapped), jnp.where(low, 0.0, y)
            q_ref[kv, 0, g0 * bm:(g0 + 1) * bm, :] = even.astype(BF16)
            q_ref[kv, 0, (g0 + 1) * bm:(g0 + 2) * bm, :] = odd.astype(BF16)

    tq = QA_W // LANES
    store_queries(qa_ref, 0)
    ka_ref[0] = normed(tq).astype(BF16)
    va = proj[:, (tq + 1) * LANES:(tq + 2) * LANES]
    _store_value_heads(va_ref, va)
    _store_value_heads_t(vat_ref, va)
    store_queries(qb_ref, tq + 2)
    kb_ref[0] = normed(2 * tq + 2).astype(BF16)
    _store_value_heads_t(vbt_ref, proj[:, (2 * tq + 3) * LANES:(2 * tq + 4) * LANES])


def _inproj_c_kernel(*refs, fused):
    x, (g_ref, s_ref, w_ref, hg_ref, cos_ref, sin_ref, seg_ref), (q_ref, k_ref, v_ref) = _inproj_rows(refs, fused)
    bm = x.shape[0]
    h = _modulated(x, g_ref, s_ref).astype(BF16)
    proj = jnp.dot(h, w_ref[...], preferred_element_type=F32)
    cos, sin, seg = cos_ref[...], sin_ref[...], seg_ref[...]
    low = _lane((bm, LANES)) < HEAD_DIM
    tail = _ones_row_block(DIFF_V_ROWS - DV_C, bm)

    def normed(j):
        cols = slice(j * LANES, (j + 1) * LANES)
        return _head_norm_rope(proj[:, cols], hg_ref[:, cols], cos, sin, seg)

    for j in range(N_HEADS_C):
        y = normed(j)
        q_ref[j, 0, 0:bm, :] = jnp.where(low, y, 0.0).astype(BF16)
        q_ref[j, 0, bm:2 * bm, :] = jnp.where(low, 0.0, y).astype(BF16)
        k_ref[j] = normed(N_HEADS_C + j).astype(BF16)
        v = proj[:, (2 * N_HEADS_C + j) * LANES:(2 * N_HEADS_C + j + 1) * LANES]
        v_ref[j, 0] = jnp.concatenate([v.T, tail], axis=0).astype(BF16)


def _rope_tables(seq, n_ctx):
    rows_n = seq // GRID_W
    rows = jnp.broadcast_to(jnp.arange(rows_n, dtype=F32)[:, None], (rows_n, GRID_W)).reshape(-1)
    cols = jnp.broadcast_to(jnp.arange(GRID_W, dtype=F32)[None, :], (rows_n, GRID_W)).reshape(-1)
    half = HEAD_DIM // 2
    inv = ROPE_THETA ** (-jnp.arange(0, half, 2, dtype=F32) / half)
    ang = jnp.concatenate([rows[:, None] * inv, cols[:, None] * inv], axis=-1)
    reps = LANES // half
    sign = jnp.tile(jnp.concatenate([-jnp.ones((half,), F32), jnp.ones((half,), F32)]), LANES // HEAD_DIM)
    cos = jnp.pad(jnp.tile(jnp.cos(ang), (1, reps)), ((n_ctx, 0), (0, 0)), constant_values=1.0)
    sin = jnp.pad(jnp.tile(jnp.sin(ang), (1, reps)) * sign, ((n_ctx, 0), (0, 0)))
    return cos, sin


def _segment_mean_matrix():
    idx = jnp.arange(LANES) // HEAD_DIM
    return jnp.where(idx[:, None] == idx[None, :], 1.0 / HEAD_DIM, 0.0).astype(BF16)


def _q_slot(i, nblk):
    return (i + nblk - 1) % nblk


def _inproj_common_specs(t, d, n):
    bm = ROW_BLOCK
    return [pl.BlockSpec((bm, d), lambda i: (i, 0)),
            pl.BlockSpec((2, 1, d), lambda i: (0, 0, 0)),
            pl.BlockSpec((2, 1, d), lambda i: (0, 0, 0)),
            pl.BlockSpec((d, n), lambda i: (0, 0)),
            pl.BlockSpec((1, n), lambda i: (0, 0)),
            pl.BlockSpec((bm, LANES), lambda i: (i, 0)),
            pl.BlockSpec((bm, LANES), lambda i: (i, 0)),
            pl.BlockSpec((LANES, LANES), lambda i: (0, 0))]


def _inproj_call(kernel_fn, name, x, pending, args, n, out_specs, out_shape):
    t, d = x.shape
    bm = ROW_BLOCK
    in_specs = _inproj_common_specs(t, d, n)
    args = [x] + list(args)
    fused = pending is not None
    if fused:
        y2, wts, gate2 = pending
        pieces, _, _, dp = y2.shape
        in_specs = [pl.BlockSpec((pieces, TOP_K, bm, dp), lambda i: (0, 0, i, 0)),
                    pl.BlockSpec((bm, TOP_K), lambda i: (i, 0)),
                    pl.BlockSpec((2, 1, d), lambda i: (0, 0, 0))] + in_specs
        args = [y2, wts, gate2] + args
        out_specs = [pl.BlockSpec((bm, d), lambda i: (i, 0))] + list(out_specs)
        out_shape = [jax.ShapeDtypeStruct((t, d), F32)] + list(out_shape)
    outs = pl.pallas_call(
        functools.partial(kernel_fn, fused=fused),
        grid=(t // bm,),
        in_specs=in_specs,
        out_specs=out_specs,
        out_shape=out_shape,
        compiler_params=_params("parallel"),
        name=name,
    )(*args)
    return (outs[0], outs[1:]) if fused else (x, outs)


def _inproj_ab(x, pending, gain2, shift2, w, qn_a, kn_a, qn_b, kn_b, cos, sin):
    t, d = x.shape
    n = w.shape[1]
    bm, nblk = ROW_BLOCK, t // ROW_BLOCK
    tile = lambda g, reps: jnp.tile(g.astype(F32), reps)
    ones_v = jnp.ones((KVA_W,), F32)
    hg = jnp.concatenate([tile(qn_a, N_HEADS_A) * Q_SCALE, tile(kn_a, N_KV_A), ones_v,
                          tile(qn_b, N_HEADS_B) * Q_SCALE, tile(kn_b, N_KV_B), ones_v]).reshape(1, n)
    q_shape = jax.ShapeDtypeStruct((N_KV_A, nblk, GROUP * bm, LANES), BF16)
    k_shape = jax.ShapeDtypeStruct((1, t, LANES), BF16)
    v_shape = jax.ShapeDtypeStruct((N_KV_A, t, LANES), BF16)
    vt_shape = jax.ShapeDtypeStruct((N_KV_A, nblk, GQA_V_ROWS, bm), BF16)
    q_spec = pl.BlockSpec((N_KV_A, 1, GROUP * bm, LANES), lambda i: (0, _q_slot(i, nblk), 0, 0))
    k_spec = pl.BlockSpec((1, bm, LANES), lambda i: (0, i, 0))
    v_spec = pl.BlockSpec((N_KV_A, bm, LANES), lambda i: (0, i, 0))
    vt_spec = pl.BlockSpec((N_KV_A, 1, GQA_V_ROWS, bm), lambda i: (0, i, 0, 0))
    return _inproj_call(_inproj_ab_kernel, "inproj_ab", x, pending,
                        [gain2, shift2, w, hg, cos, sin, _segment_mean_matrix()], n,
                        [q_spec, q_spec, k_spec, k_spec, v_spec, vt_spec, vt_spec],
                        [q_shape, q_shape, k_shape, k_shape, v_shape, vt_shape, vt_shape])


def _inproj_c(x, pending, gain2, shift2, w, qn, kn, cos, sin):
    t, d = x.shape
    n = w.shape[1]
    bm, nblk = ROW_BLOCK, t // ROW_BLOCK
    h = N_HEADS_C
    tile = lambda g: jnp.tile(g.astype(F32), 2 * h)
    hg = jnp.concatenate([tile(qn) * Q_SCALE, tile(kn), jnp.ones((h * DV_C,), F32)]).reshape(1, n)
    return _inproj_call(_inproj_c_kernel, "inproj_c", x, pending,
                        [gain2, shift2, w, hg, cos, sin, _segment_mean_matrix()], n,
                        [pl.BlockSpec((h, 1, 2 * bm, LANES), lambda i: (0, _q_slot(i, nblk), 0, 0)),
                         pl.BlockSpec((h, bm, LANES), lambda i: (0, i, 0)),
                         pl.BlockSpec((h, 1, DIFF_V_ROWS, bm), lambda i: (0, i, 0, 0))],
                        [jax.ShapeDtypeStruct((h, nblk, 2 * bm, LANES), BF16),
                         jax.ShapeDtypeStruct((h, t, LANES), BF16),
                         jax.ShapeDtypeStruct((h, nblk, DIFF_V_ROWS, bm), BF16)])


def _merge_gqa_heads(o, bq):
    low = _lane((bq, LANES)) < HEAD_DIM
    pairs = [jnp.where(low, o[g * bq:(g + 1) * bq], pltpu.roll(o[(g + 1) * bq:(g + 2) * bq], HEAD_DIM, 1))
             for g in range(0, GROUP, 2)]
    return jnp.concatenate(pairs, axis=1)


def _flash_kernel(*refs, mode, online, n_keys, bk, dv, l0):
    refs = list(refs)
    q_ref, k_ref, vt_ref = refs[:3]
    pos = 3
    if online:
        m0_ref = refs[pos]
        pos += 1
    if mode == "diff":
        lam_ref, sub_ref = refs[pos:pos + 2]
        pos += 2
    o_ref = refs[pos]
    acc_sc = refs[pos + 1]
    m_sc = refs[pos + 2] if online else None

    nsub, rb = q_ref.shape[1], q_ref.shape[2]
    r = nsub * rb
    ch = vt_ref.shape[3]
    q = q_ref[0].reshape(r, LANES)
    acc_sc[...] = jnp.where(lax.broadcasted_iota(jnp.int32, acc_sc.shape, 0) == dv, l0, 0.0).astype(F32)
    if online:
        m_sc[...] = m0_ref[0]

    def block(start, size):
        kb = k_ref[0, pl.ds(start, size), :]
        st = lax.dot_general(kb, q, (((1,), (1,)), ((), ())), preferred_element_type=F32)
        if online:
            m_prev = m_sc[...]
            m_new = jnp.maximum(m_prev, jnp.max(st, axis=0, keepdims=True))
            pt = jnp.exp2(st - m_new).astype(BF16)
            acc = jnp.exp2(m_prev - m_new) * acc_sc[...]
            m_sc[...] = m_new
        else:
            pt = jnp.exp2(st).astype(BF16)
            acc = acc_sc[...]
        c0 = start // ch
        for c in range(size // ch):
            acc = acc + jnp.dot(vt_ref[0, c0 + c], pt[c * ch:(c + 1) * ch], preferred_element_type=F32)
        acc_sc[...] = acc

    n_full, tail = n_keys // bk, n_keys % bk
    if n_full:
        def body(i, carry):
            block(pl.multiple_of(i * bk, bk), bk)
            return carry
        lax.fori_loop(0, n_full, body, 0)
    if tail:
        block(n_full * bk, tail)

    acc = acc_sc[...]
    ot = acc[:LANES] / acc[dv:dv + 1]
    bq = rb // (GROUP if mode == "gqa" else 2)
    o = jnp.concatenate([ot[:, j * bq:(j + 1) * bq].T for j in range(r // bq)], axis=0)
    if mode == "gqa":
        o_ref[...] = _merge_gqa_heads(o, bq).astype(o_ref.dtype)
    else:
        for b in range(nsub):
            d = o[b * rb:b * rb + bq] - lam_ref[...] * o[b * rb + bq:(b + 1) * rb]
            y = d * lax.rsqrt(jnp.mean(d * d, axis=-1, keepdims=True) + EPS) * sub_ref[...]
            o_ref[b * bq:(b + 1) * bq, :] = y.astype(o_ref.dtype)


def _flash_call(q, k, vt, extra, *, mode, online, n_keys, slot0, nsub, n_steps, l0=0.0):
    hkv, _, rb, _ = q.shape
    hk = k.shape[0]
    _, _, dvr, ch = vt.shape
    dv = HEAD_DIM if mode == "gqa" else DV_C
    bq = rb // (GROUP if mode == "gqa" else 2)
    ocols = GROUP * HEAD_DIM if mode == "gqa" else DV_C
    r = nsub * rb
    bk = min(KEY_BLOCK, n_keys)
    in_specs = [pl.BlockSpec((1, nsub, rb, LANES), lambda h, i: (h, slot0 // nsub + i, 0, 0)),
                pl.BlockSpec((1, n_keys, LANES), (lambda h, i: (h, 0, 0)) if hk > 1 else (lambda h, i: (0, 0, 0))),
                pl.BlockSpec((1, n_keys // ch, dvr, ch), lambda h, i: (h, 0, 0, 0))]
    args = [q, k, vt]
    scratch = [pltpu.VMEM((dvr, r), F32)]
    if online:
        m0 = extra.pop(0)
        in_specs.append(pl.BlockSpec((1, 1, r), lambda h, i: (h, 0, 0)))
        args.append(m0)
        scratch.append(pltpu.VMEM((1, r), F32))
    for a in extra:
        in_specs.append(pl.BlockSpec(a.shape, lambda h, i: (0, 0)))
        args.append(a)
    return pl.pallas_call(
        functools.partial(_flash_kernel, mode=mode, online=online, n_keys=n_keys, bk=bk, dv=dv, l0=l0),
        grid=(hkv, n_steps),
        in_specs=in_specs,
        out_specs=pl.BlockSpec((nsub * bq, ocols), lambda h, i: (i, h)),
        out_shape=jax.ShapeDtypeStruct((n_steps * nsub * bq, hkv * ocols), BF16),
        scratch_shapes=scratch,
        compiler_params=_params("parallel", "parallel"),
        name="flash_online" if online else "flash_bounded",
    )(*args)


def _logits_bounded(q_gain, k_gain):
    bound = HEAD_DIM * Q_SCALE * 1.02 * jnp.max(jnp.abs(q_gain.astype(F32))) * jnp.max(jnp.abs(k_gain.astype(F32)))
    return bound <= LOG2_LOGIT_BOUND


def _attend(q, k, v, extra, bounded, **kw):
    hkv, r = q.shape[0], kw["nsub"] * q.shape[2]
    fast = lambda q_, k_, v_, *e: _flash_call(q_, k_, v_, list(e), online=False, **kw)
    safe = lambda q_, k_, v_, *e: _flash_call(q_, k_, v_, [jnp.full((hkv, 1, r), NEG, F32)] + list(e), online=True, **kw)
    return lax.cond(bounded, fast, safe, q, k, v, *extra)


def _window_kernel(q_ref, k_ref, v_ref, sink_ref, o_ref, *, bq, n_ctx):
    q = q_ref[0, 0]
    r = q.shape[0]
    t = k_ref.shape[1]
    w = bq + 2 * WINDOW
    q0 = pl.program_id(1) * bq
    ws = pl.multiple_of(jnp.clip(n_ctx + q0 - WINDOW, 0, t - w), WINDOW)
    kw = k_ref[0, pl.ds(ws, w), :]
    vw = v_ref[0, pl.ds(ws, w), :]
    contract_last = (((1,), (1,)), ((), ()))
    s_loc = lax.dot_general(q, kw, contract_last, preferred_element_type=F32)
    qpos = q0 + (lax.broadcasted_iota(jnp.int32, (r, w), 0) & (bq - 1))
    kpos = ws - n_ctx + lax.broadcasted_iota(jnp.int32, (r, w), 1)
    mask = (kpos >= 0) & (kpos - qpos <= WINDOW) & (qpos - kpos <= WINDOW)
    s_loc = jnp.where(mask, s_loc, NEG)
    s_ctx = lax.dot_general(q, k_ref[0, 0:n_ctx, :], contract_last, preferred_element_type=F32)
    sink = sink_ref[0]
    m = jnp.maximum(sink, jnp.maximum(jnp.max(s_loc, axis=-1, keepdims=True), jnp.max(s_ctx, axis=-1, keepdims=True)))
    p_loc = jnp.exp2(s_loc - m)
    p_ctx = jnp.exp2(s_ctx - m)
    l = jnp.exp2(sink - m) + jnp.sum(p_loc, axis=-1, keepdims=True) + jnp.sum(p_ctx, axis=-1, keepdims=True)
    o = (jnp.dot(p_loc.astype(BF16), vw, preferred_element_type=F32)
         + jnp.dot(p_ctx.astype(BF16), v_ref[0, 0:n_ctx, :], preferred_element_type=F32))
    o_ref[...] = _merge_gqa_heads(o / l, bq).astype(o_ref.dtype)


def _window_attention(q, k, v, sink_rows, n_ctx):
    hkv, slots, r, _ = q.shape
    t = k.shape[1]
    bq = r // GROUP
    nq = slots - n_ctx // bq
    return pl.pallas_call(
        functools.partial(_window_kernel, bq=bq, n_ctx=n_ctx),
        grid=(hkv, nq),
        in_specs=[pl.BlockSpec((1, 1, r, LANES), lambda h, i: (h, i, 0, 0)),
                  pl.BlockSpec((1, t, LANES), lambda h, i: (0, 0, 0)),
                  pl.BlockSpec((1, t, LANES), lambda h, i: (h, 0, 0)),
                  pl.BlockSpec((1, r, 1), lambda h, i: (h, 0, 0))],
        out_specs=pl.BlockSpec((bq, GROUP * HEAD_DIM), lambda h, i: (i, h)),
        out_shape=jax.ShapeDtypeStruct((nq * bq, hkv * GROUP * HEAD_DIM), BF16),
        compiler_params=_params("parallel", "parallel"),
        name="window_attention",
    )(q, k, v, sink_rows)


def _window_bounded_kernel(q_ref, k_ref, vt_ref, mask_ref, sink_ref, o_ref, *, bq):
    q = q_ref[0, 0]
    ch = vt_ref.shape[3]
    cw = jnp.clip(pl.program_id(1), 0, vt_ref.shape[1] - 3)
    contract_last = (((1,), (1,)), ((), ()))
    k_win = k_ref[0, pl.ds(pl.multiple_of(cw * ch, ch), 3 * ch), :]
    st_win = lax.dot_general(k_win, q, contract_last, preferred_element_type=F32)
    st_ctx = lax.dot_general(k_ref[0, 0:ch, :], q, contract_last, preferred_element_type=F32)
    pt_win = jnp.exp2(st_win).astype(BF16) * mask_ref[0]
    acc = jnp.dot(vt_ref[0, 0], jnp.exp2(st_ctx).astype(BF16), preferred_element_type=F32)
    for c in range(3):
        acc = acc + jnp.dot(vt_ref[0, cw + c], pt_win[c * ch:(c + 1) * ch], preferred_element_type=F32)
    ot = acc / (acc[HEAD_DIM:HEAD_DIM + 1] + jnp.exp2(sink_ref[0]))
    o = jnp.concatenate([ot[:, g * bq:(g + 1) * bq].T for g in range(GROUP)], axis=0)
    o_ref[...] = _merge_gqa_heads(o, bq).astype(o_ref.dtype)


def _window_masks(bq, r):
    c = jnp.arange(3 * bq)[:, None]
    off = c - (jnp.arange(r)[None, :] & (bq - 1))
    centred = (off >= bq - WINDOW) & (off <= bq + WINDOW)
    shifted = (off >= 2 * bq - WINDOW) & (off <= 2 * bq + WINDOW)
    return jnp.stack([centred & (c >= bq), centred, shifted]).astype(BF16)


def _window_attention_bounded(q, k, vt, sink_cols, n_ctx):
    hkv, slots, r, _ = q.shape
    t = k.shape[1]
    bq = r // GROUP
    nblk = t // bq
    nq = nblk - 1
    assert n_ctx == bq and nblk >= 3
    return pl.pallas_call(
        functools.partial(_window_bounded_kernel, bq=bq),
        grid=(hkv, nq),
        in_specs=[pl.BlockSpec((1, 1, r, LANES), lambda h, i: (h, i, 0, 0)),
                  pl.BlockSpec((1, t, LANES), lambda h, i: (0, 0, 0)),
                  pl.BlockSpec((1, nblk, GQA_V_ROWS, bq), lambda h, i: (h, 0, 0, 0)),
                  pl.BlockSpec((1, 3 * bq, r), lambda h, i: (jnp.where(i == 0, 0, jnp.where(i >= nblk - 2, 2, 1)), 0, 0)),
                  pl.BlockSpec((1, 1, r), lambda h, i: (h, 0, 0))],
        out_specs=pl.BlockSpec((bq, GROUP * HEAD_DIM), lambda h, i: (i, h)),
        out_shape=jax.ShapeDtypeStruct((nq * bq, hkv * GROUP * HEAD_DIM), BF16),
        compiler_params=_params("parallel", "parallel"),
        name="window_bounded",
    )(q, k, vt, _window_masks(bq, r), sink_cols)


def _sink_rows(sink, bq):
    hkv, g = sink.shape
    return jnp.broadcast_to(sink.astype(F32)[:, :, None] * LOG2E, (hkv, g, bq)).reshape(hkv, g * bq, 1)


def _mixer_ab(heads, n_ctx, sink_a, with_ctx, bounded_a, bounded):
    qa, qb, ka, kb, va, vat, vbt = heads
    t = ka.shape[1]
    bq = ROW_BLOCK
    n_lat = (t - n_ctx) // bq
    sink = _sink_rows(sink_a.reshape(N_KV_A, GROUP), bq)
    sink_cols = sink.reshape(N_KV_A, 1, GROUP * bq)
    bounded_a = bounded_a & (jnp.max(jnp.abs(sink)) <= LOG2_LOGIT_BOUND)
    oa = lax.cond(bounded_a,
                  lambda: _window_attention_bounded(qa, ka, vat, sink_cols, n_ctx),
                  lambda: _window_attention(qa, ka, va, sink, n_ctx))
    ob = _attend(qb, kb, vbt, [], bounded, mode="gqa", n_keys=t, slot0=0, nsub=1, n_steps=n_lat)
    if not with_ctx:
        return (oa, ob), None
    oca = _flash_call(qa, ka, vat, [sink_cols], mode="gqa", online=True, n_keys=n_ctx,
                      slot0=n_lat, nsub=1, n_steps=1, l0=1.0)
    ocb = _attend(qb, kb, vbt, [], bounded, mode="gqa", n_keys=n_ctx, slot0=n_lat, nsub=1, n_steps=1)
    return (oa, ob), (oca, ocb)


def _mixer_c(heads, n_ctx, lam_p, subln, lam_init, with_ctx, bounded):
    q, k, v = heads
    t = k.shape[1]
    n_lat = (t - n_ctx) // ROW_BLOCK
    lp = lam_p.astype(F32)
    lam = jnp.exp(jnp.sum(lp[0] * lp[1])) - jnp.exp(jnp.sum(lp[2] * lp[3])) + lam_init
    extra = [jnp.full((1, DV_C), lam, F32), (subln.astype(F32) * (1 - lam_init)).reshape(1, DV_C)]
    nsub = 2 if n_lat % 2 == 0 else 1
    o_lat = _attend(q, k, v, extra, bounded, mode="diff", n_keys=t, slot0=0, nsub=nsub, n_steps=n_lat // nsub)
    if not with_ctx:
        return (o_lat,), None
    o_ctx = _attend(q, k, v, extra, bounded, mode="diff", n_keys=n_ctx, slot0=n_lat, nsub=1, n_steps=1)
    return (o_lat,), (o_ctx,)


def _out_proj_kernel(*refs, n_parts, has_ctx):
    lat = refs[:n_parts]
    ctx = refs[n_parts:2 * n_parts] if has_ctx else None
    w_ref, x_ref, gate_ref, g_ref, s_ref, wr_ref, y_ref, tok_ref, logit_ref = refs[-9:]
    is_ctx = pl.program_id(0) == 0 if has_ctx else False
    row = jnp.where(is_ctx, 0, 1) if has_ctx else 1
    acc = None
    col = 0
    for p in range(n_parts):
        o = lat[p][...]
        if has_ctx:
            o = jnp.where(is_ctx, ctx[p][...], o)
        width = o.shape[1]
        part = jnp.dot(o, w_ref[col:col + width, :], preferred_element_type=F32)
        acc = part if acc is None else acc + part
        col += width
    x1 = x_ref[...] + gate_ref[row] * acc
    y_ref[...] = x1
    h = x1 * lax.rsqrt(jnp.mean(x1 * x1, axis=-1, keepdims=True) + EPS) * g_ref[row] + s_ref[row]
    _store_pieces(tok_ref, h)
    hi, lo = _split_bf16(h)
    logits = (jnp.dot(hi, wr_ref[0], preferred_element_type=F32) + jnp.dot(lo, wr_ref[0], preferred_element_type=F32)
              + jnp.dot(hi, wr_ref[1], preferred_element_type=F32))
    logit_ref[...] = logits.T


def _out_proj(o_lat, o_ctx, w, x, gate2, gain2, shift2, w_router):
    t, d = x.shape
    bm = ROW_BLOCK
    dp = d // SC_ROW_SPLIT
    has_ctx = o_ctx is not None
    n_parts = len(o_lat)
    lat_map = (lambda i: (jnp.maximum(i - 1, 0), 0)) if has_ctx else (lambda i: (i, 0))
    in_specs = [pl.BlockSpec((bm, o.shape[1]), lat_map) for o in o_lat]
    args = list(o_lat)
    if has_ctx:
        in_specs += [pl.BlockSpec((bm, o.shape[1]), lambda i: (0, 0)) for o in o_ctx]
        args += list(o_ctx)
    vec_spec = pl.BlockSpec((2, 1, d), lambda i: (0, 0, 0))
    in_specs += [pl.BlockSpec(w.shape, lambda i: (0, 0)),
                 pl.BlockSpec((bm, d), lambda i: (i, 0)),
                 vec_spec, vec_spec, vec_spec,
                 pl.BlockSpec((2, d, LANES), lambda i: (0, 0, 0))]
    return pl.pallas_call(
        functools.partial(_out_proj_kernel, n_parts=n_parts, has_ctx=has_ctx),
        grid=(t // bm,),
        in_specs=in_specs,
        out_specs=[pl.BlockSpec((bm, d), lambda i: (i, 0)),
                   pl.BlockSpec((SC_ROW_SPLIT, bm, dp), lambda i: (0, i, 0)),
                   pl.BlockSpec((LANES, bm), lambda i: (0, i))],
        out_shape=[jax.ShapeDtypeStruct((t, d), F32),
                   jax.ShapeDtypeStruct((SC_ROW_SPLIT, t, dp), F32),
                   jax.ShapeDtypeStruct((LANES, t), F32)],
        compiler_params=_params("parallel"),
        name="out_proj_router",
    )(*args, w, x, gate2, gain2, shift2, w_router)


def _store_pieces(ref, rows):
    dp = ref.shape[2]
    for j in range(ref.shape[0]):
        ref[j] = rows[:, j * dp:(j + 1) * dp]


def _load_pieces(planes):
    return jnp.concatenate([planes[j] for j in range(planes.shape[0])], axis=1)


def _moe_kernel(be_ref, nu_ref, x_ref, w1_ref, w3_ref, w2_ref, y_ref):
    i = pl.program_id(0)

    @pl.when(i < nu_ref[0])
    def _():
        x = _load_pieces(x_ref[...]).astype(BF16)
        a = jnp.dot(x, w1_ref[0, 0].astype(BF16), preferred_element_type=F32)
        b = jnp.dot(x, w3_ref[0, 0].astype(BF16), preferred_element_type=F32)
        hidden = (a * jax.nn.sigmoid(a)) * b
        _store_pieces(y_ref, jnp.dot(hidden.astype(BF16), w2_ref[0, 0].astype(BF16), preferred_element_type=F32))

    @pl.when(i >= nu_ref[0])
    def _():
        y_ref[...] = jnp.zeros(y_ref.shape, y_ref.dtype)


def _moe_experts(buf, blk_e, n_used, w1, w3, w2, layer):
    pieces, n_rows, dp = buf.shape
    d, de = w1.shape[2], w1.shape[3]
    n_blk = n_rows // EXPERT_BLOCK
    row_spec = pl.BlockSpec((pieces, EXPERT_BLOCK, dp), lambda i, be, nu: (0, i, 0))
    grid_spec = pltpu.PrefetchScalarGridSpec(
        num_scalar_prefetch=2,
        grid=(n_blk,),
        in_specs=[row_spec,
                  pl.BlockSpec((1, 1, d, de), lambda i, be, nu: (layer, be[i], 0, 0)),
                  pl.BlockSpec((1, 1, d, de), lambda i, be, nu: (layer, be[i], 0, 0)),
                  pl.BlockSpec((1, 1, de, d), lambda i, be, nu: (layer, be[i], 0, 0))],
        out_specs=row_spec,
    )
    return pl.pallas_call(
        _moe_kernel,
        grid_spec=grid_spec,
        out_shape=jax.ShapeDtypeStruct((pieces, n_rows, dp), F32),
        compiler_params=_params("arbitrary"),
        name="moe_experts",
    )(blk_e, n_used, buf, w1, w3, w2)


def _sc_gather(x, idx):
    pieces, t, dp = x.shape
    n = idx.shape[0]
    flat = (jnp.arange(pieces, dtype=jnp.int32)[:, None] * t + idx[None, :]).reshape(1, pieces * n)
    mesh = plsc.VectorSubcoreMesh(core_axis_name="core", subcore_axis_name="subcore")

    @pl.kernel(out_type=jax.ShapeDtypeStruct((pieces * n, dp), x.dtype), mesh=mesh, scratch_types=[])
    def gather(x_hbm, i_hbm, o_hbm):
        def body(i_vmem, o_vmem):
            pltpu.sync_copy(x_hbm.at[i_vmem.at[0]], o_vmem)

        pltpu.emit_pipeline(
            body,
            grid=(pieces * n // SC_GATHER_WINDOW,),
            in_specs=[pl.BlockSpec((1, SC_GATHER_WINDOW), lambda i: (0, i))],
            out_specs=[pl.BlockSpec((SC_GATHER_WINDOW, dp), lambda i: (i, 0))],
            core_axis_name=("core", "subcore"),
            dimension_semantics=(pltpu.PARALLEL,),
        )(i_hbm, o_hbm)

    return gather(x.reshape(pieces * t, dp), flat).reshape(pieces, n, dp)


def _combine_kernel(y_ref, w_ref, x_ref, gate_ref, o_ref, *, row0):
    w = w_ref[...]
    f = w[:, 0:1] * _load_pieces(y_ref[:, 0]) + w[:, 1:2] * _load_pieces(y_ref[:, 1])
    row = jnp.minimum(pl.program_id(0) + row0, 1)
    o_ref[...] = x_ref[...] + gate_ref[row] * f


def _combine(y2, weights, x, gate2, has_ctx):
    t, d = x.shape
    bm = ROW_BLOCK
    pieces, _, _, dp = y2.shape
    return pl.pallas_call(
        functools.partial(_combine_kernel, row0=0 if has_ctx else 1),
        grid=(t // bm,),
        in_specs=[pl.BlockSpec((pieces, TOP_K, bm, dp), lambda i: (0, 0, i, 0)),
                  pl.BlockSpec((bm, TOP_K), lambda i: (i, 0)),
                  pl.BlockSpec((bm, d), lambda i: (i, 0)),
                  pl.BlockSpec((2, 1, d), lambda i: (0, 0, 0))],
        out_specs=pl.BlockSpec((bm, d), lambda i: (i, 0)),
        out_shape=jax.ShapeDtypeStruct((t, d), F32),
        compiler_params=_params("parallel"),
        name="moe_combine",
    )(y2, weights, x, gate2)


def _top1_rows(p):
    m = jnp.max(p, axis=0, keepdims=True)
    idx = lax.broadcasted_iota(jnp.int32, p.shape, 0)
    return m, jnp.min(jnp.where(p == m, idx, p.shape[0]), axis=0, keepdims=True)


def _prefix_rank(onehot):
    e, n = onehot.shape
    blk = EXPERT_BLOCK
    nb = n // blk
    earlier = (jnp.arange(blk)[:, None] < jnp.arange(blk)[None, :]).astype(F32)
    within = jnp.dot(onehot.reshape(e * nb, blk), earlier).reshape(e, n)
    tot = jnp.sum(onehot.reshape(e, nb, blk), axis=2)
    before = jnp.repeat(jnp.cumsum(tot, axis=1) - tot, blk, axis=1)
    return jnp.sum((within + before) * onehot, axis=0).astype(jnp.int32)


def _hier_moe(tokens, logits_t, b_group, b_expert, w1, w3, w2, layer):
    pieces, t, dp = tokens.shape
    pg = jax.nn.softmax(logits_t[:N_GROUPS] + b_group.astype(F32)[:, None], axis=0)
    g_prob, g_idx = _top1_rows(pg)
    le = (logits_t[N_GROUPS:N_GROUPS + N_EXPERTS] + b_expert.astype(F32)[:, None]).reshape(N_GROUPS, EXPERTS_PER_GROUP, t)
    group_iota = lax.broadcasted_iota(jnp.int32, (N_GROUPS, 1, t), 0)
    le = jnp.sum(jnp.where(group_iota == g_idx[None], le, 0.0), axis=0)
    pe = jax.nn.softmax(le, axis=0)
    p1, i1 = _top1_rows(pe)
    p2, i2 = _top1_rows(jnp.where(lax.broadcasted_iota(jnp.int32, pe.shape, 0) == i1, -1.0, pe))
    e_prob, e_idx = jnp.concatenate([p1, p2], axis=0), jnp.concatenate([i1, i2], axis=0)
    weights = g_prob * e_prob / jnp.sum(e_prob, axis=0, keepdims=True)
    flat_e = (g_idx * EXPERTS_PER_GROUP + e_idx).reshape(1, TOP_K * t)
    onehot = (flat_e == lax.broadcasted_iota(jnp.int32, (N_EXPERTS, TOP_K * t), 0)).astype(F32)
    rank = _prefix_rank(onehot)
    counts = jnp.sum(onehot, axis=1).astype(jnp.int32)
    padded = (counts + EXPERT_BLOCK - 1) // EXPERT_BLOCK * EXPERT_BLOCK
    pend = jnp.cumsum(padded)
    first_row = jnp.sum(onehot * (pend - padded).astype(F32)[:, None], axis=0).astype(jnp.int32)
    dest = first_row + rank
    n_rows = TOP_K * t + N_EXPERTS * EXPERT_BLOCK
    n_blk = n_rows // EXPERT_BLOCK
    src = (jnp.arange(n_rows, dtype=jnp.int32) % t).at[dest].set(jnp.arange(t * TOP_K, dtype=jnp.int32) % t)
    buf = _sc_gather(tokens, src)
    starts = jnp.arange(n_blk, dtype=jnp.int32) * EXPERT_BLOCK
    blk_e = jnp.minimum(jnp.sum(pend[None, :] <= starts[:, None], axis=1), N_EXPERTS - 1).astype(jnp.int32)
    n_used = (pend[-1] // EXPERT_BLOCK).astype(jnp.int32).reshape(1)
    yb = _moe_experts(buf, blk_e, n_used, w1, w3, w2, layer)
    return _sc_gather(yb, dest).reshape(pieces, TOP_K, t, dp), weights.T


def kernel(x, c, ctx, c_ctx, w_mod, b_mod, norm_mix, norm_ffn, w_in_ab, w_out_ab, qn_a, kn_a, sink_a, qn_b, kn_b, w_in_c, w_out_c, qn_c, kn_c, lam_c, subln_c, w_group, b_group, w_expert, b_expert, w1, w3, w2):
    b, s_lat, d = x.shape
    n_ctx = ctx.shape[1]
    assert b == 1 and n_ctx == ROW_BLOCK and s_lat % (2 * ROW_BLOCK) == 0
    depth = w_mod.shape[0]
    cos, sin = _rope_tables(s_lat, n_ctx)
    mods = _mod_vectors(c, c_ctx, w_mod, b_mod)
    xs = jnp.concatenate([ctx[0], x[0]], axis=0)
    has_ctx = True
    pending = None
    for l in range(depth):
        last = l == depth - 1
        i = l // 2
        sh1, sc1, gt1, sh2, sc2, gt2 = [mods[l, :2, j * d:(j + 1) * d].reshape(2, 1, d) for j in range(6)]
        gain1 = norm_mix[l].astype(F32) * (1 + sc1)
        if l % 2 == 0:
            xs, heads = _inproj_ab(xs, pending, gain1, sh1, w_in_ab[i].astype(BF16), qn_a[i], kn_a[i], qn_b[i], kn_b[i], cos, sin)
            o_lat, o_ctx = _mixer_ab(heads, n_ctx, sink_a[i], not last,
                                     _logits_bounded(qn_a[i], kn_a[i]), _logits_bounded(qn_b[i], kn_b[i]))
            w_out = w_out_ab[i]
        else:
            lam_init = 0.8 - 0.6 * math.exp(-0.3 * l)
            xs, heads = _inproj_c(xs, pending, gain1, sh1, w_in_c[i].astype(BF16), qn_c[i], kn_c[i], cos, sin)
            o_lat, o_ctx = _mixer_c(heads, n_ctx, lam_c[i], subln_c[i], lam_init, not last,
                                    _logits_bounded(qn_c[i], kn_c[i]))
            w_out = w_out_c[i]
        if last:
            xs, has_ctx = xs[n_ctx:], False
        gain2 = norm_ffn[l].astype(F32) * (1 + sc2)
        w_router = jnp.zeros((d, LANES), F32).at[:, :N_GROUPS].set(w_group[l]).at[:, N_GROUPS:N_GROUPS + N_EXPERTS].set(w_expert[l])
        w_router = jnp.stack(_split_bf16(w_router))
        xs, tokens, logits = _out_proj(o_lat, o_ctx, w_out.astype(BF16), xs, gt1, gain2, sh2, w_router)
        y2, weights = _hier_moe(tokens, logits, b_group[l], b_expert[l], w1, w3, w2, l)
        pending = (y2, weights, gt2)
    return _combine(y2, weights, xs, gt2, has_ctx).reshape(b, s_lat, d)
```

```python
import functools
import math

import jax
import jax.numpy as jnp
from jax import lax
from jax.experimental import pallas as pl
from jax.experimental.pallas import tpu as pltpu
from jax.experimental.pallas import tpu_sc as plsc

F32 = jnp.float32
BF16 = jnp.bfloat16

GRID_W = 64
HEAD_DIM = 64
WINDOW = 128
ROPE_THETA = 10000.0
EPS = 1e-6
NEG = -1e30
N_HEADS_A, N_KV_A = 8, 2
N_HEADS_B, N_KV_B = 8, 2
GROUP = N_HEADS_A // N_KV_A
QA_W, KVA_W = N_HEADS_A * HEAD_DIM, N_KV_A * HEAD_DIM
QB_W, KVB_W = N_HEADS_B * HEAD_DIM, N_KV_B * HEAD_DIM
N_HEADS_C = 8
DV_C = 2 * HEAD_DIM
N_GROUPS, EXPERTS_PER_GROUP, TOP_K = 4, 8, 2
N_EXPERTS = N_GROUPS * EXPERTS_PER_GROUP
LANES = 128
ROW_BLOCK = 256
EXPERT_BLOCK = 512
SC_GATHER_WINDOW = 128
SC_ROW_SPLIT = 4
KEY_BLOCK = 3328
GQA_V_ROWS = 128
DIFF_V_ROWS = 144
VMEM_LIMIT = 48 * 1024 * 1024
LOG2E = math.log2(math.e)
Q_SCALE = HEAD_DIM ** -0.5 * LOG2E
LOG2_LOGIT_BOUND = 60.0


def _params(*sem):
    return pltpu.CompilerParams(dimension_semantics=sem, vmem_limit_bytes=VMEM_LIMIT)


def _lane(shape):
    return lax.broadcasted_iota(jnp.int32, shape, 1)


def _mod_vec_kernel(a_ref, w_ref, b_ref, o_ref):
    a = a_ref[...]
    a = a * jax.nn.sigmoid(a)
    o_ref[0] = jnp.dot(a, w_ref[0], preferred_element_type=F32, precision=lax.Precision.HIGHEST) + b_ref[0]


def _mod_vectors(c, c_ctx, w_mod, b_mod):
    depth, d, n = w_mod.shape
    a = jnp.zeros((8, d), F32).at[0].set(c_ctx).at[1].set(c[0])
    bn = 1024
    return pl.pallas_call(
        _mod_vec_kernel,
        grid=(depth, n // bn),
        in_specs=[pl.BlockSpec((8, d), lambda l, j: (0, 0)),
                  pl.BlockSpec((1, d, bn), lambda l, j: (l, 0, j)),
                  pl.BlockSpec((1, 1, bn), lambda l, j: (l, 0, j))],
        out_specs=pl.BlockSpec((1, 8, bn), lambda l, j: (l, 0, j)),
        out_shape=jax.ShapeDtypeStruct((depth, 8, n), F32),
        compiler_params=_params("parallel", "parallel"),
        name="mod_vectors",
    )(a, w_mod, b_mod.reshape(depth, 1, n))


def _modulated(x, g_ref, s_ref):
    y = x * lax.rsqrt(jnp.mean(x * x, axis=-1, keepdims=True) + EPS)
    row = jnp.minimum(pl.program_id(0), 1)
    return y * g_ref[row] + s_ref[row]


def _split_bf16(x):
    top = lax.bitcast_convert_type(lax.bitcast_convert_type(x, jnp.uint32) & jnp.uint32(0xFFFF0000), F32)
    return top.astype(BF16), (x - top).astype(BF16)


def _head_norm_rope(x, gain, cos, sin, seg_mean):
    hi, lo = _split_bf16(x * x)
    ms = jnp.dot(hi, seg_mean, preferred_element_type=F32) + jnp.dot(lo, seg_mean, preferred_element_type=F32)
    y = x * lax.rsqrt(ms + EPS) * gain
    first_half = (_lane(y.shape) & (HEAD_DIM - 1)) < HEAD_DIM // 2
    partner = jnp.where(first_half, pltpu.roll(y, LANES - HEAD_DIM // 2, 1), pltpu.roll(y, HEAD_DIM // 2, 1))
    return y * cos + partner * sin


def _store_value_heads(v_ref, v):
    lane = _lane(v.shape)
    ones_col = jnp.where(lane == HEAD_DIM, 1.0, 0.0)
    v_ref[0] = jnp.where(lane < HEAD_DIM, v, ones_col).astype(BF16)
    v_ref[1] = jnp.where(lane < HEAD_DIM, pltpu.roll(v, HEAD_DIM, 1), ones_col).astype(BF16)


def _ones_row_block(rows, cols):
    return jnp.where(lax.broadcasted_iota(jnp.int32, (rows, cols), 0) == 0, 1.0, 0.0)


def _store_value_heads_t(vt_ref, v):
    vt = v.T
    tail = _ones_row_block(GQA_V_ROWS - HEAD_DIM, v.shape[0])
    vt_ref[0, 0] = jnp.concatenate([vt[:HEAD_DIM], tail], axis=0).astype(BF16)
    vt_ref[1, 0] = jnp.concatenate([vt[HEAD_DIM:], tail], axis=0).astype(BF16)


N_INPROJ_INPUTS = 8


def _inproj_rows(refs, fused):
    if not fused:
        return refs[0][...], refs[1:N_INPROJ_INPUTS], refs[N_INPROJ_INPUTS:]
    y_ref, wts_ref, gate_ref, x_ref = refs[:4]
    ins, outs = refs[4:N_INPROJ_INPUTS + 3], refs[N_INPROJ_INPUTS + 3:]
    wts = wts_ref[...]
    f = wts[:, 0:1] * _load_pieces(y_ref[:, 0]) + wts[:, 1:2] * _load_pieces(y_ref[:, 1])
    x = x_ref[...] + gate_ref[jnp.minimum(pl.program_id(0), 1)] * f
    outs[0][...] = x
    return x, ins, outs[1:]


def _inproj_ab_kernel(*refs, fused):
    x, (g_ref, s_ref, w_ref, hg_ref, cos_ref, sin_ref, seg_ref), outs = _inproj_rows(refs, fused)
    qa_ref, qb_ref, ka_ref, kb_ref, va_ref, vat_ref, vbt_ref = outs
    bm = x.shape[0]
    h = _modulated(x, g_ref, s_ref).astype(BF16)
    proj = jnp.dot(h, w_ref[...], preferred_element_type=F32)
    cos, sin, seg = cos_ref[...], sin_ref[...], seg_ref[...]
    low = _lane((bm, LANES)) < HEAD_DIM

    def normed(j):
        cols = slice(j * LANES, (j + 1) * LANES)
        return _head_norm_rope(proj[:, cols], hg_ref[:, cols], cos, sin, seg)

    def store_queries(q_ref, tile0):
        for j in range(N_HEADS_A // 2):
            y = normed(tile0 + j)
            swapped = pltpu.roll(y, HEAD_DIM, 1)
            kv, g0 = (2 * j) // GROUP, (2 * j) % GROUP
            if kv == 0:
                even, odd = jnp.where(low, y, 0.0), jnp.where(low, swapped, 0.0)
            else:
                even, odd = jnp.where(low, 0.0, swapped), jnp.where(low, 0.0, y)
            q_ref[kv, 0, g0 * bm:(g0 + 1) * bm, :] = even.astype(BF16)
            q_ref[kv, 0, (g0 + 1) * bm:(g0 + 2) * bm, :] = odd.astype(BF16)

    tq = QA_W // LANES
    store_queries(qa_ref, 0)
    ka_ref[0] = normed(tq).astype(BF16)
    va = proj[:, (tq + 1) * LANES:(tq + 2) * LANES]
    _store_value_heads(va_ref, va)
    _store_value_heads_t(vat_ref, va)
    store_queries(qb_ref, tq + 2)
    kb_ref[0] = normed(2 * tq + 2).astype(BF16)
    _store_value_heads_t(vbt_ref, proj[:, (2 * tq + 3) * LANES:(2 * tq + 4) * LANES])


def _inproj_c_kernel(*refs, fused):
    x, (g_ref, s_ref, w_ref, hg_ref, cos_ref, sin_ref, seg_ref), (q_ref, k_ref, v_ref) = _inproj_rows(refs, fused)
    bm = x.shape[0]
    h = _modulated(x, g_ref, s_ref).astype(BF16)
    proj = jnp.dot(h, w_ref[...], preferred_element_type=F32)
    cos, sin, seg = cos_ref[...], sin_ref[...], seg_ref[...]
    low = _lane((bm, LANES)) < HEAD_DIM
    tail = _ones_row_block(DIFF_V_ROWS - DV_C, bm)

    def normed(j):
        cols = slice(j * LANES, (j + 1) * LANES)
        return _head_norm_rope(proj[:, cols], hg_ref[:, cols], cos, sin, seg)

    for j in range(N_HEADS_C):
        y = normed(j)
        q_ref[j, 0, 0:bm, :] = jnp.where(low, y, 0.0).astype(BF16)
        q_ref[j, 0, bm:2 * bm, :] = jnp.where(low, 0.0, y).astype(BF16)
        k_ref[j] = normed(N_HEADS_C + j).astype(BF16)
        v = proj[:, (2 * N_HEADS_C + j) * LANES:(2 * N_HEADS_C + j + 1) * LANES]
        v_ref[j, 0] = jnp.concatenate([v.T, tail], axis=0).astype(BF16)


def _rope_tables(seq, n_ctx):
    rows_n = seq // GRID_W
    rows = jnp.broadcast_to(jnp.arange(rows_n, dtype=F32)[:, None], (rows_n, GRID_W)).reshape(-1)
    cols = jnp.broadcast_to(jnp.arange(GRID_W, dtype=F32)[None, :], (rows_n, GRID_W)).reshape(-1)
    half = HEAD_DIM // 2
    inv = ROPE_THETA ** (-jnp.arange(0, half, 2, dtype=F32) / half)
    ang = jnp.concatenate([rows[:, None] * inv, cols[:, None] * inv], axis=-1)
    reps = LANES // half
    sign = jnp.tile(jnp.concatenate([-jnp.ones((half,), F32), jnp.ones((half,), F32)]), LANES // HEAD_DIM)
    cos = jnp.pad(jnp.tile(jnp.cos(ang), (1, reps)), ((n_ctx, 0), (0, 0)), constant_values=1.0)
    sin = jnp.pad(jnp.tile(jnp.sin(ang), (1, reps)) * sign, ((n_ctx, 0), (0, 0)))
    return cos, sin


def _segment_mean_matrix():
    idx = jnp.arange(LANES) // HEAD_DIM
    return jnp.where(idx[:, None] == idx[None, :], 1.0 / HEAD_DIM, 0.0).astype(BF16)


def _q_slot(i, nblk):
    return (i + nblk - 1) % nblk


def _inproj_common_specs(t, d, n):
    bm = ROW_BLOCK
    return [pl.BlockSpec((bm, d), lambda i: (i, 0)),
            pl.BlockSpec((2, 1, d), lambda i: (0, 0, 0)),
            pl.BlockSpec((2, 1, d), lambda i: (0, 0, 0)),
            pl.BlockSpec((d, n), lambda i: (0, 0)),
            pl.BlockSpec((1, n), lambda i: (0, 0)),
            pl.BlockSpec((bm, LANES), lambda i: (i, 0)),
            pl.BlockSpec((bm, LANES), lambda i: (i, 0)),
            pl.BlockSpec((LANES, LANES), lambda i: (0, 0))]


def _inproj_call(kernel_fn, name, x, pending, args, n, out_specs, out_shape):
    t, d = x.shape
    bm = ROW_BLOCK
    in_specs = _inproj_common_specs(t, d, n)
    args = [x] + list(args)
    fused = pending is not None
    if fused:
        y2, wts, gate2 = pending
        pieces, _, _, dp = y2.shape
        in_specs = [pl.BlockSpec((pieces, TOP_K, bm, dp), lambda i: (0, 0, i, 0)),
                    pl.BlockSpec((bm, TOP_K), lambda i: (i, 0)),
                    pl.BlockSpec((2, 1, d), lambda i: (0, 0, 0))] + in_specs
        args = [y2, wts, gate2] + args
        out_specs = [pl.BlockSpec((bm, d), lambda i: (i, 0))] + list(out_specs)
        out_shape = [jax.ShapeDtypeStruct((t, d), F32)] + list(out_shape)
    outs = pl.pallas_call(
        functools.partial(kernel_fn, fused=fused),
        grid=(t // bm,),
        in_specs=in_specs,
        out_specs=out_specs,
        out_shape=out_shape,
        compiler_params=_params("parallel"),
        name=name,
    )(*args)
    return (outs[0], outs[1:]) if fused else (x, outs)


def _inproj_ab(x, pending, gain2, shift2, w, qn_a, kn_a, qn_b, kn_b, cos, sin):
    t, d = x.shape
    n = w.shape[1]
    bm, nblk = ROW_BLOCK, t // ROW_BLOCK
    tile = lambda g, reps: jnp.tile(g.astype(F32), reps)
    ones_v = jnp.ones((KVA_W,), F32)
    hg = jnp.concatenate([tile(qn_a, N_HEADS_A) * Q_SCALE, tile(kn_a, N_KV_A), ones_v,
                          tile(qn_b, N_HEADS_B) * Q_SCALE, tile(kn_b, N_KV_B), ones_v]).reshape(1, n)
    q_shape = jax.ShapeDtypeStruct((N_KV_A, nblk, GROUP * bm, LANES), BF16)
    k_shape = jax.ShapeDtypeStruct((1, t, LANES), BF16)
    v_shape = jax.ShapeDtypeStruct((N_KV_A, t, LANES), BF16)
    vt_shape = jax.ShapeDtypeStruct((N_KV_A, nblk, GQA_V_ROWS, bm), BF16)
    q_spec = pl.BlockSpec((N_KV_A, 1, GROUP * bm, LANES), lambda i: (0, _q_slot(i, nblk), 0, 0))
    k_spec = pl.BlockSpec((1, bm, LANES), lambda i: (0, i, 0))
    v_spec = pl.BlockSpec((N_KV_A, bm, LANES), lambda i: (0, i, 0))
    vt_spec = pl.BlockSpec((N_KV_A, 1, GQA_V_ROWS, bm), lambda i: (0, i, 0, 0))
    return _inproj_call(_inproj_ab_kernel, "inproj_ab", x, pending,
                        [gain2, shift2, w, hg, cos, sin, _segment_mean_matrix()], n,
                        [q_spec, q_spec, k_spec, k_spec, v_spec, vt_spec, vt_spec],
                        [q_shape, q_shape, k_shape, k_shape, v_shape, vt_shape, vt_shape])


def _inproj_c(x, pending, gain2, shift2, w, qn, kn, cos, sin):
    t, d = x.shape
    n = w.shape[1]
    bm, nblk = ROW_BLOCK, t // ROW_BLOCK
    h = N_HEADS_C
    tile = lambda g: jnp.tile(g.astype(F32), 2 * h)
    hg = jnp.concatenate([tile(qn) * Q_SCALE, tile(kn), jnp.ones((h * DV_C,), F32)]).reshape(1, n)
    return _inproj_call(_inproj_c_kernel, "inproj_c", x, pending,
                        [gain2, shift2, w, hg, cos, sin, _segment_mean_matrix()], n,
                        [pl.BlockSpec((h, 1, 2 * bm, LANES), lambda i: (0, _q_slot(i, nblk), 0, 0)),
                         pl.BlockSpec((h, bm, LANES), lambda i: (0, i, 0)),
                         pl.BlockSpec((h, 1, DIFF_V_ROWS, bm), lambda i: (0, i, 0, 0))],
                        [jax.ShapeDtypeStruct((h, nblk, 2 * bm, LANES), BF16),
                         jax.ShapeDtypeStruct((h, t, LANES), BF16),
                         jax.ShapeDtypeStruct((h, nblk, DIFF_V_ROWS, bm), BF16)])


def _merge_gqa_heads(o, bq):
    low = _lane((bq, LANES)) < HEAD_DIM
    pairs = [jnp.where(low, o[g * bq:(g + 1) * bq], pltpu.roll(o[(g + 1) * bq:(g + 2) * bq], HEAD_DIM, 1))
             for g in range(0, GROUP, 2)]
    return jnp.concatenate(pairs, axis=1)


def _flash_kernel(*refs, mode, online, n_keys, bk, dv, l0):
    refs = list(refs)
    q_ref, k_ref, vt_ref = refs[:3]
    pos = 3
    if online:
        m0_ref = refs[pos]
        pos += 1
    if mode == "diff":
        lam_ref, sub_ref = refs[pos:pos + 2]
        pos += 2
    o_ref = refs[pos]
    acc_sc = refs[pos + 1]
    m_sc = refs[pos + 2] if online else None

    nsub, rb = q_ref.shape[1], q_ref.shape[2]
    r = nsub * rb
    ch = vt_ref.shape[3]
    q = q_ref[0].reshape(r, LANES)
    acc_sc[...] = jnp.where(lax.broadcasted_iota(jnp.int32, acc_sc.shape, 0) == dv, l0, 0.0).astype(F32)
    if online:
        m_sc[...] = m0_ref[0]

    def block(start, size):
        kb = k_ref[0, pl.ds(start, size), :]
        st = lax.dot_general(kb, q, (((1,), (1,)), ((), ())), preferred_element_type=F32)
        if online:
            m_prev = m_sc[...]
            m_new = jnp.maximum(m_prev, jnp.max(st, axis=0, keepdims=True))
            pt = jnp.exp2(st - m_new).astype(BF16)
            acc = jnp.exp2(m_prev - m_new) * acc_sc[...]
            m_sc[...] = m_new
        else:
            pt = jnp.exp2(st).astype(BF16)
            acc = acc_sc[...]
        c0 = start // ch
        for c in range(size // ch):
            acc = acc + jnp.dot(vt_ref[0, c0 + c], pt[c * ch:(c + 1) * ch], preferred_element_type=F32)
        acc_sc[...] = acc

    n_full, tail = n_keys // bk, n_keys % bk
    if n_full:
        def body(i, carry):
            block(pl.multiple_of(i * bk, bk), bk)
            return carry
        lax.fori_loop(0, n_full, body, 0)
    if tail:
        block(n_full * bk, tail)

    acc = acc_sc[...]
    ot = acc[:LANES] / acc[dv:dv + 1]
    bq = rb // (GROUP if mode == "gqa" else 2)
    o = jnp.concatenate([ot[:, j * bq:(j + 1) * bq].T for j in range(r // bq)], axis=0)
    if mode == "gqa":
        o_ref[...] = _merge_gqa_heads(o, bq).astype(o_ref.dtype)
    else:
        for b in range(nsub):
            d = o[b * rb:b * rb + bq] - lam_ref[...] * o[b * rb + bq:(b + 1) * rb]
            y = d * lax.rsqrt(jnp.mean(d * d, axis=-1, keepdims=True) + EPS) * sub_ref[...]
            o_ref[b * bq:(b + 1) * bq, :] = y.astype(o_ref.dtype)


def _flash_call(q, k, vt, extra, *, mode, online, n_keys, slot0, nsub, n_steps, l0=0.0):
    hkv, _, rb, _ = q.shape
    hk = k.shape[0]
    _, _, dvr, ch = vt.shape
    dv = HEAD_DIM if mode == "gqa" else DV_C
    bq = rb // (GROUP if mode == "gqa" else 2)
    ocols = GROUP * HEAD_DIM if mode == "gqa" else DV_C
    r = nsub * rb
    bk = min(KEY_BLOCK, n_keys)
    in_specs = [pl.BlockSpec((1, nsub, rb, LANES), lambda h, i: (h, slot0 // nsub + i, 0, 0)),
                pl.BlockSpec((1, n_keys, LANES), (lambda h, i: (h, 0, 0)) if hk > 1 else (lambda h, i: (0, 0, 0))),
                pl.BlockSpec((1, n_keys // ch, dvr, ch), lambda h, i: (h, 0, 0, 0))]
    args = [q, k, vt]
    scratch = [pltpu.VMEM((dvr, r), F32)]
    if online:
        m0 = extra.pop(0)
        in_specs.append(pl.BlockSpec((1, 1, r), lambda h, i: (h, 0, 0)))
        args.append(m0)
        scratch.append(pltpu.VMEM((1, r), F32))
    for a in extra:
        in_specs.append(pl.BlockSpec(a.shape, lambda h, i: (0, 0)))
        args.append(a)
    return pl.pallas_call(
        functools.partial(_flash_kernel, mode=mode, online=online, n_keys=n_keys, bk=bk, dv=dv, l0=l0),
        grid=(hkv, n_steps),
        in_specs=in_specs,
        out_specs=pl.BlockSpec((nsub * bq, ocols), lambda h, i: (i, h)),
        out_shape=jax.ShapeDtypeStruct((n_steps * nsub * bq, hkv * ocols), BF16),
        scratch_shapes=scratch,
        compiler_params=_params("parallel", "parallel"),
        name="flash_online" if online else "flash_bounded",
    )(*args)


def _logits_bounded(q_gain, k_gain):
    bound = HEAD_DIM * Q_SCALE * 1.02 * jnp.max(jnp.abs(q_gain.astype(F32))) * jnp.max(jnp.abs(k_gain.astype(F32)))
    return bound <= LOG2_LOGIT_BOUND


def _attend(q, k, v, extra, bounded, **kw):
    hkv, r = q.shape[0], kw["nsub"] * q.shape[2]
    fast = lambda q_, k_, v_, *e: _flash_call(q_, k_, v_, list(e), online=False, **kw)
    safe = lambda q_, k_, v_, *e: _flash_call(q_, k_, v_, [jnp.full((hkv, 1, r), NEG, F32)] + list(e), online=True, **kw)
    return lax.cond(bounded, fast, safe, q, k, v, *extra)


def _window_kernel(q_ref, k_ref, v_ref, sink_ref, o_ref, *, bq, n_ctx):
    q = q_ref[0, 0]
    r = q.shape[0]
    t = k_ref.shape[1]
    w = bq + 2 * WINDOW
    q0 = pl.program_id(1) * bq
    ws = pl.multiple_of(jnp.clip(n_ctx + q0 - WINDOW, 0, t - w), WINDOW)
    kw = k_ref[0, pl.ds(ws, w), :]
    vw = v_ref[0, pl.ds(ws, w), :]
    contract_last = (((1,), (1,)), ((), ()))
    s_loc = lax.dot_general(q, kw, contract_last, preferred_element_type=F32)
    qpos = q0 + (lax.broadcasted_iota(jnp.int32, (r, w), 0) & (bq - 1))
    kpos = ws - n_ctx + lax.broadcasted_iota(jnp.int32, (r, w), 1)
    mask = (kpos >= 0) & (kpos - qpos <= WINDOW) & (qpos - kpos <= WINDOW)
    s_loc = jnp.where(mask, s_loc, NEG)
    s_ctx = lax.dot_general(q, k_ref[0, 0:n_ctx, :], contract_last, preferred_element_type=F32)
    sink = sink_ref[0]
    m = jnp.maximum(sink, jnp.maximum(jnp.max(s_loc, axis=-1, keepdims=True), jnp.max(s_ctx, axis=-1, keepdims=True)))
    p_loc = jnp.exp2(s_loc - m)
    p_ctx = jnp.exp2(s_ctx - m)
    l = jnp.exp2(sink - m) + jnp.sum(p_loc, axis=-1, keepdims=True) + jnp.sum(p_ctx, axis=-1, keepdims=True)
    o = (jnp.dot(p_loc.astype(BF16), vw, preferred_element_type=F32)
         + jnp.dot(p_ctx.astype(BF16), v_ref[0, 0:n_ctx, :], preferred_element_type=F32))
    o_ref[...] = _merge_gqa_heads(o / l, bq).astype(o_ref.dtype)


def _window_attention(q, k, v, sink_rows, n_ctx):
    hkv, slots, r, _ = q.shape
    t = k.shape[1]
    bq = r // GROUP
    nq = slots - n_ctx // bq
    return pl.pallas_call(
        functools.partial(_window_kernel, bq=bq, n_ctx=n_ctx),
        grid=(hkv, nq),
        in_specs=[pl.BlockSpec((1, 1, r, LANES), lambda h, i: (h, i, 0, 0)),
                  pl.BlockSpec((1, t, LANES), lambda h, i: (0, 0, 0)),
                  pl.BlockSpec((1, t, LANES), lambda h, i: (h, 0, 0)),
                  pl.BlockSpec((1, r, 1), lambda h, i: (h, 0, 0))],
        out_specs=pl.BlockSpec((bq, GROUP * HEAD_DIM), lambda h, i: (i, h)),
        out_shape=jax.ShapeDtypeStruct((nq * bq, hkv * GROUP * HEAD_DIM), BF16),
        compiler_params=_params("parallel", "parallel"),
        name="window_attention",
    )(q, k, v, sink_rows)


def _window_bounded_kernel(q_ref, k_ref, vt_ref, mask_ref, sink_ref, o_ref, *, bq):
    q = q_ref[0, 0]
    ch = vt_ref.shape[3]
    cw = jnp.clip(pl.program_id(1), 0, vt_ref.shape[1] - 3)
    contract_last = (((1,), (1,)), ((), ()))
    k_win = k_ref[0, pl.ds(pl.multiple_of(cw * ch, ch), 3 * ch), :]
    st_win = lax.dot_general(k_win, q, contract_last, preferred_element_type=F32)
    st_ctx = lax.dot_general(k_ref[0, 0:ch, :], q, contract_last, preferred_element_type=F32)
    pt_win = jnp.exp2(st_win).astype(BF16) * mask_ref[0]
    acc = jnp.dot(vt_ref[0, 0], jnp.exp2(st_ctx).astype(BF16), preferred_element_type=F32)
    for c in range(3):
        acc = acc + jnp.dot(vt_ref[0, cw + c], pt_win[c * ch:(c + 1) * ch], preferred_element_type=F32)
    ot = acc / (acc[HEAD_DIM:HEAD_DIM + 1] + jnp.exp2(sink_ref[0]))
    o = jnp.concatenate([ot[:, g * bq:(g + 1) * bq].T for g in range(GROUP)], axis=0)
    o_ref[...] = _merge_gqa_heads(o, bq).astype(o_ref.dtype)


def _window_masks(bq, r):
    c = jnp.arange(3 * bq)[:, None]
    off = c - (jnp.arange(r)[None, :] & (bq - 1))
    centred = (off >= bq - WINDOW) & (off <= bq + WINDOW)
    shifted = (off >= 2 * bq - WINDOW) & (off <= 2 * bq + WINDOW)
    return jnp.stack([centred & (c >= bq), centred, shifted]).astype(BF16)


def _window_attention_bounded(q, k, vt, sink_cols, n_ctx):
    hkv, slots, r, _ = q.shape
    t = k.shape[1]
    bq = r // GROUP
    nblk = t // bq
    nq = nblk - 1
    assert n_ctx == bq and nblk >= 3
    return pl.pallas_call(
        functools.partial(_window_bounded_kernel, bq=bq),
        grid=(hkv, nq),
        in_specs=[pl.BlockSpec((1, 1, r, LANES), lambda h, i: (h, i, 0, 0)),
                  pl.BlockSpec((1, t, LANES), lambda h, i: (0, 0, 0)),
                  pl.BlockSpec((1, nblk, GQA_V_ROWS, bq), lambda h, i: (h, 0, 0, 0)),
                  pl.BlockSpec((1, 3 * bq, r), lambda h, i: (jnp.where(i == 0, 0, jnp.where(i >= nblk - 2, 2, 1)), 0, 0)),
                  pl.BlockSpec((1, 1, r), lambda h, i: (h, 0, 0))],
        out_specs=pl.BlockSpec((bq, GROUP * HEAD_DIM), lambda h, i: (i, h)),
        out_shape=jax.ShapeDtypeStruct((nq * bq, hkv * GROUP * HEAD_DIM), BF16),
        compiler_params=_params("parallel", "parallel"),
        name="window_bounded",
    )(q, k, vt, _window_masks(bq, r), sink_cols)


def _sink_rows(sink, bq):
    hkv, g = sink.shape
    return jnp.broadcast_to(sink.astype(F32)[:, :, None] * LOG2E, (hkv, g, bq)).reshape(hkv, g * bq, 1)


def _mixer_ab(heads, n_ctx, sink_a, with_ctx, bounded_a, bounded):
    qa, qb, ka, kb, va, vat, vbt = heads
    t = ka.shape[1]
    bq = ROW_BLOCK
    n_lat = (t - n_ctx) // bq
    sink = _sink_rows(sink_a.reshape(N_KV_A, GROUP), bq)
    sink_cols = sink.reshape(N_KV_A, 1, GROUP * bq)
    bounded_a = bounded_a & (jnp.max(jnp.abs(sink)) <= LOG2_LOGIT_BOUND)
    oa = lax.cond(bounded_a,
                  lambda: _window_attention_bounded(qa, ka, vat, sink_cols, n_ctx),
                  lambda: _window_attention(qa, ka, va, sink, n_ctx))
    ob = _attend(qb, kb, vbt, [], bounded, mode="gqa", n_keys=t, slot0=0, nsub=1, n_steps=n_lat)
    if not with_ctx:
        return (oa, ob), None
    oca = _flash_call(qa, ka, vat, [sink_cols], mode="gqa", online=True, n_keys=n_ctx,
                      slot0=n_lat, nsub=1, n_steps=1, l0=1.0)
    ocb = _attend(qb, kb, vbt, [], bounded, mode="gqa", n_keys=n_ctx, slot0=n_lat, nsub=1, n_steps=1)
    return (oa, ob), (oca, ocb)


def _mixer_c(heads, n_ctx, lam_p, subln, lam_init, with_ctx, bounded):
    q, k, v = heads
    t = k.shape[1]
    n_lat = (t - n_ctx) // ROW_BLOCK
    lp = lam_p.astype(F32)
    lam = jnp.exp(jnp.sum(lp[0] * lp[1])) - jnp.exp(jnp.sum(lp[2] * lp[3])) + lam_init
    extra = [jnp.full((1, DV_C), lam, F32), (subln.astype(F32) * (1 - lam_init)).reshape(1, DV_C)]
    nsub = 2 if n_lat % 2 == 0 else 1
    o_lat = _attend(q, k, v, extra, bounded, mode="diff", n_keys=t, slot0=0, nsub=nsub, n_steps=n_lat // nsub)
    if not with_ctx:
        return (o_lat,), None
    o_ctx = _attend(q, k, v, extra, bounded, mode="diff", n_keys=n_ctx, slot0=n_lat, nsub=1, n_steps=1)
    return (o_lat,), (o_ctx,)


def _out_proj_kernel(*refs, n_parts, has_ctx):
    lat = refs[:n_parts]
    ctx = refs[n_parts:2 * n_parts] if has_ctx else None
    w_ref, x_ref, gate_ref, g_ref, s_ref, wr_ref, y_ref, tok_ref, logit_ref = refs[-9:]
    is_ctx = pl.program_id(0) == 0 if has_ctx else False
    row = jnp.where(is_ctx, 0, 1) if has_ctx else 1
    acc = None
    col = 0
    for p in range(n_parts):
        o = lat[p][...]
        if has_ctx:
            o = jnp.where(is_ctx, ctx[p][...], o)
        width = o.shape[1]
        part = jnp.dot(o, w_ref[col:col + width, :], preferred_element_type=F32)
        acc = part if acc is None else acc + part
        col += width
    x1 = x_ref[...] + gate_ref[row] * acc
    y_ref[...] = x1
    h = x1 * lax.rsqrt(jnp.mean(x1 * x1, axis=-1, keepdims=True) + EPS) * g_ref[row] + s_ref[row]
    _store_pieces(tok_ref, h)
    hi, lo = _split_bf16(h)
    logits = (jnp.dot(hi, wr_ref[0], preferred_element_type=F32) + jnp.dot(lo, wr_ref[0], preferred_element_type=F32)
              + jnp.dot(hi, wr_ref[1], preferred_element_type=F32))
    logit_ref[...] = logits.T


def _out_proj(o_lat, o_ctx, w, x, gate2, gain2, shift2, w_router):
    t, d = x.shape
    bm = ROW_BLOCK
    dp = d // SC_ROW_SPLIT
    has_ctx = o_ctx is not None
    n_parts = len(o_lat)
    lat_map = (lambda i: (jnp.maximum(i - 1, 0), 0)) if has_ctx else (lambda i: (i, 0))
    in_specs = [pl.BlockSpec((bm, o.shape[1]), lat_map) for o in o_lat]
    args = list(o_lat)
    if has_ctx:
        in_specs += [pl.BlockSpec((bm, o.shape[1]), lambda i: (0, 0)) for o in o_ctx]
        args += list(o_ctx)
    vec_spec = pl.BlockSpec((2, 1, d), lambda i: (0, 0, 0))
    in_specs += [pl.BlockSpec(w.shape, lambda i: (0, 0)),
                 pl.BlockSpec((bm, d), lambda i: (i, 0)),
                 vec_spec, vec_spec, vec_spec,
                 pl.BlockSpec((2, d, LANES), lambda i: (0, 0, 0))]
    return pl.pallas_call(
        functools.partial(_out_proj_kernel, n_parts=n_parts, has_ctx=has_ctx),
        grid=(t // bm,),
        in_specs=in_specs,
        out_specs=[pl.BlockSpec((bm, d), lambda i: (i, 0)),
                   pl.BlockSpec((SC_ROW_SPLIT, bm, dp), lambda i: (0, i, 0)),
                   pl.BlockSpec((LANES, bm), lambda i: (0, i))],
        out_shape=[jax.ShapeDtypeStruct((t, d), F32),
                   jax.ShapeDtypeStruct((SC_ROW_SPLIT, t, dp), F32),
                   jax.ShapeDtypeStruct((LANES, t), F32)],
        compiler_params=_params("parallel"),
        name="out_proj_router",
    )(*args, w, x, gate2, gain2, shift2, w_router)


def _store_pieces(ref, rows):
    dp = ref.shape[2]
    for j in range(ref.shape[0]):
        ref[j] = rows[:, j * dp:(j + 1) * dp]


def _load_pieces(planes):
    return jnp.concatenate([planes[j] for j in range(planes.shape[0])], axis=1)


def _moe_kernel(blk_ref, exp_ref, lo_ref, hi_ref, n_ref, x_ref, w1_ref, w3_ref, w2_ref, y_ref):
    i = pl.program_id(0)

    @pl.when(i < n_ref[0])
    def _():
        x = _load_pieces(x_ref[...]).astype(BF16)
        a = jnp.dot(x, w1_ref[0, 0].astype(BF16), preferred_element_type=F32)
        b = jnp.dot(x, w3_ref[0, 0].astype(BF16), preferred_element_type=F32)
        hidden = (a * jax.nn.sigmoid(a)) * b
        y = jnp.dot(hidden.astype(BF16), w2_ref[0, 0].astype(BF16), preferred_element_type=F32)
        rows = lax.broadcasted_iota(jnp.int32, (y.shape[0], 1), 0)
        y = jnp.where((rows >= lo_ref[i]) & (rows < hi_ref[i]), y, 0.0)
        first = (i == 0) | (blk_ref[i] != blk_ref[jnp.maximum(i - 1, 0)])

        @pl.when(first)
        def _():
            _store_pieces(y_ref, y)

        @pl.when(jnp.logical_not(first))
        def _():
            _store_pieces(y_ref, _load_pieces(y_ref[...]) + y)


def _moe_experts(buf, items, w1, w3, w2, layer):
    pieces, n_rows, dp = buf.shape
    d, de = w1.shape[2], w1.shape[3]
    n_items = items[0].shape[0]
    row_spec = pl.BlockSpec((pieces, EXPERT_BLOCK, dp), lambda i, blk, ex, lo, hi, n: (0, blk[i], 0))
    w_map = lambda i, blk, ex, lo, hi, n: (layer, ex[i], 0, 0)
    grid_spec = pltpu.PrefetchScalarGridSpec(
        num_scalar_prefetch=5,
        grid=(n_items,),
        in_specs=[row_spec,
                  pl.BlockSpec((1, 1, d, de), w_map),
                  pl.BlockSpec((1, 1, d, de), w_map),
                  pl.BlockSpec((1, 1, de, d), w_map)],
        out_specs=row_spec,
    )
    return pl.pallas_call(
        _moe_kernel,
        grid_spec=grid_spec,
        out_shape=jax.ShapeDtypeStruct((pieces, n_rows, dp), F32),
        compiler_params=_params("arbitrary"),
        name="moe_experts",
    )(*items, buf, w1, w3, w2)


def _expert_items(counts, n_rows):
    e = counts.shape[0]
    n_blk = n_rows // EXPERT_BLOCK
    n_items = n_blk + e - 1
    end = jnp.cumsum(counts)
    start = end - counts
    first_blk = start // EXPERT_BLOCK
    per_expert = jnp.where(counts > 0, (end - 1) // EXPERT_BLOCK - first_blk + 1, 0)
    cum = jnp.cumsum(per_expert)
    total = cum[-1]
    w = jnp.minimum(jnp.arange(n_items, dtype=jnp.int32), total - 1)
    onehot = ((cum - per_expert)[None, :] <= w[:, None]) & (w[:, None] < cum[None, :])
    pick = lambda tab: jnp.sum(jnp.where(onehot, tab[None, :], 0), axis=1).astype(jnp.int32)
    expert = pick(jnp.arange(e, dtype=jnp.int32))
    blk = pick(first_blk) + w - pick(cum - per_expert)
    lo = jnp.clip(pick(start) - blk * EXPERT_BLOCK, 0, EXPERT_BLOCK)
    hi = jnp.clip(pick(end) - blk * EXPERT_BLOCK, 0, EXPERT_BLOCK)
    return blk, expert, lo, hi, total.astype(jnp.int32).reshape(1)


def _sc_scatter(x, pos):
    pieces, t, dp = x.shape
    n = pos.shape[0]
    kk = n // t
    offs = jnp.arange(pieces, dtype=jnp.int32)[:, None] * n
    idx = [(offs + pos[k * t:(k + 1) * t][None, :]).reshape(1, pieces * t) for k in range(kk)]
    mesh = plsc.VectorSubcoreMesh(core_axis_name="core", subcore_axis_name="subcore")

    @pl.kernel(out_type=jax.ShapeDtypeStruct((pieces * n, dp), x.dtype), mesh=mesh, scratch_types=[])
    def scatter(x_hbm, *refs):
        i_hbm, o_hbm = refs[:kk], refs[kk]

        def body(x_vmem, *i_vmem):
            for iv in i_vmem:
                pltpu.sync_copy(x_vmem, o_hbm.at[iv.at[0]])

        pltpu.emit_pipeline(
            body,
            grid=(pieces * t // SC_GATHER_WINDOW,),
            in_specs=[pl.BlockSpec((SC_GATHER_WINDOW, dp), lambda i: (i, 0))]
                     + [pl.BlockSpec((1, SC_GATHER_WINDOW), lambda i: (0, i))] * kk,
            out_specs=[],
            core_axis_name=("core", "subcore"),
            dimension_semantics=(pltpu.PARALLEL,),
        )(x_hbm, *i_hbm)

    return scatter(x.reshape(pieces * t, dp), *idx).reshape(pieces, n, dp)


def _sc_gather(x, idx):
    pieces, t, dp = x.shape
    n = idx.shape[0]
    flat = (jnp.arange(pieces, dtype=jnp.int32)[:, None] * t + idx[None, :]).reshape(1, pieces * n)
    mesh = plsc.VectorSubcoreMesh(core_axis_name="core", subcore_axis_name="subcore")

    @pl.kernel(out_type=jax.ShapeDtypeStruct((pieces * n, dp), x.dtype), mesh=mesh, scratch_types=[])
    def gather(x_hbm, i_hbm, o_hbm):
        def body(i_vmem, o_vmem):
            pltpu.sync_copy(x_hbm.at[i_vmem.at[0]], o_vmem)

        pltpu.emit_pipeline(
            body,
            grid=(pieces * n // SC_GATHER_WINDOW,),
            in_specs=[pl.BlockSpec((1, SC_GATHER_WINDOW), lambda i: (0, i))],
            out_specs=[pl.BlockSpec((SC_GATHER_WINDOW, dp), lambda i: (i, 0))],
            core_axis_name=("core", "subcore"),
            dimension_semantics=(pltpu.PARALLEL,),
        )(i_hbm, o_hbm)

    return gather(x.reshape(pieces * t, dp), flat).reshape(pieces, n, dp)


def _combine_kernel(y_ref, w_ref, x_ref, gate_ref, o_ref, *, row0):
    w = w_ref[...]
    f = w[:, 0:1] * _load_pieces(y_ref[:, 0]) + w[:, 1:2] * _load_pieces(y_ref[:, 1])
    row = jnp.minimum(pl.program_id(0) + row0, 1)
    o_ref[...] = x_ref[...] + gate_ref[row] * f


def _combine(y2, weights, x, gate2, has_ctx):
    t, d = x.shape
    bm = ROW_BLOCK
    pieces, _, _, dp = y2.shape
    return pl.pallas_call(
        functools.partial(_combine_kernel, row0=0 if has_ctx else 1),
        grid=(t // bm,),
        in_specs=[pl.BlockSpec((pieces, TOP_K, bm, dp), lambda i: (0, 0, i, 0)),
                  pl.BlockSpec((bm, TOP_K), lambda i: (i, 0)),
                  pl.BlockSpec((bm, d), lambda i: (i, 0)),
                  pl.BlockSpec((2, 1, d), lambda i: (0, 0, 0))],
        out_specs=pl.BlockSpec((bm, d), lambda i: (i, 0)),
        out_shape=jax.ShapeDtypeStruct((t, d), F32),
        compiler_params=_params("parallel"),
        name="moe_combine",
    )(y2, weights, x, gate2)


def _top1_rows(p):
    m = jnp.max(p, axis=0, keepdims=True)
    idx = lax.broadcasted_iota(jnp.int32, p.shape, 0)
    return m, jnp.min(jnp.where(p == m, idx, p.shape[0]), axis=0, keepdims=True)


def _prefix_rank(onehot):
    e, n = onehot.shape
    blk = EXPERT_BLOCK
    nb = n // blk
    earlier = (jnp.arange(blk)[:, None] < jnp.arange(blk)[None, :]).astype(F32)
    within = jnp.dot(onehot.reshape(e * nb, blk), earlier).reshape(e, n)
    tot = jnp.sum(onehot.reshape(e, nb, blk), axis=2)
    before = jnp.repeat(jnp.cumsum(tot, axis=1) - tot, blk, axis=1)
    return jnp.sum((within + before) * onehot, axis=0).astype(jnp.int32)


def _hier_moe(tokens, logits_t, b_group, b_expert, w1, w3, w2, layer):
    pieces, t, dp = tokens.shape
    pg = jax.nn.softmax(logits_t[:N_GROUPS] + b_group.astype(F32)[:, None], axis=0)
    g_prob, g_idx = _top1_rows(pg)
    le = (logits_t[N_GROUPS:N_GROUPS + N_EXPERTS] + b_expert.astype(F32)[:, None]).reshape(N_GROUPS, EXPERTS_PER_GROUP, t)
    group_iota = lax.broadcasted_iota(jnp.int32, (N_GROUPS, 1, t), 0)
    le = jnp.sum(jnp.where(group_iota == g_idx[None], le, 0.0), axis=0)
    pe = jax.nn.softmax(le, axis=0)
    p1, i1 = _top1_rows(pe)
    p2, i2 = _top1_rows(jnp.where(lax.broadcasted_iota(jnp.int32, pe.shape, 0) == i1, -1.0, pe))
    e_prob, e_idx = jnp.concatenate([p1, p2], axis=0), jnp.concatenate([i1, i2], axis=0)
    weights = g_prob * e_prob / jnp.sum(e_prob, axis=0, keepdims=True)
    flat_e = (g_idx * EXPERTS_PER_GROUP + e_idx).reshape(1, TOP_K * t)
    onehot = (flat_e == lax.broadcasted_iota(jnp.int32, (N_EXPERTS, TOP_K * t), 0)).astype(F32)
    rank = _prefix_rank(onehot)
    counts = jnp.sum(onehot, axis=1).astype(jnp.int32)
    first_row = jnp.sum(onehot * (jnp.cumsum(counts) - counts).astype(F32)[:, None], axis=0).astype(jnp.int32)
    dest = first_row + rank
    buf = _sc_scatter(tokens, dest)
    yb = _moe_experts(buf, _expert_items(counts, TOP_K * t), w1, w3, w2, layer)
    return _sc_gather(yb, dest).reshape(pieces, TOP_K, t, dp), weights.T


def kernel(x, c, ctx, c_ctx, w_mod, b_mod, norm_mix, norm_ffn, w_in_ab, w_out_ab, qn_a, kn_a, sink_a, qn_b, kn_b, w_in_c, w_out_c, qn_c, kn_c, lam_c, subln_c, w_group, b_group, w_expert, b_expert, w1, w3, w2):
    b, s_lat, d = x.shape
    n_ctx = ctx.shape[1]
    assert b == 1 and n_ctx == ROW_BLOCK and s_lat % (2 * ROW_BLOCK) == 0
    depth = w_mod.shape[0]
    cos, sin = _rope_tables(s_lat, n_ctx)
    mods = _mod_vectors(c, c_ctx, w_mod, b_mod)
    xs = jnp.concatenate([ctx[0], x[0]], axis=0)
    has_ctx = True
    pending = None
    for l in range(depth):
        last = l == depth - 1
        i = l // 2
        sh1, sc1, gt1, sh2, sc2, gt2 = [mods[l, :2, j * d:(j + 1) * d].reshape(2, 1, d) for j in range(6)]
        gain1 = norm_mix[l].astype(F32) * (1 + sc1)
        if l % 2 == 0:
            xs, heads = _inproj_ab(xs, pending, gain1, sh1, w_in_ab[i].astype(BF16), qn_a[i], kn_a[i], qn_b[i], kn_b[i], cos, sin)
            o_lat, o_ctx = _mixer_ab(heads, n_ctx, sink_a[i], not last,
                                     _logits_bounded(qn_a[i], kn_a[i]), _logits_bounded(qn_b[i], kn_b[i]))
            w_out = w_out_ab[i]
        else:
            lam_init = 0.8 - 0.6 * math.exp(-0.3 * l)
            xs, heads = _inproj_c(xs, pending, gain1, sh1, w_in_c[i].astype(BF16), qn_c[i], kn_c[i], cos, sin)
            o_lat, o_ctx = _mixer_c(heads, n_ctx, lam_c[i], subln_c[i], lam_init, not last,
                                    _logits_bounded(qn_c[i], kn_c[i]))
            w_out = w_out_c[i]
        if last:
            xs, has_ctx = xs[n_ctx:], False
        gain2 = norm_ffn[l].astype(F32) * (1 + sc2)
        w_router = jnp.zeros((d, LANES), F32).at[:, :N_GROUPS].set(w_group[l]).at[:, N_GROUPS:N_GROUPS + N_EXPERTS].set(w_expert[l])
        w_router = jnp.stack(_split_bf16(w_router))
        xs, tokens, logits = _out_proj(o_lat, o_ctx, w_out.astype(BF16), xs, gt1, gain2, sh2, w_router)
        y2, weights = _hier_moe(tokens, logits, b_group[l], b_expert[l], w1, w3, w2, l)
        pending = (y2, weights, gt2)
    return _combine(y2, weights, xs, gt2, has_ctx).reshape(b, s_lat, d)
```

```python
import functools
import math

import jax
import jax.numpy as jnp
from jax import lax
from jax.experimental import pallas as pl
from jax.experimental.pallas import tpu as pltpu
from jax.experimental.pallas import tpu_sc as plsc

F32 = jnp.float32
BF16 = jnp.bfloat16

GRID_W = 64
HEAD_DIM = 64
WINDOW = 128
ROPE_THETA = 10000.0
EPS = 1e-6
NEG = -1e30
N_HEADS_A, N_KV_A = 8, 2
N_HEADS_B, N_KV_B = 8, 2
GROUP = N_HEADS_A // N_KV_A
QA_W, KVA_W = N_HEADS_A * HEAD_DIM, N_KV_A * HEAD_DIM
QB_W, KVB_W = N_HEADS_B * HEAD_DIM, N_KV_B * HEAD_DIM
N_HEADS_C = 8
DV_C = 2 * HEAD_DIM
N_GROUPS, EXPERTS_PER_GROUP, TOP_K = 4, 8, 2
N_EXPERTS = N_GROUPS * EXPERTS_PER_GROUP
LANES = 128
ROW_BLOCK = 256
EXPERT_BLOCK = 512
SC_GATHER_WINDOW = 128
SC_ROW_SPLIT = 4
KEY_BLOCK = 3328
GQA_V_ROWS = 80
DIFF_V_ROWS = 128
VMEM_LIMIT = 48 * 1024 * 1024
LOG2E = math.log2(math.e)
Q_SCALE = HEAD_DIM ** -0.5 * LOG2E
LOG2_LOGIT_BOUND = 60.0


def _params(*sem):
    return pltpu.CompilerParams(dimension_semantics=sem, vmem_limit_bytes=VMEM_LIMIT)


def _lane(shape):
    return lax.broadcasted_iota(jnp.int32, shape, 1)


def _mod_vec_kernel(a_ref, w_ref, b_ref, o_ref):
    a = a_ref[...]
    a = a * jax.nn.sigmoid(a)
    o_ref[0] = jnp.dot(a, w_ref[0], preferred_element_type=F32, precision=lax.Precision.HIGHEST) + b_ref[0]


def _mod_vectors(c, c_ctx, w_mod, b_mod):
    depth, d, n = w_mod.shape
    a = jnp.zeros((8, d), F32).at[0].set(c_ctx).at[1].set(c[0])
    bn = 1024
    return pl.pallas_call(
        _mod_vec_kernel,
        grid=(depth, n // bn),
        in_specs=[pl.BlockSpec((8, d), lambda l, j: (0, 0)),
                  pl.BlockSpec((1, d, bn), lambda l, j: (l, 0, j)),
                  pl.BlockSpec((1, 1, bn), lambda l, j: (l, 0, j))],
        out_specs=pl.BlockSpec((1, 8, bn), lambda l, j: (l, 0, j)),
        out_shape=jax.ShapeDtypeStruct((depth, 8, n), F32),
        compiler_params=_params("parallel", "parallel"),
        name="mod_vectors",
    )(a, w_mod, b_mod.reshape(depth, 1, n))


def _modulated(x, g_ref, s_ref):
    y = x * lax.rsqrt(jnp.mean(x * x, axis=-1, keepdims=True) + EPS)
    row = jnp.minimum(pl.program_id(0), 1)
    return y * g_ref[row] + s_ref[row]


def _split_bf16(x):
    top = lax.bitcast_convert_type(lax.bitcast_convert_type(x, jnp.uint32) & jnp.uint32(0xFFFF0000), F32)
    return top.astype(BF16), (x - top).astype(BF16)


def _head_norm_rope(x, gain, cos, sin, seg_mean):
    hi, lo = _split_bf16(x * x)
    ms = jnp.dot(hi, seg_mean, preferred_element_type=F32) + jnp.dot(lo, seg_mean, preferred_element_type=F32)
    y = x * lax.rsqrt(ms + EPS) * gain
    first_half = (_lane(y.shape) & (HEAD_DIM - 1)) < HEAD_DIM // 2
    partner = jnp.where(first_half, pltpu.roll(y, LANES - HEAD_DIM // 2, 1), pltpu.roll(y, HEAD_DIM // 2, 1))
    return y * cos + partner * sin


def _store_value_heads(v_ref, v):
    lane = _lane(v.shape)
    ones_col = jnp.where(lane == HEAD_DIM, 1.0, 0.0)
    v_ref[0] = jnp.where(lane < HEAD_DIM, v, ones_col).astype(BF16)
    v_ref[1] = jnp.where(lane < HEAD_DIM, pltpu.roll(v, HEAD_DIM, 1), ones_col).astype(BF16)


def _ones_row_block(rows, cols):
    return jnp.where(lax.broadcasted_iota(jnp.int32, (rows, cols), 0) == 0, 1.0, 0.0)


def _store_value_heads_t(vt_ref, v):
    vt = v.T
    tail = _ones_row_block(GQA_V_ROWS - HEAD_DIM, v.shape[0])
    vt_ref[0, 0] = jnp.concatenate([vt[:HEAD_DIM], tail], axis=0).astype(BF16)
    vt_ref[1, 0] = jnp.concatenate([vt[HEAD_DIM:], tail], axis=0).astype(BF16)


N_INPROJ_INPUTS = 8


def _inproj_rows(refs, fused):
    if not fused:
        return refs[0][...], refs[1:N_INPROJ_INPUTS], refs[N_INPROJ_INPUTS:]
    y_ref, wts_ref, gate_ref, x_ref = refs[:4]
    ins, outs = refs[4:N_INPROJ_INPUTS + 3], refs[N_INPROJ_INPUTS + 3:]
    wts = wts_ref[...]
    f = wts[:, 0:1] * _load_pieces(y_ref[:, 0]) + wts[:, 1:2] * _load_pieces(y_ref[:, 1])
    x = x_ref[...] + gate_ref[jnp.minimum(pl.program_id(0), 1)] * f
    outs[0][...] = x
    return x, ins, outs[1:]


def _inproj_ab_kernel(*refs, fused):
    x, (g_ref, s_ref, w_ref, hg_ref, cos_ref, sin_ref, seg_ref), outs = _inproj_rows(refs, fused)
    qa_ref, qb_ref, ka_ref, kb_ref, va_ref, vat_ref, vbt_ref = outs
    bm = x.shape[0]
    h = _modulated(x, g_ref, s_ref).astype(BF16)
    proj = jnp.dot(h, w_ref[...], preferred_element_type=F32)
    cos, sin, seg = cos_ref[...], sin_ref[...], seg_ref[...]
    low = _lane((bm, LANES)) < HEAD_DIM

    def normed(j):
        cols = slice(j * LANES, (j + 1) * LANES)
        return _head_norm_rope(proj[:, cols], hg_ref[:, cols], cos, sin, seg)

    def store_queries(q_ref, tile0):
        for j in range(N_HEADS_A // 2):
            y = normed(tile0 + j)
            swapped = pltpu.roll(y, HEAD_DIM, 1)
            kv, g0 = (2 * j) // GROUP, (2 * j) % GROUP
            if kv == 0:
                even, odd = jnp.where(low, y, 0.0), jnp.where(low, swapped, 0.0)
            else:
                even, odd = jnp.where(low, 0.0, swapped), jnp.where(low, 0.0, y)
            q_ref[kv, 0, g0 * bm:(g0 + 1) * bm, :] = even.astype(BF16)
            q_ref[kv, 0, (g0 + 1) * bm:(g0 + 2) * bm, :] = odd.astype(BF16)

    tq = QA_W // LANES
    store_queries(qa_ref, 0)
    ka_ref[0] = normed(tq).astype(BF16)
    va = proj[:, (tq + 1) * LANES:(tq + 2) * LANES]
    _store_value_heads(va_ref, va)
    _store_value_heads_t(vat_ref, va)
    store_queries(qb_ref, tq + 2)
    kb_ref[0] = normed(2 * tq + 2).astype(BF16)
    _store_value_heads_t(vbt_ref, proj[:, (2 * tq + 3) * LANES:(2 * tq + 4) * LANES])


def _inproj_c_kernel(*refs, fused):
    x, (g_ref, s_ref, w_ref, hg_ref, cos_ref, sin_ref, seg_ref), (q_ref, k_ref, v_ref) = _inproj_rows(refs, fused)
    bm = x.shape[0]
    h = _modulated(x, g_ref, s_ref).astype(BF16)
    proj = jnp.dot(h, w_ref[...], preferred_element_type=F32)
    cos, sin, seg = cos_ref[...], sin_ref[...], seg_ref[...]
    low = _lane((bm, LANES)) < HEAD_DIM

    def normed(j):
        cols = slice(j * LANES, (j + 1) * LANES)
        return _head_norm_rope(proj[:, cols], hg_ref[:, cols], cos, sin, seg)

    for j in range(N_HEADS_C):
        y = normed(j)
        q_ref[j, 0, 0:bm, :] = jnp.where(low, y, 0.0).astype(BF16)
        q_ref[j, 0, bm:2 * bm, :] = jnp.where(low, 0.0, y).astype(BF16)
        k_ref[j] = normed(N_HEADS_C + j).astype(BF16)
        v = proj[:, (2 * N_HEADS_C + j) * LANES:(2 * N_HEADS_C + j + 1) * LANES]
        v_ref[j, 0] = v.T.astype(BF16)


def _rope_tables(seq, n_ctx):
    rows_n = seq // GRID_W
    rows = jnp.broadcast_to(jnp.arange(rows_n, dtype=F32)[:, None], (rows_n, GRID_W)).reshape(-1)
    cols = jnp.broadcast_to(jnp.arange(GRID_W, dtype=F32)[None, :], (rows_n, GRID_W)).reshape(-1)
    half = HEAD_DIM // 2
    inv = ROPE_THETA ** (-jnp.arange(0, half, 2, dtype=F32) / half)
    ang = jnp.concatenate([rows[:, None] * inv, cols[:, None] * inv], axis=-1)
    reps = LANES // half
    sign = jnp.tile(jnp.concatenate([-jnp.ones((half,), F32), jnp.ones((half,), F32)]), LANES // HEAD_DIM)
    cos = jnp.pad(jnp.tile(jnp.cos(ang), (1, reps)), ((n_ctx, 0), (0, 0)), constant_values=1.0)
    sin = jnp.pad(jnp.tile(jnp.sin(ang), (1, reps)) * sign, ((n_ctx, 0), (0, 0)))
    return cos, sin


def _segment_mean_matrix():
    idx = jnp.arange(LANES) // HEAD_DIM
    return jnp.where(idx[:, None] == idx[None, :], 1.0 / HEAD_DIM, 0.0).astype(BF16)


def _q_slot(i, nblk):
    return (i + nblk - 1) % nblk


def _inproj_common_specs(t, d, n):
    bm = ROW_BLOCK
    return [pl.BlockSpec((bm, d), lambda i: (i, 0)),
            pl.BlockSpec((2, 1, d), lambda i: (0, 0, 0)),
            pl.BlockSpec((2, 1, d), lambda i: (0, 0, 0)),
            pl.BlockSpec((d, n), lambda i: (0, 0)),
            pl.BlockSpec((1, n), lambda i: (0, 0)),
            pl.BlockSpec((bm, LANES), lambda i: (i, 0)),
            pl.BlockSpec((bm, LANES), lambda i: (i, 0)),
            pl.BlockSpec((LANES, LANES), lambda i: (0, 0))]


def _inproj_call(kernel_fn, name, x, pending, args, n, out_specs, out_shape):
    t, d = x.shape
    bm = ROW_BLOCK
    in_specs = _inproj_common_specs(t, d, n)
    args = [x] + list(args)
    fused = pending is not None
    if fused:
        y2, wts, gate2 = pending
        pieces, _, _, dp = y2.shape
        in_specs = [pl.BlockSpec((pieces, TOP_K, bm, dp), lambda i: (0, 0, i, 0)),
                    pl.BlockSpec((bm, TOP_K), lambda i: (i, 0)),
                    pl.BlockSpec((2, 1, d), lambda i: (0, 0, 0))] + in_specs
        args = [y2, wts, gate2] + args
        out_specs = [pl.BlockSpec((bm, d), lambda i: (i, 0))] + list(out_specs)
        out_shape = [jax.ShapeDtypeStruct((t, d), F32)] + list(out_shape)
    outs = pl.pallas_call(
        functools.partial(kernel_fn, fused=fused),
        grid=(t // bm,),
        in_specs=in_specs,
        out_specs=out_specs,
        out_shape=out_shape,
        compiler_params=_params("parallel"),
        name=name,
    )(*args)
    return (outs[0], outs[1:]) if fused else (x, outs)


def _inproj_ab(x, pending, gain2, shift2, w, qn_a, kn_a, qn_b, kn_b, cos, sin):
    t, d = x.shape
    n = w.shape[1]
    bm, nblk = ROW_BLOCK, t // ROW_BLOCK
    tile = lambda g, reps: jnp.tile(g.astype(F32), reps)
    ones_v = jnp.ones((KVA_W,), F32)
    hg = jnp.concatenate([tile(qn_a, N_HEADS_A) * Q_SCALE, tile(kn_a, N_KV_A), ones_v,
                          tile(qn_b, N_HEADS_B) * Q_SCALE, tile(kn_b, N_KV_B), ones_v]).reshape(1, n)
    q_shape = jax.ShapeDtypeStruct((N_KV_A, nblk, GROUP * bm, LANES), BF16)
    k_shape = jax.ShapeDtypeStruct((1, t, LANES), BF16)
    v_shape = jax.ShapeDtypeStruct((N_KV_A, t, LANES), BF16)
    vt_shape = jax.ShapeDtypeStruct((N_KV_A, nblk, GQA_V_ROWS, bm), BF16)
    q_spec = pl.BlockSpec((N_KV_A, 1, GROUP * bm, LANES), lambda i: (0, _q_slot(i, nblk), 0, 0))
    k_spec = pl.BlockSpec((1, bm, LANES), lambda i: (0, i, 0))
    v_spec = pl.BlockSpec((N_KV_A, bm, LANES), lambda i: (0, i, 0))
    vt_spec = pl.BlockSpec((N_KV_A, 1, GQA_V_ROWS, bm), lambda i: (0, i, 0, 0))
    return _inproj_call(_inproj_ab_kernel, "inproj_ab", x, pending,
                        [gain2, shift2, w, hg, cos, sin, _segment_mean_matrix()], n,
                        [q_spec, q_spec, k_spec, k_spec, v_spec, vt_spec, vt_spec],
                        [q_shape, q_shape, k_shape, k_shape, v_shape, vt_shape, vt_shape])


def _inproj_c(x, pending, gain2, shift2, w, qn, kn, cos, sin):
    t, d = x.shape
    n = w.shape[1]
    bm, nblk = ROW_BLOCK, t // ROW_BLOCK
    h = N_HEADS_C
    tile = lambda g: jnp.tile(g.astype(F32), 2 * h)
    hg = jnp.concatenate([tile(qn) * Q_SCALE, tile(kn), jnp.ones((h * DV_C,), F32)]).reshape(1, n)
    return _inproj_call(_inproj_c_kernel, "inproj_c", x, pending,
                        [gain2, shift2, w, hg, cos, sin, _segment_mean_matrix()], n,
                        [pl.BlockSpec((h, 1, 2 * bm, LANES), lambda i: (0, _q_slot(i, nblk), 0, 0)),
                         pl.BlockSpec((h, bm, LANES), lambda i: (0, i, 0)),
                         pl.BlockSpec((h, 1, DIFF_V_ROWS, bm), lambda i: (0, i, 0, 0))],
                        [jax.ShapeDtypeStruct((h, nblk, 2 * bm, LANES), BF16),
                         jax.ShapeDtypeStruct((h, t, LANES), BF16),
                         jax.ShapeDtypeStruct((h, nblk, DIFF_V_ROWS, bm), BF16)])


def _pad_rows(x):
    rows = x.shape[0]
    if rows >= LANES:
        return x[:LANES]
    return jnp.concatenate([x, jnp.zeros((LANES - rows, x.shape[1]), x.dtype)], axis=0)


def _merge_gqa_heads(o, bq):
    low = _lane((bq, LANES)) < HEAD_DIM
    pairs = [jnp.where(low, o[g * bq:(g + 1) * bq], pltpu.roll(o[(g + 1) * bq:(g + 2) * bq], HEAD_DIM, 1))
             for g in range(0, GROUP, 2)]
    return jnp.concatenate(pairs, axis=1)


def _flash_kernel(*refs, mode, online, n_keys, bk, dv, l0):
    refs = list(refs)
    q_ref, k_ref, vt_ref = refs[:3]
    pos = 3
    if online:
        m0_ref = refs[pos]
        pos += 1
    if mode == "diff":
        lam_ref, sub_ref = refs[pos:pos + 2]
        pos += 2
    o_ref = refs[pos]
    scratch = refs[pos + 1:]
    acc_sc = scratch[0]
    ones_row = acc_sc.shape[0] > dv
    m_sc = scratch[1] if online else None
    l_sc = None if ones_row else scratch[-1]

    nsub, rb = q_ref.shape[1], q_ref.shape[2]
    r = nsub * rb
    ch = vt_ref.shape[3]
    q = q_ref[0].reshape(r, LANES)
    acc_sc[...] = jnp.where(lax.broadcasted_iota(jnp.int32, acc_sc.shape, 0) == dv, l0, 0.0).astype(F32)
    if online:
        m_sc[...] = m0_ref[0]
    if not ones_row:
        l_sc[...] = jnp.full(l_sc.shape, l0, F32)

    def block(start, size):
        kb = k_ref[0, pl.ds(start, size), :]
        st = lax.dot_general(kb, q, (((1,), (1,)), ((), ())), preferred_element_type=F32)
        if online:
            m_prev = m_sc[...]
            m_new = jnp.maximum(m_prev, jnp.max(st, axis=0, keepdims=True))
            p = jnp.exp2(st - m_new)
            alpha = jnp.exp2(m_prev - m_new)
            acc = alpha * acc_sc[...]
            m_sc[...] = m_new
        else:
            p = jnp.exp2(st)
            alpha = 1.0
            acc = acc_sc[...]
        if not ones_row:
            l_sc[...] = alpha * l_sc[...] + jnp.sum(p, axis=0, keepdims=True)
        pt = p.astype(BF16)
        c0 = start // ch
        for c in range(size // ch):
            acc = acc + jnp.dot(vt_ref[0, c0 + c], pt[c * ch:(c + 1) * ch], preferred_element_type=F32)
        acc_sc[...] = acc

    n_full, tail = n_keys // bk, n_keys % bk
    if n_full:
        def body(i, carry):
            block(pl.multiple_of(i * bk, bk), bk)
            return carry
        lax.fori_loop(0, n_full, body, 0)
    if tail:
        block(n_full * bk, tail)

    acc = acc_sc[...]
    den = acc[dv:dv + 1] if ones_row else l_sc[...]
    ot = _pad_rows(acc / den)
    bq = rb // (GROUP if mode == "gqa" else 2)
    o = jnp.concatenate([ot[:, j * bq:(j + 1) * bq].T for j in range(r // bq)], axis=0)
    if mode == "gqa":
        o_ref[...] = _merge_gqa_heads(o, bq).astype(o_ref.dtype)
    else:
        for b in range(nsub):
            d = o[b * rb:b * rb + bq] - lam_ref[...] * o[b * rb + bq:(b + 1) * rb]
            y = d * lax.rsqrt(jnp.mean(d * d, axis=-1, keepdims=True) + EPS) * sub_ref[...]
            o_ref[b * bq:(b + 1) * bq, :] = y.astype(o_ref.dtype)


def _flash_call(q, k, vt, extra, *, mode, online, n_keys, slot0, nsub, n_steps, l0=0.0):
    hkv, _, rb, _ = q.shape
    hk = k.shape[0]
    _, _, dvr, ch = vt.shape
    dv = HEAD_DIM if mode == "gqa" else DV_C
    bq = rb // (GROUP if mode == "gqa" else 2)
    ocols = GROUP * HEAD_DIM if mode == "gqa" else DV_C
    r = nsub * rb
    bk = min(KEY_BLOCK, n_keys)
    in_specs = [pl.BlockSpec((1, nsub, rb, LANES), lambda h, i: (h, slot0 // nsub + i, 0, 0)),
                pl.BlockSpec((1, n_keys, LANES), (lambda h, i: (h, 0, 0)) if hk > 1 else (lambda h, i: (0, 0, 0))),
                pl.BlockSpec((1, n_keys // ch, dvr, ch), lambda h, i: (h, 0, 0, 0))]
    args = [q, k, vt]
    scratch = [pltpu.VMEM((dvr, r), F32)]
    if online:
        m0 = extra.pop(0)
        in_specs.append(pl.BlockSpec((1, 1, r), lambda h, i: (h, 0, 0)))
        args.append(m0)
        scratch.append(pltpu.VMEM((1, r), F32))
    if dvr <= dv:
        scratch.append(pltpu.VMEM((1, r), F32))
    for a in extra:
        in_specs.append(pl.BlockSpec(a.shape, lambda h, i: (0, 0)))
        args.append(a)
    return pl.pallas_call(
        functools.partial(_flash_kernel, mode=mode, online=online, n_keys=n_keys, bk=bk, dv=dv, l0=l0),
        grid=(hkv, n_steps),
        in_specs=in_specs,
        out_specs=pl.BlockSpec((nsub * bq, ocols), lambda h, i: (i, h)),
        out_shape=jax.ShapeDtypeStruct((n_steps * nsub * bq, hkv * ocols), BF16),
        scratch_shapes=scratch,
        compiler_params=_params("parallel", "parallel"),
        name="flash_online" if online else "flash_bounded",
    )(*args)


def _logits_bounded(q_gain, k_gain):
    bound = HEAD_DIM * Q_SCALE * 1.02 * jnp.max(jnp.abs(q_gain.astype(F32))) * jnp.max(jnp.abs(k_gain.astype(F32)))
    return bound <= LOG2_LOGIT_BOUND


def _attend(q, k, v, extra, bounded, **kw):
    hkv, r = q.shape[0], kw["nsub"] * q.shape[2]
    fast = lambda q_, k_, v_, *e: _flash_call(q_, k_, v_, list(e), online=False, **kw)
    safe = lambda q_, k_, v_, *e: _flash_call(q_, k_, v_, [jnp.full((hkv, 1, r), NEG, F32)] + list(e), online=True, **kw)
    return lax.cond(bounded, fast, safe, q, k, v, *extra)


def _window_kernel(q_ref, k_ref, v_ref, sink_ref, o_ref, *, bq, n_ctx):
    q = q_ref[0, 0]
    r = q.shape[0]
    t = k_ref.shape[1]
    w = bq + 2 * WINDOW
    q0 = pl.program_id(1) * bq
    ws = pl.multiple_of(jnp.clip(n_ctx + q0 - WINDOW, 0, t - w), WINDOW)
    kw = k_ref[0, pl.ds(ws, w), :]
    vw = v_ref[0, pl.ds(ws, w), :]
    contract_last = (((1,), (1,)), ((), ()))
    s_loc = lax.dot_general(q, kw, contract_last, preferred_element_type=F32)
    qpos = q0 + (lax.broadcasted_iota(jnp.int32, (r, w), 0) & (bq - 1))
    kpos = ws - n_ctx + lax.broadcasted_iota(jnp.int32, (r, w), 1)
    mask = (kpos >= 0) & (kpos - qpos <= WINDOW) & (qpos - kpos <= WINDOW)
    s_loc = jnp.where(mask, s_loc, NEG)
    s_ctx = lax.dot_general(q, k_ref[0, 0:n_ctx, :], contract_last, preferred_element_type=F32)
    sink = sink_ref[0]
    m = jnp.maximum(sink, jnp.maximum(jnp.max(s_loc, axis=-1, keepdims=True), jnp.max(s_ctx, axis=-1, keepdims=True)))
    p_loc = jnp.exp2(s_loc - m)
    p_ctx = jnp.exp2(s_ctx - m)
    l = jnp.exp2(sink - m) + jnp.sum(p_loc, axis=-1, keepdims=True) + jnp.sum(p_ctx, axis=-1, keepdims=True)
    o = (jnp.dot(p_loc.astype(BF16), vw, preferred_element_type=F32)
         + jnp.dot(p_ctx.astype(BF16), v_ref[0, 0:n_ctx, :], preferred_element_type=F32))
    o_ref[...] = _merge_gqa_heads(o / l, bq).astype(o_ref.dtype)


def _window_attention(q, k, v, sink_rows, n_ctx):
    hkv, slots, r, _ = q.shape
    t = k.shape[1]
    bq = r // GROUP
    nq = slots - n_ctx // bq
    return pl.pallas_call(
        functools.partial(_window_kernel, bq=bq, n_ctx=n_ctx),
        grid=(hkv, nq),
        in_specs=[pl.BlockSpec((1, 1, r, LANES), lambda h, i: (h, i, 0, 0)),
                  pl.BlockSpec((1, t, LANES), lambda h, i: (0, 0, 0)),
                  pl.BlockSpec((1, t, LANES), lambda h, i: (h, 0, 0)),
                  pl.BlockSpec((1, r, 1), lambda h, i: (h, 0, 0))],
        out_specs=pl.BlockSpec((bq, GROUP * HEAD_DIM), lambda h, i: (i, h)),
        out_shape=jax.ShapeDtypeStruct((nq * bq, hkv * GROUP * HEAD_DIM), BF16),
        compiler_params=_params("parallel", "parallel"),
        name="window_attention",
    )(q, k, v, sink_rows)


def _window_bounded_kernel(q_ref, k_ref, vt_ref, mask_ref, sink_ref, o_ref, *, bq):
    q = q_ref[0, 0]
    ch = vt_ref.shape[3]
    cw = jnp.clip(pl.program_id(1), 0, vt_ref.shape[1] - 3)
    contract_last = (((1,), (1,)), ((), ()))
    k_win = k_ref[0, pl.ds(pl.multiple_of(cw * ch, ch), 3 * ch), :]
    st_win = lax.dot_general(k_win, q, contract_last, preferred_element_type=F32)
    st_ctx = lax.dot_general(k_ref[0, 0:ch, :], q, contract_last, preferred_element_type=F32)
    pt_win = jnp.exp2(st_win).astype(BF16) * mask_ref[0]
    acc = jnp.dot(vt_ref[0, 0], jnp.exp2(st_ctx).astype(BF16), preferred_element_type=F32)
    for c in range(3):
        acc = acc + jnp.dot(vt_ref[0, cw + c], pt_win[c * ch:(c + 1) * ch], preferred_element_type=F32)
    ot = _pad_rows(acc / (acc[HEAD_DIM:HEAD_DIM + 1] + jnp.exp2(sink_ref[0])))
    o = jnp.concatenate([ot[:, g * bq:(g + 1) * bq].T for g in range(GROUP)], axis=0)
    o_ref[...] = _merge_gqa_heads(o, bq).astype(o_ref.dtype)


def _window_masks(bq, r):
    c = jnp.arange(3 * bq)[:, None]
    off = c - (jnp.arange(r)[None, :] & (bq - 1))
    centred = (off >= bq - WINDOW) & (off <= bq + WINDOW)
    shifted = (off >= 2 * bq - WINDOW) & (off <= 2 * bq + WINDOW)
    return jnp.stack([centred & (c >= bq), centred, shifted]).astype(BF16)


def _window_attention_bounded(q, k, vt, sink_cols, n_ctx):
    hkv, slots, r, _ = q.shape
    t = k.shape[1]
    bq = r // GROUP
    nblk = t // bq
    nq = nblk - 1
    assert n_ctx == bq and nblk >= 3
    return pl.pallas_call(
        functools.partial(_window_bounded_kernel, bq=bq),
        grid=(hkv, nq),
        in_specs=[pl.BlockSpec((1, 1, r, LANES), lambda h, i: (h, i, 0, 0)),
                  pl.BlockSpec((1, t, LANES), lambda h, i: (0, 0, 0)),
                  pl.BlockSpec((1, nblk, GQA_V_ROWS, bq), lambda h, i: (h, 0, 0, 0)),
                  pl.BlockSpec((1, 3 * bq, r), lambda h, i: (jnp.where(i == 0, 0, jnp.where(i >= nblk - 2, 2, 1)), 0, 0)),
                  pl.BlockSpec((1, 1, r), lambda h, i: (h, 0, 0))],
        out_specs=pl.BlockSpec((bq, GROUP * HEAD_DIM), lambda h, i: (i, h)),
        out_shape=jax.ShapeDtypeStruct((nq * bq, hkv * GROUP * HEAD_DIM), BF16),
        compiler_params=_params("parallel", "parallel"),
        name="window_bounded",
    )(q, k, vt, _window_masks(bq, r), sink_cols)


def _sink_rows(sink, bq):
    hkv, g = sink.shape
    return jnp.broadcast_to(sink.astype(F32)[:, :, None] * LOG2E, (hkv, g, bq)).reshape(hkv, g * bq, 1)


def _mixer_ab(heads, n_ctx, sink_a, with_ctx, bounded_a, bounded):
    qa, qb, ka, kb, va, vat, vbt = heads
    t = ka.shape[1]
    bq = ROW_BLOCK
    n_lat = (t - n_ctx) // bq
    sink = _sink_rows(sink_a.reshape(N_KV_A, GROUP), bq)
    sink_cols = sink.reshape(N_KV_A, 1, GROUP * bq)
    bounded_a = bounded_a & (jnp.max(jnp.abs(sink)) <= LOG2_LOGIT_BOUND)
    oa = lax.cond(bounded_a,
                  lambda: _window_attention_bounded(qa, ka, vat, sink_cols, n_ctx),
                  lambda: _window_attention(qa, ka, va, sink, n_ctx))
    ob = _attend(qb, kb, vbt, [], bounded, mode="gqa", n_keys=t, slot0=0, nsub=1, n_steps=n_lat)
    if not with_ctx:
        return (oa, ob), None
    oca = _flash_call(qa, ka, vat, [sink_cols], mode="gqa", online=True, n_keys=n_ctx,
                      slot0=n_lat, nsub=1, n_steps=1, l0=1.0)
    ocb = _attend(qb, kb, vbt, [], bounded, mode="gqa", n_keys=n_ctx, slot0=n_lat, nsub=1, n_steps=1)
    return (oa, ob), (oca, ocb)


def _mixer_c(heads, n_ctx, lam_p, subln, lam_init, with_ctx, bounded):
    q, k, v = heads
    t = k.shape[1]
    n_lat = (t - n_ctx) // ROW_BLOCK
    lp = lam_p.astype(F32)
    lam = jnp.exp(jnp.sum(lp[0] * lp[1])) - jnp.exp(jnp.sum(lp[2] * lp[3])) + lam_init
    extra = [jnp.full((1, DV_C), lam, F32), (subln.astype(F32) * (1 - lam_init)).reshape(1, DV_C)]
    nsub = 2 if n_lat % 2 == 0 else 1
    o_lat = _attend(q, k, v, extra, bounded, mode="diff", n_keys=t, slot0=0, nsub=nsub, n_steps=n_lat // nsub)
    if not with_ctx:
        return (o_lat,), None
    o_ctx = _attend(q, k, v, extra, bounded, mode="diff", n_keys=n_ctx, slot0=n_lat, nsub=1, n_steps=1)
    return (o_lat,), (o_ctx,)


def _out_proj_kernel(*refs, n_parts, has_ctx):
    lat = refs[:n_parts]
    ctx = refs[n_parts:2 * n_parts] if has_ctx else None
    w_ref, x_ref, gate_ref, g_ref, s_ref, wr_ref, y_ref, tok_ref, logit_ref = refs[-9:]
    is_ctx = pl.program_id(0) == 0 if has_ctx else False
    row = jnp.where(is_ctx, 0, 1) if has_ctx else 1
    acc = None
    col = 0
    for p in range(n_parts):
        o = lat[p][...]
        if has_ctx:
            o = jnp.where(is_ctx, ctx[p][...], o)
        width = o.shape[1]
        part = jnp.dot(o, w_ref[col:col + width, :], preferred_element_type=F32)
        acc = part if acc is None else acc + part
        col += width
    x1 = x_ref[...] + gate_ref[row] * acc
    y_ref[...] = x1
    h = x1 * lax.rsqrt(jnp.mean(x1 * x1, axis=-1, keepdims=True) + EPS) * g_ref[row] + s_ref[row]
    _store_pieces(tok_ref, h)
    hi, lo = _split_bf16(h)
    logits = (jnp.dot(hi, wr_ref[0], preferred_element_type=F32) + jnp.dot(lo, wr_ref[0], preferred_element_type=F32)
              + jnp.dot(hi, wr_ref[1], preferred_element_type=F32))
    logit_ref[...] = logits.T


def _out_proj(o_lat, o_ctx, w, x, gate2, gain2, shift2, w_router):
    t, d = x.shape
    bm = ROW_BLOCK
    dp = d // SC_ROW_SPLIT
    has_ctx = o_ctx is not None
    n_parts = len(o_lat)
    lat_map = (lambda i: (jnp.maximum(i - 1, 0), 0)) if has_ctx else (lambda i: (i, 0))
    in_specs = [pl.BlockSpec((bm, o.shape[1]), lat_map) for o in o_lat]
    args = list(o_lat)
    if has_ctx:
        in_specs += [pl.BlockSpec((bm, o.shape[1]), lambda i: (0, 0)) for o in o_ctx]
        args += list(o_ctx)
    vec_spec = pl.BlockSpec((2, 1, d), lambda i: (0, 0, 0))
    in_specs += [pl.BlockSpec(w.shape, lambda i: (0, 0)),
                 pl.BlockSpec((bm, d), lambda i: (i, 0)),
                 vec_spec, vec_spec, vec_spec,
                 pl.BlockSpec((2, d, LANES), lambda i: (0, 0, 0))]
    return pl.pallas_call(
        functools.partial(_out_proj_kernel, n_parts=n_parts, has_ctx=has_ctx),
        grid=(t // bm,),
        in_specs=in_specs,
        out_specs=[pl.BlockSpec((bm, d), lambda i: (i, 0)),
                   pl.BlockSpec((SC_ROW_SPLIT, bm, dp), lambda i: (0, i, 0)),
                   pl.BlockSpec((LANES, bm), lambda i: (0, i))],
        out_shape=[jax.ShapeDtypeStruct((t, d), F32),
                   jax.ShapeDtypeStruct((SC_ROW_SPLIT, t, dp), F32),
                   jax.ShapeDtypeStruct((LANES, t), F32)],
        compiler_params=_params("parallel"),
        name="out_proj_router",
    )(*args, w, x, gate2, gain2, shift2, w_router)


def _store_pieces(ref, rows):
    dp = ref.shape[2]
    for j in range(ref.shape[0]):
        ref[j] = rows[:, j * dp:(j + 1) * dp]


def _load_pieces(planes):
    return jnp.concatenate([planes[j] for j in range(planes.shape[0])], axis=1)


def _moe_kernel(blk_ref, exp_ref, lo_ref, hi_ref, n_ref, x_ref, w1_ref, w3_ref, w2_ref, y_ref):
    i = pl.program_id(0)

    @pl.when(i < n_ref[0])
    def _():
        x = _load_pieces(x_ref[...]).astype(BF16)
        a = jnp.dot(x, w1_ref[0, 0].astype(BF16), preferred_element_type=F32)
        b = jnp.dot(x, w3_ref[0, 0].astype(BF16), preferred_element_type=F32)
        hidden = (a * jax.nn.sigmoid(a)) * b
        y = jnp.dot(hidden.astype(BF16), w2_ref[0, 0].astype(BF16), preferred_element_type=F32)
        rows = lax.broadcasted_iota(jnp.int32, (y.shape[0], 1), 0)
        y = jnp.where((rows >= lo_ref[i]) & (rows < hi_ref[i]), y, 0.0)
        first = (i == 0) | (blk_ref[i] != blk_ref[jnp.maximum(i - 1, 0)])

        @pl.when(first)
        def _():
            _store_pieces(y_ref, y)

        @pl.when(jnp.logical_not(first))
        def _():
            _store_pieces(y_ref, _load_pieces(y_ref[...]) + y)


def _moe_experts(buf, items, w1, w3, w2, layer):
    pieces, n_rows, dp = buf.shape
    d, de = w1.shape[2], w1.shape[3]
    n_items = items[0].shape[0]
    row_spec = pl.BlockSpec((pieces, EXPERT_BLOCK, dp), lambda i, blk, ex, lo, hi, n: (0, blk[i], 0))
    w_map = lambda i, blk, ex, lo, hi, n: (layer, ex[i], 0, 0)
    grid_spec = pltpu.PrefetchScalarGridSpec(
        num_scalar_prefetch=5,
        grid=(n_items,),
        in_specs=[row_spec,
                  pl.BlockSpec((1, 1, d, de), w_map),
                  pl.BlockSpec((1, 1, d, de), w_map),
                  pl.BlockSpec((1, 1, de, d), w_map)],
        out_specs=row_spec,
    )
    return pl.pallas_call(
        _moe_kernel,
        grid_spec=grid_spec,
        out_shape=jax.ShapeDtypeStruct((pieces, n_rows, dp), F32),
        compiler_params=_params("arbitrary"),
        name="moe_experts",
    )(*items, buf, w1, w3, w2)


def _expert_items(counts, n_rows):
    e = counts.shape[0]
    n_blk = n_rows // EXPERT_BLOCK
    n_items = n_blk + e - 1
    end = jnp.cumsum(counts)
    start = end - counts
    first_blk = start // EXPERT_BLOCK
    per_expert = jnp.where(counts > 0, (end - 1) // EXPERT_BLOCK - first_blk + 1, 0)
    cum = jnp.cumsum(per_expert)
    total = cum[-1]
    w = jnp.minimum(jnp.arange(n_items, dtype=jnp.int32), total - 1)
    onehot = ((cum - per_expert)[None, :] <= w[:, None]) & (w[:, None] < cum[None, :])
    pick = lambda tab: jnp.sum(jnp.where(onehot, tab[None, :], 0), axis=1).astype(jnp.int32)
    expert = pick(jnp.arange(e, dtype=jnp.int32))
    blk = pick(first_blk) + w - pick(cum - per_expert)
    lo = jnp.clip(pick(start) - blk * EXPERT_BLOCK, 0, EXPERT_BLOCK)
    hi = jnp.clip(pick(end) - blk * EXPERT_BLOCK, 0, EXPERT_BLOCK)
    return blk, expert, lo, hi, total.astype(jnp.int32).reshape(1)


def _sc_scatter(x, pos):
    pieces, t, dp = x.shape
    n = pos.shape[0]
    kk = n // t
    offs = jnp.arange(pieces, dtype=jnp.int32)[:, None] * n
    idx = [(offs + pos[k * t:(k + 1) * t][None, :]).reshape(1, pieces * t) for k in range(kk)]
    mesh = plsc.VectorSubcoreMesh(core_axis_name="core", subcore_axis_name="subcore")

    @pl.kernel(out_type=jax.ShapeDtypeStruct((pieces * n, dp), x.dtype), mesh=mesh, scratch_types=[])
    def scatter(x_hbm, *refs):
        i_hbm, o_hbm = refs[:kk], refs[kk]

        def body(x_vmem, *i_vmem):
            for iv in i_vmem:
                pltpu.sync_copy(x_vmem, o_hbm.at[iv.at[0]])

        pltpu.emit_pipeline(
            body,
            grid=(pieces * t // SC_GATHER_WINDOW,),
            in_specs=[pl.BlockSpec((SC_GATHER_WINDOW, dp), lambda i: (i, 0))]
                     + [pl.BlockSpec((1, SC_GATHER_WINDOW), lambda i: (0, i))] * kk,
            out_specs=[],
            core_axis_name=("core", "subcore"),
            dimension_semantics=(pltpu.PARALLEL,),
        )(x_hbm, *i_hbm)

    return scatter(x.reshape(pieces * t, dp), *idx).reshape(pieces, n, dp)


def _sc_gather(x, idx):
    pieces, t, dp = x.shape
    n = idx.shape[0]
    flat = (jnp.arange(pieces, dtype=jnp.int32)[:, None] * t + idx[None, :]).reshape(1, pieces * n)
    mesh = plsc.VectorSubcoreMesh(core_axis_name="core", subcore_axis_name="subcore")

    @pl.kernel(out_type=jax.ShapeDtypeStruct((pieces * n, dp), x.dtype), mesh=mesh, scratch_types=[])
    def gather(x_hbm, i_hbm, o_hbm):
        def body(i_vmem, o_vmem):
            pltpu.sync_copy(x_hbm.at[i_vmem.at[0]], o_vmem)

        pltpu.emit_pipeline(
            body,
            grid=(pieces * n // SC_GATHER_WINDOW,),
            in_specs=[pl.BlockSpec((1, SC_GATHER_WINDOW), lambda i: (0, i))],
            out_specs=[pl.BlockSpec((SC_GATHER_WINDOW, dp), lambda i: (i, 0))],
            core_axis_name=("core", "subcore"),
            dimension_semantics=(pltpu.PARALLEL,),
        )(i_hbm, o_hbm)

    return gather(x.reshape(pieces * t, dp), flat).reshape(pieces, n, dp)


def _combine_kernel(y_ref, w_ref, x_ref, gate_ref, o_ref, *, row0):
    w = w_ref[...]
    f = w[:, 0:1] * _load_pieces(y_ref[:, 0]) + w[:, 1:2] * _load_pieces(y_ref[:, 1])
    row = jnp.minimum(pl.program_id(0) + row0, 1)
    o_ref[...] = x_ref[...] + gate_ref[row] * f


def _combine(y2, weights, x, gate2, has_ctx):
    t, d = x.shape
    bm = ROW_BLOCK
    pieces, _, _, dp = y2.shape
    return pl.pallas_call(
        functools.partial(_combine_kernel, row0=0 if has_ctx else 1),
        grid=(t // bm,),
        in_specs=[pl.BlockSpec((pieces, TOP_K, bm, dp), lambda i: (0, 0, i, 0)),
                  pl.BlockSpec((bm, TOP_K), lambda i: (i, 0)),
                  pl.BlockSpec((bm, d), lambda i: (i, 0)),
                  pl.BlockSpec((2, 1, d), lambda i: (0, 0, 0))],
        out_specs=pl.BlockSpec((bm, d), lambda i: (i, 0)),
        out_shape=jax.ShapeDtypeStruct((t, d), F32),
        compiler_params=_params("parallel"),
        name="moe_combine",
    )(y2, weights, x, gate2)


def _top1_rows(p):
    m = jnp.max(p, axis=0, keepdims=True)
    idx = lax.broadcasted_iota(jnp.int32, p.shape, 0)
    return m, jnp.min(jnp.where(p == m, idx, p.shape[0]), axis=0, keepdims=True)


def _prefix_rank(onehot):
    e, n = onehot.shape
    blk = EXPERT_BLOCK
    nb = n // blk
    earlier = (jnp.arange(blk)[:, None] < jnp.arange(blk)[None, :]).astype(F32)
    within = jnp.dot(onehot.reshape(e * nb, blk), earlier).reshape(e, n)
    tot = jnp.sum(onehot.reshape(e, nb, blk), axis=2)
    before = jnp.repeat(jnp.cumsum(tot, axis=1) - tot, blk, axis=1)
    return jnp.sum((within + before) * onehot, axis=0).astype(jnp.int32)


def _hier_moe(tokens, logits_t, b_group, b_expert, w1, w3, w2, layer):
    pieces, t, dp = tokens.shape
    pg = jax.nn.softmax(logits_t[:N_GROUPS] + b_group.astype(F32)[:, None], axis=0)
    g_prob, g_idx = _top1_rows(pg)
    le = (logits_t[N_GROUPS:N_GROUPS + N_EXPERTS] + b_expert.astype(F32)[:, None]).reshape(N_GROUPS, EXPERTS_PER_GROUP, t)
    group_iota = lax.broadcasted_iota(jnp.int32, (N_GROUPS, 1, t), 0)
    le = jnp.sum(jnp.where(group_iota == g_idx[None], le, 0.0), axis=0)
    pe = jax.nn.softmax(le, axis=0)
    p1, i1 = _top1_rows(pe)
    p2, i2 = _top1_rows(jnp.where(lax.broadcasted_iota(jnp.int32, pe.shape, 0) == i1, -1.0, pe))
    e_prob, e_idx = jnp.concatenate([p1, p2], axis=0), jnp.concatenate([i1, i2], axis=0)
    weights = g_prob * e_prob / jnp.sum(e_prob, axis=0, keepdims=True)
    flat_e = (g_idx * EXPERTS_PER_GROUP + e_idx).reshape(1, TOP_K * t)
    onehot = (flat_e == lax.broadcasted_iota(jnp.int32, (N_EXPERTS, TOP_K * t), 0)).astype(F32)
    rank = _prefix_rank(onehot)
    counts = jnp.sum(onehot, axis=1).astype(jnp.int32)
    first_row = jnp.sum(onehot * (jnp.cumsum(counts) - counts).astype(F32)[:, None], axis=0).astype(jnp.int32)
    dest = first_row + rank
    buf = _sc_scatter(tokens, dest)
    yb = _moe_experts(buf, _expert_items(counts, TOP_K * t), w1, w3, w2, layer)
    return _sc_gather(yb, dest).reshape(pieces, TOP_K, t, dp), weights.T


def kernel(x, c, ctx, c_ctx, w_mod, b_mod, norm_mix, norm_ffn, w_in_ab, w_out_ab, qn_a, kn_a, sink_a, qn_b, kn_b, w_in_c, w_out_c, qn_c, kn_c, lam_c, subln_c, w_group, b_group, w_expert, b_expert, w1, w3, w2):
    b, s_lat, d = x.shape
    n_ctx = ctx.shape[1]
    assert b == 1 and n_ctx == ROW_BLOCK and s_lat % (2 * ROW_BLOCK) == 0
    depth = w_mod.shape[0]
    cos, sin = _rope_tables(s_lat, n_ctx)
    mods = _mod_vectors(c, c_ctx, w_mod, b_mod)
    xs = jnp.concatenate([ctx[0], x[0]], axis=0)
    has_ctx = True
    pending = None
    for l in range(depth):
        last = l == depth - 1
        i = l // 2
        sh1, sc1, gt1, sh2, sc2, gt2 = [mods[l, :2, j * d:(j + 1) * d].reshape(2, 1, d) for j in range(6)]
        gain1 = norm_mix[l].astype(F32) * (1 + sc1)
        if l % 2 == 0:
            xs, heads = _inproj_ab(xs, pending, gain1, sh1, w_in_ab[i].astype(BF16), qn_a[i], kn_a[i], qn_b[i], kn_b[i], cos, sin)
            o_lat, o_ctx = _mixer_ab(heads, n_ctx, sink_a[i], not last,
                                     _logits_bounded(qn_a[i], kn_a[i]), _logits_bounded(qn_b[i], kn_b[i]))
            w_out = w_out_ab[i]
        else:
            lam_init = 0.8 - 0.6 * math.exp(-0.3 * l)
            xs, heads = _inproj_c(xs, pending, gain1, sh1, w_in_c[i].astype(BF16), qn_c[i], kn_c[i], cos, sin)
            o_lat, o_ctx = _mixer_c(heads, n_ctx, lam_c[i], subln_c[i], lam_init, not last,
                                    _logits_bounded(qn_c[i], kn_c[i]))
            w_out = w_out_c[i]
        if last:
            xs, has_ctx = xs[n_ctx:], False
        gain2 = norm_ffn[l].astype(F32) * (1 + sc2)
        w_router = jnp.zeros((d, LANES), F32).at[:, :N_GROUPS].set(w_group[l]).at[:, N_GROUPS:N_GROUPS + N_EXPERTS].set(w_expert[l])
        w_router = jnp.stack(_split_bf16(w_router))
        xs, tokens, logits = _out_proj(o_lat, o_ctx, w_out.astype(BF16), xs, gt1, gain2, sh2, w_router)
        y2, weights = _hier_moe(tokens, logits, b_group[l], b_expert[l], w1, w3, w2, l)
        pending = (y2, weights, gt2)
    return _combine(y2, weights, xs, gt2, has_ctx).reshape(b, s_lat, d)
```

```python
import functools
import math

import jax
import jax.numpy as jnp
from jax import lax
from jax.experimental import pallas as pl
from jax.experimental.pallas import tpu as pltpu
from jax.experimental.pallas import tpu_sc as plsc

F32 = jnp.float32
BF16 = jnp.bfloat16

GRID_W = 64
HEAD_DIM = 64
WINDOW = 128
ROPE_THETA = 10000.0
EPS = 1e-6
NEG = -1e30
N_HEADS_A, N_KV_A = 8, 2
N_HEADS_B, N_KV_B = 8, 2
GROUP = N_HEADS_A // N_KV_A
QA_W, KVA_W = N_HEADS_A * HEAD_DIM, N_KV_A * HEAD_DIM
QB_W, KVB_W = N_HEADS_B * HEAD_DIM, N_KV_B * HEAD_DIM
N_HEADS_C = 8
DV_C = 2 * HEAD_DIM
N_GROUPS, EXPERTS_PER_GROUP, TOP_K = 4, 8, 2
N_EXPERTS = N_GROUPS * EXPERTS_PER_GROUP
LANES = 128
ROW_BLOCK = 256
EXPERT_BLOCK = 512
SC_GATHER_WINDOW = 128
SC_ROW_SPLIT = 4
KEY_BLOCK = 3328
GQA_V_ROWS = 128
DIFF_V_ROWS = 128
VMEM_LIMIT = 48 * 1024 * 1024
LOG2E = math.log2(math.e)
Q_SCALE = HEAD_DIM ** -0.5 * LOG2E
LOG2_LOGIT_BOUND = 60.0


def _params(*sem):
    return pltpu.CompilerParams(dimension_semantics=sem, vmem_limit_bytes=VMEM_LIMIT)


def _lane(shape):
    return lax.broadcasted_iota(jnp.int32, shape, 1)


def _mod_vec_kernel(a_ref, w_ref, b_ref, o_ref):
    a = a_ref[...]
    a = a * jax.nn.sigmoid(a)
    o_ref[0] = jnp.dot(a, w_ref[0], preferred_element_type=F32, precision=lax.Precision.HIGHEST) + b_ref[0]


def _mod_vectors(c, c_ctx, w_mod, b_mod):
    depth, d, n = w_mod.shape
    a = jnp.zeros((8, d), F32).at[0].set(c_ctx).at[1].set(c[0])
    bn = 1024
    return pl.pallas_call(
        _mod_vec_kernel,
        grid=(depth, n // bn),
        in_specs=[pl.BlockSpec((8, d), lambda l, j: (0, 0)),
                  pl.BlockSpec((1, d, bn), lambda l, j: (l, 0, j)),
                  pl.BlockSpec((1, 1, bn), lambda l, j: (l, 0, j))],
        out_specs=pl.BlockSpec((1, 8, bn), lambda l, j: (l, 0, j)),
        out_shape=jax.ShapeDtypeStruct((depth, 8, n), F32),
        compiler_params=_params("parallel", "parallel"),
        name="mod_vectors",
    )(a, w_mod, b_mod.reshape(depth, 1, n))


def _modulated(x, g_ref, s_ref):
    y = x * lax.rsqrt(jnp.mean(x * x, axis=-1, keepdims=True) + EPS)
    row = jnp.minimum(pl.program_id(0), 1)
    return y * g_ref[row] + s_ref[row]


def _split_bf16(x):
    top = lax.bitcast_convert_type(lax.bitcast_convert_type(x, jnp.uint32) & jnp.uint32(0xFFFF0000), F32)
    return top.astype(BF16), (x - top).astype(BF16)


def _head_norm_rope(x, gain, cos, sin, seg_mean):
    hi, lo = _split_bf16(x * x)
    ms = jnp.dot(hi, seg_mean, preferred_element_type=F32) + jnp.dot(lo, seg_mean, preferred_element_type=F32)
    y = x * lax.rsqrt(ms + EPS) * gain
    first_half = (_lane(y.shape) & (HEAD_DIM - 1)) < HEAD_DIM // 2
    partner = jnp.where(first_half, pltpu.roll(y, LANES - HEAD_DIM // 2, 1), pltpu.roll(y, HEAD_DIM // 2, 1))
    return y * cos + partner * sin


def _store_value_heads(v_ref, v):
    lane = _lane(v.shape)
    ones_col = jnp.where(lane == HEAD_DIM, 1.0, 0.0)
    v_ref[0] = jnp.where(lane < HEAD_DIM, v, ones_col).astype(BF16)
    v_ref[1] = jnp.where(lane < HEAD_DIM, pltpu.roll(v, HEAD_DIM, 1), ones_col).astype(BF16)


def _ones_row_block(rows, cols):
    return jnp.where(lax.broadcasted_iota(jnp.int32, (rows, cols), 0) == 0, 1.0, 0.0)


def _store_value_heads_t(vt_ref, v):
    vt = v.T
    tail = _ones_row_block(GQA_V_ROWS - HEAD_DIM, v.shape[0])
    vt_ref[0, 0] = jnp.concatenate([vt[:HEAD_DIM], tail], axis=0).astype(BF16)
    vt_ref[1, 0] = jnp.concatenate([vt[HEAD_DIM:], tail], axis=0).astype(BF16)


N_INPROJ_INPUTS = 8


def _inproj_rows(refs, fused):
    if not fused:
        return refs[0][...], refs[1:N_INPROJ_INPUTS], refs[N_INPROJ_INPUTS:]
    y_ref, wts_ref, gate_ref, x_ref = refs[:4]
    ins, outs = refs[4:N_INPROJ_INPUTS + 3], refs[N_INPROJ_INPUTS + 3:]
    wts = wts_ref[...]
    f = wts[:, 0:1] * _load_pieces(y_ref[:, 0]) + wts[:, 1:2] * _load_pieces(y_ref[:, 1])
    x = x_ref[...] + gate_ref[jnp.minimum(pl.program_id(0), 1)] * f
    outs[0][...] = x
    return x, ins, outs[1:]


def _inproj_ab_kernel(*refs, fused):
    x, (g_ref, s_ref, w_ref, hg_ref, cos_ref, sin_ref, seg_ref), outs = _inproj_rows(refs, fused)
    qa_ref, qb_ref, ka_ref, kb_ref, va_ref, vat_ref, vbt_ref = outs
    bm = x.shape[0]
    h = _modulated(x, g_ref, s_ref).astype(BF16)
    proj = jnp.dot(h, w_ref[...], preferred_element_type=F32)
    cos, sin, seg = cos_ref[...], sin_ref[...], seg_ref[...]
    low = _lane((bm, LANES)) < HEAD_DIM

    def normed(j):
        cols = slice(j * LANES, (j + 1) * LANES)
        return _head_norm_rope(proj[:, cols], hg_ref[:, cols], cos, sin, seg)

    def store_queries(q_ref, tile0):
        for j in range(N_HEADS_A // 2):
            y = normed(tile0 + j)
            swapped = pltpu.roll(y, HEAD_DIM, 1)
            kv, g0 = (2 * j) // GROUP, (2 * j) % GROUP
            if kv == 0:
                even, odd = jnp.where(low, y, 0.0), jnp.where(low, swapped, 0.0)
            else:
                even, odd = jnp.where(low, 0.0, swapped), jnp.where(low, 0.0, y)
            q_ref[kv, 0, g0 * bm:(g0 + 1) * bm, :] = even.astype(BF16)
            q_ref[kv, 0, (g0 + 1) * bm:(g0 + 2) * bm, :] = odd.astype(BF16)

    tq = QA_W // LANES
    store_queries(qa_ref, 0)
    ka_ref[0] = normed(tq).astype(BF16)
    va = proj[:, (tq + 1) * LANES:(tq + 2) * LANES]
    _store_value_heads(va_ref, va)
    _store_value_heads_t(vat_ref, va)
    store_queries(qb_ref, tq + 2)
    kb_ref[0] = normed(2 * tq + 2).astype(BF16)
    _store_value_heads_t(vbt_ref, proj[:, (2 * tq + 3) * LANES:(2 * tq + 4) * LANES])


def _inproj_c_kernel(*refs, fused):
    x, (g_ref, s_ref, w_ref, hg_ref, cos_ref, sin_ref, seg_ref), (q_ref, k_ref, v_ref) = _inproj_rows(refs, fused)
    bm = x.shape[0]
    h = _modulated(x, g_ref, s_ref).astype(BF16)
    proj = jnp.dot(h, w_ref[...], preferred_element_type=F32)
    cos, sin, seg = cos_ref[...], sin_ref[...], seg_ref[...]
    low = _lane((bm, LANES)) < HEAD_DIM

    def normed(j):
        cols = slice(j * LANES, (j + 1) * LANES)
        return _head_norm_rope(proj[:, cols], hg_ref[:, cols], cos, sin, seg)

    for j in range(N_HEADS_C):
        y = normed(j)
        q_ref[j, 0, 0:bm, :] = jnp.where(low, y, 0.0).astype(BF16)
        q_ref[j, 0, bm:2 * bm, :] = jnp.where(low, 0.0, y).astype(BF16)
        k_ref[j] = normed(N_HEADS_C + j).astype(BF16)
        v = proj[:, (2 * N_HEADS_C + j) * LANES:(2 * N_HEADS_C + j + 1) * LANES]
        v_ref[j, 0] = v.T.astype(BF16)


def _rope_tables(seq, n_ctx):
    rows_n = seq // GRID_W
    rows = jnp.broadcast_to(jnp.arange(rows_n, dtype=F32)[:, None], (rows_n, GRID_W)).reshape(-1)
    cols = jnp.broadcast_to(jnp.arange(GRID_W, dtype=F32)[None, :], (rows_n, GRID_W)).reshape(-1)
    half = HEAD_DIM // 2
    inv = ROPE_THETA ** (-jnp.arange(0, half, 2, dtype=F32) / half)
    ang = jnp.concatenate([rows[:, None] * inv, cols[:, None] * inv], axis=-1)
    reps = LANES // half
    sign = jnp.tile(jnp.concatenate([-jnp.ones((half,), F32), jnp.ones((half,), F32)]), LANES // HEAD_DIM)
    cos = jnp.pad(jnp.tile(jnp.cos(ang), (1, reps)), ((n_ctx, 0), (0, 0)), constant_values=1.0)
    sin = jnp.pad(jnp.tile(jnp.sin(ang), (1, reps)) * sign, ((n_ctx, 0), (0, 0)))
    return cos, sin


def _segment_mean_matrix():
    idx = jnp.arange(LANES) // HEAD_DIM
    return jnp.where(idx[:, None] == idx[None, :], 1.0 / HEAD_DIM, 0.0).astype(BF16)


def _q_slot(i, nblk):
    return (i + nblk - 1) % nblk


def _inproj_common_specs(t, d, n):
    bm = ROW_BLOCK
    return [pl.BlockSpec((bm, d), lambda i: (i, 0)),
            pl.BlockSpec((2, 1, d), lambda i: (0, 0, 0)),
            pl.BlockSpec((2, 1, d), lambda i: (0, 0, 0)),
            pl.BlockSpec((d, n), lambda i: (0, 0)),
            pl.BlockSpec((1, n), lambda i: (0, 0)),
            pl.BlockSpec((bm, LANES), lambda i: (i, 0)),
            pl.BlockSpec((bm, LANES), lambda i: (i, 0)),
            pl.BlockSpec((LANES, LANES), lambda i: (0, 0))]


def _inproj_call(kernel_fn, name, x, pending, args, n, out_specs, out_shape):
    t, d = x.shape
    bm = ROW_BLOCK
    in_specs = _inproj_common_specs(t, d, n)
    args = [x] + list(args)
    fused = pending is not None
    if fused:
        y2, wts, gate2 = pending
        pieces, _, _, dp = y2.shape
        in_specs = [pl.BlockSpec((pieces, TOP_K, bm, dp), lambda i: (0, 0, i, 0)),
                    pl.BlockSpec((bm, TOP_K), lambda i: (i, 0)),
                    pl.BlockSpec((2, 1, d), lambda i: (0, 0, 0))] + in_specs
        args = [y2, wts, gate2] + args
        out_specs = [pl.BlockSpec((bm, d), lambda i: (i, 0))] + list(out_specs)
        out_shape = [jax.ShapeDtypeStruct((t, d), F32)] + list(out_shape)
    outs = pl.pallas_call(
        functools.partial(kernel_fn, fused=fused),
        grid=(t // bm,),
        in_specs=in_specs,
        out_specs=out_specs,
        out_shape=out_shape,
        compiler_params=_params("parallel"),
        name=name,
    )(*args)
    return (outs[0], outs[1:]) if fused else (x, outs)


def _inproj_ab(x, pending, gain2, shift2, w, qn_a, kn_a, qn_b, kn_b, cos, sin):
    t, d = x.shape
    n = w.shape[1]
    bm, nblk = ROW_BLOCK, t // ROW_BLOCK
    tile = lambda g, reps: jnp.tile(g.astype(F32), reps)
    ones_v = jnp.ones((KVA_W,), F32)
    hg = jnp.concatenate([tile(qn_a, N_HEADS_A) * Q_SCALE, tile(kn_a, N_KV_A), ones_v,
                          tile(qn_b, N_HEADS_B) * Q_SCALE, tile(kn_b, N_KV_B), ones_v]).reshape(1, n)
    q_shape = jax.ShapeDtypeStruct((N_KV_A, nblk, GROUP * bm, LANES), BF16)
    k_shape = jax.ShapeDtypeStruct((1, t, LANES), BF16)
    v_shape = jax.ShapeDtypeStruct((N_KV_A, t, LANES), BF16)
    vt_shape = jax.ShapeDtypeStruct((N_KV_A, nblk, GQA_V_ROWS, bm), BF16)
    q_spec = pl.BlockSpec((N_KV_A, 1, GROUP * bm, LANES), lambda i: (0, _q_slot(i, nblk), 0, 0))
    k_spec = pl.BlockSpec((1, bm, LANES), lambda i: (0, i, 0))
    v_spec = pl.BlockSpec((N_KV_A, bm, LANES), lambda i: (0, i, 0))
    vt_spec = pl.BlockSpec((N_KV_A, 1, GQA_V_ROWS, bm), lambda i: (0, i, 0, 0))
    return _inproj_call(_inproj_ab_kernel, "inproj_ab", x, pending,
                        [gain2, shift2, w, hg, cos, sin, _segment_mean_matrix()], n,
                        [q_spec, q_spec, k_spec, k_spec, v_spec, vt_spec, vt_spec],
                        [q_shape, q_shape, k_shape, k_shape, v_shape, vt_shape, vt_shape])


def _inproj_c(x, pending, gain2, shift2, w, qn, kn, cos, sin):
    t, d = x.shape
    n = w.shape[1]
    bm, nblk = ROW_BLOCK, t // ROW_BLOCK
    h = N_HEADS_C
    tile = lambda g: jnp.tile(g.astype(F32), 2 * h)
    hg = jnp.concatenate([tile(qn) * Q_SCALE, tile(kn), jnp.ones((h * DV_C,), F32)]).reshape(1, n)
    return _inproj_call(_inproj_c_kernel, "inproj_c", x, pending,
                        [gain2, shift2, w, hg, cos, sin, _segment_mean_matrix()], n,
                        [pl.BlockSpec((h, 1, 2 * bm, LANES), lambda i: (0, _q_slot(i, nblk), 0, 0)),
                         pl.BlockSpec((h, bm, LANES), lambda i: (0, i, 0)),
                         pl.BlockSpec((h, 1, DIFF_V_ROWS, bm), lambda i: (0, i, 0, 0))],
                        [jax.ShapeDtypeStruct((h, nblk, 2 * bm, LANES), BF16),
                         jax.ShapeDtypeStruct((h, t, LANES), BF16),
                         jax.ShapeDtypeStruct((h, nblk, DIFF_V_ROWS, bm), BF16)])


def _pad_rows(x):
    rows = x.shape[0]
    if rows >= LANES:
        return x[:LANES]
    return jnp.concatenate([x, jnp.zeros((LANES - rows, x.shape[1]), x.dtype)], axis=0)


def _merge_gqa_heads(o, bq):
    low = _lane((bq, LANES)) < HEAD_DIM
    pairs = [jnp.where(low, o[g * bq:(g + 1) * bq], pltpu.roll(o[(g + 1) * bq:(g + 2) * bq], HEAD_DIM, 1))
             for g in range(0, GROUP, 2)]
    return jnp.concatenate(pairs, axis=1)


def _flash_kernel(*refs, mode, online, n_keys, bk, dv, l0):
    refs = list(refs)
    q_ref, k_ref, vt_ref = refs[:3]
    pos = 3
    if online:
        m0_ref = refs[pos]
        pos += 1
    if mode == "diff":
        lam_ref, sub_ref = refs[pos:pos + 2]
        pos += 2
    o_ref = refs[pos]
    scratch = refs[pos + 1:]
    acc_sc = scratch[0]
    ones_row = acc_sc.shape[0] > dv
    m_sc = scratch[1] if online else None
    l_sc = None if ones_row else scratch[-1]

    nsub, rb = q_ref.shape[1], q_ref.shape[2]
    r = nsub * rb
    ch = vt_ref.shape[3]
    q = q_ref[0].reshape(r, LANES)
    acc_sc[...] = jnp.where(lax.broadcasted_iota(jnp.int32, acc_sc.shape, 0) == dv, l0, 0.0).astype(F32)
    if online:
        m_sc[...] = m0_ref[0]
    if not ones_row:
        l_sc[...] = jnp.full(l_sc.shape, l0, F32)

    def block(start, size):
        kb = k_ref[0, pl.ds(start, size), :]
        st = lax.dot_general(kb, q, (((1,), (1,)), ((), ())), preferred_element_type=F32)
        if online:
            m_prev = m_sc[...]
            m_new = jnp.maximum(m_prev, jnp.max(st, axis=0, keepdims=True))
            p = jnp.exp2(st - m_new)
            alpha = jnp.exp2(m_prev - m_new)
            acc = alpha * acc_sc[...]
            m_sc[...] = m_new
        else:
            p = jnp.exp2(st)
            alpha = 1.0
            acc = acc_sc[...]
        if not ones_row:
            l_sc[...] = alpha * l_sc[...] + jnp.sum(p, axis=0, keepdims=True)
        pt = p.astype(BF16)
        c0 = start // ch
        for c in range(size // ch):
            acc = acc + jnp.dot(vt_ref[0, c0 + c], pt[c * ch:(c + 1) * ch], preferred_element_type=F32)
        acc_sc[...] = acc

    n_full, tail = n_keys // bk, n_keys % bk
    if n_full:
        def body(i, carry):
            block(pl.multiple_of(i * bk, bk), bk)
            return carry
        lax.fori_loop(0, n_full, body, 0)
    if tail:
        block(n_full * bk, tail)

    acc = acc_sc[...]
    den = acc[dv:dv + 1] if ones_row else l_sc[...]
    ot = _pad_rows(acc / den)
    bq = rb // (GROUP if mode == "gqa" else 2)
    o = jnp.concatenate([ot[:, j * bq:(j + 1) * bq].T for j in range(r // bq)], axis=0)
    if mode == "gqa":
        o_ref[...] = _merge_gqa_heads(o, bq).astype(o_ref.dtype)
    else:
        for b in range(nsub):
            d = o[b * rb:b * rb + bq] - lam_ref[...] * o[b * rb + bq:(b + 1) * rb]
            y = d * lax.rsqrt(jnp.mean(d * d, axis=-1, keepdims=True) + EPS) * sub_ref[...]
            o_ref[b * bq:(b + 1) * bq, :] = y.astype(o_ref.dtype)


def _flash_call(q, k, vt, extra, *, mode, online, n_keys, slot0, nsub, n_steps, l0=0.0):
    hkv, _, rb, _ = q.shape
    hk = k.shape[0]
    _, _, dvr, ch = vt.shape
    dv = HEAD_DIM if mode == "gqa" else DV_C
    bq = rb // (GROUP if mode == "gqa" else 2)
    ocols = GROUP * HEAD_DIM if mode == "gqa" else DV_C
    r = nsub * rb
    bk = min(KEY_BLOCK, n_keys)
    in_specs = [pl.BlockSpec((1, nsub, rb, LANES), lambda h, i: (h, slot0 // nsub + i, 0, 0)),
                pl.BlockSpec((1, n_keys, LANES), (lambda h, i: (h, 0, 0)) if hk > 1 else (lambda h, i: (0, 0, 0))),
                pl.BlockSpec((1, n_keys // ch, dvr, ch), lambda h, i: (h, 0, 0, 0))]
    args = [q, k, vt]
    scratch = [pltpu.VMEM((dvr, r), F32)]
    if online:
        m0 = extra.pop(0)
        in_specs.append(pl.BlockSpec((1, 1, r), lambda h, i: (h, 0, 0)))
        args.append(m0)
        scratch.append(pltpu.VMEM((1, r), F32))
    if dvr <= dv:
        scratch.append(pltpu.VMEM((1, r), F32))
    for a in extra:
        in_specs.append(pl.BlockSpec(a.shape, lambda h, i: (0, 0)))
        args.append(a)
    return pl.pallas_call(
        functools.partial(_flash_kernel, mode=mode, online=online, n_keys=n_keys, bk=bk, dv=dv, l0=l0),
        grid=(hkv, n_steps),
        in_specs=in_specs,
        out_specs=pl.BlockSpec((nsub * bq, ocols), lambda h, i: (i, h)),
        out_shape=jax.ShapeDtypeStruct((n_steps * nsub * bq, hkv * ocols), BF16),
        scratch_shapes=scratch,
        compiler_params=_params("parallel", "parallel"),
        name="flash_online" if online else "flash_bounded",
    )(*args)


def _logits_bounded(q_gain, k_gain):
    bound = HEAD_DIM * Q_SCALE * 1.02 * jnp.max(jnp.abs(q_gain.astype(F32))) * jnp.max(jnp.abs(k_gain.astype(F32)))
    return bound <= LOG2_LOGIT_BOUND


def _attend(q, k, v, extra, bounded, **kw):
    hkv, r = q.shape[0], kw["nsub"] * q.shape[2]
    fast = lambda q_, k_, v_, *e: _flash_call(q_, k_, v_, list(e), online=False, **kw)
    safe = lambda q_, k_, v_, *e: _flash_call(q_, k_, v_, [jnp.full((hkv, 1, r), NEG, F32)] + list(e), online=True, **kw)
    return lax.cond(bounded, fast, safe, q, k, v, *extra)


def _window_kernel(q_ref, k_ref, v_ref, sink_ref, o_ref, *, bq, n_ctx):
    q = q_ref[0, 0]
    r = q.shape[0]
    t = k_ref.shape[1]
    w = bq + 2 * WINDOW
    q0 = pl.program_id(1) * bq
    ws = pl.multiple_of(jnp.clip(n_ctx + q0 - WINDOW, 0, t - w), WINDOW)
    kw = k_ref[0, pl.ds(ws, w), :]
    vw = v_ref[0, pl.ds(ws, w), :]
    contract_last = (((1,), (1,)), ((), ()))
    s_loc = lax.dot_general(q, kw, contract_last, preferred_element_type=F32)
    qpos = q0 + (lax.broadcasted_iota(jnp.int32, (r, w), 0) & (bq - 1))
    kpos = ws - n_ctx + lax.broadcasted_iota(jnp.int32, (r, w), 1)
    mask = (kpos >= 0) & (kpos - qpos <= WINDOW) & (qpos - kpos <= WINDOW)
    s_loc = jnp.where(mask, s_loc, NEG)
    s_ctx = lax.dot_general(q, k_ref[0, 0:n_ctx, :], contract_last, preferred_element_type=F32)
    sink = sink_ref[0]
    m = jnp.maximum(sink, jnp.maximum(jnp.max(s_loc, axis=-1, keepdims=True), jnp.max(s_ctx, axis=-1, keepdims=True)))
    p_loc = jnp.exp2(s_loc - m)
    p_ctx = jnp.exp2(s_ctx - m)
    l = jnp.exp2(sink - m) + jnp.sum(p_loc, axis=-1, keepdims=True) + jnp.sum(p_ctx, axis=-1, keepdims=True)
    o = (jnp.dot(p_loc.astype(BF16), vw, preferred_element_type=F32)
         + jnp.dot(p_ctx.astype(BF16), v_ref[0, 0:n_ctx, :], preferred_element_type=F32))
    o_ref[...] = _merge_gqa_heads(o / l, bq).astype(o_ref.dtype)


def _window_attention(q, k, v, sink_rows, n_ctx):
    hkv, slots, r, _ = q.shape
    t = k.shape[1]
    bq = r // GROUP
    nq = slots - n_ctx // bq
    return pl.pallas_call(
        functools.partial(_window_kernel, bq=bq, n_ctx=n_ctx),
        grid=(hkv, nq),
        in_specs=[pl.BlockSpec((1, 1, r, LANES), lambda h, i: (h, i, 0, 0)),
                  pl.BlockSpec((1, t, LANES), lambda h, i: (0, 0, 0)),
                  pl.BlockSpec((1, t, LANES), lambda h, i: (h, 0, 0)),
                  pl.BlockSpec((1, r, 1), lambda h, i: (h, 0, 0))],
        out_specs=pl.BlockSpec((bq, GROUP * HEAD_DIM), lambda h, i: (i, h)),
        out_shape=jax.ShapeDtypeStruct((nq * bq, hkv * GROUP * HEAD_DIM), BF16),
        compiler_params=_params("parallel", "parallel"),
        name="window_attention",
    )(q, k, v, sink_rows)


def _window_bounded_kernel(q_ref, k_ref, vt_ref, mask_ref, sink_ref, o_ref, *, bq):
    q = q_ref[0, 0]
    ch = vt_ref.shape[3]
    cw = jnp.clip(pl.program_id(1), 0, vt_ref.shape[1] - 3)
    contract_last = (((1,), (1,)), ((), ()))
    k_win = k_ref[0, pl.ds(pl.multiple_of(cw * ch, ch), 3 * ch), :]
    st_win = lax.dot_general(k_win, q, contract_last, preferred_element_type=F32)
    st_ctx = lax.dot_general(k_ref[0, 0:ch, :], q, contract_last, preferred_element_type=F32)
    pt_win = jnp.exp2(st_win).astype(BF16) * mask_ref[0]
    acc = jnp.dot(vt_ref[0, 0], jnp.exp2(st_ctx).astype(BF16), preferred_element_type=F32)
    for c in range(3):
        acc = acc + jnp.dot(vt_ref[0, cw + c], pt_win[c * ch:(c + 1) * ch], preferred_element_type=F32)
    ot = _pad_rows(acc / (acc[HEAD_DIM:HEAD_DIM + 1] + jnp.exp2(sink_ref[0])))
    o = jnp.concatenate([ot[:, g * bq:(g + 1) * bq].T for g in range(GROUP)], axis=0)
    o_ref[...] = _merge_gqa_heads(o, bq).astype(o_ref.dtype)


def _window_masks(bq, r):
    c = jnp.arange(3 * bq)[:, None]
    off = c - (jnp.arange(r)[None, :] & (bq - 1))
    centred = (off >= bq - WINDOW) & (off <= bq + WINDOW)
    shifted = (off >= 2 * bq - WINDOW) & (off <= 2 * bq + WINDOW)
    return jnp.stack([centred & (c >= bq), centred, shifted]).astype(BF16)


def _window_attention_bounded(q, k, vt, sink_cols, n_ctx):
    hkv, slots, r, _ = q.shape
    t = k.shape[1]
    bq = r // GROUP
    nblk = t // bq
    nq = nblk - 1
    assert n_ctx == bq and nblk >= 3
    return pl.pallas_call(
        functools.partial(_window_bounded_kernel, bq=bq),
        grid=(hkv, nq),
        in_specs=[pl.BlockSpec((1, 1, r, LANES), lambda h, i: (h, i, 0, 0)),
                  pl.BlockSpec((1, t, LANES), lambda h, i: (0, 0, 0)),
                  pl.BlockSpec((1, nblk, GQA_V_ROWS, bq), lambda h, i: (h, 0, 0, 0)),
                  pl.BlockSpec((1, 3 * bq, r), lambda h, i: (jnp.where(i == 0, 0, jnp.where(i >= nblk - 2, 2, 1)), 0, 0)),
                  pl.BlockSpec((1, 1, r), lambda h, i: (h, 0, 0))],
        out_specs=pl.BlockSpec((bq, GROUP * HEAD_DIM), lambda h, i: (i, h)),
        out_shape=jax.ShapeDtypeStruct((nq * bq, hkv * GROUP * HEAD_DIM), BF16),
        compiler_params=_params("parallel", "parallel"),
        name="window_bounded",
    )(q, k, vt, _window_masks(bq, r), sink_cols)


def _sink_rows(sink, bq):
    hkv, g = sink.shape
    return jnp.broadcast_to(sink.astype(F32)[:, :, None] * LOG2E, (hkv, g, bq)).reshape(hkv, g * bq, 1)


def _mixer_ab(heads, n_ctx, sink_a, with_ctx, bounded_a, bounded):
    qa, qb, ka, kb, va, vat, vbt = heads
    t = ka.shape[1]
    bq = ROW_BLOCK
    n_lat = (t - n_ctx) // bq
    sink = _sink_rows(sink_a.reshape(N_KV_A, GROUP), bq)
    sink_cols = sink.reshape(N_KV_A, 1, GROUP * bq)
    bounded_a = bounded_a & (jnp.max(jnp.abs(sink)) <= LOG2_LOGIT_BOUND)
    oa = lax.cond(bounded_a,
                  lambda: _window_attention_bounded(qa, ka, vat, sink_cols, n_ctx),
                  lambda: _window_attention(qa, ka, va, sink, n_ctx))
    ob = _attend(qb, kb, vbt, [], bounded, mode="gqa", n_keys=t, slot0=0, nsub=1, n_steps=n_lat)
    if not with_ctx:
        return (oa, ob), None
    oca = _flash_call(qa, ka, vat, [sink_cols], mode="gqa", online=True, n_keys=n_ctx,
                      slot0=n_lat, nsub=1, n_steps=1, l0=1.0)
    ocb = _attend(qb, kb, vbt, [], bounded, mode="gqa", n_keys=n_ctx, slot0=n_lat, nsub=1, n_steps=1)
    return (oa, ob), (oca, ocb)


def _mixer_c(heads, n_ctx, lam_p, subln, lam_init, with_ctx, bounded):
    q, k, v = heads
    t = k.shape[1]
    n_lat = (t - n_ctx) // ROW_BLOCK
    lp = lam_p.astype(F32)
    lam = jnp.exp(jnp.sum(lp[0] * lp[1])) - jnp.exp(jnp.sum(lp[2] * lp[3])) + lam_init
    extra = [jnp.full((1, DV_C), lam, F32), (subln.astype(F32) * (1 - lam_init)).reshape(1, DV_C)]
    nsub = 2 if n_lat % 2 == 0 else 1
    o_lat = _attend(q, k, v, extra, bounded, mode="diff", n_keys=t, slot0=0, nsub=nsub, n_steps=n_lat // nsub)
    if not with_ctx:
        return (o_lat,), None
    o_ctx = _attend(q, k, v, extra, bounded, mode="diff", n_keys=n_ctx, slot0=n_lat, nsub=1, n_steps=1)
    return (o_lat,), (o_ctx,)


def _out_proj_kernel(*refs, n_parts, has_ctx):
    lat = refs[:n_parts]
    ctx = refs[n_parts:2 * n_parts] if has_ctx else None
    w_ref, x_ref, gate_ref, g_ref, s_ref, wr_ref, y_ref, tok_ref, logit_ref = refs[-9:]
    is_ctx = pl.program_id(0) == 0 if has_ctx else False
    row = jnp.where(is_ctx, 0, 1) if has_ctx else 1
    acc = None
    col = 0
    for p in range(n_parts):
        o = lat[p][...]
        if has_ctx:
            o = jnp.where(is_ctx, ctx[p][...], o)
        width = o.shape[1]
        part = jnp.dot(o, w_ref[col:col + width, :], preferred_element_type=F32)
        acc = part if acc is None else acc + part
        col += width
    x1 = x_ref[...] + gate_ref[row] * acc
    y_ref[...] = x1
    h = x1 * lax.rsqrt(jnp.mean(x1 * x1, axis=-1, keepdims=True) + EPS) * g_ref[row] + s_ref[row]
    _store_pieces(tok_ref, h)
    hi, lo = _split_bf16(h)
    logits = (jnp.dot(hi, wr_ref[0], preferred_element_type=F32) + jnp.dot(lo, wr_ref[0], preferred_element_type=F32)
              + jnp.dot(hi, wr_ref[1], preferred_element_type=F32))
    logit_ref[...] = logits.T


def _out_proj(o_lat, o_ctx, w, x, gate2, gain2, shift2, w_router):
    t, d = x.shape
    bm = ROW_BLOCK
    dp = d // SC_ROW_SPLIT
    has_ctx = o_ctx is not None
    n_parts = len(o_lat)
    lat_map = (lambda i: (jnp.maximum(i - 1, 0), 0)) if has_ctx else (lambda i: (i, 0))
    in_specs = [pl.BlockSpec((bm, o.shape[1]), lat_map) for o in o_lat]
    args = list(o_lat)
    if has_ctx:
        in_specs += [pl.BlockSpec((bm, o.shape[1]), lambda i: (0, 0)) for o in o_ctx]
        args += list(o_ctx)
    vec_spec = pl.BlockSpec((2, 1, d), lambda i: (0, 0, 0))
    in_specs += [pl.BlockSpec(w.shape, lambda i: (0, 0)),
                 pl.BlockSpec((bm, d), lambda i: (i, 0)),
                 vec_spec, vec_spec, vec_spec,
                 pl.BlockSpec((2, d, LANES), lambda i: (0, 0, 0))]
    return pl.pallas_call(
        functools.partial(_out_proj_kernel, n_parts=n_parts, has_ctx=has_ctx),
        grid=(t // bm,),
        in_specs=in_specs,
        out_specs=[pl.BlockSpec((bm, d), lambda i: (i, 0)),
                   pl.BlockSpec((SC_ROW_SPLIT, bm, dp), lambda i: (0, i, 0)),
                   pl.BlockSpec((LANES, bm), lambda i: (0, i))],
        out_shape=[jax.ShapeDtypeStruct((t, d), F32),
                   jax.ShapeDtypeStruct((SC_ROW_SPLIT, t, dp), F32),
                   jax.ShapeDtypeStruct((LANES, t), F32)],
        compiler_params=_params("parallel"),
        name="out_proj_router",
    )(*args, w, x, gate2, gain2, shift2, w_router)


def _store_pieces(ref, rows):
    dp = ref.shape[2]
    for j in range(ref.shape[0]):
        ref[j] = rows[:, j * dp:(j + 1) * dp]


def _load_pieces(planes):
    return jnp.concatenate([planes[j] for j in range(planes.shape[0])], axis=1)


def _moe_kernel(blk_ref, exp_ref, lo_ref, hi_ref, n_ref, x_ref, w1_ref, w3_ref, w2_ref, y_ref):
    i = pl.program_id(0)

    @pl.when(i < n_ref[0])
    def _():
        x = _load_pieces(x_ref[...]).astype(BF16)
        a = jnp.dot(x, w1_ref[0, 0].astype(BF16), preferred_element_type=F32)
        b = jnp.dot(x, w3_ref[0, 0].astype(BF16), preferred_element_type=F32)
        hidden = (a * jax.nn.sigmoid(a)) * b
        y = jnp.dot(hidden.astype(BF16), w2_ref[0, 0].astype(BF16), preferred_element_type=F32)
        rows = lax.broadcasted_iota(jnp.int32, (y.shape[0], 1), 0)
        y = jnp.where((rows >= lo_ref[i]) & (rows < hi_ref[i]), y, 0.0)
        first = (i == 0) | (blk_ref[i] != blk_ref[jnp.maximum(i - 1, 0)])

        @pl.when(first)
        def _():
            _store_pieces(y_ref, y)

        @pl.when(jnp.logical_not(first))
        def _():
            _store_pieces(y_ref, _load_pieces(y_ref[...]) + y)


def _moe_experts(buf, items, w1, w3, w2, layer):
    pieces, n_rows, dp = buf.shape
    d, de = w1.shape[2], w1.shape[3]
    n_items = items[0].shape[0]
    row_spec = pl.BlockSpec((pieces, EXPERT_BLOCK, dp), lambda i, blk, ex, lo, hi, n: (0, blk[i], 0))
    w_map = lambda i, blk, ex, lo, hi, n: (layer, ex[i], 0, 0)
    grid_spec = pltpu.PrefetchScalarGridSpec(
        num_scalar_prefetch=5,
        grid=(n_items,),
        in_specs=[row_spec,
                  pl.BlockSpec((1, 1, d, de), w_map),
                  pl.BlockSpec((1, 1, d, de), w_map),
                  pl.BlockSpec((1, 1, de, d), w_map)],
        out_specs=row_spec,
    )
    return pl.pallas_call(
        _moe_kernel,
        grid_spec=grid_spec,
        out_shape=jax.ShapeDtypeStruct((pieces, n_rows, dp), F32),
        compiler_params=_params("arbitrary"),
        name="moe_experts",
    )(*items, buf, w1, w3, w2)


def _expert_items(counts, n_rows):
    e = counts.shape[0]
    n_blk = n_rows // EXPERT_BLOCK
    n_items = n_blk + e - 1
    end = jnp.cumsum(counts)
    start = end - counts
    first_blk = start // EXPERT_BLOCK
    per_expert = jnp.where(counts > 0, (end - 1) // EXPERT_BLOCK - first_blk + 1, 0)
    cum = jnp.cumsum(per_expert)
    total = cum[-1]
    w = jnp.minimum(jnp.arange(n_items, dtype=jnp.int32), total - 1)
    onehot = ((cum - per_expert)[None, :] <= w[:, None]) & (w[:, None] < cum[None, :])
    pick = lambda tab: jnp.sum(jnp.where(onehot, tab[None, :], 0), axis=1).astype(jnp.int32)
    expert = pick(jnp.arange(e, dtype=jnp.int32))
    blk = pick(first_blk) + w - pick(cum - per_expert)
    lo = jnp.clip(pick(start) - blk * EXPERT_BLOCK, 0, EXPERT_BLOCK)
    hi = jnp.clip(pick(end) - blk * EXPERT_BLOCK, 0, EXPERT_BLOCK)
    return blk, expert, lo, hi, total.astype(jnp.int32).reshape(1)


def _sc_scatter(x, pos):
    pieces, t, dp = x.shape
    n = pos.shape[0]
    kk = n // t
    offs = jnp.arange(pieces, dtype=jnp.int32)[:, None] * n
    idx = [(offs + pos[k * t:(k + 1) * t][None, :]).reshape(1, pieces * t) for k in range(kk)]
    mesh = plsc.VectorSubcoreMesh(core_axis_name="core", subcore_axis_name="subcore")

    @pl.kernel(out_type=jax.ShapeDtypeStruct((pieces * n, dp), x.dtype), mesh=mesh, scratch_types=[])
    def scatter(x_hbm, *refs):
        i_hbm, o_hbm = refs[:kk], refs[kk]

        def body(x_vmem, *i_vmem):
            for iv in i_vmem:
                pltpu.sync_copy(x_vmem, o_hbm.at[iv.at[0]])

        pltpu.emit_pipeline(
            body,
            grid=(pieces * t // SC_GATHER_WINDOW,),
            in_specs=[pl.BlockSpec((SC_GATHER_WINDOW, dp), lambda i: (i, 0))]
                     + [pl.BlockSpec((1, SC_GATHER_WINDOW), lambda i: (0, i))] * kk,
            out_specs=[],
            core_axis_name=("core", "subcore"),
            dimension_semantics=(pltpu.PARALLEL,),
        )(x_hbm, *i_hbm)

    return scatter(x.reshape(pieces * t, dp), *idx).reshape(pieces, n, dp)


def _sc_gather(x, idx):
    pieces, t, dp = x.shape
    n = idx.shape[0]
    flat = (jnp.arange(pieces, dtype=jnp.int32)[:, None] * t + idx[None, :]).reshape(1, pieces * n)
    mesh = plsc.VectorSubcoreMesh(core_axis_name="core", subcore_axis_name="subcore")

    @pl.kernel(out_type=jax.ShapeDtypeStruct((pieces * n, dp), x.dtype), mesh=mesh, scratch_types=[])
    def gather(x_hbm, i_hbm, o_hbm):
        def body(i_vmem, o_vmem):
            pltpu.sync_copy(x_hbm.at[i_vmem.at[0]], o_vmem)

        pltpu.emit_pipeline(
            body,
            grid=(pieces * n // SC_GATHER_WINDOW,),
            in_specs=[pl.BlockSpec((1, SC_GATHER_WINDOW), lambda i: (0, i))],
            out_specs=[pl.BlockSpec((SC_GATHER_WINDOW, dp), lambda i: (i, 0))],
            core_axis_name=("core", "subcore"),
            dimension_semantics=(pltpu.PARALLEL,),
        )(i_hbm, o_hbm)

    return gather(x.reshape(pieces * t, dp), flat).reshape(pieces, n, dp)


def _combine_kernel(y_ref, w_ref, x_ref, gate_ref, o_ref, *, row0):
    w = w_ref[...]
    f = w[:, 0:1] * _load_pieces(y_ref[:, 0]) + w[:, 1:2] * _load_pieces(y_ref[:, 1])
    row = jnp.minimum(pl.program_id(0) + row0, 1)
    o_ref[...] = x_ref[...] + gate_ref[row] * f


def _combine(y2, weights, x, gate2, has_ctx):
    t, d = x.shape
    bm = ROW_BLOCK
    pieces, _, _, dp = y2.shape
    return pl.pallas_call(
        functools.partial(_combine_kernel, row0=0 if has_ctx else 1),
        grid=(t // bm,),
        in_specs=[pl.BlockSpec((pieces, TOP_K, bm, dp), lambda i: (0, 0, i, 0)),
                  pl.BlockSpec((bm, TOP_K), lambda i: (i, 0)),
                  pl.BlockSpec((bm, d), lambda i: (i, 0)),
                  pl.BlockSpec((2, 1, d), lambda i: (0, 0, 0))],
        out_specs=pl.BlockSpec((bm, d), lambda i: (i, 0)),
        out_shape=jax.ShapeDtypeStruct((t, d), F32),
        compiler_params=_params("parallel"),
        name="moe_combine",
    )(y2, weights, x, gate2)


def _top1_rows(p):
    m = jnp.max(p, axis=0, keepdims=True)
    idx = lax.broadcasted_iota(jnp.int32, p.shape, 0)
    return m, jnp.min(jnp.where(p == m, idx, p.shape[0]), axis=0, keepdims=True)


def _prefix_rank(onehot):
    e, n = onehot.shape
    blk = EXPERT_BLOCK
    nb = n // blk
    earlier = (jnp.arange(blk)[:, None] < jnp.arange(blk)[None, :]).astype(F32)
    within = jnp.dot(onehot.reshape(e * nb, blk), earlier).reshape(e, n)
    tot = jnp.sum(onehot.reshape(e, nb, blk), axis=2)
    before = jnp.repeat(jnp.cumsum(tot, axis=1) - tot, blk, axis=1)
    return jnp.sum((within + before) * onehot, axis=0).astype(jnp.int32)


def _hier_moe(tokens, logits_t, b_group, b_expert, w1, w3, w2, layer):
    pieces, t, dp = tokens.shape
    pg = jax.nn.softmax(logits_t[:N_GROUPS] + b_group.astype(F32)[:, None], axis=0)
    g_prob, g_idx = _top1_rows(pg)
    le = (logits_t[N_GROUPS:N_GROUPS + N_EXPERTS] + b_expert.astype(F32)[:, None]).reshape(N_GROUPS, EXPERTS_PER_GROUP, t)
    group_iota = lax.broadcasted_iota(jnp.int32, (N_GROUPS, 1, t), 0)
    le = jnp.sum(jnp.where(group_iota == g_idx[None], le, 0.0), axis=0)
    pe = jax.nn.softmax(le, axis=0)
    p1, i1 = _top1_rows(pe)
    p2, i2 = _top1_rows(jnp.where(lax.broadcasted_iota(jnp.int32, pe.shape, 0) == i1, -1.0, pe))
    e_prob, e_idx = jnp.concatenate([p1, p2], axis=0), jnp.concatenate([i1, i2], axis=0)
    weights = g_prob * e_prob / jnp.sum(e_prob, axis=0, keepdims=True)
    flat_e = (g_idx * EXPERTS_PER_GROUP + e_idx).reshape(1, TOP_K * t)
    onehot = (flat_e == lax.broadcasted_iota(jnp.int32, (N_EXPERTS, TOP_K * t), 0)).astype(F32)
    rank = _prefix_rank(onehot)
    counts = jnp.sum(onehot, axis=1).astype(jnp.int32)
    first_row = jnp.sum(onehot * (jnp.cumsum(counts) - counts).astype(F32)[:, None], axis=0).astype(jnp.int32)
    dest = first_row + rank
    buf = _sc_scatter(tokens, dest)
    yb = _moe_experts(buf, _expert_items(counts, TOP_K * t), w1, w3, w2, layer)
    return _sc_gather(yb, dest).reshape(pieces, TOP_K, t, dp), weights.T


def kernel(x, c, ctx, c_ctx, w_mod, b_mod, norm_mix, norm_ffn, w_in_ab, w_out_ab, qn_a, kn_a, sink_a, qn_b, kn_b, w_in_c, w_out_c, qn_c, kn_c, lam_c, subln_c, w_group, b_group, w_expert, b_expert, w1, w3, w2):
    b, s_lat, d = x.shape
    n_ctx = ctx.shape[1]
    assert b == 1 and n_ctx == ROW_BLOCK and s_lat % (2 * ROW_BLOCK) == 0
    depth = w_mod.shape[0]
    cos, sin = _rope_tables(s_lat, n_ctx)
    mods = _mod_vectors(c, c_ctx, w_mod, b_mod)
    xs = jnp.concatenate([ctx[0], x[0]], axis=0)
    has_ctx = True
    pending = None
    for l in range(depth):
        last = l == depth - 1
        i = l // 2
        sh1, sc1, gt1, sh2, sc2, gt2 = [mods[l, :2, j * d:(j + 1) * d].reshape(2, 1, d) for j in range(6)]
        gain1 = norm_mix[l].astype(F32) * (1 + sc1)
        if l % 2 == 0:
            xs, heads = _inproj_ab(xs, pending, gain1, sh1, w_in_ab[i].astype(BF16), qn_a[i], kn_a[i], qn_b[i], kn_b[i], cos, sin)
            o_lat, o_ctx = _mixer_ab(heads, n_ctx, sink_a[i], not last,
                                     _logits_bounded(qn_a[i], kn_a[i]), _logits_bounded(qn_b[i], kn_b[i]))
            w_out = w_out_ab[i]
        else:
            lam_init = 0.8 - 0.6 * math.exp(-0.3 * l)
            xs, heads = _inproj_c(xs, pending, gain1, sh1, w_in_c[i].astype(BF16), qn_c[i], kn_c[i], cos, sin)
            o_lat, o_ctx = _mixer_c(heads, n_ctx, lam_c[i], subln_c[i], lam_init, not last,
                                    _logits_bounded(qn_c[i], kn_c[i]))
            w_out = w_out_c[i]
        if last:
            xs, has_ctx = xs[n_ctx:], False
        gain2 = norm_ffn[l].astype(F32) * (1 + sc2)
        w_router = jnp.zeros((d, LANES), F32).at[:, :N_GROUPS].set(w_group[l]).at[:, N_GROUPS:N_GROUPS + N_EXPERTS].set(w_expert[l])
        w_router = jnp.stack(_split_bf16(w_router))
        xs, tokens, logits = _out_proj(o_lat, o_ctx, w_out.astype(BF16), xs, gt1, gain2, sh2, w_router)
        y2, weights = _hier_moe(tokens, logits, b_group[l], b_expert[l], w1, w3, w2, l)
        pending = (y2, weights, gt2)
    return _combine(y2, weights, xs, gt2, has_ctx).reshape(b, s_lat, d)
```

```python
import functools
import math

import jax
import jax.numpy as jnp
from jax import lax
from jax.experimental import pallas as pl
from jax.experimental.pallas import tpu as pltpu
from jax.experimental.pallas import tpu_sc as plsc

F32 = jnp.float32
BF16 = jnp.bfloat16

GRID_W = 64
HEAD_DIM = 64
WINDOW = 128
ROPE_THETA = 10000.0
EPS = 1e-6
NEG = -1e30
N_HEADS_A, N_KV_A = 8, 2
N_HEADS_B, N_KV_B = 8, 2
GROUP = N_HEADS_A // N_KV_A
QA_W, KVA_W = N_HEADS_A * HEAD_DIM, N_KV_A * HEAD_DIM
QB_W, KVB_W = N_HEADS_B * HEAD_DIM, N_KV_B * HEAD_DIM
N_HEADS_C = 8
DV_C = 2 * HEAD_DIM
N_GROUPS, EXPERTS_PER_GROUP, TOP_K = 4, 8, 2
N_EXPERTS = N_GROUPS * EXPERTS_PER_GROUP
LANES = 128
ROW_BLOCK = 256
EXPERT_BLOCK = 512
SC_GATHER_WINDOW = 128
SC_ROW_SPLIT = 4
KEY_BLOCK = 3328
GQA_V_ROWS = 128
DIFF_V_ROWS = 128
VMEM_LIMIT = 48 * 1024 * 1024
LOG2E = math.log2(math.e)
Q_SCALE = HEAD_DIM ** -0.5 * LOG2E
LOG2_LOGIT_BOUND = 60.0


def _params(*sem):
    return pltpu.CompilerParams(dimension_semantics=sem, vmem_limit_bytes=VMEM_LIMIT)


def _lane(shape):
    return lax.broadcasted_iota(jnp.int32, shape, 1)


def _mod_vec_kernel(a_ref, w_ref, b_ref, o_ref):
    a = a_ref[...]
    a = a * jax.nn.sigmoid(a)
    o_ref[0] = jnp.dot(a, w_ref[0], preferred_element_type=F32, precision=lax.Precision.HIGHEST) + b_ref[0]


def _mod_vectors(c, c_ctx, w_mod, b_mod):
    depth, d, n = w_mod.shape
    a = jnp.zeros((8, d), F32).at[0].set(c_ctx).at[1].set(c[0])
    bn = 1024
    return pl.pallas_call(
        _mod_vec_kernel,
        grid=(depth, n // bn),
        in_specs=[pl.BlockSpec((8, d), lambda l, j: (0, 0)),
                  pl.BlockSpec((1, d, bn), lambda l, j: (l, 0, j)),
                  pl.BlockSpec((1, 1, bn), lambda l, j: (l, 0, j))],
        out_specs=pl.BlockSpec((1, 8, bn), lambda l, j: (l, 0, j)),
        out_shape=jax.ShapeDtypeStruct((depth, 8, n), F32),
        compiler_params=_params("parallel", "parallel"),
        name="mod_vectors",
    )(a, w_mod, b_mod.reshape(depth, 1, n))


def _modulated(x, g_ref, s_ref):
    y = x * lax.rsqrt(jnp.mean(x * x, axis=-1, keepdims=True) + EPS)
    row = jnp.minimum(pl.program_id(0), 1)
    return y * g_ref[row] + s_ref[row]


def _split_bf16(x):
    top = lax.bitcast_convert_type(lax.bitcast_convert_type(x, jnp.uint32) & jnp.uint32(0xFFFF0000), F32)
    return top.astype(BF16), (x - top).astype(BF16)


def _head_norm_rope(x, gain, cos, sin, seg_mean):
    hi, lo = _split_bf16(x * x)
    ms = jnp.dot(hi, seg_mean, preferred_element_type=F32) + jnp.dot(lo, seg_mean, preferred_element_type=F32)
    y = x * lax.rsqrt(ms + EPS) * gain
    first_half = (_lane(y.shape) & (HEAD_DIM - 1)) < HEAD_DIM // 2
    partner = jnp.where(first_half, pltpu.roll(y, LANES - HEAD_DIM // 2, 1), pltpu.roll(y, HEAD_DIM // 2, 1))
    return y * cos + partner * sin


def _store_value_heads(v_ref, v):
    lane = _lane(v.shape)
    ones_col = jnp.where(lane == HEAD_DIM, 1.0, 0.0)
    v_ref[0] = jnp.where(lane < HEAD_DIM, v, ones_col).astype(BF16)
    v_ref[1] = jnp.where(lane < HEAD_DIM, pltpu.roll(v, HEAD_DIM, 1), ones_col).astype(BF16)


def _ones_row_block(rows, cols):
    return jnp.where(lax.broadcasted_iota(jnp.int32, (rows, cols), 0) == 0, 1.0, 0.0)


def _store_value_heads_t(vt_ref, v):
    vt = v.T
    tail = _ones_row_block(GQA_V_ROWS - HEAD_DIM, v.shape[0])
    vt_ref[0, 0] = jnp.concatenate([vt[:HEAD_DIM], tail], axis=0).astype(BF16)
    vt_ref[1, 0] = jnp.concatenate([vt[HEAD_DIM:], tail], axis=0).astype(BF16)


N_INPROJ_INPUTS = 8


def _inproj_rows(refs, fused):
    if not fused:
        return refs[0][...], refs[1:N_INPROJ_INPUTS], refs[N_INPROJ_INPUTS:]
    y_ref, wts_ref, gate_ref, x_ref = refs[:4]
    ins, outs = refs[4:N_INPROJ_INPUTS + 3], refs[N_INPROJ_INPUTS + 3:]
    wts = wts_ref[...]
    f = wts[:, 0:1] * _load_pieces(y_ref[:, 0]) + wts[:, 1:2] * _load_pieces(y_ref[:, 1])
    x = x_ref[...] + gate_ref[jnp.minimum(pl.program_id(0), 1)] * f
    outs[0][...] = x
    return x, ins, outs[1:]


def _inproj_ab_kernel(*refs, fused):
    x, (g_ref, s_ref, w_ref, hg_ref, cos_ref, sin_ref, seg_ref), outs = _inproj_rows(refs, fused)
    qa_ref, qb_ref, ka_ref, kb_ref, va_ref, vat_ref, vbt_ref = outs
    bm = x.shape[0]
    h = _modulated(x, g_ref, s_ref).astype(BF16)
    proj = jnp.dot(h, w_ref[...], preferred_element_type=F32)
    cos, sin, seg = cos_ref[...], sin_ref[...], seg_ref[...]
    low = _lane((bm, LANES)) < HEAD_DIM

    def normed(j):
        cols = slice(j * LANES, (j + 1) * LANES)
        return _head_norm_rope(proj[:, cols], hg_ref[:, cols], cos, sin, seg)

    def store_queries(q_ref, tile0):
        for j in range(N_HEADS_A // 2):
            y = normed(tile0 + j)
            swapped = pltpu.roll(y, HEAD_DIM, 1)
            kv, g0 = (2 * j) // GROUP, (2 * j) % GROUP
            if kv == 0:
                even, odd = jnp.where(low, y, 0.0), jnp.where(low, swapped, 0.0)
            else:
                even, odd = jnp.where(low, 0.0, swapped), jnp.where(low, 0.0, y)
            q_ref[kv, 0, g0 * bm:(g0 + 1) * bm, :] = even.astype(BF16)
            q_ref[kv, 0, (g0 + 1) * bm:(g0 + 2) * bm, :] = odd.astype(BF16)

    tq = QA_W // LANES
    store_queries(qa_ref, 0)
    ka_ref[0] = normed(tq).astype(BF16)
    va = proj[:, (tq + 1) * LANES:(tq + 2) * LANES]
    _store_value_heads(va_ref, va)
    _store_value_heads_t(vat_ref, va)
    store_queries(qb_ref, tq + 2)
    kb_ref[0] = normed(2 * tq + 2).astype(BF16)
    _store_value_heads_t(vbt_ref, proj[:, (2 * tq + 3) * LANES:(2 * tq + 4) * LANES])


def _inproj_c_kernel(*refs, fused):
    x, (g_ref, s_ref, w_ref, hg_ref, cos_ref, sin_ref, seg_ref), (q_ref, k_ref, v_ref) = _inproj_rows(refs, fused)
    bm = x.shape[0]
    h = _modulated(x, g_ref, s_ref).astype(BF16)
    proj = jnp.dot(h, w_ref[...], preferred_element_type=F32)
    cos, sin, seg = cos_ref[...], sin_ref[...], seg_ref[...]
    low = _lane((bm, LANES)) < HEAD_DIM

    def normed(j):
        cols = slice(j * LANES, (j + 1) * LANES)
        return _head_norm_rope(proj[:, cols], hg_ref[:, cols], cos, sin, seg)

    for j in range(N_HEADS_C):
        y = normed(j)
        q_ref[j, 0, 0:bm, :] = jnp.where(low, y, 0.0).astype(BF16)
        q_ref[j, 0, bm:2 * bm, :] = jnp.where(low, 0.0, y).astype(BF16)
        k_ref[j] = normed(N_HEADS_C + j).astype(BF16)
        v = proj[:, (2 * N_HEADS_C + j) * LANES:(2 * N_HEADS_C + j + 1) * LANES]
        v_ref[j, 0] = v.T.astype(BF16)


def _rope_tables(seq, n_ctx):
    rows_n = seq // GRID_W
    rows = jnp.broadcast_to(jnp.arange(rows_n, dtype=F32)[:, None], (rows_n, GRID_W)).reshape(-1)
    cols = jnp.broadcast_to(jnp.arange(GRID_W, dtype=F32)[None, :], (rows_n, GRID_W)).reshape(-1)
    half = HEAD_DIM // 2
    inv = ROPE_THETA ** (-jnp.arange(0, half, 2, dtype=F32) / half)
    ang = jnp.concatenate([rows[:, None] * inv, cols[:, None] * inv], axis=-1)
    reps = LANES // half
    sign = jnp.tile(jnp.concatenate([-jnp.ones((half,), F32), jnp.ones((half,), F32)]), LANES // HEAD_DIM)
    cos = jnp.pad(jnp.tile(jnp.cos(ang), (1, reps)), ((n_ctx, 0), (0, 0)), constant_values=1.0)
    sin = jnp.pad(jnp.tile(jnp.sin(ang), (1, reps)) * sign, ((n_ctx, 0), (0, 0)))
    return cos, sin


def _segment_mean_matrix():
    idx = jnp.arange(LANES) // HEAD_DIM
    return jnp.where(idx[:, None] == idx[None, :], 1.0 / HEAD_DIM, 0.0).astype(BF16)


def _q_slot(i, nblk):
    return (i + nblk - 1) % nblk


def _inproj_common_specs(t, d, n):
    bm = ROW_BLOCK
    return [pl.BlockSpec((bm, d), lambda i: (i, 0)),
            pl.BlockSpec((2, 1, d), lambda i: (0, 0, 0)),
            pl.BlockSpec((2, 1, d), lambda i: (0, 0, 0)),
            pl.BlockSpec((d, n), lambda i: (0, 0)),
            pl.BlockSpec((1, n), lambda i: (0, 0)),
            pl.BlockSpec((bm, LANES), lambda i: (i, 0)),
            pl.BlockSpec((bm, LANES), lambda i: (i, 0)),
            pl.BlockSpec((LANES, LANES), lambda i: (0, 0))]


def _inproj_call(kernel_fn, name, x, pending, args, n, out_specs, out_shape):
    t, d = x.shape
    bm = ROW_BLOCK
    in_specs = _inproj_common_specs(t, d, n)
    args = [x] + list(args)
    fused = pending is not None
    if fused:
        y2, wts, gate2 = pending
        pieces, _, _, dp = y2.shape
        in_specs = [pl.BlockSpec((pieces, TOP_K, bm, dp), lambda i: (0, 0, i, 0)),
                    pl.BlockSpec((bm, TOP_K), lambda i: (i, 0)),
                    pl.BlockSpec((2, 1, d), lambda i: (0, 0, 0))] + in_specs
        args = [y2, wts, gate2] + args
        out_specs = [pl.BlockSpec((bm, d), lambda i: (i, 0))] + list(out_specs)
        out_shape = [jax.ShapeDtypeStruct((t, d), F32)] + list(out_shape)
    outs = pl.pallas_call(
        functools.partial(kernel_fn, fused=fused),
        grid=(t // bm,),
        in_specs=in_specs,
        out_specs=out_specs,
        out_shape=out_shape,
        compiler_params=_params("parallel"),
        name=name,
    )(*args)
    return (outs[0], outs[1:]) if fused else (x, outs)


def _inproj_ab(x, pending, gain2, shift2, w, qn_a, kn_a, qn_b, kn_b, cos, sin):
    t, d = x.shape
    n = w.shape[1]
    bm, nblk = ROW_BLOCK, t // ROW_BLOCK
    tile = lambda g, reps: jnp.tile(g.astype(F32), reps)
    ones_v = jnp.ones((KVA_W,), F32)
    hg = jnp.concatenate([tile(qn_a, N_HEADS_A) * Q_SCALE, tile(kn_a, N_KV_A), ones_v,
                          tile(qn_b, N_HEADS_B) * Q_SCALE, tile(kn_b, N_KV_B), ones_v]).reshape(1, n)
    q_shape = jax.ShapeDtypeStruct((N_KV_A, nblk, GROUP * bm, LANES), BF16)
    k_shape = jax.ShapeDtypeStruct((1, t, LANES), BF16)
    v_shape = jax.ShapeDtypeStruct((N_KV_A, t, LANES), BF16)
    vt_shape = jax.ShapeDtypeStruct((N_KV_A, nblk, GQA_V_ROWS, bm), BF16)
    q_spec = pl.BlockSpec((N_KV_A, 1, GROUP * bm, LANES), lambda i: (0, _q_slot(i, nblk), 0, 0))
    k_spec = pl.BlockSpec((1, bm, LANES), lambda i: (0, i, 0))
    v_spec = pl.BlockSpec((N_KV_A, bm, LANES), lambda i: (0, i, 0))
    vt_spec = pl.BlockSpec((N_KV_A, 1, GQA_V_ROWS, bm), lambda i: (0, i, 0, 0))
    return _inproj_call(_inproj_ab_kernel, "inproj_ab", x, pending,
                        [gain2, shift2, w, hg, cos, sin, _segment_mean_matrix()], n,
                        [q_spec, q_spec, k_spec, k_spec, v_spec, vt_spec, vt_spec],
                        [q_shape, q_shape, k_shape, k_shape, v_shape, vt_shape, vt_shape])


def _inproj_c(x, pending, gain2, shift2, w, qn, kn, cos, sin):
    t, d = x.shape
    n = w.shape[1]
    bm, nblk = ROW_BLOCK, t // ROW_BLOCK
    h = N_HEADS_C
    tile = lambda g: jnp.tile(g.astype(F32), 2 * h)
    hg = jnp.concatenate([tile(qn) * Q_SCALE, tile(kn), jnp.ones((h * DV_C,), F32)]).reshape(1, n)
    return _inproj_call(_inproj_c_kernel, "inproj_c", x, pending,
                        [gain2, shift2, w, hg, cos, sin, _segment_mean_matrix()], n,
                        [pl.BlockSpec((h, 1, 2 * bm, LANES), lambda i: (0, _q_slot(i, nblk), 0, 0)),
                         pl.BlockSpec((h, bm, LANES), lambda i: (0, i, 0)),
                         pl.BlockSpec((h, 1, DIFF_V_ROWS, bm), lambda i: (0, i, 0, 0))],
                        [jax.ShapeDtypeStruct((h, nblk, 2 * bm, LANES), BF16),
                         jax.ShapeDtypeStruct((h, t, LANES), BF16),
                         jax.ShapeDtypeStruct((h, nblk, DIFF_V_ROWS, bm), BF16)])


def _pad_rows(x):
    rows = x.shape[0]
    if rows >= LANES:
        return x[:LANES]
    return jnp.concatenate([x, jnp.zeros((LANES - rows, x.shape[1]), x.dtype)], axis=0)


def _merge_gqa_heads(o, bq):
    low = _lane((bq, LANES)) < HEAD_DIM
    pairs = [jnp.where(low, o[g * bq:(g + 1) * bq], pltpu.roll(o[(g + 1) * bq:(g + 2) * bq], HEAD_DIM, 1))
             for g in range(0, GROUP, 2)]
    return jnp.concatenate(pairs, axis=1)


def _flash_kernel(*refs, mode, online, n_keys, bk, dv, l0):
    refs = list(refs)
    q_ref, k_ref, vt_ref = refs[:3]
    pos = 3
    if online:
        m0_ref = refs[pos]
        pos += 1
    if mode == "diff":
        lam_ref, sub_ref = refs[pos:pos + 2]
        pos += 2
    o_ref = refs[pos]
    scratch = refs[pos + 1:]
    acc_sc = scratch[0]
    ones_row = acc_sc.shape[0] > dv
    m_sc = scratch[1] if online else None
    l_sc = None if ones_row else scratch[-1]

    nsub, rb = q_ref.shape[1], q_ref.shape[2]
    r = nsub * rb
    ch = vt_ref.shape[3]
    q = q_ref[0].reshape(r, LANES)
    acc_sc[...] = jnp.where(lax.broadcasted_iota(jnp.int32, acc_sc.shape, 0) == dv, l0, 0.0).astype(F32)
    if online:
        m_sc[...] = m0_ref[0]
    if not ones_row:
        l_sc[...] = jnp.full(l_sc.shape, l0, F32)

    def block(start, size):
        kb = k_ref[0, pl.ds(start, size), :]
        st = lax.dot_general(kb, q, (((1,), (1,)), ((), ())), preferred_element_type=F32)
        if online:
            m_prev = m_sc[...]
            m_new = jnp.maximum(m_prev, jnp.max(st, axis=0, keepdims=True))
            p = jnp.exp2(st - m_new)
            alpha = jnp.exp2(m_prev - m_new)
            acc = alpha * acc_sc[...]
            m_sc[...] = m_new
        else:
            p = jnp.exp2(st)
            alpha = 1.0
            acc = acc_sc[...]
        if not ones_row:
            l_sc[...] = alpha * l_sc[...] + jnp.sum(p, axis=0, keepdims=True)
        pt = p.astype(BF16)
        c0 = start // ch
        for c in range(size // ch):
            acc = acc + jnp.dot(vt_ref[0, c0 + c], pt[c * ch:(c + 1) * ch], preferred_element_type=F32)
        acc_sc[...] = acc

    n_full, tail = n_keys // bk, n_keys % bk
    if n_full:
        def body(i, carry):
            block(pl.multiple_of(i * bk, bk), bk)
            return carry
        lax.fori_loop(0, n_full, body, 0)
    if tail:
        block(n_full * bk, tail)

    acc = acc_sc[...]
    den = acc[dv:dv + 1] if ones_row else l_sc[...]
    ot = _pad_rows(acc / den)
    bq = rb // (GROUP if mode == "gqa" else 2)
    o = jnp.concatenate([ot[:, j * bq:(j + 1) * bq].T for j in range(r // bq)], axis=0)
    if mode == "gqa":
        for b in range(nsub):
            o_ref[b * bq:(b + 1) * bq, :] = _merge_gqa_heads(o[b * rb:(b + 1) * rb], bq).astype(o_ref.dtype)
    else:
        for b in range(nsub):
            d = o[b * rb:b * rb + bq] - lam_ref[...] * o[b * rb + bq:(b + 1) * rb]
            y = d * lax.rsqrt(jnp.mean(d * d, axis=-1, keepdims=True) + EPS) * sub_ref[...]
            o_ref[b * bq:(b + 1) * bq, :] = y.astype(o_ref.dtype)


def _flash_call(q, k, vt, extra, *, mode, online, n_keys, slot0, nsub, n_steps, l0=0.0):
    hkv, _, rb, _ = q.shape
    hk = k.shape[0]
    _, _, dvr, ch = vt.shape
    dv = HEAD_DIM if mode == "gqa" else DV_C
    bq = rb // (GROUP if mode == "gqa" else 2)
    ocols = GROUP * HEAD_DIM if mode == "gqa" else DV_C
    r = nsub * rb
    bk = min(KEY_BLOCK, n_keys)
    in_specs = [pl.BlockSpec((1, nsub, rb, LANES), lambda h, i: (h, slot0 // nsub + i, 0, 0)),
                pl.BlockSpec((1, n_keys, LANES), (lambda h, i: (h, 0, 0)) if hk > 1 else (lambda h, i: (0, 0, 0))),
                pl.BlockSpec((1, n_keys // ch, dvr, ch), lambda h, i: (h, 0, 0, 0))]
    args = [q, k, vt]
    scratch = [pltpu.VMEM((dvr, r), F32)]
    if online:
        m0 = extra.pop(0)
        in_specs.append(pl.BlockSpec((1, 1, r), lambda h, i: (h, 0, 0)))
        args.append(m0)
        scratch.append(pltpu.VMEM((1, r), F32))
    if dvr <= dv:
        scratch.append(pltpu.VMEM((1, r), F32))
    for a in extra:
        in_specs.append(pl.BlockSpec(a.shape, lambda h, i: (0, 0)))
        args.append(a)
    return pl.pallas_call(
        functools.partial(_flash_kernel, mode=mode, online=online, n_keys=n_keys, bk=bk, dv=dv, l0=l0),
        grid=(hkv, n_steps),
        in_specs=in_specs,
        out_specs=pl.BlockSpec((nsub * bq, ocols), lambda h, i: (i, h)),
        out_shape=jax.ShapeDtypeStruct((n_steps * nsub * bq, hkv * ocols), BF16),
        scratch_shapes=scratch,
        compiler_params=_params("parallel", "parallel"),
        name="flash_online" if online else "flash_bounded",
    )(*args)


def _logits_bounded(q_gain, k_gain):
    bound = HEAD_DIM * Q_SCALE * 1.02 * jnp.max(jnp.abs(q_gain.astype(F32))) * jnp.max(jnp.abs(k_gain.astype(F32)))
    return bound <= LOG2_LOGIT_BOUND


def _attend(q, k, v, extra, bounded, **kw):
    hkv, r = q.shape[0], kw["nsub"] * q.shape[2]
    fast = lambda q_, k_, v_, *e: _flash_call(q_, k_, v_, list(e), online=False, **kw)
    safe = lambda q_, k_, v_, *e: _flash_call(q_, k_, v_, [jnp.full((hkv, 1, r), NEG, F32)] + list(e), online=True, **kw)
    return lax.cond(bounded, fast, safe, q, k, v, *extra)


def _window_kernel(q_ref, k_ref, v_ref, sink_ref, o_ref, *, bq, n_ctx):
    q = q_ref[0, 0]
    r = q.shape[0]
    t = k_ref.shape[1]
    w = bq + 2 * WINDOW
    q0 = pl.program_id(1) * bq
    ws = pl.multiple_of(jnp.clip(n_ctx + q0 - WINDOW, 0, t - w), WINDOW)
    kw = k_ref[0, pl.ds(ws, w), :]
    vw = v_ref[0, pl.ds(ws, w), :]
    contract_last = (((1,), (1,)), ((), ()))
    s_loc = lax.dot_general(q, kw, contract_last, preferred_element_type=F32)
    qpos = q0 + (lax.broadcasted_iota(jnp.int32, (r, w), 0) & (bq - 1))
    kpos = ws - n_ctx + lax.broadcasted_iota(jnp.int32, (r, w), 1)
    mask = (kpos >= 0) & (kpos - qpos <= WINDOW) & (qpos - kpos <= WINDOW)
    s_loc = jnp.where(mask, s_loc, NEG)
    s_ctx = lax.dot_general(q, k_ref[0, 0:n_ctx, :], contract_last, preferred_element_type=F32)
    sink = sink_ref[0]
    m = jnp.maximum(sink, jnp.maximum(jnp.max(s_loc, axis=-1, keepdims=True), jnp.max(s_ctx, axis=-1, keepdims=True)))
    p_loc = jnp.exp2(s_loc - m)
    p_ctx = jnp.exp2(s_ctx - m)
    l = jnp.exp2(sink - m) + jnp.sum(p_loc, axis=-1, keepdims=True) + jnp.sum(p_ctx, axis=-1, keepdims=True)
    o = (jnp.dot(p_loc.astype(BF16), vw, preferred_element_type=F32)
         + jnp.dot(p_ctx.astype(BF16), v_ref[0, 0:n_ctx, :], preferred_element_type=F32))
    o_ref[...] = _merge_gqa_heads(o / l, bq).astype(o_ref.dtype)


def _window_attention(q, k, v, sink_rows, n_ctx):
    hkv, slots, r, _ = q.shape
    t = k.shape[1]
    bq = r // GROUP
    nq = slots - n_ctx // bq
    return pl.pallas_call(
        functools.partial(_window_kernel, bq=bq, n_ctx=n_ctx),
        grid=(hkv, nq),
        in_specs=[pl.BlockSpec((1, 1, r, LANES), lambda h, i: (h, i, 0, 0)),
                  pl.BlockSpec((1, t, LANES), lambda h, i: (0, 0, 0)),
                  pl.BlockSpec((1, t, LANES), lambda h, i: (h, 0, 0)),
                  pl.BlockSpec((1, r, 1), lambda h, i: (h, 0, 0))],
        out_specs=pl.BlockSpec((bq, GROUP * HEAD_DIM), lambda h, i: (i, h)),
        out_shape=jax.ShapeDtypeStruct((nq * bq, hkv * GROUP * HEAD_DIM), BF16),
        compiler_params=_params("parallel", "parallel"),
        name="window_attention",
    )(q, k, v, sink_rows)


def _window_bounded_kernel(q_ref, k_ref, vt_ref, mask_ref, sink_ref, o_ref, *, bq):
    q = q_ref[0, 0]
    ch = vt_ref.shape[3]
    cw = jnp.clip(pl.program_id(1), 0, vt_ref.shape[1] - 3)
    contract_last = (((1,), (1,)), ((), ()))
    k_win = k_ref[0, pl.ds(pl.multiple_of(cw * ch, ch), 3 * ch), :]
    st_win = lax.dot_general(k_win, q, contract_last, preferred_element_type=F32)
    st_ctx = lax.dot_general(k_ref[0, 0:ch, :], q, contract_last, preferred_element_type=F32)
    pt_win = jnp.exp2(st_win).astype(BF16) * mask_ref[0]
    acc = jnp.dot(vt_ref[0, 0], jnp.exp2(st_ctx).astype(BF16), preferred_element_type=F32)
    for c in range(3):
        acc = acc + jnp.dot(vt_ref[0, cw + c], pt_win[c * ch:(c + 1) * ch], preferred_element_type=F32)
    ot = _pad_rows(acc / (acc[HEAD_DIM:HEAD_DIM + 1] + jnp.exp2(sink_ref[0])))
    o = jnp.concatenate([ot[:, g * bq:(g + 1) * bq].T for g in range(GROUP)], axis=0)
    o_ref[...] = _merge_gqa_heads(o, bq).astype(o_ref.dtype)


def _window_masks(bq, r):
    c = jnp.arange(3 * bq)[:, None]
    off = c - (jnp.arange(r)[None, :] & (bq - 1))
    centred = (off >= bq - WINDOW) & (off <= bq + WINDOW)
    shifted = (off >= 2 * bq - WINDOW) & (off <= 2 * bq + WINDOW)
    return jnp.stack([centred & (c >= bq), centred, shifted]).astype(BF16)


def _window_attention_bounded(q, k, vt, sink_cols, n_ctx):
    hkv, slots, r, _ = q.shape
    t = k.shape[1]
    bq = r // GROUP
    nblk = t // bq
    nq = nblk - 1
    assert n_ctx == bq and nblk >= 3
    return pl.pallas_call(
        functools.partial(_window_bounded_kernel, bq=bq),
        grid=(hkv, nq),
        in_specs=[pl.BlockSpec((1, 1, r, LANES), lambda h, i: (h, i, 0, 0)),
                  pl.BlockSpec((1, t, LANES), lambda h, i: (0, 0, 0)),
                  pl.BlockSpec((1, nblk, GQA_V_ROWS, bq), lambda h, i: (h, 0, 0, 0)),
                  pl.BlockSpec((1, 3 * bq, r), lambda h, i: (jnp.where(i == 0, 0, jnp.where(i >= nblk - 2, 2, 1)), 0, 0)),
                  pl.BlockSpec((1, 1, r), lambda h, i: (h, 0, 0))],
        out_specs=pl.BlockSpec((bq, GROUP * HEAD_DIM), lambda h, i: (i, h)),
        out_shape=jax.ShapeDtypeStruct((nq * bq, hkv * GROUP * HEAD_DIM), BF16),
        compiler_params=_params("parallel", "parallel"),
        name="window_bounded",
    )(q, k, vt, _window_masks(bq, r), sink_cols)


def _sink_rows(sink, bq):
    hkv, g = sink.shape
    return jnp.broadcast_to(sink.astype(F32)[:, :, None] * LOG2E, (hkv, g, bq)).reshape(hkv, g * bq, 1)


def _mixer_ab(heads, n_ctx, sink_a, with_ctx, bounded_a, bounded):
    qa, qb, ka, kb, va, vat, vbt = heads
    t = ka.shape[1]
    bq = ROW_BLOCK
    n_lat = (t - n_ctx) // bq
    sink = _sink_rows(sink_a.reshape(N_KV_A, GROUP), bq)
    sink_cols = sink.reshape(N_KV_A, 1, GROUP * bq)
    bounded_a = bounded_a & (jnp.max(jnp.abs(sink)) <= LOG2_LOGIT_BOUND)
    oa = lax.cond(bounded_a,
                  lambda: _window_attention_bounded(qa, ka, vat, sink_cols, n_ctx),
                  lambda: _window_attention(qa, ka, va, sink, n_ctx))
    nsub = 2 if n_lat % 2 == 0 else 1
    ob = _attend(qb, kb, vbt, [], bounded, mode="gqa", n_keys=t, slot0=0, nsub=nsub, n_steps=n_lat // nsub)
    if not with_ctx:
        return (oa, ob), None
    oca = _flash_call(qa, ka, vat, [sink_cols], mode="gqa", online=True, n_keys=n_ctx,
                      slot0=n_lat, nsub=1, n_steps=1, l0=1.0)
    ocb = _attend(qb, kb, vbt, [], bounded, mode="gqa", n_keys=n_ctx, slot0=n_lat, nsub=1, n_steps=1)
    return (oa, ob), (oca, ocb)


def _mixer_c(heads, n_ctx, lam_p, subln, lam_init, with_ctx, bounded):
    q, k, v = heads
    t = k.shape[1]
    n_lat = (t - n_ctx) // ROW_BLOCK
    lp = lam_p.astype(F32)
    lam = jnp.exp(jnp.sum(lp[0] * lp[1])) - jnp.exp(jnp.sum(lp[2] * lp[3])) + lam_init
    extra = [jnp.full((1, DV_C), lam, F32), (subln.astype(F32) * (1 - lam_init)).reshape(1, DV_C)]
    nsub = 4 if n_lat % 4 == 0 else 1
    o_lat = _attend(q, k, v, extra, bounded, mode="diff", n_keys=t, slot0=0, nsub=nsub, n_steps=n_lat // nsub)
    if not with_ctx:
        return (o_lat,), None
    o_ctx = _attend(q, k, v, extra, bounded, mode="diff", n_keys=n_ctx, slot0=n_lat, nsub=1, n_steps=1)
    return (o_lat,), (o_ctx,)


def _out_proj_kernel(*refs, n_parts, has_ctx):
    lat = refs[:n_parts]
    ctx = refs[n_parts:2 * n_parts] if has_ctx else None
    w_ref, x_ref, gate_ref, g_ref, s_ref, wr_ref, y_ref, tok_ref, logit_ref = refs[-9:]
    is_ctx = pl.program_id(0) == 0 if has_ctx else False
    row = jnp.where(is_ctx, 0, 1) if has_ctx else 1
    acc = None
    col = 0
    for p in range(n_parts):
        o = lat[p][...]
        if has_ctx:
            o = jnp.where(is_ctx, ctx[p][...], o)
        width = o.shape[1]
        part = jnp.dot(o, w_ref[col:col + width, :], preferred_element_type=F32)
        acc = part if acc is None else acc + part
        col += width
    x1 = x_ref[...] + gate_ref[row] * acc
    y_ref[...] = x1
    h = x1 * lax.rsqrt(jnp.mean(x1 * x1, axis=-1, keepdims=True) + EPS) * g_ref[row] + s_ref[row]
    _store_pieces(tok_ref, h)
    hi, lo = _split_bf16(h)
    logits = (jnp.dot(hi, wr_ref[0], preferred_element_type=F32) + jnp.dot(lo, wr_ref[0], preferred_element_type=F32)
              + jnp.dot(hi, wr_ref[1], preferred_element_type=F32))
    logit_ref[...] = logits.T


def _out_proj(o_lat, o_ctx, w, x, gate2, gain2, shift2, w_router):
    t, d = x.shape
    bm = ROW_BLOCK
    dp = d // SC_ROW_SPLIT
    has_ctx = o_ctx is not None
    n_parts = len(o_lat)
    lat_map = (lambda i: (jnp.maximum(i - 1, 0), 0)) if has_ctx else (lambda i: (i, 0))
    in_specs = [pl.BlockSpec((bm, o.shape[1]), lat_map) for o in o_lat]
    args = list(o_lat)
    if has_ctx:
        in_specs += [pl.BlockSpec((bm, o.shape[1]), lambda i: (0, 0)) for o in o_ctx]
        args += list(o_ctx)
    vec_spec = pl.BlockSpec((2, 1, d), lambda i: (0, 0, 0))
    in_specs += [pl.BlockSpec(w.shape, lambda i: (0, 0)),
                 pl.BlockSpec((bm, d), lambda i: (i, 0)),
                 vec_spec, vec_spec, vec_spec,
                 pl.BlockSpec((2, d, LANES), lambda i: (0, 0, 0))]
    return pl.pallas_call(
        functools.partial(_out_proj_kernel, n_parts=n_parts, has_ctx=has_ctx),
        grid=(t // bm,),
        in_specs=in_specs,
        out_specs=[pl.BlockSpec((bm, d), lambda i: (i, 0)),
                   pl.BlockSpec((SC_ROW_SPLIT, bm, dp), lambda i: (0, i, 0)),
                   pl.BlockSpec((LANES, bm), lambda i: (0, i))],
        out_shape=[jax.ShapeDtypeStruct((t, d), F32),
                   jax.ShapeDtypeStruct((SC_ROW_SPLIT, t, dp), F32),
                   jax.ShapeDtypeStruct((LANES, t), F32)],
        compiler_params=_params("parallel"),
        name="out_proj_router",
    )(*args, w, x, gate2, gain2, shift2, w_router)


def _store_pieces(ref, rows):
    dp = ref.shape[2]
    for j in range(ref.shape[0]):
        ref[j] = rows[:, j * dp:(j + 1) * dp]


def _load_pieces(planes):
    return jnp.concatenate([planes[j] for j in range(planes.shape[0])], axis=1)


def _moe_kernel(blk_ref, exp_ref, lo_ref, hi_ref, n_ref, x_ref, w1_ref, w3_ref, w2_ref, y_ref):
    i = pl.program_id(0)

    @pl.when(i < n_ref[0])
    def _():
        x = _load_pieces(x_ref[...]).astype(BF16)
        a = jnp.dot(x, w1_ref[0, 0].astype(BF16), preferred_element_type=F32)
        b = jnp.dot(x, w3_ref[0, 0].astype(BF16), preferred_element_type=F32)
        hidden = (a * jax.nn.sigmoid(a)) * b
        y = jnp.dot(hidden.astype(BF16), w2_ref[0, 0].astype(BF16), preferred_element_type=F32)
        rows = lax.broadcasted_iota(jnp.int32, (y.shape[0], 1), 0)
        y = jnp.where((rows >= lo_ref[i]) & (rows < hi_ref[i]), y, 0.0)
        first = (i == 0) | (blk_ref[i] != blk_ref[jnp.maximum(i - 1, 0)])

        @pl.when(first)
        def _():
            _store_pieces(y_ref, y)

        @pl.when(jnp.logical_not(first))
        def _():
            _store_pieces(y_ref, _load_pieces(y_ref[...]) + y)


def _moe_experts(buf, items, w1, w3, w2, layer):
    pieces, n_rows, dp = buf.shape
    d, de = w1.shape[2], w1.shape[3]
    n_items = items[0].shape[0]
    row_spec = pl.BlockSpec((pieces, EXPERT_BLOCK, dp), lambda i, blk, ex, lo, hi, n: (0, blk[i], 0))
    w_map = lambda i, blk, ex, lo, hi, n: (layer, ex[i], 0, 0)
    grid_spec = pltpu.PrefetchScalarGridSpec(
        num_scalar_prefetch=5,
        grid=(n_items,),
        in_specs=[row_spec,
                  pl.BlockSpec((1, 1, d, de), w_map),
                  pl.BlockSpec((1, 1, d, de), w_map),
                  pl.BlockSpec((1, 1, de, d), w_map)],
        out_specs=row_spec,
    )
    return pl.pallas_call(
        _moe_kernel,
        grid_spec=grid_spec,
        out_shape=jax.ShapeDtypeStruct((pieces, n_rows, dp), F32),
        compiler_params=_params("arbitrary"),
        name="moe_experts",
    )(*items, buf, w1, w3, w2)


def _expert_items(counts, n_rows):
    e = counts.shape[0]
    n_blk = n_rows // EXPERT_BLOCK
    n_items = n_blk + e - 1
    end = jnp.cumsum(counts)
    start = end - counts
    first_blk = start // EXPERT_BLOCK
    per_expert = jnp.where(counts > 0, (end - 1) // EXPERT_BLOCK - first_blk + 1, 0)
    cum = jnp.cumsum(per_expert)
    total = cum[-1]
    w = jnp.minimum(jnp.arange(n_items, dtype=jnp.int32), total - 1)
    onehot = ((cum - per_expert)[None, :] <= w[:, None]) & (w[:, None] < cum[None, :])
    pick = lambda tab: jnp.sum(jnp.where(onehot, tab[None, :], 0), axis=1).astype(jnp.int32)
    expert = pick(jnp.arange(e, dtype=jnp.int32))
    blk = pick(first_blk) + w - pick(cum - per_expert)
    lo = jnp.clip(pick(start) - blk * EXPERT_BLOCK, 0, EXPERT_BLOCK)
    hi = jnp.clip(pick(end) - blk * EXPERT_BLOCK, 0, EXPERT_BLOCK)
    return blk, expert, lo, hi, total.astype(jnp.int32).reshape(1)


def _sc_scatter(x, pos):
    pieces, t, dp = x.shape
    n = pos.shape[0]
    kk = n // t
    offs = jnp.arange(pieces, dtype=jnp.int32)[:, None] * n
    idx = [(offs + pos[k * t:(k + 1) * t][None, :]).reshape(1, pieces * t) for k in range(kk)]
    mesh = plsc.VectorSubcoreMesh(core_axis_name="core", subcore_axis_name="subcore")

    @pl.kernel(out_type=jax.ShapeDtypeStruct((pieces * n, dp), x.dtype), mesh=mesh, scratch_types=[])
    def scatter(x_hbm, *refs):
        i_hbm, o_hbm = refs[:kk], refs[kk]

        def body(x_vmem, *i_vmem):
            for iv in i_vmem:
                pltpu.sync_copy(x_vmem, o_hbm.at[iv.at[0]])

        pltpu.emit_pipeline(
            body,
            grid=(pieces * t // SC_GATHER_WINDOW,),
            in_specs=[pl.BlockSpec((SC_GATHER_WINDOW, dp), lambda i: (i, 0))]
                     + [pl.BlockSpec((1, SC_GATHER_WINDOW), lambda i: (0, i))] * kk,
            out_specs=[],
            core_axis_name=("core", "subcore"),
            dimension_semantics=(pltpu.PARALLEL,),
        )(x_hbm, *i_hbm)

    return scatter(x.reshape(pieces * t, dp), *idx).reshape(pieces, n, dp)


def _sc_gather(x, idx):
    pieces, t, dp = x.shape
    n = idx.shape[0]
    flat = (jnp.arange(pieces, dtype=jnp.int32)[:, None] * t + idx[None, :]).reshape(1, pieces * n)
    mesh = plsc.VectorSubcoreMesh(core_axis_name="core", subcore_axis_name="subcore")

    @pl.kernel(out_type=jax.ShapeDtypeStruct((pieces * n, dp), x.dtype), mesh=mesh, scratch_types=[])
    def gather(x_hbm, i_hbm, o_hbm):
        def body(i_vmem, o_vmem):
            pltpu.sync_copy(x_hbm.at[i_vmem.at[0]], o_vmem)

        pltpu.emit_pipeline(
            body,
            grid=(pieces * n // SC_GATHER_WINDOW,),
            in_specs=[pl.BlockSpec((1, SC_GATHER_WINDOW), lambda i: (0, i))],
            out_specs=[pl.BlockSpec((SC_GATHER_WINDOW, dp), lambda i: (i, 0))],
            core_axis_name=("core", "subcore"),
            dimension_semantics=(pltpu.PARALLEL,),
        )(i_hbm, o_hbm)

    return gather(x.reshape(pieces * t, dp), flat).reshape(pieces, n, dp)


def _combine_kernel(y_ref, w_ref, x_ref, gate_ref, o_ref, *, row0):
    w = w_ref[...]
    f = w[:, 0:1] * _load_pieces(y_ref[:, 0]) + w[:, 1:2] * _load_pieces(y_ref[:, 1])
    row = jnp.minimum(pl.program_id(0) + row0, 1)
    o_ref[...] = x_ref[...] + gate_ref[row] * f


def _combine(y2, weights, x, gate2, has_ctx):
    t, d = x.shape
    bm = ROW_BLOCK
    pieces, _, _, dp = y2.shape
    return pl.pallas_call(
        functools.partial(_combine_kernel, row0=0 if has_ctx else 1),
        grid=(t // bm,),
        in_specs=[pl.BlockSpec((pieces, TOP_K, bm, dp), lambda i: (0, 0, i, 0)),
                  pl.BlockSpec((bm, TOP_K), lambda i: (i, 0)),
                  pl.BlockSpec((bm, d), lambda i: (i, 0)),
                  pl.BlockSpec((2, 1, d), lambda i: (0, 0, 0))],
        out_specs=pl.BlockSpec((bm, d), lambda i: (i, 0)),
        out_shape=jax.ShapeDtypeStruct((t, d), F32),
        compiler_params=_params("parallel"),
        name="moe_combine",
    )(y2, weights, x, gate2)


def _top1_rows(p):
    m = jnp.max(p, axis=0, keepdims=True)
    idx = lax.broadcasted_iota(jnp.int32, p.shape, 0)
    return m, jnp.min(jnp.where(p == m, idx, p.shape[0]), axis=0, keepdims=True)


def _prefix_rank(onehot):
    e, n = onehot.shape
    blk = EXPERT_BLOCK
    nb = n // blk
    earlier = (jnp.arange(blk)[:, None] < jnp.arange(blk)[None, :]).astype(F32)
    within = jnp.dot(onehot.reshape(e * nb, blk), earlier).reshape(e, n)
    tot = jnp.sum(onehot.reshape(e, nb, blk), axis=2)
    before = jnp.repeat(jnp.cumsum(tot, axis=1) - tot, blk, axis=1)
    return jnp.sum((within + before) * onehot, axis=0).astype(jnp.int32)


def _hier_moe(tokens, logits_t, b_group, b_expert, w1, w3, w2, layer):
    pieces, t, dp = tokens.shape
    pg = jax.nn.softmax(logits_t[:N_GROUPS] + b_group.astype(F32)[:, None], axis=0)
    g_prob, g_idx = _top1_rows(pg)
    le = (logits_t[N_GROUPS:N_GROUPS + N_EXPERTS] + b_expert.astype(F32)[:, None]).reshape(N_GROUPS, EXPERTS_PER_GROUP, t)
    group_iota = lax.broadcasted_iota(jnp.int32, (N_GROUPS, 1, t), 0)
    le = jnp.sum(jnp.where(group_iota == g_idx[None], le, 0.0), axis=0)
    pe = jax.nn.softmax(le, axis=0)
    p1, i1 = _top1_rows(pe)
    p2, i2 = _top1_rows(jnp.where(lax.broadcasted_iota(jnp.int32, pe.shape, 0) == i1, -1.0, pe))
    e_prob, e_idx = jnp.concatenate([p1, p2], axis=0), jnp.concatenate([i1, i2], axis=0)
    weights = g_prob * e_prob / jnp.sum(e_prob, axis=0, keepdims=True)
    flat_e = (g_idx * EXPERTS_PER_GROUP + e_idx).reshape(1, TOP_K * t)
    onehot = (flat_e == lax.broadcasted_iota(jnp.int32, (N_EXPERTS, TOP_K * t), 0)).astype(F32)
    rank = _prefix_rank(onehot)
    counts = jnp.sum(onehot, axis=1).astype(jnp.int32)
    first_row = jnp.sum(onehot * (jnp.cumsum(counts) - counts).astype(F32)[:, None], axis=0).astype(jnp.int32)
    dest = first_row + rank
    buf = _sc_scatter(tokens, dest)
    yb = _moe_experts(buf, _expert_items(counts, TOP_K * t), w1, w3, w2, layer)
    return _sc_gather(yb, dest).reshape(pieces, TOP_K, t, dp), weights.T


def kernel(x, c, ctx, c_ctx, w_mod, b_mod, norm_mix, norm_ffn, w_in_ab, w_out_ab, qn_a, kn_a, sink_a, qn_b, kn_b, w_in_c, w_out_c, qn_c, kn_c, lam_c, subln_c, w_group, b_group, w_expert, b_expert, w1, w3, w2):
    b, s_lat, d = x.shape
    n_ctx = ctx.shape[1]
    assert b == 1 and n_ctx == ROW_BLOCK and s_lat % (2 * ROW_BLOCK) == 0
    depth = w_mod.shape[0]
    cos, sin = _rope_tables(s_lat, n_ctx)
    mods = _mod_vectors(c, c_ctx, w_mod, b_mod)
    xs = jnp.concatenate([ctx[0], x[0]], axis=0)
    has_ctx = True
    pending = None
    for l in range(depth):
        last = l == depth - 1
        i = l // 2
        sh1, sc1, gt1, sh2, sc2, gt2 = [mods[l, :2, j * d:(j + 1) * d].reshape(2, 1, d) for j in range(6)]
        gain1 = norm_mix[l].astype(F32) * (1 + sc1)
        if l % 2 == 0:
            xs, heads = _inproj_ab(xs, pending, gain1, sh1, w_in_ab[i].astype(BF16), qn_a[i], kn_a[i], qn_b[i], kn_b[i], cos, sin)
            o_lat, o_ctx = _mixer_ab(heads, n_ctx, sink_a[i], not last,
                                     _logits_bounded(qn_a[i], kn_a[i]), _logits_bounded(qn_b[i], kn_b[i]))
            w_out = w_out_ab[i]
        else:
            lam_init = 0.8 - 0.6 * math.exp(-0.3 * l)
            xs, heads = _inproj_c(xs, pending, gain1, sh1, w_in_c[i].astype(BF16), qn_c[i], kn_c[i], cos, sin)
            o_lat, o_ctx = _mixer_c(heads, n_ctx, lam_c[i], subln_c[i], lam_init, not last,
                                    _logits_bounded(qn_c[i], kn_c[i]))
            w_out = w_out_c[i]
        if last:
            xs, has_ctx = xs[n_ctx:], False
        gain2 = norm_ffn[l].astype(F32) * (1 + sc2)
        w_router = jnp.zeros((d, LANES), F32).at[:, :N_GROUPS].set(w_group[l]).at[:, N_GROUPS:N_GROUPS + N_EXPERTS].set(w_expert[l])
        w_router = jnp.stack(_split_bf16(w_router))
        xs, tokens, logits = _out_proj(o_lat, o_ctx, w_out.astype(BF16), xs, gt1, gain2, sh2, w_router)
        y2, weights = _hier_moe(tokens, logits, b_group[l], b_expert[l], w1, w3, w2, l)
        pending = (y2, weights, gt2)
    return _combine(y2, weights, xs, gt2, has_ctx).reshape(b, s_lat, d)
```

```python
import functools
import math

import jax
import jax.numpy as jnp
from jax import lax
from jax.experimental import pallas as pl
from jax.experimental.pallas import tpu as pltpu
from jax.experimental.pallas import tpu_sc as plsc

F32 = jnp.float32
BF16 = jnp.bfloat16

GRID_W = 64
HEAD_DIM = 64
WINDOW = 128
ROPE_THETA = 10000.0
EPS = 1e-6
NEG = -1e30
N_HEADS_A, N_KV_A = 8, 2
N_HEADS_B, N_KV_B = 8, 2
GROUP = N_HEADS_A // N_KV_A
QA_W, KVA_W = N_HEADS_A * HEAD_DIM, N_KV_A * HEAD_DIM
QB_W, KVB_W = N_HEADS_B * HEAD_DIM, N_KV_B * HEAD_DIM
N_HEADS_C = 8
DV_C = 2 * HEAD_DIM
N_GROUPS, EXPERTS_PER_GROUP, TOP_K = 4, 8, 2
N_EXPERTS = N_GROUPS * EXPERTS_PER_GROUP
LANES = 128
ROW_BLOCK = 256
EXPERT_BLOCK = 512
SC_GATHER_WINDOW = 128
SC_ROW_SPLIT = 4
KEY_BLOCK = 3328
GQA_V_ROWS = 128
DIFF_V_ROWS = 128
VMEM_LIMIT = 48 * 1024 * 1024
LOG2E = math.log2(math.e)
Q_SCALE = HEAD_DIM ** -0.5 * LOG2E
LOG2_LOGIT_BOUND = 60.0


def _params(*sem):
    return pltpu.CompilerParams(dimension_semantics=sem, vmem_limit_bytes=VMEM_LIMIT)


def _lane(shape):
    return lax.broadcasted_iota(jnp.int32, shape, 1)


def _mod_vec_kernel(a_ref, w_ref, b_ref, o_ref):
    a = a_ref[...]
    a = a * jax.nn.sigmoid(a)
    o_ref[0] = jnp.dot(a, w_ref[0], preferred_element_type=F32, precision=lax.Precision.HIGHEST) + b_ref[0]


def _mod_vectors(c, c_ctx, w_mod, b_mod):
    depth, d, n = w_mod.shape
    a = jnp.zeros((8, d), F32).at[0].set(c_ctx).at[1].set(c[0])
    bn = 1024
    return pl.pallas_call(
        _mod_vec_kernel,
        grid=(depth, n // bn),
        in_specs=[pl.BlockSpec((8, d), lambda l, j: (0, 0)),
                  pl.BlockSpec((1, d, bn), lambda l, j: (l, 0, j)),
                  pl.BlockSpec((1, 1, bn), lambda l, j: (l, 0, j))],
        out_specs=pl.BlockSpec((1, 8, bn), lambda l, j: (l, 0, j)),
        out_shape=jax.ShapeDtypeStruct((depth, 8, n), F32),
        compiler_params=_params("parallel", "parallel"),
        name="mod_vectors",
    )(a, w_mod, b_mod.reshape(depth, 1, n))


def _modulated(x, g_ref, s_ref):
    y = x * lax.rsqrt(jnp.mean(x * x, axis=-1, keepdims=True) + EPS)
    row = jnp.minimum(pl.program_id(0), 1)
    return y * g_ref[row] + s_ref[row]


def _split_bf16(x):
    top = lax.bitcast_convert_type(lax.bitcast_convert_type(x, jnp.uint32) & jnp.uint32(0xFFFF0000), F32)
    return top.astype(BF16), (x - top).astype(BF16)


def _head_norm_rope(x, gain, cos, sin, seg_mean):
    hi, lo = _split_bf16(x * x)
    ms = jnp.dot(hi, seg_mean, preferred_element_type=F32) + jnp.dot(lo, seg_mean, preferred_element_type=F32)
    y = x * lax.rsqrt(ms + EPS) * gain
    first_half = (_lane(y.shape) & (HEAD_DIM - 1)) < HEAD_DIM // 2
    partner = jnp.where(first_half, pltpu.roll(y, LANES - HEAD_DIM // 2, 1), pltpu.roll(y, HEAD_DIM // 2, 1))
    return y * cos + partner * sin


def _store_value_heads(v_ref, v):
    lane = _lane(v.shape)
    ones_col = jnp.where(lane == HEAD_DIM, 1.0, 0.0)
    v_ref[0] = jnp.where(lane < HEAD_DIM, v, ones_col).astype(BF16)
    v_ref[1] = jnp.where(lane < HEAD_DIM, pltpu.roll(v, HEAD_DIM, 1), ones_col).astype(BF16)


def _ones_row_block(rows, cols):
    return jnp.where(lax.broadcasted_iota(jnp.int32, (rows, cols), 0) == 0, 1.0, 0.0)


def _store_value_heads_t(vt_ref, v):
    vt = v.T
    tail = _ones_row_block(GQA_V_ROWS - HEAD_DIM, v.shape[0])
    vt_ref[0, 0] = jnp.concatenate([vt[:HEAD_DIM], tail], axis=0).astype(BF16)
    vt_ref[1, 0] = jnp.concatenate([vt[HEAD_DIM:], tail], axis=0).astype(BF16)


N_INPROJ_INPUTS = 8


def _inproj_rows(refs, fused):
    if not fused:
        return refs[0][...], refs[1:N_INPROJ_INPUTS], refs[N_INPROJ_INPUTS:]
    y_ref, wts_ref, gate_ref, x_ref = refs[:4]
    ins, outs = refs[4:N_INPROJ_INPUTS + 3], refs[N_INPROJ_INPUTS + 3:]
    wts = wts_ref[...]
    f = wts[:, 0:1] * _load_pieces(y_ref[:, 0]) + wts[:, 1:2] * _load_pieces(y_ref[:, 1])
    x = x_ref[...] + gate_ref[jnp.minimum(pl.program_id(0), 1)] * f
    outs[0][...] = x
    return x, ins, outs[1:]


def _inproj_ab_kernel(*refs, fused):
    x, (g_ref, s_ref, w_ref, hg_ref, cos_ref, sin_ref, seg_ref), outs = _inproj_rows(refs, fused)
    qa_ref, qb_ref, ka_ref, kb_ref, va_ref, vat_ref, vbt_ref = outs
    bm = x.shape[0]
    h = _modulated(x, g_ref, s_ref).astype(BF16)
    proj = jnp.dot(h, w_ref[...], preferred_element_type=F32)
    cos, sin, seg = cos_ref[...], sin_ref[...], seg_ref[...]
    low = _lane((bm, LANES)) < HEAD_DIM

    def normed(j):
        cols = slice(j * LANES, (j + 1) * LANES)
        return _head_norm_rope(proj[:, cols], hg_ref[:, cols], cos, sin, seg)

    def store_queries(q_ref, tile0):
        for j in range(N_HEADS_A // 2):
            y = normed(tile0 + j)
            swapped = pltpu.roll(y, HEAD_DIM, 1)
            kv, g0 = (2 * j) // GROUP, (2 * j) % GROUP
            if kv == 0:
                even, odd = jnp.where(low, y, 0.0), jnp.where(low, swapped, 0.0)
            else:
                even, odd = jnp.where(low, 0.0, swapped), jnp.where(low, 0.0, y)
            q_ref[kv, 0, g0 * bm:(g0 + 1) * bm, :] = even.astype(BF16)
            q_ref[kv, 0, (g0 + 1) * bm:(g0 + 2) * bm, :] = odd.astype(BF16)

    tq = QA_W // LANES
    store_queries(qa_ref, 0)
    ka_ref[0] = normed(tq).astype(BF16)
    va = proj[:, (tq + 1) * LANES:(tq + 2) * LANES]
    _store_value_heads(va_ref, va)
    _store_value_heads_t(vat_ref, va)
    store_queries(qb_ref, tq + 2)
    kb_ref[0] = normed(2 * tq + 2).astype(BF16)
    _store_value_heads_t(vbt_ref, proj[:, (2 * tq + 3) * LANES:(2 * tq + 4) * LANES])


def _inproj_c_kernel(*refs, fused):
    x, (g_ref, s_ref, w_ref, hg_ref, cos_ref, sin_ref, seg_ref), (q_ref, k_ref, v_ref) = _inproj_rows(refs, fused)
    bm = x.shape[0]
    h = _modulated(x, g_ref, s_ref).astype(BF16)
    proj = jnp.dot(h, w_ref[...], preferred_element_type=F32)
    cos, sin, seg = cos_ref[...], sin_ref[...], seg_ref[...]
    low = _lane((bm, LANES)) < HEAD_DIM

    def normed(j):
        cols = slice(j * LANES, (j + 1) * LANES)
        return _head_norm_rope(proj[:, cols], hg_ref[:, cols], cos, sin, seg)

    for j in range(N_HEADS_C):
        y = normed(j)
        q_ref[j, 0, 0:bm, :] = jnp.where(low, y, 0.0).astype(BF16)
        q_ref[j, 0, bm:2 * bm, :] = jnp.where(low, 0.0, y).astype(BF16)
        k_ref[j] = normed(N_HEADS_C + j).astype(BF16)
        v = proj[:, (2 * N_HEADS_C + j) * LANES:(2 * N_HEADS_C + j + 1) * LANES]
        v_ref[j, 0] = v.T.astype(BF16)


def _rope_tables(seq, n_ctx):
    rows_n = seq // GRID_W
    rows = jnp.broadcast_to(jnp.arange(rows_n, dtype=F32)[:, None], (rows_n, GRID_W)).reshape(-1)
    cols = jnp.broadcast_to(jnp.arange(GRID_W, dtype=F32)[None, :], (rows_n, GRID_W)).reshape(-1)
    half = HEAD_DIM // 2
    inv = ROPE_THETA ** (-jnp.arange(0, half, 2, dtype=F32) / half)
    ang = jnp.concatenate([rows[:, None] * inv, cols[:, None] * inv], axis=-1)
    reps = LANES // half
    sign = jnp.tile(jnp.concatenate([-jnp.ones((half,), F32), jnp.ones((half,), F32)]), LANES // HEAD_DIM)
    cos = jnp.pad(jnp.tile(jnp.cos(ang), (1, reps)), ((n_ctx, 0), (0, 0)), constant_values=1.0)
    sin = jnp.pad(jnp.tile(jnp.sin(ang), (1, reps)) * sign, ((n_ctx, 0), (0, 0)))
    return cos, sin


def _segment_mean_matrix():
    idx = jnp.arange(LANES) // HEAD_DIM
    return jnp.where(idx[:, None] == idx[None, :], 1.0 / HEAD_DIM, 0.0).astype(BF16)


def _q_slot(i, nblk):
    return (i + nblk - 1) % nblk


def _inproj_common_specs(t, d, n):
    bm = ROW_BLOCK
    return [pl.BlockSpec((bm, d), lambda i: (i, 0)),
            pl.BlockSpec((2, 1, d), lambda i: (0, 0, 0)),
            pl.BlockSpec((2, 1, d), lambda i: (0, 0, 0)),
            pl.BlockSpec((d, n), lambda i: (0, 0)),
            pl.BlockSpec((1, n), lambda i: (0, 0)),
            pl.BlockSpec((bm, LANES), lambda i: (i, 0)),
            pl.BlockSpec((bm, LANES), lambda i: (i, 0)),
            pl.BlockSpec((LANES, LANES), lambda i: (0, 0))]


def _inproj_call(kernel_fn, name, x, pending, args, n, out_specs, out_shape):
    t, d = x.shape
    bm = ROW_BLOCK
    in_specs = _inproj_common_specs(t, d, n)
    args = [x] + list(args)
    fused = pending is not None
    if fused:
        y2, wts, gate2 = pending
        pieces, _, _, dp = y2.shape
        in_specs = [pl.BlockSpec((pieces, TOP_K, bm, dp), lambda i: (0, 0, i, 0)),
                    pl.BlockSpec((bm, TOP_K), lambda i: (i, 0)),
                    pl.BlockSpec((2, 1, d), lambda i: (0, 0, 0))] + in_specs
        args = [y2, wts, gate2] + args
        out_specs = [pl.BlockSpec((bm, d), lambda i: (i, 0))] + list(out_specs)
        out_shape = [jax.ShapeDtypeStruct((t, d), F32)] + list(out_shape)
    outs = pl.pallas_call(
        functools.partial(kernel_fn, fused=fused),
        grid=(t // bm,),
        in_specs=in_specs,
        out_specs=out_specs,
        out_shape=out_shape,
        compiler_params=_params("parallel"),
        name=name,
    )(*args)
    return (outs[0], outs[1:]) if fused else (x, outs)


def _inproj_ab(x, pending, gain2, shift2, w, qn_a, kn_a, qn_b, kn_b, cos, sin):
    t, d = x.shape
    n = w.shape[1]
    bm, nblk = ROW_BLOCK, t // ROW_BLOCK
    tile = lambda g, reps: jnp.tile(g.astype(F32), reps)
    ones_v = jnp.ones((KVA_W,), F32)
    hg = jnp.concatenate([tile(qn_a, N_HEADS_A) * Q_SCALE, tile(kn_a, N_KV_A), ones_v,
                          tile(qn_b, N_HEADS_B) * Q_SCALE, tile(kn_b, N_KV_B), ones_v]).reshape(1, n)
    q_shape = jax.ShapeDtypeStruct((N_KV_A, nblk, GROUP * bm, LANES), BF16)
    k_shape = jax.ShapeDtypeStruct((1, t, LANES), BF16)
    v_shape = jax.ShapeDtypeStruct((N_KV_A, t, LANES), BF16)
    vt_shape = jax.ShapeDtypeStruct((N_KV_A, nblk, GQA_V_ROWS, bm), BF16)
    q_spec = pl.BlockSpec((N_KV_A, 1, GROUP * bm, LANES), lambda i: (0, _q_slot(i, nblk), 0, 0))
    k_spec = pl.BlockSpec((1, bm, LANES), lambda i: (0, i, 0))
    v_spec = pl.BlockSpec((N_KV_A, bm, LANES), lambda i: (0, i, 0))
    vt_spec = pl.BlockSpec((N_KV_A, 1, GQA_V_ROWS, bm), lambda i: (0, i, 0, 0))
    return _inproj_call(_inproj_ab_kernel, "inproj_ab", x, pending,
                        [gain2, shift2, w, hg, cos, sin, _segment_mean_matrix()], n,
                        [q_spec, q_spec, k_spec, k_spec, v_spec, vt_spec, vt_spec],
                        [q_shape, q_shape, k_shape, k_shape, v_shape, vt_shape, vt_shape])


def _inproj_c(x, pending, gain2, shift2, w, qn, kn, cos, sin):
    t, d = x.shape
    n = w.shape[1]
    bm, nblk = ROW_BLOCK, t // ROW_BLOCK
    h = N_HEADS_C
    tile = lambda g: jnp.tile(g.astype(F32), 2 * h)
    hg = jnp.concatenate([tile(qn) * Q_SCALE, tile(kn), jnp.ones((h * DV_C,), F32)]).reshape(1, n)
    return _inproj_call(_inproj_c_kernel, "inproj_c", x, pending,
                        [gain2, shift2, w, hg, cos, sin, _segment_mean_matrix()], n,
                        [pl.BlockSpec((h, 1, 2 * bm, LANES), lambda i: (0, _q_slot(i, nblk), 0, 0)),
                         pl.BlockSpec((h, bm, LANES), lambda i: (0, i, 0)),
                         pl.BlockSpec((h, 1, DIFF_V_ROWS, bm), lambda i: (0, i, 0, 0))],
                        [jax.ShapeDtypeStruct((h, nblk, 2 * bm, LANES), BF16),
                         jax.ShapeDtypeStruct((h, t, LANES), BF16),
                         jax.ShapeDtypeStruct((h, nblk, DIFF_V_ROWS, bm), BF16)])


def _pad_rows(x):
    rows = x.shape[0]
    if rows >= LANES:
        return x[:LANES]
    return jnp.concatenate([x, jnp.zeros((LANES - rows, x.shape[1]), x.dtype)], axis=0)


def _merge_gqa_heads(o, bq):
    low = _lane((bq, LANES)) < HEAD_DIM
    pairs = [jnp.where(low, o[g * bq:(g + 1) * bq], pltpu.roll(o[(g + 1) * bq:(g + 2) * bq], HEAD_DIM, 1))
             for g in range(0, GROUP, 2)]
    return jnp.concatenate(pairs, axis=1)


def _flash_kernel(*refs, mode, online, n_keys, bk, dv, l0):
    refs = list(refs)
    q_ref, k_ref, vt_ref = refs[:3]
    pos = 3
    if online:
        m0_ref = refs[pos]
        pos += 1
    if mode == "diff":
        lam_ref, sub_ref = refs[pos:pos + 2]
        pos += 2
    o_ref = refs[pos]
    scratch = refs[pos + 1:]
    acc_sc = scratch[0]
    ones_row = acc_sc.shape[0] > dv
    m_sc = scratch[1] if online else None
    l_sc = None if ones_row else scratch[-1]

    nsub, rb = q_ref.shape[1], q_ref.shape[2]
    r = nsub * rb
    ch = vt_ref.shape[3]
    q = q_ref[0].reshape(r, LANES)
    acc_sc[...] = jnp.where(lax.broadcasted_iota(jnp.int32, acc_sc.shape, 0) == dv, l0, 0.0).astype(F32)
    if online:
        m_sc[...] = m0_ref[0]
    if not ones_row:
        l_sc[...] = jnp.full(l_sc.shape, l0, F32)

    def block(start, size):
        kb = k_ref[0, pl.ds(start, size), :]
        st = lax.dot_general(kb, q, (((1,), (1,)), ((), ())), preferred_element_type=F32)
        if online:
            m_prev = m_sc[...]
            m_new = jnp.maximum(m_prev, jnp.max(st, axis=0, keepdims=True))
            p = jnp.exp2(st - m_new)
            alpha = jnp.exp2(m_prev - m_new)
            acc = alpha * acc_sc[...]
            m_sc[...] = m_new
        else:
            p = jnp.exp2(st)
            alpha = 1.0
            acc = acc_sc[...]
        if not ones_row:
            l_sc[...] = alpha * l_sc[...] + jnp.sum(p, axis=0, keepdims=True)
        pt = p.astype(BF16)
        c0 = start // ch
        for c in range(size // ch):
            acc = acc + jnp.dot(vt_ref[0, c0 + c], pt[c * ch:(c + 1) * ch], preferred_element_type=F32)
        acc_sc[...] = acc

    n_full, tail = n_keys // bk, n_keys % bk
    if n_full:
        def body(i, carry):
            block(pl.multiple_of(i * bk, bk), bk)
            return carry
        lax.fori_loop(0, n_full, body, 0)
    if tail:
        block(n_full * bk, tail)

    acc = acc_sc[...]
    den = acc[dv:dv + 1] if ones_row else l_sc[...]
    ot = _pad_rows(acc / den)
    bq = rb // (GROUP if mode == "gqa" else 2)
    o = jnp.concatenate([ot[:, j * bq:(j + 1) * bq].T for j in range(r // bq)], axis=0)
    if mode == "gqa":
        for b in range(nsub):
            o_ref[b * bq:(b + 1) * bq, :] = _merge_gqa_heads(o[b * rb:(b + 1) * rb], bq).astype(o_ref.dtype)
    else:
        for b in range(nsub):
            d = o[b * rb:b * rb + bq] - lam_ref[...] * o[b * rb + bq:(b + 1) * rb]
            y = d * lax.rsqrt(jnp.mean(d * d, axis=-1, keepdims=True) + EPS) * sub_ref[...]
            o_ref[b * bq:(b + 1) * bq, :] = y.astype(o_ref.dtype)


def _flash_call(q, k, vt, extra, *, mode, online, n_keys, slot0, nsub, n_steps, l0=0.0):
    hkv, _, rb, _ = q.shape
    hk = k.shape[0]
    _, _, dvr, ch = vt.shape
    dv = HEAD_DIM if mode == "gqa" else DV_C
    bq = rb // (GROUP if mode == "gqa" else 2)
    ocols = GROUP * HEAD_DIM if mode == "gqa" else DV_C
    r = nsub * rb
    bk = min(KEY_BLOCK, n_keys)
    in_specs = [pl.BlockSpec((1, nsub, rb, LANES), lambda h, i: (h, slot0 // nsub + i, 0, 0)),
                pl.BlockSpec((1, n_keys, LANES), (lambda h, i: (h, 0, 0)) if hk > 1 else (lambda h, i: (0, 0, 0))),
                pl.BlockSpec((1, n_keys // ch, dvr, ch), lambda h, i: (h, 0, 0, 0))]
    args = [q, k, vt]
    scratch = [pltpu.VMEM((dvr, r), F32)]
    if online:
        m0 = extra.pop(0)
        in_specs.append(pl.BlockSpec((1, 1, r), lambda h, i: (h, 0, 0)))
        args.append(m0)
        scratch.append(pltpu.VMEM((1, r), F32))
    if dvr <= dv:
        scratch.append(pltpu.VMEM((1, r), F32))
    for a in extra:
        in_specs.append(pl.BlockSpec(a.shape, lambda h, i: (0, 0)))
        args.append(a)
    return pl.pallas_call(
        functools.partial(_flash_kernel, mode=mode, online=online, n_keys=n_keys, bk=bk, dv=dv, l0=l0),
        grid=(hkv, n_steps),
        in_specs=in_specs,
        out_specs=pl.BlockSpec((nsub * bq, ocols), lambda h, i: (i, h)),
        out_shape=jax.ShapeDtypeStruct((n_steps * nsub * bq, hkv * ocols), BF16),
        scratch_shapes=scratch,
        compiler_params=_params("parallel", "parallel"),
        name="flash_online" if online else "flash_bounded",
    )(*args)


def _logits_bounded(q_gain, k_gain):
    bound = HEAD_DIM * Q_SCALE * 1.02 * jnp.max(jnp.abs(q_gain.astype(F32))) * jnp.max(jnp.abs(k_gain.astype(F32)))
    return bound <= LOG2_LOGIT_BOUND


def _attend(q, k, v, extra, bounded, **kw):
    hkv, r = q.shape[0], kw["nsub"] * q.shape[2]
    fast = lambda q_, k_, v_, *e: _flash_call(q_, k_, v_, list(e), online=False, **kw)
    safe = lambda q_, k_, v_, *e: _flash_call(q_, k_, v_, [jnp.full((hkv, 1, r), NEG, F32)] + list(e), online=True, **kw)
    return lax.cond(bounded, fast, safe, q, k, v, *extra)


def _window_kernel(q_ref, k_ref, v_ref, sink_ref, o_ref, *, bq, n_ctx):
    q = q_ref[0, 0]
    r = q.shape[0]
    t = k_ref.shape[1]
    w = bq + 2 * WINDOW
    q0 = pl.program_id(1) * bq
    ws = pl.multiple_of(jnp.clip(n_ctx + q0 - WINDOW, 0, t - w), WINDOW)
    kw = k_ref[0, pl.ds(ws, w), :]
    vw = v_ref[0, pl.ds(ws, w), :]
    contract_last = (((1,), (1,)), ((), ()))
    s_loc = lax.dot_general(q, kw, contract_last, preferred_element_type=F32)
    qpos = q0 + (lax.broadcasted_iota(jnp.int32, (r, w), 0) & (bq - 1))
    kpos = ws - n_ctx + lax.broadcasted_iota(jnp.int32, (r, w), 1)
    mask = (kpos >= 0) & (kpos - qpos <= WINDOW) & (qpos - kpos <= WINDOW)
    s_loc = jnp.where(mask, s_loc, NEG)
    s_ctx = lax.dot_general(q, k_ref[0, 0:n_ctx, :], contract_last, preferred_element_type=F32)
    sink = sink_ref[0]
    m = jnp.maximum(sink, jnp.maximum(jnp.max(s_loc, axis=-1, keepdims=True), jnp.max(s_ctx, axis=-1, keepdims=True)))
    p_loc = jnp.exp2(s_loc - m)
    p_ctx = jnp.exp2(s_ctx - m)
    l = jnp.exp2(sink - m) + jnp.sum(p_loc, axis=-1, keepdims=True) + jnp.sum(p_ctx, axis=-1, keepdims=True)
    o = (jnp.dot(p_loc.astype(BF16), vw, preferred_element_type=F32)
         + jnp.dot(p_ctx.astype(BF16), v_ref[0, 0:n_ctx, :], preferred_element_type=F32))
    o_ref[...] = _merge_gqa_heads(o / l, bq).astype(o_ref.dtype)


def _window_attention(q, k, v, sink_rows, n_ctx):
    hkv, slots, r, _ = q.shape
    t = k.shape[1]
    bq = r // GROUP
    nq = slots - n_ctx // bq
    return pl.pallas_call(
        functools.partial(_window_kernel, bq=bq, n_ctx=n_ctx),
        grid=(hkv, nq),
        in_specs=[pl.BlockSpec((1, 1, r, LANES), lambda h, i: (h, i, 0, 0)),
                  pl.BlockSpec((1, t, LANES), lambda h, i: (0, 0, 0)),
                  pl.BlockSpec((1, t, LANES), lambda h, i: (h, 0, 0)),
                  pl.BlockSpec((1, r, 1), lambda h, i: (h, 0, 0))],
        out_specs=pl.BlockSpec((bq, GROUP * HEAD_DIM), lambda h, i: (i, h)),
        out_shape=jax.ShapeDtypeStruct((nq * bq, hkv * GROUP * HEAD_DIM), BF16),
        compiler_params=_params("parallel", "parallel"),
        name="window_attention",
    )(q, k, v, sink_rows)


def _window_bounded_kernel(q_ref, k_ref, vt_ref, mask_ref, sink_ref, o_ref, *, bq):
    q = q_ref[0, 0]
    ch = vt_ref.shape[3]
    cw = jnp.clip(pl.program_id(1), 0, vt_ref.shape[1] - 3)
    contract_last = (((1,), (1,)), ((), ()))
    k_win = k_ref[0, pl.ds(pl.multiple_of(cw * ch, ch), 3 * ch), :]
    st_win = lax.dot_general(k_win, q, contract_last, preferred_element_type=F32)
    st_ctx = lax.dot_general(k_ref[0, 0:ch, :], q, contract_last, preferred_element_type=F32)
    pt_win = jnp.exp2(st_win).astype(BF16) * mask_ref[0]
    acc = jnp.dot(vt_ref[0, 0], jnp.exp2(st_ctx).astype(BF16), preferred_element_type=F32)
    for c in range(3):
        acc = acc + jnp.dot(vt_ref[0, cw + c], pt_win[c * ch:(c + 1) * ch], preferred_element_type=F32)
    ot = _pad_rows(acc / (acc[HEAD_DIM:HEAD_DIM + 1] + jnp.exp2(sink_ref[0])))
    o = jnp.concatenate([ot[:, g * bq:(g + 1) * bq].T for g in range(GROUP)], axis=0)
    o_ref[...] = _merge_gqa_heads(o, bq).astype(o_ref.dtype)


def _window_masks(bq, r):
    c = jnp.arange(3 * bq)[:, None]
    off = c - (jnp.arange(r)[None, :] & (bq - 1))
    centred = (off >= bq - WINDOW) & (off <= bq + WINDOW)
    shifted = (off >= 2 * bq - WINDOW) & (off <= 2 * bq + WINDOW)
    return jnp.stack([centred & (c >= bq), centred, shifted]).astype(BF16)


def _window_attention_bounded(q, k, vt, sink_cols, n_ctx):
    hkv, slots, r, _ = q.shape
    t = k.shape[1]
    bq = r // GROUP
    nblk = t // bq
    nq = nblk - 1
    assert n_ctx == bq and nblk >= 3
    return pl.pallas_call(
        functools.partial(_window_bounded_kernel, bq=bq),
        grid=(hkv, nq),
        in_specs=[pl.BlockSpec((1, 1, r, LANES), lambda h, i: (h, i, 0, 0)),
                  pl.BlockSpec((1, t, LANES), lambda h, i: (0, 0, 0)),
                  pl.BlockSpec((1, nblk, GQA_V_ROWS, bq), lambda h, i: (h, 0, 0, 0)),
                  pl.BlockSpec((1, 3 * bq, r), lambda h, i: (jnp.where(i == 0, 0, jnp.where(i >= nblk - 2, 2, 1)), 0, 0)),
                  pl.BlockSpec((1, 1, r), lambda h, i: (h, 0, 0))],
        out_specs=pl.BlockSpec((bq, GROUP * HEAD_DIM), lambda h, i: (i, h)),
        out_shape=jax.ShapeDtypeStruct((nq * bq, hkv * GROUP * HEAD_DIM), BF16),
        compiler_params=_params("parallel", "parallel"),
        name="window_bounded",
    )(q, k, vt, _window_masks(bq, r), sink_cols)


def _sink_rows(sink, bq):
    hkv, g = sink.shape
    return jnp.broadcast_to(sink.astype(F32)[:, :, None] * LOG2E, (hkv, g, bq)).reshape(hkv, g * bq, 1)


def _mixer_ab(heads, n_ctx, sink_a, with_ctx, bounded_a, bounded):
    qa, qb, ka, kb, va, vat, vbt = heads
    t = ka.shape[1]
    bq = ROW_BLOCK
    n_lat = (t - n_ctx) // bq
    sink = _sink_rows(sink_a.reshape(N_KV_A, GROUP), bq)
    sink_cols = sink.reshape(N_KV_A, 1, GROUP * bq)
    bounded_a = bounded_a & (jnp.max(jnp.abs(sink)) <= LOG2_LOGIT_BOUND)
    oa = lax.cond(bounded_a,
                  lambda: _window_attention_bounded(qa, ka, vat, sink_cols, n_ctx),
                  lambda: _window_attention(qa, ka, va, sink, n_ctx))
    nsub = 2 if n_lat % 2 == 0 else 1
    ob = _attend(qb, kb, vbt, [], bounded, mode="gqa", n_keys=t, slot0=0, nsub=nsub, n_steps=n_lat // nsub)
    if not with_ctx:
        return (oa, ob), None
    oca = _flash_call(qa, ka, vat, [sink_cols], mode="gqa", online=True, n_keys=n_ctx,
                      slot0=n_lat, nsub=1, n_steps=1, l0=1.0)
    ocb = _attend(qb, kb, vbt, [], bounded, mode="gqa", n_keys=n_ctx, slot0=n_lat, nsub=1, n_steps=1)
    return (oa, ob), (oca, ocb)


def _mixer_c(heads, n_ctx, lam_p, subln, lam_init, with_ctx, bounded):
    q, k, v = heads
    t = k.shape[1]
    n_lat = (t - n_ctx) // ROW_BLOCK
    lp = lam_p.astype(F32)
    lam = jnp.exp(jnp.sum(lp[0] * lp[1])) - jnp.exp(jnp.sum(lp[2] * lp[3])) + lam_init
    extra = [jnp.full((1, DV_C), lam, F32), (subln.astype(F32) * (1 - lam_init)).reshape(1, DV_C)]
    nsub = 4 if n_lat % 4 == 0 else 1
    o_lat = _attend(q, k, v, extra, bounded, mode="diff", n_keys=t, slot0=0, nsub=nsub, n_steps=n_lat // nsub)
    if not with_ctx:
        return (o_lat,), None
    o_ctx = _attend(q, k, v, extra, bounded, mode="diff", n_keys=n_ctx, slot0=n_lat, nsub=1, n_steps=1)
    return (o_lat,), (o_ctx,)


def _out_proj_kernel(*refs, n_parts, has_ctx):
    lat = refs[:n_parts]
    ctx = refs[n_parts:2 * n_parts] if has_ctx else None
    w_ref, x_ref, gate_ref, g_ref, s_ref, wr_ref, y_ref, tok_ref, logit_ref = refs[-9:]
    is_ctx = pl.program_id(0) == 0 if has_ctx else False
    row = jnp.where(is_ctx, 0, 1) if has_ctx else 1
    acc = None
    col = 0
    for p in range(n_parts):
        o = lat[p][...]
        if has_ctx:
            o = jnp.where(is_ctx, ctx[p][...], o)
        width = o.shape[1]
        part = jnp.dot(o, w_ref[col:col + width, :], preferred_element_type=F32)
        acc = part if acc is None else acc + part
        col += width
    x1 = x_ref[...] + gate_ref[row] * acc
    y_ref[...] = x1
    h = x1 * lax.rsqrt(jnp.mean(x1 * x1, axis=-1, keepdims=True) + EPS) * g_ref[row] + s_ref[row]
    _store_pieces(tok_ref, h)
    hi, lo = _split_bf16(h)
    logits = (jnp.dot(hi, wr_ref[0], preferred_element_type=F32) + jnp.dot(lo, wr_ref[0], preferred_element_type=F32)
              + jnp.dot(hi, wr_ref[1], preferred_element_type=F32))
    logit_ref[...] = logits.T


def _out_proj(o_lat, o_ctx, w, x, gate2, gain2, shift2, w_router):
    d = x.shape[1]
    bm = ROW_BLOCK
    dp = d // SC_ROW_SPLIT
    has_ctx = o_ctx is not None
    row0 = 0 if has_ctx else (x.shape[0] - o_lat[0].shape[0]) // bm
    t = x.shape[0] - row0 * bm
    n_parts = len(o_lat)
    lat_map = (lambda i: (jnp.maximum(i - 1, 0), 0)) if has_ctx else (lambda i: (i, 0))
    in_specs = [pl.BlockSpec((bm, o.shape[1]), lat_map) for o in o_lat]
    args = list(o_lat)
    if has_ctx:
        in_specs += [pl.BlockSpec((bm, o.shape[1]), lambda i: (0, 0)) for o in o_ctx]
        args += list(o_ctx)
    vec_spec = pl.BlockSpec((2, 1, d), lambda i: (0, 0, 0))
    in_specs += [pl.BlockSpec(w.shape, lambda i: (0, 0)),
                 pl.BlockSpec((bm, d), lambda i: (i + row0, 0)),
                 vec_spec, vec_spec, vec_spec,
                 pl.BlockSpec((2, d, LANES), lambda i: (0, 0, 0))]
    return pl.pallas_call(
        functools.partial(_out_proj_kernel, n_parts=n_parts, has_ctx=has_ctx),
        grid=(t // bm,),
        in_specs=in_specs,
        out_specs=[pl.BlockSpec((bm, d), lambda i: (i, 0)),
                   pl.BlockSpec((SC_ROW_SPLIT, bm, dp), lambda i: (0, i, 0)),
                   pl.BlockSpec((LANES, bm), lambda i: (0, i))],
        out_shape=[jax.ShapeDtypeStruct((t, d), F32),
                   jax.ShapeDtypeStruct((SC_ROW_SPLIT, t, dp), F32),
                   jax.ShapeDtypeStruct((LANES, t), F32)],
        compiler_params=_params("parallel"),
        name="out_proj_router",
    )(*args, w, x, gate2, gain2, shift2, w_router)


def _store_pieces(ref, rows):
    dp = ref.shape[2]
    for j in range(ref.shape[0]):
        ref[j] = rows[:, j * dp:(j + 1) * dp]


def _load_pieces(planes):
    return jnp.concatenate([planes[j] for j in range(planes.shape[0])], axis=1)


def _moe_kernel(blk_ref, exp_ref, lo_ref, hi_ref, n_ref, x_ref, w1_ref, w3_ref, w2_ref, y_ref, w1_sc, w3_sc, w2_sc):
    i = pl.program_id(0)

    @pl.when(i < n_ref[0])
    def _():
        @pl.when((i == 0) | (exp_ref[i] != exp_ref[jnp.maximum(i - 1, 0)]))
        def _():
            w1_sc[...] = w1_ref[0, 0].astype(BF16)
            w3_sc[...] = w3_ref[0, 0].astype(BF16)
            w2_sc[...] = w2_ref[0, 0].astype(BF16)

        x = _load_pieces(x_ref[...]).astype(BF16)
        a = jnp.dot(x, w1_sc[...], preferred_element_type=F32)
        b = jnp.dot(x, w3_sc[...], preferred_element_type=F32)
        hidden = (a * jax.nn.sigmoid(a)) * b
        y = jnp.dot(hidden.astype(BF16), w2_sc[...], preferred_element_type=F32)
        rows = lax.broadcasted_iota(jnp.int32, (y.shape[0], 1), 0)
        y = jnp.where((rows >= lo_ref[i]) & (rows < hi_ref[i]), y, 0.0)
        first = (i == 0) | (blk_ref[i] != blk_ref[jnp.maximum(i - 1, 0)])

        @pl.when(first)
        def _():
            _store_pieces(y_ref, y)

        @pl.when(jnp.logical_not(first))
        def _():
            _store_pieces(y_ref, _load_pieces(y_ref[...]) + y)


def _moe_experts(buf, items, w1, w3, w2, layer):
    pieces, n_rows, dp = buf.shape
    d, de = w1.shape[2], w1.shape[3]
    n_items = items[0].shape[0]
    row_spec = pl.BlockSpec((pieces, EXPERT_BLOCK, dp), lambda i, blk, ex, lo, hi, n: (0, blk[i], 0))
    w_map = lambda i, blk, ex, lo, hi, n: (layer, ex[i], 0, 0)
    grid_spec = pltpu.PrefetchScalarGridSpec(
        num_scalar_prefetch=5,
        grid=(n_items,),
        in_specs=[row_spec,
                  pl.BlockSpec((1, 1, d, de), w_map),
                  pl.BlockSpec((1, 1, d, de), w_map),
                  pl.BlockSpec((1, 1, de, d), w_map)],
        out_specs=row_spec,
        scratch_shapes=[pltpu.VMEM((d, de), BF16), pltpu.VMEM((d, de), BF16), pltpu.VMEM((de, d), BF16)],
    )
    return pl.pallas_call(
        _moe_kernel,
        grid_spec=grid_spec,
        out_shape=jax.ShapeDtypeStruct((pieces, n_rows, dp), F32),
        compiler_params=_params("arbitrary"),
        name="moe_experts",
    )(*items, buf, w1, w3, w2)


def _expert_items(counts, n_rows):
    e = counts.shape[0]
    n_blk = n_rows // EXPERT_BLOCK
    n_items = n_blk + e - 1
    end = jnp.cumsum(counts)
    start = end - counts
    first_blk = start // EXPERT_BLOCK
    per_expert = jnp.where(counts > 0, (end - 1) // EXPERT_BLOCK - first_blk + 1, 0)
    cum = jnp.cumsum(per_expert)
    total = cum[-1]
    w = jnp.minimum(jnp.arange(n_items, dtype=jnp.int32), total - 1)
    onehot = ((cum - per_expert)[None, :] <= w[:, None]) & (w[:, None] < cum[None, :])
    pick = lambda tab: jnp.sum(jnp.where(onehot, tab[None, :], 0), axis=1).astype(jnp.int32)
    expert = pick(jnp.arange(e, dtype=jnp.int32))
    blk = pick(first_blk) + w - pick(cum - per_expert)
    lo = jnp.clip(pick(start) - blk * EXPERT_BLOCK, 0, EXPERT_BLOCK)
    hi = jnp.clip(pick(end) - blk * EXPERT_BLOCK, 0, EXPERT_BLOCK)
    return blk, expert, lo, hi, total.astype(jnp.int32).reshape(1)


def _sc_scatter(x, pos):
    pieces, t, dp = x.shape
    n = pos.shape[0]
    kk = n // t
    offs = jnp.arange(pieces, dtype=jnp.int32)[:, None] * n
    idx = [(offs + pos[k * t:(k + 1) * t][None, :]).reshape(1, pieces * t) for k in range(kk)]
    mesh = plsc.VectorSubcoreMesh(core_axis_name="core", subcore_axis_name="subcore")

    @pl.kernel(out_type=jax.ShapeDtypeStruct((pieces * n, dp), x.dtype), mesh=mesh, scratch_types=[])
    def scatter(x_hbm, *refs):
        i_hbm, o_hbm = refs[:kk], refs[kk]

        def body(x_vmem, *i_vmem):
            for iv in i_vmem:
                pltpu.sync_copy(x_vmem, o_hbm.at[iv.at[0]])

        pltpu.emit_pipeline(
            body,
            grid=(pieces * t // SC_GATHER_WINDOW,),
            in_specs=[pl.BlockSpec((SC_GATHER_WINDOW, dp), lambda i: (i, 0))]
                     + [pl.BlockSpec((1, SC_GATHER_WINDOW), lambda i: (0, i))] * kk,
            out_specs=[],
            core_axis_name=("core", "subcore"),
            dimension_semantics=(pltpu.PARALLEL,),
        )(x_hbm, *i_hbm)

    return scatter(x.reshape(pieces * t, dp), *idx).reshape(pieces, n, dp)


def _sc_gather(x, idx):
    pieces, t, dp = x.shape
    n = idx.shape[0]
    flat = (jnp.arange(pieces, dtype=jnp.int32)[:, None] * t + idx[None, :]).reshape(1, pieces * n)
    mesh = plsc.VectorSubcoreMesh(core_axis_name="core", subcore_axis_name="subcore")

    @pl.kernel(out_type=jax.ShapeDtypeStruct((pieces * n, dp), x.dtype), mesh=mesh, scratch_types=[])
    def gather(x_hbm, i_hbm, o_hbm):
        def body(i_vmem, o_vmem):
            pltpu.sync_copy(x_hbm.at[i_vmem.at[0]], o_vmem)

        pltpu.emit_pipeline(
            body,
            grid=(pieces * n // SC_GATHER_WINDOW,),
            in_specs=[pl.BlockSpec((1, SC_GATHER_WINDOW), lambda i: (0, i))],
            out_specs=[pl.BlockSpec((SC_GATHER_WINDOW, dp), lambda i: (i, 0))],
            core_axis_name=("core", "subcore"),
            dimension_semantics=(pltpu.PARALLEL,),
        )(i_hbm, o_hbm)

    return gather(x.reshape(pieces * t, dp), flat).reshape(pieces, n, dp)


def _combine_kernel(y_ref, w_ref, x_ref, gate_ref, o_ref, *, row0):
    w = w_ref[...]
    f = w[:, 0:1] * _load_pieces(y_ref[:, 0]) + w[:, 1:2] * _load_pieces(y_ref[:, 1])
    row = jnp.minimum(pl.program_id(0) + row0, 1)
    o_ref[...] = x_ref[...] + gate_ref[row] * f


def _combine(y2, weights, x, gate2, has_ctx):
    t, d = x.shape
    bm = ROW_BLOCK
    pieces, _, _, dp = y2.shape
    return pl.pallas_call(
        functools.partial(_combine_kernel, row0=0 if has_ctx else 1),
        grid=(t // bm,),
        in_specs=[pl.BlockSpec((pieces, TOP_K, bm, dp), lambda i: (0, 0, i, 0)),
                  pl.BlockSpec((bm, TOP_K), lambda i: (i, 0)),
                  pl.BlockSpec((bm, d), lambda i: (i, 0)),
                  pl.BlockSpec((2, 1, d), lambda i: (0, 0, 0))],
        out_specs=pl.BlockSpec((bm, d), lambda i: (i, 0)),
        out_shape=jax.ShapeDtypeStruct((t, d), F32),
        compiler_params=_params("parallel"),
        name="moe_combine",
    )(y2, weights, x, gate2)


def _top1_rows(p):
    m = jnp.max(p, axis=0, keepdims=True)
    idx = lax.broadcasted_iota(jnp.int32, p.shape, 0)
    return m, jnp.min(jnp.where(p == m, idx, p.shape[0]), axis=0, keepdims=True)


def _prefix_rank(onehot):
    e, n = onehot.shape
    blk = EXPERT_BLOCK
    nb = n // blk
    earlier = (jnp.arange(blk)[:, None] < jnp.arange(blk)[None, :]).astype(F32)
    within = jnp.dot(onehot.reshape(e * nb, blk), earlier).reshape(e, n)
    tot = jnp.sum(onehot.reshape(e, nb, blk), axis=2)
    before = jnp.repeat(jnp.cumsum(tot, axis=1) - tot, blk, axis=1)
    return jnp.sum((within + before) * onehot, axis=0).astype(jnp.int32)


def _hier_moe(tokens, logits_t, b_group, b_expert, w1, w3, w2, layer):
    pieces, t, dp = tokens.shape
    pg = jax.nn.softmax(logits_t[:N_GROUPS] + b_group.astype(F32)[:, None], axis=0)
    g_prob, g_idx = _top1_rows(pg)
    le = (logits_t[N_GROUPS:N_GROUPS + N_EXPERTS] + b_expert.astype(F32)[:, None]).reshape(N_GROUPS, EXPERTS_PER_GROUP, t)
    group_iota = lax.broadcasted_iota(jnp.int32, (N_GROUPS, 1, t), 0)
    le = jnp.sum(jnp.where(group_iota == g_idx[None], le, 0.0), axis=0)
    pe = jax.nn.softmax(le, axis=0)
    p1, i1 = _top1_rows(pe)
    p2, i2 = _top1_rows(jnp.where(lax.broadcasted_iota(jnp.int32, pe.shape, 0) == i1, -1.0, pe))
    e_prob, e_idx = jnp.concatenate([p1, p2], axis=0), jnp.concatenate([i1, i2], axis=0)
    weights = g_prob * e_prob / jnp.sum(e_prob, axis=0, keepdims=True)
    flat_e = (g_idx * EXPERTS_PER_GROUP + e_idx).reshape(1, TOP_K * t)
    onehot = (flat_e == lax.broadcasted_iota(jnp.int32, (N_EXPERTS, TOP_K * t), 0)).astype(F32)
    rank = _prefix_rank(onehot)
    counts = jnp.sum(onehot, axis=1).astype(jnp.int32)
    first_row = jnp.sum(onehot * (jnp.cumsum(counts) - counts).astype(F32)[:, None], axis=0).astype(jnp.int32)
    dest = first_row + rank
    buf = _sc_scatter(tokens, dest)
    yb = _moe_experts(buf, _expert_items(counts, TOP_K * t), w1, w3, w2, layer)
    return _sc_gather(yb, dest).reshape(pieces, TOP_K, t, dp), weights.T


def kernel(x, c, ctx, c_ctx, w_mod, b_mod, norm_mix, norm_ffn, w_in_ab, w_out_ab, qn_a, kn_a, sink_a, qn_b, kn_b, w_in_c, w_out_c, qn_c, kn_c, lam_c, subln_c, w_group, b_group, w_expert, b_expert, w1, w3, w2):
    b, s_lat, d = x.shape
    n_ctx = ctx.shape[1]
    assert b == 1 and n_ctx == ROW_BLOCK and s_lat % (2 * ROW_BLOCK) == 0
    depth = w_mod.shape[0]
    cos, sin = _rope_tables(s_lat, n_ctx)
    mods = _mod_vectors(c, c_ctx, w_mod, b_mod)
    xs = jnp.concatenate([ctx[0], x[0]], axis=0)
    has_ctx = True
    pending = None
    for l in range(depth):
        last = l == depth - 1
        i = l // 2
        sh1, sc1, gt1, sh2, sc2, gt2 = [mods[l, :2, j * d:(j + 1) * d].reshape(2, 1, d) for j in range(6)]
        gain1 = norm_mix[l].astype(F32) * (1 + sc1)
        if l % 2 == 0:
            xs, heads = _inproj_ab(xs, pending, gain1, sh1, w_in_ab[i].astype(BF16), qn_a[i], kn_a[i], qn_b[i], kn_b[i], cos, sin)
            o_lat, o_ctx = _mixer_ab(heads, n_ctx, sink_a[i], not last,
                                     _logits_bounded(qn_a[i], kn_a[i]), _logits_bounded(qn_b[i], kn_b[i]))
            w_out = w_out_ab[i]
        else:
            lam_init = 0.8 - 0.6 * math.exp(-0.3 * l)
            xs, heads = _inproj_c(xs, pending, gain1, sh1, w_in_c[i].astype(BF16), qn_c[i], kn_c[i], cos, sin)
            o_lat, o_ctx = _mixer_c(heads, n_ctx, lam_c[i], subln_c[i], lam_init, not last,
                                    _logits_bounded(qn_c[i], kn_c[i]))
            w_out = w_out_c[i]
        if last:
            has_ctx = False
        gain2 = norm_ffn[l].astype(F32) * (1 + sc2)
        w_router = jnp.zeros((d, LANES), F32).at[:, :N_GROUPS].set(w_group[l]).at[:, N_GROUPS:N_GROUPS + N_EXPERTS].set(w_expert[l])
        w_router = jnp.stack(_split_bf16(w_router))
        xs, tokens, logits = _out_proj(o_lat, o_ctx, w_out.astype(BF16), xs, gt1, gain2, sh2, w_router)
        y2, weights = _hier_moe(tokens, logits, b_group[l], b_expert[l], w1, w3, w2, l)
        pending = (y2, weights, gt2)
    return _combine(y2, weights, xs, gt2, has_ctx).reshape(b, s_lat, d)
```

```python
import functools
import math

import jax
import jax.numpy as jnp
from jax import lax
from jax.experimental import pallas as pl
from jax.experimental.pallas import tpu as pltpu
from jax.experimental.pallas import tpu_sc as plsc

F32 = jnp.float32
BF16 = jnp.bfloat16

GRID_W = 64
HEAD_DIM = 64
WINDOW = 128
ROPE_THETA = 10000.0
EPS = 1e-6
NEG = -1e30
N_HEADS_A, N_KV_A = 8, 2
N_HEADS_B, N_KV_B = 8, 2
GROUP = N_HEADS_A // N_KV_A
QA_W, KVA_W = N_HEADS_A * HEAD_DIM, N_KV_A * HEAD_DIM
QB_W, KVB_W = N_HEADS_B * HEAD_DIM, N_KV_B * HEAD_DIM
N_HEADS_C = 8
DV_C = 2 * HEAD_DIM
N_GROUPS, EXPERTS_PER_GROUP, TOP_K = 4, 8, 2
N_EXPERTS = N_GROUPS * EXPERTS_PER_GROUP
LANES = 128
ROW_BLOCK = 256
EXPERT_BLOCK = 512
SC_GATHER_WINDOW = 128
SC_ROW_SPLIT = 4
KEY_BLOCK = 3328
GQA_V_ROWS = 128
DIFF_V_ROWS = 128
VMEM_LIMIT = 48 * 1024 * 1024
LOG2E = math.log2(math.e)
Q_SCALE = HEAD_DIM ** -0.5 * LOG2E
LOG2_LOGIT_BOUND = 60.0


def _params(*sem):
    return pltpu.CompilerParams(dimension_semantics=sem, vmem_limit_bytes=VMEM_LIMIT)


def _lane(shape):
    return lax.broadcasted_iota(jnp.int32, shape, 1)


def _mod_vec_kernel(a_ref, w_ref, b_ref, o_ref):
    a = a_ref[...]
    a = a * jax.nn.sigmoid(a)
    o_ref[0] = jnp.dot(a, w_ref[0], preferred_element_type=F32, precision=lax.Precision.HIGHEST) + b_ref[0]


def _mod_vectors(c, c_ctx, w_mod, b_mod):
    depth, d, n = w_mod.shape
    a = jnp.zeros((8, d), F32).at[0].set(c_ctx).at[1].set(c[0])
    bn = 1024
    return pl.pallas_call(
        _mod_vec_kernel,
        grid=(depth, n // bn),
        in_specs=[pl.BlockSpec((8, d), lambda l, j: (0, 0)),
                  pl.BlockSpec((1, d, bn), lambda l, j: (l, 0, j)),
                  pl.BlockSpec((1, 1, bn), lambda l, j: (l, 0, j))],
        out_specs=pl.BlockSpec((1, 8, bn), lambda l, j: (l, 0, j)),
        out_shape=jax.ShapeDtypeStruct((depth, 8, n), F32),
        compiler_params=_params("parallel", "parallel"),
        name="mod_vectors",
    )(a, w_mod, b_mod.reshape(depth, 1, n))


def _modulated(x, g_ref, s_ref):
    y = x * lax.rsqrt(jnp.mean(x * x, axis=-1, keepdims=True) + EPS)
    row = jnp.minimum(pl.program_id(0), 1)
    return y * g_ref[row] + s_ref[row]


def _split_bf16(x):
    top = lax.bitcast_convert_type(lax.bitcast_convert_type(x, jnp.uint32) & jnp.uint32(0xFFFF0000), F32)
    return top.astype(BF16), (x - top).astype(BF16)


def _head_norm_rope(x, gain, cos, sin, seg_mean):
    hi, lo = _split_bf16(x * x)
    ms = jnp.dot(hi, seg_mean, preferred_element_type=F32) + jnp.dot(lo, seg_mean, preferred_element_type=F32)
    y = x * lax.rsqrt(ms + EPS) * gain
    first_half = (_lane(y.shape) & (HEAD_DIM - 1)) < HEAD_DIM // 2
    partner = jnp.where(first_half, pltpu.roll(y, LANES - HEAD_DIM // 2, 1), pltpu.roll(y, HEAD_DIM // 2, 1))
    return y * cos + partner * sin


def _store_value_heads(v_ref, v):
    lane = _lane(v.shape)
    ones_col = jnp.where(lane == HEAD_DIM, 1.0, 0.0)
    v_ref[0] = jnp.where(lane < HEAD_DIM, v, ones_col).astype(BF16)
    v_ref[1] = jnp.where(lane < HEAD_DIM, pltpu.roll(v, HEAD_DIM, 1), ones_col).astype(BF16)


def _ones_row_block(rows, cols):
    return jnp.where(lax.broadcasted_iota(jnp.int32, (rows, cols), 0) == 0, 1.0, 0.0)


def _store_value_heads_t(vt_ref, v):
    vt = v.T
    tail = _ones_row_block(GQA_V_ROWS - HEAD_DIM, v.shape[0])
    vt_ref[0, 0] = jnp.concatenate([vt[:HEAD_DIM], tail], axis=0).astype(BF16)
    vt_ref[1, 0] = jnp.concatenate([vt[HEAD_DIM:], tail], axis=0).astype(BF16)


N_INPROJ_INPUTS = 8


def _inproj_rows(refs, fused):
    if not fused:
        return refs[0][...], refs[1:N_INPROJ_INPUTS], refs[N_INPROJ_INPUTS:]
    y_ref, wts_ref, gate_ref, x_ref = refs[:4]
    ins, outs = refs[4:N_INPROJ_INPUTS + 3], refs[N_INPROJ_INPUTS + 3:]
    wts = wts_ref[...]
    f = wts[:, 0:1] * _load_pieces(y_ref[:, 0]) + wts[:, 1:2] * _load_pieces(y_ref[:, 1])
    x = x_ref[...] + gate_ref[jnp.minimum(pl.program_id(0), 1)] * f
    outs[0][...] = x
    return x, ins, outs[1:]


def _inproj_ab_kernel(*refs, fused):
    x, (g_ref, s_ref, w_ref, hg_ref, cos_ref, sin_ref, seg_ref), outs = _inproj_rows(refs, fused)
    qa_ref, qb_ref, ka_ref, kb_ref, va_ref, vat_ref, vbt_ref = outs
    bm = x.shape[0]
    h = _modulated(x, g_ref, s_ref).astype(BF16)
    proj = jnp.dot(h, w_ref[...], preferred_element_type=F32)
    cos, sin, seg = cos_ref[...], sin_ref[...], seg_ref[...]
    low = _lane((bm, LANES)) < HEAD_DIM

    def normed(j):
        cols = slice(j * LANES, (j + 1) * LANES)
        return _head_norm_rope(proj[:, cols], hg_ref[:, cols], cos, sin, seg)

    def store_queries(q_ref, tile0):
        for j in range(N_HEADS_A // 2):
            y = normed(tile0 + j)
            swapped = pltpu.roll(y, HEAD_DIM, 1)
            kv, g0 = (2 * j) // GROUP, (2 * j) % GROUP
            if kv == 0:
                even, odd = jnp.where(low, y, 0.0), jnp.where(low, swapped, 0.0)
            else:
                even, odd = jnp.where(low, 0.0, swapped), jnp.where(low, 0.0, y)
            q_ref[kv, 0, g0 * bm:(g0 + 1) * bm, :] = even.astype(BF16)
            q_ref[kv, 0, (g0 + 1) * bm:(g0 + 2) * bm, :] = odd.astype(BF16)

    tq = QA_W // LANES
    store_queries(qa_ref, 0)
    ka_ref[0] = normed(tq).astype(BF16)
    va = proj[:, (tq + 1) * LANES:(tq + 2) * LANES]
    _store_value_heads(va_ref, va)
    _store_value_heads_t(vat_ref, va)
    store_queries(qb_ref, tq + 2)
    kb_ref[0] = normed(2 * tq + 2).astype(BF16)
    _store_value_heads_t(vbt_ref, proj[:, (2 * tq + 3) * LANES:(2 * tq + 4) * LANES])


def _inproj_c_kernel(*refs, fused):
    x, (g_ref, s_ref, w_ref, hg_ref, cos_ref, sin_ref, seg_ref), (q_ref, k_ref, v_ref) = _inproj_rows(refs, fused)
    bm = x.shape[0]
    h = _modulated(x, g_ref, s_ref).astype(BF16)
    proj = jnp.dot(h, w_ref[...], preferred_element_type=F32)
    cos, sin, seg = cos_ref[...], sin_ref[...], seg_ref[...]
    low = _lane((bm, LANES)) < HEAD_DIM

    def normed(j):
        cols = slice(j * LANES, (j + 1) * LANES)
        return _head_norm_rope(proj[:, cols], hg_ref[:, cols], cos, sin, seg)

    for j in range(N_HEADS_C):
        y = normed(j)
        q_ref[j, 0, 0:bm, :] = jnp.where(low, y, 0.0).astype(BF16)
        q_ref[j, 0, bm:2 * bm, :] = jnp.where(low, 0.0, y).astype(BF16)
        k_ref[j] = normed(N_HEADS_C + j).astype(BF16)
        v = proj[:, (2 * N_HEADS_C + j) * LANES:(2 * N_HEADS_C + j + 1) * LANES]
        v_ref[j, 0] = v.T.astype(BF16)


def _rope_tables(seq, n_ctx):
    rows_n = seq // GRID_W
    rows = jnp.broadcast_to(jnp.arange(rows_n, dtype=F32)[:, None], (rows_n, GRID_W)).reshape(-1)
    cols = jnp.broadcast_to(jnp.arange(GRID_W, dtype=F32)[None, :], (rows_n, GRID_W)).reshape(-1)
    half = HEAD_DIM // 2
    inv = ROPE_THETA ** (-jnp.arange(0, half, 2, dtype=F32) / half)
    ang = jnp.concatenate([rows[:, None] * inv, cols[:, None] * inv], axis=-1)
    reps = LANES // half
    sign = jnp.tile(jnp.concatenate([-jnp.ones((half,), F32), jnp.ones((half,), F32)]), LANES // HEAD_DIM)
    cos = jnp.pad(jnp.tile(jnp.cos(ang), (1, reps)), ((n_ctx, 0), (0, 0)), constant_values=1.0)
    sin = jnp.pad(jnp.tile(jnp.sin(ang), (1, reps)) * sign, ((n_ctx, 0), (0, 0)))
    return cos, sin


def _segment_mean_matrix():
    idx = jnp.arange(LANES) // HEAD_DIM
    return jnp.where(idx[:, None] == idx[None, :], 1.0 / HEAD_DIM, 0.0).astype(BF16)


def _q_slot(i, nblk):
    return (i + nblk - 1) % nblk


def _inproj_common_specs(t, d, n):
    bm = ROW_BLOCK
    return [pl.BlockSpec((bm, d), lambda i: (i, 0)),
            pl.BlockSpec((2, 1, d), lambda i: (0, 0, 0)),
            pl.BlockSpec((2, 1, d), lambda i: (0, 0, 0)),
            pl.BlockSpec((d, n), lambda i: (0, 0)),
            pl.BlockSpec((1, n), lambda i: (0, 0)),
            pl.BlockSpec((bm, LANES), lambda i: (i, 0)),
            pl.BlockSpec((bm, LANES), lambda i: (i, 0)),
            pl.BlockSpec((LANES, LANES), lambda i: (0, 0))]


def _inproj_call(kernel_fn, name, x, pending, args, n, out_specs, out_shape):
    t, d = x.shape
    bm = ROW_BLOCK
    in_specs = _inproj_common_specs(t, d, n)
    args = [x] + list(args)
    fused = pending is not None
    if fused:
        y2, wts, gate2 = pending
        pieces, _, _, dp = y2.shape
        in_specs = [pl.BlockSpec((pieces, TOP_K, bm, dp), lambda i: (0, 0, i, 0)),
                    pl.BlockSpec((bm, TOP_K), lambda i: (i, 0)),
                    pl.BlockSpec((2, 1, d), lambda i: (0, 0, 0))] + in_specs
        args = [y2, wts, gate2] + args
        out_specs = [pl.BlockSpec((bm, d), lambda i: (i, 0))] + list(out_specs)
        out_shape = [jax.ShapeDtypeStruct((t, d), F32)] + list(out_shape)
    outs = pl.pallas_call(
        functools.partial(kernel_fn, fused=fused),
        grid=(t // bm,),
        in_specs=in_specs,
        out_specs=out_specs,
        out_shape=out_shape,
        compiler_params=_params("parallel"),
        name=name,
    )(*args)
    return (outs[0], outs[1:]) if fused else (x, outs)


def _inproj_ab(x, pending, gain2, shift2, w, qn_a, kn_a, qn_b, kn_b, cos, sin):
    t, d = x.shape
    n = w.shape[1]
    bm, nblk = ROW_BLOCK, t // ROW_BLOCK
    tile = lambda g, reps: jnp.tile(g.astype(F32), reps)
    ones_v = jnp.ones((KVA_W,), F32)
    hg = jnp.concatenate([tile(qn_a, N_HEADS_A) * Q_SCALE, tile(kn_a, N_KV_A), ones_v,
                          tile(qn_b, N_HEADS_B) * Q_SCALE, tile(kn_b, N_KV_B), ones_v]).reshape(1, n)
    q_shape = jax.ShapeDtypeStruct((N_KV_A, nblk, GROUP * bm, LANES), BF16)
    k_shape = jax.ShapeDtypeStruct((1, t, LANES), BF16)
    v_shape = jax.ShapeDtypeStruct((N_KV_A, t, LANES), BF16)
    vt_shape = jax.ShapeDtypeStruct((N_KV_A, nblk, GQA_V_ROWS, bm), BF16)
    q_spec = pl.BlockSpec((N_KV_A, 1, GROUP * bm, LANES), lambda i: (0, _q_slot(i, nblk), 0, 0))
    k_spec = pl.BlockSpec((1, bm, LANES), lambda i: (0, i, 0))
    v_spec = pl.BlockSpec((N_KV_A, bm, LANES), lambda i: (0, i, 0))
    vt_spec = pl.BlockSpec((N_KV_A, 1, GQA_V_ROWS, bm), lambda i: (0, i, 0, 0))
    return _inproj_call(_inproj_ab_kernel, "inproj_ab", x, pending,
                        [gain2, shift2, w, hg, cos, sin, _segment_mean_matrix()], n,
                        [q_spec, q_spec, k_spec, k_spec, v_spec, vt_spec, vt_spec],
                        [q_shape, q_shape, k_shape, k_shape, v_shape, vt_shape, vt_shape])


def _inproj_c(x, pending, gain2, shift2, w, qn, kn, cos, sin):
    t, d = x.shape
    n = w.shape[1]
    bm, nblk = ROW_BLOCK, t // ROW_BLOCK
    h = N_HEADS_C
    tile = lambda g: jnp.tile(g.astype(F32), 2 * h)
    hg = jnp.concatenate([tile(qn) * Q_SCALE, tile(kn), jnp.ones((h * DV_C,), F32)]).reshape(1, n)
    return _inproj_call(_inproj_c_kernel, "inproj_c", x, pending,
                        [gain2, shift2, w, hg, cos, sin, _segment_mean_matrix()], n,
                        [pl.BlockSpec((h, 1, 2 * bm, LANES), lambda i: (0, _q_slot(i, nblk), 0, 0)),
                         pl.BlockSpec((h, bm, LANES), lambda i: (0, i, 0)),
                         pl.BlockSpec((h, 1, DIFF_V_ROWS, bm), lambda i: (0, i, 0, 0))],
                        [jax.ShapeDtypeStruct((h, nblk, 2 * bm, LANES), BF16),
                         jax.ShapeDtypeStruct((h, t, LANES), BF16),
                         jax.ShapeDtypeStruct((h, nblk, DIFF_V_ROWS, bm), BF16)])


def _pad_rows(x):
    rows = x.shape[0]
    if rows >= LANES:
        return x[:LANES]
    return jnp.concatenate([x, jnp.zeros((LANES - rows, x.shape[1]), x.dtype)], axis=0)


def _merge_gqa_heads(o, bq):
    low = _lane((bq, LANES)) < HEAD_DIM
    pairs = [jnp.where(low, o[g * bq:(g + 1) * bq], pltpu.roll(o[(g + 1) * bq:(g + 2) * bq], HEAD_DIM, 1))
             for g in range(0, GROUP, 2)]
    return jnp.concatenate(pairs, axis=1)


def _flash_kernel(*refs, mode, online, n_keys, bk, dv, l0):
    refs = list(refs)
    q_ref, k_ref, vt_ref = refs[:3]
    pos = 3
    if online:
        m0_ref = refs[pos]
        pos += 1
    if mode == "diff":
        lam_ref, sub_ref = refs[pos:pos + 2]
        pos += 2
    o_ref = refs[pos]
    scratch = refs[pos + 1:]
    acc_sc = scratch[0]
    ones_row = acc_sc.shape[0] > dv
    m_sc = scratch[1] if online else None
    l_sc = None if ones_row else scratch[-1]

    nsub, rb = q_ref.shape[1], q_ref.shape[2]
    r = nsub * rb
    ch = vt_ref.shape[3]
    q = q_ref[0].reshape(r, LANES)
    acc_sc[...] = jnp.where(lax.broadcasted_iota(jnp.int32, acc_sc.shape, 0) == dv, l0, 0.0).astype(F32)
    if online:
        m_sc[...] = m0_ref[0]
    if not ones_row:
        l_sc[...] = jnp.full(l_sc.shape, l0, F32)

    def block(start, size):
        kb = k_ref[0, pl.ds(start, size), :]
        st = lax.dot_general(kb, q, (((1,), (1,)), ((), ())), preferred_element_type=F32)
        if online:
            m_prev = m_sc[...]
            m_new = jnp.maximum(m_prev, jnp.max(st, axis=0, keepdims=True))
            p = jnp.exp2(st - m_new)
            alpha = jnp.exp2(m_prev - m_new)
            acc = alpha * acc_sc[...]
            m_sc[...] = m_new
        else:
            p = jnp.exp2(st)
            alpha = 1.0
            acc = acc_sc[...]
        if not ones_row:
            l_sc[...] = alpha * l_sc[...] + jnp.sum(p, axis=0, keepdims=True)
        pt = p.astype(BF16)
        c0 = start // ch
        for c in range(size // ch):
            acc = acc + jnp.dot(vt_ref[0, c0 + c], pt[c * ch:(c + 1) * ch], preferred_element_type=F32)
        acc_sc[...] = acc

    n_full, tail = n_keys // bk, n_keys % bk
    if n_full:
        def body(i, carry):
            block(pl.multiple_of(i * bk, bk), bk)
            return carry
        lax.fori_loop(0, n_full, body, 0)
    if tail:
        block(n_full * bk, tail)

    acc = acc_sc[...]
    den = acc[dv:dv + 1] if ones_row else l_sc[...]
    ot = _pad_rows(acc / den)
    bq = rb // (GROUP if mode == "gqa" else 2)
    o = jnp.concatenate([ot[:, j * bq:(j + 1) * bq].T for j in range(r // bq)], axis=0)
    if mode == "gqa":
        for b in range(nsub):
            o_ref[b * bq:(b + 1) * bq, :] = _merge_gqa_heads(o[b * rb:(b + 1) * rb], bq).astype(o_ref.dtype)
    else:
        for b in range(nsub):
            d = o[b * rb:b * rb + bq] - lam_ref[...] * o[b * rb + bq:(b + 1) * rb]
            y = d * lax.rsqrt(jnp.mean(d * d, axis=-1, keepdims=True) + EPS) * sub_ref[...]
            o_ref[b * bq:(b + 1) * bq, :] = y.astype(o_ref.dtype)


def _flash_call(q, k, vt, extra, *, mode, online, n_keys, slot0, nsub, n_steps, l0=0.0):
    hkv, _, rb, _ = q.shape
    hk = k.shape[0]
    _, _, dvr, ch = vt.shape
    dv = HEAD_DIM if mode == "gqa" else DV_C
    bq = rb // (GROUP if mode == "gqa" else 2)
    ocols = GROUP * HEAD_DIM if mode == "gqa" else DV_C
    r = nsub * rb
    bk = min(KEY_BLOCK, n_keys)
    in_specs = [pl.BlockSpec((1, nsub, rb, LANES), lambda h, i: (h, slot0 // nsub + i, 0, 0)),
                pl.BlockSpec((1, n_keys, LANES), (lambda h, i: (h, 0, 0)) if hk > 1 else (lambda h, i: (0, 0, 0))),
                pl.BlockSpec((1, n_keys // ch, dvr, ch), lambda h, i: (h, 0, 0, 0))]
    args = [q, k, vt]
    scratch = [pltpu.VMEM((dvr, r), F32)]
    if online:
        m0 = extra.pop(0)
        in_specs.append(pl.BlockSpec((1, 1, r), lambda h, i: (h, 0, 0)))
        args.append(m0)
        scratch.append(pltpu.VMEM((1, r), F32))
    if dvr <= dv:
        scratch.append(pltpu.VMEM((1, r), F32))
    for a in extra:
        in_specs.append(pl.BlockSpec(a.shape, lambda h, i: (0, 0)))
        args.append(a)
    return pl.pallas_call(
        functools.partial(_flash_kernel, mode=mode, online=online, n_keys=n_keys, bk=bk, dv=dv, l0=l0),
        grid=(hkv, n_steps),
        in_specs=in_specs,
        out_specs=pl.BlockSpec((nsub * bq, ocols), lambda h, i: (i, h)),
        out_shape=jax.ShapeDtypeStruct((n_steps * nsub * bq, hkv * ocols), BF16),
        scratch_shapes=scratch,
        compiler_params=_params("parallel", "parallel"),
        name="flash_online" if online else "flash_bounded",
    )(*args)


def _logits_bounded(q_gain, k_gain):
    bound = HEAD_DIM * Q_SCALE * 1.02 * jnp.max(jnp.abs(q_gain.astype(F32))) * jnp.max(jnp.abs(k_gain.astype(F32)))
    return bound <= LOG2_LOGIT_BOUND


def _attend(q, k, v, extra, bounded, **kw):
    hkv, r = q.shape[0], kw["nsub"] * q.shape[2]
    fast = lambda q_, k_, v_, *e: _flash_call(q_, k_, v_, list(e), online=False, **kw)
    safe = lambda q_, k_, v_, *e: _flash_call(q_, k_, v_, [jnp.full((hkv, 1, r), NEG, F32)] + list(e), online=True, **kw)
    return lax.cond(bounded, fast, safe, q, k, v, *extra)


def _window_kernel(q_ref, k_ref, v_ref, sink_ref, o_ref, *, bq, n_ctx):
    q = q_ref[0, 0]
    r = q.shape[0]
    t = k_ref.shape[1]
    w = bq + 2 * WINDOW
    q0 = pl.program_id(1) * bq
    ws = pl.multiple_of(jnp.clip(n_ctx + q0 - WINDOW, 0, t - w), WINDOW)
    kw = k_ref[0, pl.ds(ws, w), :]
    vw = v_ref[0, pl.ds(ws, w), :]
    contract_last = (((1,), (1,)), ((), ()))
    s_loc = lax.dot_general(q, kw, contract_last, preferred_element_type=F32)
    qpos = q0 + (lax.broadcasted_iota(jnp.int32, (r, w), 0) & (bq - 1))
    kpos = ws - n_ctx + lax.broadcasted_iota(jnp.int32, (r, w), 1)
    mask = (kpos >= 0) & (kpos - qpos <= WINDOW) & (qpos - kpos <= WINDOW)
    s_loc = jnp.where(mask, s_loc, NEG)
    s_ctx = lax.dot_general(q, k_ref[0, 0:n_ctx, :], contract_last, preferred_element_type=F32)
    sink = sink_ref[0]
    m = jnp.maximum(sink, jnp.maximum(jnp.max(s_loc, axis=-1, keepdims=True), jnp.max(s_ctx, axis=-1, keepdims=True)))
    p_loc = jnp.exp2(s_loc - m)
    p_ctx = jnp.exp2(s_ctx - m)
    l = jnp.exp2(sink - m) + jnp.sum(p_loc, axis=-1, keepdims=True) + jnp.sum(p_ctx, axis=-1, keepdims=True)
    o = (jnp.dot(p_loc.astype(BF16), vw, preferred_element_type=F32)
         + jnp.dot(p_ctx.astype(BF16), v_ref[0, 0:n_ctx, :], preferred_element_type=F32))
    o_ref[...] = _merge_gqa_heads(o / l, bq).astype(o_ref.dtype)


def _window_attention(q, k, v, sink_rows, n_ctx):
    hkv, slots, r, _ = q.shape
    t = k.shape[1]
    bq = r // GROUP
    nq = slots - n_ctx // bq
    return pl.pallas_call(
        functools.partial(_window_kernel, bq=bq, n_ctx=n_ctx),
        grid=(hkv, nq),
        in_specs=[pl.BlockSpec((1, 1, r, LANES), lambda h, i: (h, i, 0, 0)),
                  pl.BlockSpec((1, t, LANES), lambda h, i: (0, 0, 0)),
                  pl.BlockSpec((1, t, LANES), lambda h, i: (h, 0, 0)),
                  pl.BlockSpec((1, r, 1), lambda h, i: (h, 0, 0))],
        out_specs=pl.BlockSpec((bq, GROUP * HEAD_DIM), lambda h, i: (i, h)),
        out_shape=jax.ShapeDtypeStruct((nq * bq, hkv * GROUP * HEAD_DIM), BF16),
        compiler_params=_params("parallel", "parallel"),
        name="window_attention",
    )(q, k, v, sink_rows)


def _window_bounded_kernel(q_ref, k_ref, vt_ref, mask_ref, sink_ref, o_ref, *, bq):
    q = q_ref[0, 0]
    ch = vt_ref.shape[3]
    cw = jnp.clip(pl.program_id(1), 0, vt_ref.shape[1] - 3)
    contract_last = (((1,), (1,)), ((), ()))
    k_win = k_ref[0, pl.ds(pl.multiple_of(cw * ch, ch), 3 * ch), :]
    st_win = lax.dot_general(k_win, q, contract_last, preferred_element_type=F32)
    st_ctx = lax.dot_general(k_ref[0, 0:ch, :], q, contract_last, preferred_element_type=F32)
    pt_win = jnp.exp2(st_win).astype(BF16) * mask_ref[0]
    acc = jnp.dot(vt_ref[0, 0], jnp.exp2(st_ctx).astype(BF16), preferred_element_type=F32)
    for c in range(3):
        acc = acc + jnp.dot(vt_ref[0, cw + c], pt_win[c * ch:(c + 1) * ch], preferred_element_type=F32)
    ot = _pad_rows(acc / (acc[HEAD_DIM:HEAD_DIM + 1] + jnp.exp2(sink_ref[0])))
    o = jnp.concatenate([ot[:, g * bq:(g + 1) * bq].T for g in range(GROUP)], axis=0)
    o_ref[...] = _merge_gqa_heads(o, bq).astype(o_ref.dtype)


def _window_masks(bq, r):
    c = jnp.arange(3 * bq)[:, None]
    off = c - (jnp.arange(r)[None, :] & (bq - 1))
    centred = (off >= bq - WINDOW) & (off <= bq + WINDOW)
    shifted = (off >= 2 * bq - WINDOW) & (off <= 2 * bq + WINDOW)
    return jnp.stack([centred & (c >= bq), centred, shifted]).astype(BF16)


def _window_attention_bounded(q, k, vt, sink_cols, n_ctx):
    hkv, slots, r, _ = q.shape
    t = k.shape[1]
    bq = r // GROUP
    nblk = t // bq
    nq = nblk - 1
    assert n_ctx == bq and nblk >= 3
    return pl.pallas_call(
        functools.partial(_window_bounded_kernel, bq=bq),
        grid=(hkv, nq),
        in_specs=[pl.BlockSpec((1, 1, r, LANES), lambda h, i: (h, i, 0, 0)),
                  pl.BlockSpec((1, t, LANES), lambda h, i: (0, 0, 0)),
                  pl.BlockSpec((1, nblk, GQA_V_ROWS, bq), lambda h, i: (h, 0, 0, 0)),
                  pl.BlockSpec((1, 3 * bq, r), lambda h, i: (jnp.where(i == 0, 0, jnp.where(i >= nblk - 2, 2, 1)), 0, 0)),
                  pl.BlockSpec((1, 1, r), lambda h, i: (h, 0, 0))],
        out_specs=pl.BlockSpec((bq, GROUP * HEAD_DIM), lambda h, i: (i, h)),
        out_shape=jax.ShapeDtypeStruct((nq * bq, hkv * GROUP * HEAD_DIM), BF16),
        compiler_params=_params("parallel", "parallel"),
        name="window_bounded",
    )(q, k, vt, _window_masks(bq, r), sink_cols)


def _sink_rows(sink, bq):
    hkv, g = sink.shape
    return jnp.broadcast_to(sink.astype(F32)[:, :, None] * LOG2E, (hkv, g, bq)).reshape(hkv, g * bq, 1)


def _mixer_ab(heads, n_ctx, sink_a, with_ctx, bounded_a, bounded):
    qa, qb, ka, kb, va, vat, vbt = heads
    t = ka.shape[1]
    bq = ROW_BLOCK
    n_lat = (t - n_ctx) // bq
    sink = _sink_rows(sink_a.reshape(N_KV_A, GROUP), bq)
    sink_cols = sink.reshape(N_KV_A, 1, GROUP * bq)
    bounded_a = bounded_a & (jnp.max(jnp.abs(sink)) <= LOG2_LOGIT_BOUND)
    oa = lax.cond(bounded_a,
                  lambda: _window_attention_bounded(qa, ka, vat, sink_cols, n_ctx),
                  lambda: _window_attention(qa, ka, va, sink, n_ctx))
    nsub = 2 if n_lat % 2 == 0 else 1
    ob = _attend(qb, kb, vbt, [], bounded, mode="gqa", n_keys=t, slot0=0, nsub=nsub, n_steps=n_lat // nsub)
    if not with_ctx:
        return (oa, ob), None
    oca = _flash_call(qa, ka, vat, [sink_cols], mode="gqa", online=True, n_keys=n_ctx,
                      slot0=n_lat, nsub=1, n_steps=1, l0=1.0)
    ocb = _attend(qb, kb, vbt, [], bounded, mode="gqa", n_keys=n_ctx, slot0=n_lat, nsub=1, n_steps=1)
    return (oa, ob), (oca, ocb)


def _mixer_c(heads, n_ctx, lam_p, subln, lam_init, with_ctx, bounded):
    q, k, v = heads
    t = k.shape[1]
    n_lat = (t - n_ctx) // ROW_BLOCK
    lp = lam_p.astype(F32)
    lam = jnp.exp(jnp.sum(lp[0] * lp[1])) - jnp.exp(jnp.sum(lp[2] * lp[3])) + lam_init
    extra = [jnp.full((1, DV_C), lam, F32), (subln.astype(F32) * (1 - lam_init)).reshape(1, DV_C)]
    nsub = 4 if n_lat % 4 == 0 else 1
    o_lat = _attend(q, k, v, extra, bounded, mode="diff", n_keys=t, slot0=0, nsub=nsub, n_steps=n_lat // nsub)
    if not with_ctx:
        return (o_lat,), None
    o_ctx = _attend(q, k, v, extra, bounded, mode="diff", n_keys=n_ctx, slot0=n_lat, nsub=1, n_steps=1)
    return (o_lat,), (o_ctx,)


def _out_proj_kernel(*refs, n_parts, has_ctx):
    lat = refs[:n_parts]
    ctx = refs[n_parts:2 * n_parts] if has_ctx else None
    w_ref, x_ref, gate_ref, g_ref, s_ref, wr_ref, y_ref, tok_ref, logit_ref = refs[-9:]
    is_ctx = pl.program_id(0) == 0 if has_ctx else False
    row = jnp.where(is_ctx, 0, 1) if has_ctx else 1
    acc = None
    col = 0
    for p in range(n_parts):
        o = lat[p][...]
        if has_ctx:
            o = jnp.where(is_ctx, ctx[p][...], o)
        width = o.shape[1]
        part = jnp.dot(o, w_ref[col:col + width, :], preferred_element_type=F32)
        acc = part if acc is None else acc + part
        col += width
    x1 = x_ref[...] + gate_ref[row] * acc
    y_ref[...] = x1
    h = x1 * lax.rsqrt(jnp.mean(x1 * x1, axis=-1, keepdims=True) + EPS) * g_ref[row] + s_ref[row]
    _store_pieces(tok_ref, h)
    hi, lo = _split_bf16(h)
    logits = (jnp.dot(hi, wr_ref[0], preferred_element_type=F32) + jnp.dot(lo, wr_ref[0], preferred_element_type=F32)
              + jnp.dot(hi, wr_ref[1], preferred_element_type=F32))
    logit_ref[...] = logits.T


def _out_proj(o_lat, o_ctx, w, x, gate2, gain2, shift2, w_router):
    d = x.shape[1]
    bm = ROW_BLOCK
    dp = d // SC_ROW_SPLIT
    has_ctx = o_ctx is not None
    row0 = 0 if has_ctx else (x.shape[0] - o_lat[0].shape[0]) // bm
    t = x.shape[0] - row0 * bm
    n_parts = len(o_lat)
    lat_map = (lambda i: (jnp.maximum(i - 1, 0), 0)) if has_ctx else (lambda i: (i, 0))
    in_specs = [pl.BlockSpec((bm, o.shape[1]), lat_map) for o in o_lat]
    args = list(o_lat)
    if has_ctx:
        in_specs += [pl.BlockSpec((bm, o.shape[1]), lambda i: (0, 0)) for o in o_ctx]
        args += list(o_ctx)
    vec_spec = pl.BlockSpec((2, 1, d), lambda i: (0, 0, 0))
    in_specs += [pl.BlockSpec(w.shape, lambda i: (0, 0)),
                 pl.BlockSpec((bm, d), lambda i: (i + row0, 0)),
                 vec_spec, vec_spec, vec_spec,
                 pl.BlockSpec((2, d, LANES), lambda i: (0, 0, 0))]
    return pl.pallas_call(
        functools.partial(_out_proj_kernel, n_parts=n_parts, has_ctx=has_ctx),
        grid=(t // bm,),
        in_specs=in_specs,
        out_specs=[pl.BlockSpec((bm, d), lambda i: (i, 0)),
                   pl.BlockSpec((SC_ROW_SPLIT, bm, dp), lambda i: (0, i, 0)),
                   pl.BlockSpec((LANES, bm), lambda i: (0, i))],
        out_shape=[jax.ShapeDtypeStruct((t, d), F32),
                   jax.ShapeDtypeStruct((SC_ROW_SPLIT, t, dp), F32),
                   jax.ShapeDtypeStruct((LANES, t), F32)],
        compiler_params=_params("parallel"),
        name="out_proj_router",
    )(*args, w, x, gate2, gain2, shift2, w_router)


def _store_pieces(ref, rows):
    dp = ref.shape[2]
    for j in range(ref.shape[0]):
        ref[j] = rows[:, j * dp:(j + 1) * dp]


def _load_pieces(planes):
    return jnp.concatenate([planes[j] for j in range(planes.shape[0])], axis=1)


def _moe_kernel(blk_ref, exp_ref, lo_ref, hi_ref, n_ref, x_ref, w1_ref, w3_ref, w2_ref, y_ref):
    i = pl.program_id(0)

    @pl.when(i < n_ref[0])
    def _():
        x = _load_pieces(x_ref[...]).astype(BF16)
        a = jnp.dot(x, w1_ref[0, 0].astype(BF16), preferred_element_type=F32)
        b = jnp.dot(x, w3_ref[0, 0].astype(BF16), preferred_element_type=F32)
        hidden = (a * jax.nn.sigmoid(a)) * b
        y = jnp.dot(hidden.astype(BF16), w2_ref[0, 0].astype(BF16), preferred_element_type=F32)
        rows = lax.broadcasted_iota(jnp.int32, (y.shape[0], 1), 0)
        y = jnp.where((rows >= lo_ref[i]) & (rows < hi_ref[i]), y, 0.0)
        first = (i == 0) | (blk_ref[i] != blk_ref[jnp.maximum(i - 1, 0)])

        @pl.when(first)
        def _():
            _store_pieces(y_ref, y)

        @pl.when(jnp.logical_not(first))
        def _():
            _store_pieces(y_ref, _load_pieces(y_ref[...]) + y)


def _moe_experts(buf, items, w1, w3, w2, layer):
    pieces, n_rows, dp = buf.shape
    d, de = w1.shape[2], w1.shape[3]
    n_items = items[0].shape[0]
    row_spec = pl.BlockSpec((pieces, EXPERT_BLOCK, dp), lambda i, blk, ex, lo, hi, n: (0, blk[i], 0))
    w_map = lambda i, blk, ex, lo, hi, n: (layer, ex[i], 0, 0)
    grid_spec = pltpu.PrefetchScalarGridSpec(
        num_scalar_prefetch=5,
        grid=(n_items,),
        in_specs=[row_spec,
                  pl.BlockSpec((1, 1, d, de), w_map),
                  pl.BlockSpec((1, 1, d, de), w_map),
                  pl.BlockSpec((1, 1, de, d), w_map)],
        out_specs=row_spec,
    )
    return pl.pallas_call(
        _moe_kernel,
        grid_spec=grid_spec,
        out_shape=jax.ShapeDtypeStruct((pieces, n_rows, dp), F32),
        compiler_params=_params("arbitrary"),
        name="moe_experts",
    )(*items, buf, w1, w3, w2)


def _expert_items(counts, n_rows):
    e = counts.shape[0]
    n_blk = n_rows // EXPERT_BLOCK
    n_items = n_blk + e - 1
    end = jnp.cumsum(counts)
    start = end - counts
    first_blk = start // EXPERT_BLOCK
    per_expert = jnp.where(counts > 0, (end - 1) // EXPERT_BLOCK - first_blk + 1, 0)
    cum = jnp.cumsum(per_expert)
    total = cum[-1]
    w = jnp.minimum(jnp.arange(n_items, dtype=jnp.int32), total - 1)
    onehot = ((cum - per_expert)[None, :] <= w[:, None]) & (w[:, None] < cum[None, :])
    pick = lambda tab: jnp.sum(jnp.where(onehot, tab[None, :], 0), axis=1).astype(jnp.int32)
    expert = pick(jnp.arange(e, dtype=jnp.int32))
    blk = pick(first_blk) + w - pick(cum - per_expert)
    lo = jnp.clip(pick(start) - blk * EXPERT_BLOCK, 0, EXPERT_BLOCK)
    hi = jnp.clip(pick(end) - blk * EXPERT_BLOCK, 0, EXPERT_BLOCK)
    return blk, expert, lo, hi, total.astype(jnp.int32).reshape(1)


def _sc_scatter(x, pos):
    pieces, t, dp = x.shape
    n = pos.shape[0]
    kk = n // t
    offs = jnp.arange(pieces, dtype=jnp.int32)[:, None] * n
    idx = [(offs + pos[k * t:(k + 1) * t][None, :]).reshape(1, pieces * t) for k in range(kk)]
    mesh = plsc.VectorSubcoreMesh(core_axis_name="core", subcore_axis_name="subcore")

    @pl.kernel(out_type=jax.ShapeDtypeStruct((pieces * n, dp), x.dtype), mesh=mesh, scratch_types=[])
    def scatter(x_hbm, *refs):
        i_hbm, o_hbm = refs[:kk], refs[kk]

        def body(x_vmem, *i_vmem):
            for iv in i_vmem:
                pltpu.sync_copy(x_vmem, o_hbm.at[iv.at[0]])

        pltpu.emit_pipeline(
            body,
            grid=(pieces * t // SC_GATHER_WINDOW,),
            in_specs=[pl.BlockSpec((SC_GATHER_WINDOW, dp), lambda i: (i, 0))]
                     + [pl.BlockSpec((1, SC_GATHER_WINDOW), lambda i: (0, i))] * kk,
            out_specs=[],
            core_axis_name=("core", "subcore"),
            dimension_semantics=(pltpu.PARALLEL,),
        )(x_hbm, *i_hbm)

    return scatter(x.reshape(pieces * t, dp), *idx).reshape(pieces, n, dp)


def _sc_gather(x, idx):
    pieces, t, dp = x.shape
    n = idx.shape[0]
    flat = (jnp.arange(pieces, dtype=jnp.int32)[:, None] * t + idx[None, :]).reshape(1, pieces * n)
    mesh = plsc.VectorSubcoreMesh(core_axis_name="core", subcore_axis_name="subcore")

    @pl.kernel(out_type=jax.ShapeDtypeStruct((pieces * n, dp), x.dtype), mesh=mesh, scratch_types=[])
    def gather(x_hbm, i_hbm, o_hbm):
        def body(i_vmem, o_vmem):
            pltpu.sync_copy(x_hbm.at[i_vmem.at[0]], o_vmem)

        pltpu.emit_pipeline(
            body,
            grid=(pieces * n // SC_GATHER_WINDOW,),
            in_specs=[pl.BlockSpec((1, SC_GATHER_WINDOW), lambda i: (0, i))],
            out_specs=[pl.BlockSpec((SC_GATHER_WINDOW, dp), lambda i: (i, 0))],
            core_axis_name=("core", "subcore"),
            dimension_semantics=(pltpu.PARALLEL,),
        )(i_hbm, o_hbm)

    return gather(x.reshape(pieces * t, dp), flat).reshape(pieces, n, dp)


def _combine_kernel(y_ref, w_ref, x_ref, gate_ref, o_ref, *, row0):
    w = w_ref[...]
    f = w[:, 0:1] * _load_pieces(y_ref[:, 0]) + w[:, 1:2] * _load_pieces(y_ref[:, 1])
    row = jnp.minimum(pl.program_id(0) + row0, 1)
    o_ref[...] = x_ref[...] + gate_ref[row] * f


def _combine(y2, weights, x, gate2, has_ctx):
    t, d = x.shape
    bm = ROW_BLOCK
    pieces, _, _, dp = y2.shape
    return pl.pallas_call(
        functools.partial(_combine_kernel, row0=0 if has_ctx else 1),
        grid=(t // bm,),
        in_specs=[pl.BlockSpec((pieces, TOP_K, bm, dp), lambda i: (0, 0, i, 0)),
                  pl.BlockSpec((bm, TOP_K), lambda i: (i, 0)),
                  pl.BlockSpec((bm, d), lambda i: (i, 0)),
                  pl.BlockSpec((2, 1, d), lambda i: (0, 0, 0))],
        out_specs=pl.BlockSpec((bm, d), lambda i: (i, 0)),
        out_shape=jax.ShapeDtypeStruct((t, d), F32),
        compiler_params=_params("parallel"),
        name="moe_combine",
    )(y2, weights, x, gate2)


def _top1_rows(p):
    m = jnp.max(p, axis=0, keepdims=True)
    idx = lax.broadcasted_iota(jnp.int32, p.shape, 0)
    return m, jnp.min(jnp.where(p == m, idx, p.shape[0]), axis=0, keepdims=True)


def _prefix_rank(onehot):
    e, n = onehot.shape
    blk = EXPERT_BLOCK
    nb = n // blk
    earlier = (jnp.arange(blk)[:, None] < jnp.arange(blk)[None, :]).astype(F32)
    within = jnp.dot(onehot.reshape(e * nb, blk), earlier).reshape(e, n)
    tot = jnp.sum(onehot.reshape(e, nb, blk), axis=2)
    before = jnp.repeat(jnp.cumsum(tot, axis=1) - tot, blk, axis=1)
    return jnp.sum((within + before) * onehot, axis=0).astype(jnp.int32)


def _hier_moe(tokens, logits_t, b_group, b_expert, w1, w3, w2, layer):
    pieces, t, dp = tokens.shape
    pg = jax.nn.softmax(logits_t[:N_GROUPS] + b_group.astype(F32)[:, None], axis=0)
    g_prob, g_idx = _top1_rows(pg)
    le = (logits_t[N_GROUPS:N_GROUPS + N_EXPERTS] + b_expert.astype(F32)[:, None]).reshape(N_GROUPS, EXPERTS_PER_GROUP, t)
    group_iota = lax.broadcasted_iota(jnp.int32, (N_GROUPS, 1, t), 0)
    le = jnp.sum(jnp.where(group_iota == g_idx[None], le, 0.0), axis=0)
    pe = jax.nn.softmax(le, axis=0)
    p1, i1 = _top1_rows(pe)
    p2, i2 = _top1_rows(jnp.where(lax.broadcasted_iota(jnp.int32, pe.shape, 0) == i1, -1.0, pe))
    e_prob, e_idx = jnp.concatenate([p1, p2], axis=0), jnp.concatenate([i1, i2], axis=0)
    weights = g_prob * e_prob / jnp.sum(e_prob, axis=0, keepdims=True)
    flat_e = (g_idx * EXPERTS_PER_GROUP + e_idx).reshape(1, TOP_K * t)
    onehot = (flat_e == lax.broadcasted_iota(jnp.int32, (N_EXPERTS, TOP_K * t), 0)).astype(F32)
    rank = _prefix_rank(onehot)
    counts = jnp.sum(onehot, axis=1).astype(jnp.int32)
    first_row = jnp.sum(onehot * (jnp.cumsum(counts) - counts).astype(F32)[:, None], axis=0).astype(jnp.int32)
    dest = first_row + rank
    buf = _sc_scatter(tokens, dest)
    yb = _moe_experts(buf, _expert_items(counts, TOP_K * t), w1, w3, w2, layer)
    return _sc_gather(yb, dest).reshape(pieces, TOP_K, t, dp), weights.T


def kernel(x, c, ctx, c_ctx, w_mod, b_mod, norm_mix, norm_ffn, w_in_ab, w_out_ab, qn_a, kn_a, sink_a, qn_b, kn_b, w_in_c, w_out_c, qn_c, kn_c, lam_c, subln_c, w_group, b_group, w_expert, b_expert, w1, w3, w2):
    b, s_lat, d = x.shape
    n_ctx = ctx.shape[1]
    assert b == 1 and n_ctx == ROW_BLOCK and s_lat % (2 * ROW_BLOCK) == 0
    depth = w_mod.shape[0]
    cos, sin = _rope_tables(s_lat, n_ctx)
    mods = _mod_vectors(c, c_ctx, w_mod, b_mod)
    xs = jnp.concatenate([ctx[0], x[0]], axis=0)
    has_ctx = True
    pending = None
    for l in range(depth):
        last = l == depth - 1
        i = l // 2
        sh1, sc1, gt1, sh2, sc2, gt2 = [mods[l, :2, j * d:(j + 1) * d].reshape(2, 1, d) for j in range(6)]
        gain1 = norm_mix[l].astype(F32) * (1 + sc1)
        if l % 2 == 0:
            xs, heads = _inproj_ab(xs, pending, gain1, sh1, w_in_ab[i].astype(BF16), qn_a[i], kn_a[i], qn_b[i], kn_b[i], cos, sin)
            o_lat, o_ctx = _mixer_ab(heads, n_ctx, sink_a[i], not last,
                                     _logits_bounded(qn_a[i], kn_a[i]), _logits_bounded(qn_b[i], kn_b[i]))
            w_out = w_out_ab[i]
        else:
            lam_init = 0.8 - 0.6 * math.exp(-0.3 * l)
            xs, heads = _inproj_c(xs, pending, gain1, sh1, w_in_c[i].astype(BF16), qn_c[i], kn_c[i], cos, sin)
            o_lat, o_ctx = _mixer_c(heads, n_ctx, lam_c[i], subln_c[i], lam_init, not last,
                                    _logits_bounded(qn_c[i], kn_c[i]))
            w_out = w_out_c[i]
        if last:
            has_ctx = False
        gain2 = norm_ffn[l].astype(F32) * (1 + sc2)
        w_router = jnp.zeros((d, LANES), F32).at[:, :N_GROUPS].set(w_group[l]).at[:, N_GROUPS:N_GROUPS + N_EXPERTS].set(w_expert[l])
        w_router = jnp.stack(_split_bf16(w_router))
        xs, tokens, logits = _out_proj(o_lat, o_ctx, w_out.astype(BF16), xs, gt1, gain2, sh2, w_router)
        y2, weights = _hier_moe(tokens, logits, b_group[l], b_expert[l], w1, w3, w2, l)
        pending = (y2, weights, gt2)
    return _combine(y2, weights, xs, gt2, has_ctx).reshape(b, s_lat, d)
```

```python
import functools
import math

import jax
import jax.numpy as jnp
from jax import lax
from jax.experimental import pallas as pl
from jax.experimental.pallas import tpu as pltpu
from jax.experimental.pallas import tpu_sc as plsc

F32 = jnp.float32
BF16 = jnp.bfloat16

GRID_W = 64
HEAD_DIM = 64
WINDOW = 128
ROPE_THETA = 10000.0
EPS = 1e-6
NEG = -1e30
N_HEADS_A, N_KV_A = 8, 2
N_HEADS_B, N_KV_B = 8, 2
GROUP = N_HEADS_A // N_KV_A
QA_W, KVA_W = N_HEADS_A * HEAD_DIM, N_KV_A * HEAD_DIM
QB_W, KVB_W = N_HEADS_B * HEAD_DIM, N_KV_B * HEAD_DIM
N_HEADS_C = 8
DV_C = 2 * HEAD_DIM
N_GROUPS, EXPERTS_PER_GROUP, TOP_K = 4, 8, 2
N_EXPERTS = N_GROUPS * EXPERTS_PER_GROUP
LANES = 128
ROW_BLOCK = 256
EXPERT_BLOCK = 512
SC_GATHER_WINDOW = 128
SC_ROW_SPLIT = 4
KEY_BLOCK = 3328
GQA_V_ROWS = 128
DIFF_V_ROWS = 128
VMEM_LIMIT = 48 * 1024 * 1024
LOG2E = math.log2(math.e)
Q_SCALE = HEAD_DIM ** -0.5 * LOG2E
LOG2_LOGIT_BOUND = 60.0


def _params(*sem):
    return pltpu.CompilerParams(dimension_semantics=sem, vmem_limit_bytes=VMEM_LIMIT)


def _lane(shape):
    return lax.broadcasted_iota(jnp.int32, shape, 1)


def _mod_vec_kernel(a_ref, w_ref, b_ref, o_ref):
    a = a_ref[...]
    a = a * jax.nn.sigmoid(a)
    o_ref[0] = jnp.dot(a, w_ref[0], preferred_element_type=F32, precision=lax.Precision.HIGHEST) + b_ref[0]


def _mod_vectors(c, c_ctx, w_mod, b_mod):
    depth, d, n = w_mod.shape
    a = jnp.zeros((8, d), F32).at[0].set(c_ctx).at[1].set(c[0])
    bn = 1024
    return pl.pallas_call(
        _mod_vec_kernel,
        grid=(depth, n // bn),
        in_specs=[pl.BlockSpec((8, d), lambda l, j: (0, 0)),
                  pl.BlockSpec((1, d, bn), lambda l, j: (l, 0, j)),
                  pl.BlockSpec((1, 1, bn), lambda l, j: (l, 0, j))],
        out_specs=pl.BlockSpec((1, 8, bn), lambda l, j: (l, 0, j)),
        out_shape=jax.ShapeDtypeStruct((depth, 8, n), F32),
        compiler_params=_params("parallel", "parallel"),
        name="mod_vectors",
    )(a, w_mod, b_mod.reshape(depth, 1, n))


def _modulated(x, g_ref, s_ref):
    y = x * lax.rsqrt(jnp.mean(x * x, axis=-1, keepdims=True) + EPS)
    row = jnp.minimum(pl.program_id(0), 1)
    return y * g_ref[row] + s_ref[row]


def _split_bf16(x):
    top = lax.bitcast_convert_type(lax.bitcast_convert_type(x, jnp.uint32) & jnp.uint32(0xFFFF0000), F32)
    return top.astype(BF16), (x - top).astype(BF16)


def _head_norm_rope(x, gain, cos, sin, seg_mean):
    hi, lo = _split_bf16(x * x)
    ms = jnp.dot(hi, seg_mean, preferred_element_type=F32) + jnp.dot(lo, seg_mean, preferred_element_type=F32)
    y = x * lax.rsqrt(ms + EPS) * gain
    first_half = (_lane(y.shape) & (HEAD_DIM - 1)) < HEAD_DIM // 2
    partner = jnp.where(first_half, pltpu.roll(y, LANES - HEAD_DIM // 2, 1), pltpu.roll(y, HEAD_DIM // 2, 1))
    return y * cos + partner * sin


def _store_value_heads(v_ref, v):
    lane = _lane(v.shape)
    ones_col = jnp.where(lane == HEAD_DIM, 1.0, 0.0)
    v_ref[0] = jnp.where(lane < HEAD_DIM, v, ones_col).astype(BF16)
    v_ref[1] = jnp.where(lane < HEAD_DIM, pltpu.roll(v, HEAD_DIM, 1), ones_col).astype(BF16)


def _ones_row_block(rows, cols):
    return jnp.where(lax.broadcasted_iota(jnp.int32, (rows, cols), 0) == 0, 1.0, 0.0)


def _store_value_heads_t(vt_ref, v):
    vt = v.T
    tail = _ones_row_block(GQA_V_ROWS - HEAD_DIM, v.shape[0])
    vt_ref[0, 0] = jnp.concatenate([vt[:HEAD_DIM], tail], axis=0).astype(BF16)
    vt_ref[1, 0] = jnp.concatenate([vt[HEAD_DIM:], tail], axis=0).astype(BF16)


N_INPROJ_INPUTS = 8


def _inproj_rows(refs, fused):
    if not fused:
        return refs[0][...], refs[1:N_INPROJ_INPUTS], refs[N_INPROJ_INPUTS:]
    y_ref, wts_ref, gate_ref, x_ref = refs[:4]
    ins, outs = refs[4:N_INPROJ_INPUTS + 3], refs[N_INPROJ_INPUTS + 3:]
    wts = wts_ref[...]
    f = wts[:, 0:1] * _load_pieces(y_ref[:, 0]) + wts[:, 1:2] * _load_pieces(y_ref[:, 1])
    x = x_ref[...] + gate_ref[jnp.minimum(pl.program_id(0), 1)] * f
    outs[0][...] = x
    return x, ins, outs[1:]


def _inproj_ab_kernel(*refs, fused):
    x, (g_ref, s_ref, w_ref, hg_ref, cos_ref, sin_ref, seg_ref), outs = _inproj_rows(refs, fused)
    qa_ref, qb_ref, ka_ref, kb_ref, va_ref, vat_ref, vbt_ref = outs
    bm = x.shape[0]
    h = _modulated(x, g_ref, s_ref).astype(BF16)
    proj = jnp.dot(h, w_ref[...], preferred_element_type=F32)
    cos, sin, seg = cos_ref[...], sin_ref[...], seg_ref[...]
    low = _lane((bm, LANES)) < HEAD_DIM

    def normed(j):
        cols = slice(j * LANES, (j + 1) * LANES)
        return _head_norm_rope(proj[:, cols], hg_ref[:, cols], cos, sin, seg)

    def store_queries(q_ref, tile0):
        for j in range(N_HEADS_A // 2):
            y = normed(tile0 + j)
            swapped = pltpu.roll(y, HEAD_DIM, 1)
            kv, g0 = (2 * j) // GROUP, (2 * j) % GROUP
            if kv == 0:
                even, odd = jnp.where(low, y, 0.0), jnp.where(low, swapped, 0.0)
            else:
                even, odd = jnp.where(low, 0.0, swapped), jnp.where(low, 0.0, y)
            q_ref[kv, 0, g0 * bm:(g0 + 1) * bm, :] = even.astype(BF16)
            q_ref[kv, 0, (g0 + 1) * bm:(g0 + 2) * bm, :] = odd.astype(BF16)

    tq = QA_W // LANES
    store_queries(qa_ref, 0)
    ka_ref[0] = normed(tq).astype(BF16)
    va = proj[:, (tq + 1) * LANES:(tq + 2) * LANES]
    _store_value_heads(va_ref, va)
    _store_value_heads_t(vat_ref, va)
    store_queries(qb_ref, tq + 2)
    kb_ref[0] = normed(2 * tq + 2).astype(BF16)
    _store_value_heads_t(vbt_ref, proj[:, (2 * tq + 3) * LANES:(2 * tq + 4) * LANES])


def _inproj_c_kernel(*refs, fused):
    x, (g_ref, s_ref, w_ref, hg_ref, cos_ref, sin_ref, seg_ref), (q_ref, k_ref, v_ref) = _inproj_rows(refs, fused)
    bm = x.shape[0]
    h = _modulated(x, g_ref, s_ref).astype(BF16)
    proj = jnp.dot(h, w_ref[...], preferred_element_type=F32)
    cos, sin, seg = cos_ref[...], sin_ref[...], seg_ref[...]
    low = _lane((bm, LANES)) < HEAD_DIM

    def normed(j):
        cols = slice(j * LANES, (j + 1) * LANES)
        return _head_norm_rope(proj[:, cols], hg_ref[:, cols], cos, sin, seg)

    for j in range(N_HEADS_C):
        y = normed(j)
        q_ref[j, 0, 0:bm, :] = jnp.where(low, y, 0.0).astype(BF16)
        q_ref[j, 0, bm:2 * bm, :] = jnp.where(low, 0.0, y).astype(BF16)
        k_ref[j] = normed(N_HEADS_C + j).astype(BF16)
        v = proj[:, (2 * N_HEADS_C + j) * LANES:(2 * N_HEADS_C + j + 1) * LANES]
        v_ref[j, 0] = v.T.astype(BF16)


def _rope_tables(seq, n_ctx):
    rows_n = seq // GRID_W
    rows = jnp.broadcast_to(jnp.arange(rows_n, dtype=F32)[:, None], (rows_n, GRID_W)).reshape(-1)
    cols = jnp.broadcast_to(jnp.arange(GRID_W, dtype=F32)[None, :], (rows_n, GRID_W)).reshape(-1)
    half = HEAD_DIM // 2
    inv = ROPE_THETA ** (-jnp.arange(0, half, 2, dtype=F32) / half)
    ang = jnp.concatenate([rows[:, None] * inv, cols[:, None] * inv], axis=-1)
    reps = LANES // half
    sign = jnp.tile(jnp.concatenate([-jnp.ones((half,), F32), jnp.ones((half,), F32)]), LANES // HEAD_DIM)
    cos = jnp.pad(jnp.tile(jnp.cos(ang), (1, reps)), ((n_ctx, 0), (0, 0)), constant_values=1.0)
    sin = jnp.pad(jnp.tile(jnp.sin(ang), (1, reps)) * sign, ((n_ctx, 0), (0, 0)))
    return cos, sin


def _segment_mean_matrix():
    idx = jnp.arange(LANES) // HEAD_DIM
    return jnp.where(idx[:, None] == idx[None, :], 1.0 / HEAD_DIM, 0.0).astype(BF16)


def _q_slot(i, nblk):
    return (i + nblk - 1) % nblk


def _inproj_common_specs(t, d, n):
    bm = ROW_BLOCK
    return [pl.BlockSpec((bm, d), lambda i: (i, 0)),
            pl.BlockSpec((2, 1, d), lambda i: (0, 0, 0)),
            pl.BlockSpec((2, 1, d), lambda i: (0, 0, 0)),
            pl.BlockSpec((d, n), lambda i: (0, 0)),
            pl.BlockSpec((1, n), lambda i: (0, 0)),
            pl.BlockSpec((bm, LANES), lambda i: (i, 0)),
            pl.BlockSpec((bm, LANES), lambda i: (i, 0)),
            pl.BlockSpec((LANES, LANES), lambda i: (0, 0))]


def _inproj_call(kernel_fn, name, x, pending, args, n, out_specs, out_shape):
    t, d = x.shape
    bm = ROW_BLOCK
    in_specs = _inproj_common_specs(t, d, n)
    args = [x] + list(args)
    fused = pending is not None
    if fused:
        y2, wts, gate2 = pending
        pieces, _, _, dp = y2.shape
        in_specs = [pl.BlockSpec((pieces, TOP_K, bm, dp), lambda i: (0, 0, i, 0)),
                    pl.BlockSpec((bm, TOP_K), lambda i: (i, 0)),
                    pl.BlockSpec((2, 1, d), lambda i: (0, 0, 0))] + in_specs
        args = [y2, wts, gate2] + args
        out_specs = [pl.BlockSpec((bm, d), lambda i: (i, 0))] + list(out_specs)
        out_shape = [jax.ShapeDtypeStruct((t, d), F32)] + list(out_shape)
    outs = pl.pallas_call(
        functools.partial(kernel_fn, fused=fused),
        grid=(t // bm,),
        in_specs=in_specs,
        out_specs=out_specs,
        out_shape=out_shape,
        compiler_params=_params("parallel"),
        name=name,
    )(*args)
    return (outs[0], outs[1:]) if fused else (x, outs)


def _inproj_ab(x, pending, gain2, shift2, w, qn_a, kn_a, qn_b, kn_b, cos, sin):
    t, d = x.shape
    n = w.shape[1]
    bm, nblk = ROW_BLOCK, t // ROW_BLOCK
    tile = lambda g, reps: jnp.tile(g.astype(F32), reps)
    ones_v = jnp.ones((KVA_W,), F32)
    hg = jnp.concatenate([tile(qn_a, N_HEADS_A) * Q_SCALE, tile(kn_a, N_KV_A), ones_v,
                          tile(qn_b, N_HEADS_B) * Q_SCALE, tile(kn_b, N_KV_B), ones_v]).reshape(1, n)
    q_shape = jax.ShapeDtypeStruct((N_KV_A, nblk, GROUP * bm, LANES), BF16)
    k_shape = jax.ShapeDtypeStruct((1, t, LANES), BF16)
    v_shape = jax.ShapeDtypeStruct((N_KV_A, t, LANES), BF16)
    vt_shape = jax.ShapeDtypeStruct((N_KV_A, nblk, GQA_V_ROWS, bm), BF16)
    q_spec = pl.BlockSpec((N_KV_A, 1, GROUP * bm, LANES), lambda i: (0, _q_slot(i, nblk), 0, 0))
    k_spec = pl.BlockSpec((1, bm, LANES), lambda i: (0, i, 0))
    v_spec = pl.BlockSpec((N_KV_A, bm, LANES), lambda i: (0, i, 0))
    vt_spec = pl.BlockSpec((N_KV_A, 1, GQA_V_ROWS, bm), lambda i: (0, i, 0, 0))
    return _inproj_call(_inproj_ab_kernel, "inproj_ab", x, pending,
                        [gain2, shift2, w, hg, cos, sin, _segment_mean_matrix()], n,
                        [q_spec, q_spec, k_spec, k_spec, v_spec, vt_spec, vt_spec],
                        [q_shape, q_shape, k_shape, k_shape, v_shape, vt_shape, vt_shape])


def _inproj_c(x, pending, gain2, shift2, w, qn, kn, cos, sin):
    t, d = x.shape
    n = w.shape[1]
    bm, nblk = ROW_BLOCK, t // ROW_BLOCK
    h = N_HEADS_C
    tile = lambda g: jnp.tile(g.astype(F32), 2 * h)
    hg = jnp.concatenate([tile(qn) * Q_SCALE, tile(kn), jnp.ones((h * DV_C,), F32)]).reshape(1, n)
    return _inproj_call(_inproj_c_kernel, "inproj_c", x, pending,
                        [gain2, shift2, w, hg, cos, sin, _segment_mean_matrix()], n,
                        [pl.BlockSpec((h, 1, 2 * bm, LANES), lambda i: (0, _q_slot(i, nblk), 0, 0)),
                         pl.BlockSpec((h, bm, LANES), lambda i: (0, i, 0)),
                         pl.BlockSpec((h, 1, DIFF_V_ROWS, bm), lambda i: (0, i, 0, 0))],
                        [jax.ShapeDtypeStruct((h, nblk, 2 * bm, LANES), BF16),
                         jax.ShapeDtypeStruct((h, t, LANES), BF16),
                         jax.ShapeDtypeStruct((h, nblk, DIFF_V_ROWS, bm), BF16)])


def _pad_rows(x):
    rows = x.shape[0]
    if rows >= LANES:
        return x[:LANES]
    return jnp.concatenate([x, jnp.zeros((LANES - rows, x.shape[1]), x.dtype)], axis=0)


def _merge_gqa_heads(o, bq):
    low = _lane((bq, LANES)) < HEAD_DIM
    pairs = [jnp.where(low, o[g * bq:(g + 1) * bq], pltpu.roll(o[(g + 1) * bq:(g + 2) * bq], HEAD_DIM, 1))
             for g in range(0, GROUP, 2)]
    return jnp.concatenate(pairs, axis=1)


def _flash_kernel(*refs, mode, online, n_keys, bk, dv, l0):
    refs = list(refs)
    q_ref, k_ref, vt_ref = refs[:3]
    pos = 3
    if online:
        m0_ref = refs[pos]
        pos += 1
    if mode == "diff":
        lam_ref, sub_ref = refs[pos:pos + 2]
        pos += 2
    o_ref = refs[pos]
    scratch = refs[pos + 1:]
    acc_sc = scratch[0]
    ones_row = acc_sc.shape[0] > dv
    m_sc = scratch[1] if online else None
    l_sc = None if ones_row else scratch[-1]

    nsub, rb = q_ref.shape[1], q_ref.shape[2]
    r = nsub * rb
    ch = vt_ref.shape[3]
    q = q_ref[0].reshape(r, LANES)
    acc_sc[...] = jnp.where(lax.broadcasted_iota(jnp.int32, acc_sc.shape, 0) == dv, l0, 0.0).astype(F32)
    if online:
        m_sc[...] = m0_ref[0]
    if not ones_row:
        l_sc[...] = jnp.full(l_sc.shape, l0, F32)

    def block(start, size):
        kb = k_ref[0, pl.ds(start, size), :]
        st = lax.dot_general(kb, q, (((1,), (1,)), ((), ())), preferred_element_type=F32)
        if online:
            m_prev = m_sc[...]
            m_new = jnp.maximum(m_prev, jnp.max(st, axis=0, keepdims=True))
            p = jnp.exp2(st - m_new)
            alpha = jnp.exp2(m_prev - m_new)
            acc = alpha * acc_sc[...]
            m_sc[...] = m_new
        else:
            p = jnp.exp2(st)
            alpha = 1.0
            acc = acc_sc[...]
        if not ones_row:
            l_sc[...] = alpha * l_sc[...] + jnp.sum(p, axis=0, keepdims=True)
        pt = p.astype(BF16)
        c0 = start // ch
        for c in range(size // ch):
            acc = acc + jnp.dot(vt_ref[0, c0 + c], pt[c * ch:(c + 1) * ch], preferred_element_type=F32)
        acc_sc[...] = acc

    n_full, tail = n_keys // bk, n_keys % bk
    if n_full:
        def body(i, carry):
            block(pl.multiple_of(i * bk, bk), bk)
            return carry
        lax.fori_loop(0, n_full, body, 0)
    if tail:
        block(n_full * bk, tail)

    acc = acc_sc[...]
    den = acc[dv:dv + 1] if ones_row else l_sc[...]
    ot = _pad_rows(acc / den)
    bq = rb // (GROUP if mode == "gqa" else 2)
    o = jnp.concatenate([ot[:, j * bq:(j + 1) * bq].T for j in range(r // bq)], axis=0)
    if mode == "gqa":
        for b in range(nsub):
            o_ref[b * bq:(b + 1) * bq, :] = _merge_gqa_heads(o[b * rb:(b + 1) * rb], bq).astype(o_ref.dtype)
    else:
        for b in range(nsub):
            d = o[b * rb:b * rb + bq] - lam_ref[...] * o[b * rb + bq:(b + 1) * rb]
            y = d * lax.rsqrt(jnp.mean(d * d, axis=-1, keepdims=True) + EPS) * sub_ref[...]
            o_ref[b * bq:(b + 1) * bq, :] = y.astype(o_ref.dtype)


def _flash_call(q, k, vt, extra, *, mode, online, n_keys, slot0, nsub, n_steps, l0=0.0):
    hkv, _, rb, _ = q.shape
    hk = k.shape[0]
    _, _, dvr, ch = vt.shape
    dv = HEAD_DIM if mode == "gqa" else DV_C
    bq = rb // (GROUP if mode == "gqa" else 2)
    ocols = GROUP * HEAD_DIM if mode == "gqa" else DV_C
    r = nsub * rb
    bk = min(KEY_BLOCK, n_keys)
    in_specs = [pl.BlockSpec((1, nsub, rb, LANES), lambda h, i: (h, slot0 // nsub + i, 0, 0)),
                pl.BlockSpec((1, n_keys, LANES), (lambda h, i: (h, 0, 0)) if hk > 1 else (lambda h, i: (0, 0, 0))),
                pl.BlockSpec((1, n_keys // ch, dvr, ch), lambda h, i: (h, 0, 0, 0))]
    args = [q, k, vt]
    scratch = [pltpu.VMEM((dvr, r), F32)]
    if online:
        m0 = extra.pop(0)
        in_specs.append(pl.BlockSpec((1, 1, r), lambda h, i: (h, 0, 0)))
        args.append(m0)
        scratch.append(pltpu.VMEM((1, r), F32))
    if dvr <= dv:
        scratch.append(pltpu.VMEM((1, r), F32))
    for a in extra:
        in_specs.append(pl.BlockSpec(a.shape, lambda h, i: (0, 0)))
        args.append(a)
    return pl.pallas_call(
        functools.partial(_flash_kernel, mode=mode, online=online, n_keys=n_keys, bk=bk, dv=dv, l0=l0),
        grid=(hkv, n_steps),
        in_specs=in_specs,
        out_specs=pl.BlockSpec((nsub * bq, ocols), lambda h, i: (i, h)),
        out_shape=jax.ShapeDtypeStruct((n_steps * nsub * bq, hkv * ocols), BF16),
        scratch_shapes=scratch,
        compiler_params=_params("parallel", "parallel"),
        name="flash_online" if online else "flash_bounded",
    )(*args)


def _logits_bounded(q_gain, k_gain):
    bound = HEAD_DIM * Q_SCALE * 1.02 * jnp.max(jnp.abs(q_gain.astype(F32))) * jnp.max(jnp.abs(k_gain.astype(F32)))
    return bound <= LOG2_LOGIT_BOUND


def _attend(q, k, v, extra, bounded, **kw):
    hkv, r = q.shape[0], kw["nsub"] * q.shape[2]
    fast = lambda q_, k_, v_, *e: _flash_call(q_, k_, v_, list(e), online=False, **kw)
    safe = lambda q_, k_, v_, *e: _flash_call(q_, k_, v_, [jnp.full((hkv, 1, r), NEG, F32)] + list(e), online=True, **kw)
    return lax.cond(bounded, fast, safe, q, k, v, *extra)


def _window_kernel(q_ref, k_ref, v_ref, sink_ref, o_ref, *, bq, n_ctx):
    q = q_ref[0, 0]
    r = q.shape[0]
    t = k_ref.shape[1]
    w = bq + 2 * WINDOW
    q0 = pl.program_id(1) * bq
    ws = pl.multiple_of(jnp.clip(n_ctx + q0 - WINDOW, 0, t - w), WINDOW)
    kw = k_ref[0, pl.ds(ws, w), :]
    vw = v_ref[0, pl.ds(ws, w), :]
    contract_last = (((1,), (1,)), ((), ()))
    s_loc = lax.dot_general(q, kw, contract_last, preferred_element_type=F32)
    qpos = q0 + (lax.broadcasted_iota(jnp.int32, (r, w), 0) & (bq - 1))
    kpos = ws - n_ctx + lax.broadcasted_iota(jnp.int32, (r, w), 1)
    mask = (kpos >= 0) & (kpos - qpos <= WINDOW) & (qpos - kpos <= WINDOW)
    s_loc = jnp.where(mask, s_loc, NEG)
    s_ctx = lax.dot_general(q, k_ref[0, 0:n_ctx, :], contract_last, preferred_element_type=F32)
    sink = sink_ref[0]
    m = jnp.maximum(sink, jnp.maximum(jnp.max(s_loc, axis=-1, keepdims=True), jnp.max(s_ctx, axis=-1, keepdims=True)))
    p_loc = jnp.exp2(s_loc - m)
    p_ctx = jnp.exp2(s_ctx - m)
    l = jnp.exp2(sink - m) + jnp.sum(p_loc, axis=-1, keepdims=True) + jnp.sum(p_ctx, axis=-1, keepdims=True)
    o = (jnp.dot(p_loc.astype(BF16), vw, preferred_element_type=F32)
         + jnp.dot(p_ctx.astype(BF16), v_ref[0, 0:n_ctx, :], preferred_element_type=F32))
    o_ref[...] = _merge_gqa_heads(o / l, bq).astype(o_ref.dtype)


def _window_attention(q, k, v, sink_rows, n_ctx):
    hkv, slots, r, _ = q.shape
    t = k.shape[1]
    bq = r // GROUP
    nq = slots - n_ctx // bq
    return pl.pallas_call(
        functools.partial(_window_kernel, bq=bq, n_ctx=n_ctx),
        grid=(hkv, nq),
        in_specs=[pl.BlockSpec((1, 1, r, LANES), lambda h, i: (h, i, 0, 0)),
                  pl.BlockSpec((1, t, LANES), lambda h, i: (0, 0, 0)),
                  pl.BlockSpec((1, t, LANES), lambda h, i: (h, 0, 0)),
                  pl.BlockSpec((1, r, 1), lambda h, i: (h, 0, 0))],
        out_specs=pl.BlockSpec((bq, GROUP * HEAD_DIM), lambda h, i: (i, h)),
        out_shape=jax.ShapeDtypeStruct((nq * bq, hkv * GROUP * HEAD_DIM), BF16),
        compiler_params=_params("parallel", "parallel"),
        name="window_attention",
    )(q, k, v, sink_rows)


def _window_bounded_kernel(q_ref, k_ref, vt_ref, mask_ref, sink_ref, o_ref, *, bq):
    q = q_ref[0, 0]
    ch = vt_ref.shape[3]
    cw = jnp.clip(pl.program_id(1), 0, vt_ref.shape[1] - 3)
    contract_last = (((1,), (1,)), ((), ()))
    k_win = k_ref[0, pl.ds(pl.multiple_of(cw * ch, ch), 3 * ch), :]
    st_win = lax.dot_general(k_win, q, contract_last, preferred_element_type=F32)
    st_ctx = lax.dot_general(k_ref[0, 0:ch, :], q, contract_last, preferred_element_type=F32)
    pt_win = jnp.exp2(st_win).astype(BF16) * mask_ref[0]
    acc = jnp.dot(vt_ref[0, 0], jnp.exp2(st_ctx).astype(BF16), preferred_element_type=F32)
    for c in range(3):
        acc = acc + jnp.dot(vt_ref[0, cw + c], pt_win[c * ch:(c + 1) * ch], preferred_element_type=F32)
    ot = _pad_rows(acc / (acc[HEAD_DIM:HEAD_DIM + 1] + jnp.exp2(sink_ref[0])))
    o = jnp.concatenate([ot[:, g * bq:(g + 1) * bq].T for g in range(GROUP)], axis=0)
    o_ref[...] = _merge_gqa_heads(o, bq).astype(o_ref.dtype)


def _window_masks(bq, r):
    c = jnp.arange(3 * bq)[:, None]
    off = c - (jnp.arange(r)[None, :] & (bq - 1))
    centred = (off >= bq - WINDOW) & (off <= bq + WINDOW)
    shifted = (off >= 2 * bq - WINDOW) & (off <= 2 * bq + WINDOW)
    return jnp.stack([centred & (c >= bq), centred, shifted]).astype(BF16)


def _window_attention_bounded(q, k, vt, sink_cols, n_ctx):
    hkv, slots, r, _ = q.shape
    t = k.shape[1]
    bq = r // GROUP
    nblk = t // bq
    nq = nblk - 1
    assert n_ctx == bq and nblk >= 3
    return pl.pallas_call(
        functools.partial(_window_bounded_kernel, bq=bq),
        grid=(hkv, nq),
        in_specs=[pl.BlockSpec((1, 1, r, LANES), lambda h, i: (h, i, 0, 0)),
                  pl.BlockSpec((1, t, LANES), lambda h, i: (0, 0, 0)),
                  pl.BlockSpec((1, nblk, GQA_V_ROWS, bq), lambda h, i: (h, 0, 0, 0)),
                  pl.BlockSpec((1, 3 * bq, r), lambda h, i: (jnp.where(i == 0, 0, jnp.where(i >= nblk - 2, 2, 1)), 0, 0)),
                  pl.BlockSpec((1, 1, r), lambda h, i: (h, 0, 0))],
        out_specs=pl.BlockSpec((bq, GROUP * HEAD_DIM), lambda h, i: (i, h)),
        out_shape=jax.ShapeDtypeStruct((nq * bq, hkv * GROUP * HEAD_DIM), BF16),
        compiler_params=_params("parallel", "parallel"),
        name="window_bounded",
    )(q, k, vt, _window_masks(bq, r), sink_cols)


def _sink_rows(sink, bq):
    hkv, g = sink.shape
    return jnp.broadcast_to(sink.astype(F32)[:, :, None] * LOG2E, (hkv, g, bq)).reshape(hkv, g * bq, 1)


def _mixer_ab(heads, n_ctx, sink_a, with_ctx, bounded_a, bounded):
    qa, qb, ka, kb, va, vat, vbt = heads
    t = ka.shape[1]
    bq = ROW_BLOCK
    n_lat = (t - n_ctx) // bq
    sink = _sink_rows(sink_a.reshape(N_KV_A, GROUP), bq)
    sink_cols = sink.reshape(N_KV_A, 1, GROUP * bq)
    bounded_a = bounded_a & (jnp.max(jnp.abs(sink)) <= LOG2_LOGIT_BOUND)
    oa = lax.cond(bounded_a,
                  lambda: _window_attention_bounded(qa, ka, vat, sink_cols, n_ctx),
                  lambda: _window_attention(qa, ka, va, sink, n_ctx))
    nsub = 2 if n_lat % 2 == 0 else 1
    ob = _attend(qb, kb, vbt, [], bounded, mode="gqa", n_keys=t, slot0=0, nsub=nsub, n_steps=n_lat // nsub)
    if not with_ctx:
        return (oa, ob), None
    oca = _flash_call(qa, ka, vat, [sink_cols], mode="gqa", online=True, n_keys=n_ctx,
                      slot0=n_lat, nsub=1, n_steps=1, l0=1.0)
    ocb = _attend(qb, kb, vbt, [], bounded, mode="gqa", n_keys=n_ctx, slot0=n_lat, nsub=1, n_steps=1)
    return (oa, ob), (oca, ocb)


def _mixer_c(heads, n_ctx, lam_p, subln, lam_init, with_ctx, bounded):
    q, k, v = heads
    t = k.shape[1]
    n_lat = (t - n_ctx) // ROW_BLOCK
    lp = lam_p.astype(F32)
    lam = jnp.exp(jnp.sum(lp[0] * lp[1])) - jnp.exp(jnp.sum(lp[2] * lp[3])) + lam_init
    extra = [jnp.full((1, DV_C), lam, F32), (subln.astype(F32) * (1 - lam_init)).reshape(1, DV_C)]
    nsub = 4 if n_lat % 4 == 0 else 1
    o_lat = _attend(q, k, v, extra, bounded, mode="diff", n_keys=t, slot0=0, nsub=nsub, n_steps=n_lat // nsub)
    if not with_ctx:
        return (o_lat,), None
    o_ctx = _attend(q, k, v, extra, bounded, mode="diff", n_keys=n_ctx, slot0=n_lat, nsub=1, n_steps=1)
    return (o_lat,), (o_ctx,)


def _out_proj_kernel(*refs, n_parts, has_ctx):
    lat = refs[:n_parts]
    ctx = refs[n_parts:2 * n_parts] if has_ctx else None
    w_ref, x_ref, gate_ref, g_ref, s_ref, wr_ref, y_ref, tok_ref, logit_ref = refs[-9:]
    is_ctx = pl.program_id(0) == 0 if has_ctx else False
    row = jnp.where(is_ctx, 0, 1) if has_ctx else 1
    acc = None
    col = 0
    for p in range(n_parts):
        o = lat[p][...]
        if has_ctx:
            o = jnp.where(is_ctx, ctx[p][...], o)
        width = o.shape[1]
        part = jnp.dot(o, w_ref[col:col + width, :], preferred_element_type=F32)
        acc = part if acc is None else acc + part
        col += width
    x1 = x_ref[...] + gate_ref[row] * acc
    y_ref[...] = x1
    h = x1 * lax.rsqrt(jnp.mean(x1 * x1, axis=-1, keepdims=True) + EPS) * g_ref[row] + s_ref[row]
    _store_pieces(tok_ref, _pack_bf16_pairs(h))
    hi, lo = _split_bf16(h)
    logits = (jnp.dot(hi, wr_ref[0], preferred_element_type=F32) + jnp.dot(lo, wr_ref[0], preferred_element_type=F32)
              + jnp.dot(hi, wr_ref[1], preferred_element_type=F32))
    logit_ref[...] = logits.T


def _out_proj(o_lat, o_ctx, w, x, gate2, gain2, shift2, w_router):
    d = x.shape[1]
    bm = ROW_BLOCK
    dp = d // SC_ROW_SPLIT
    has_ctx = o_ctx is not None
    row0 = 0 if has_ctx else (x.shape[0] - o_lat[0].shape[0]) // bm
    t = x.shape[0] - row0 * bm
    n_parts = len(o_lat)
    lat_map = (lambda i: (jnp.maximum(i - 1, 0), 0)) if has_ctx else (lambda i: (i, 0))
    in_specs = [pl.BlockSpec((bm, o.shape[1]), lat_map) for o in o_lat]
    args = list(o_lat)
    if has_ctx:
        in_specs += [pl.BlockSpec((bm, o.shape[1]), lambda i: (0, 0)) for o in o_ctx]
        args += list(o_ctx)
    vec_spec = pl.BlockSpec((2, 1, d), lambda i: (0, 0, 0))
    in_specs += [pl.BlockSpec(w.shape, lambda i: (0, 0)),
                 pl.BlockSpec((bm, d), lambda i: (i + row0, 0)),
                 vec_spec, vec_spec, vec_spec,
                 pl.BlockSpec((2, d, LANES), lambda i: (0, 0, 0))]
    return pl.pallas_call(
        functools.partial(_out_proj_kernel, n_parts=n_parts, has_ctx=has_ctx),
        grid=(t // bm,),
        in_specs=in_specs,
        out_specs=[pl.BlockSpec((bm, d), lambda i: (i, 0)),
                   pl.BlockSpec((SC_ROW_SPLIT // 2, bm, dp), lambda i: (0, i, 0)),
                   pl.BlockSpec((LANES, bm), lambda i: (0, i))],
        out_shape=[jax.ShapeDtypeStruct((t, d), F32),
                   jax.ShapeDtypeStruct((SC_ROW_SPLIT // 2, t, dp), F32),
                   jax.ShapeDtypeStruct((LANES, t), F32)],
        compiler_params=_params("parallel"),
        name="out_proj_router",
    )(*args, w, x, gate2, gain2, shift2, w_router)


def _pack_bf16_pairs(x):
    bits = lax.bitcast_convert_type(x, jnp.uint32)
    rounded = (bits + jnp.uint32(0x7FFF) + ((bits >> 16) & jnp.uint32(1))) & jnp.uint32(0xFFFF0000)
    half = x.shape[1] // 2
    return lax.bitcast_convert_type(rounded[:, half:] | (rounded[:, :half] >> 16), F32)


def _unpack_bf16_pairs(packed):
    bits = lax.bitcast_convert_type(packed, jnp.uint32)
    low = lax.bitcast_convert_type(bits << 16, F32)
    high = lax.bitcast_convert_type(bits & jnp.uint32(0xFFFF0000), F32)
    return jnp.concatenate([low, high], axis=1).astype(BF16)


def _store_pieces(ref, rows):
    dp = ref.shape[2]
    for j in range(ref.shape[0]):
        ref[j] = rows[:, j * dp:(j + 1) * dp]


def _load_pieces(planes):
    return jnp.concatenate([planes[j] for j in range(planes.shape[0])], axis=1)


def _moe_kernel(blk_ref, exp_ref, lo_ref, hi_ref, n_ref, x_ref, w1_ref, w3_ref, w2_ref, y_ref):
    i = pl.program_id(0)

    @pl.when(i < n_ref[0])
    def _():
        x = _unpack_bf16_pairs(_load_pieces(x_ref[...]))
        a = jnp.dot(x, w1_ref[0, 0].astype(BF16), preferred_element_type=F32)
        b = jnp.dot(x, w3_ref[0, 0].astype(BF16), preferred_element_type=F32)
        hidden = (a * jax.nn.sigmoid(a)) * b
        y = jnp.dot(hidden.astype(BF16), w2_ref[0, 0].astype(BF16), preferred_element_type=F32)
        rows = lax.broadcasted_iota(jnp.int32, (y.shape[0], 1), 0)
        y = jnp.where((rows >= lo_ref[i]) & (rows < hi_ref[i]), y, 0.0)
        first = (i == 0) | (blk_ref[i] != blk_ref[jnp.maximum(i - 1, 0)])

        @pl.when(first)
        def _():
            _store_pieces(y_ref, y)

        @pl.when(jnp.logical_not(first))
        def _():
            _store_pieces(y_ref, _load_pieces(y_ref[...]) + y)


def _moe_experts(buf, items, w1, w3, w2, layer):
    pieces, n_rows, dp = buf.shape
    d, de = w1.shape[2], w1.shape[3]
    n_items = items[0].shape[0]
    row_map = lambda i, blk, ex, lo, hi, n: (0, blk[i], 0)
    in_row_spec = pl.BlockSpec((pieces, EXPERT_BLOCK, dp), row_map)
    row_spec = pl.BlockSpec((SC_ROW_SPLIT, EXPERT_BLOCK, dp), row_map)
    w_map = lambda i, blk, ex, lo, hi, n: (layer, ex[i], 0, 0)
    grid_spec = pltpu.PrefetchScalarGridSpec(
        num_scalar_prefetch=5,
        grid=(n_items,),
        in_specs=[in_row_spec,
                  pl.BlockSpec((1, 1, d, de), w_map),
                  pl.BlockSpec((1, 1, d, de), w_map),
                  pl.BlockSpec((1, 1, de, d), w_map)],
        out_specs=row_spec,
    )
    return pl.pallas_call(
        _moe_kernel,
        grid_spec=grid_spec,
        out_shape=jax.ShapeDtypeStruct((SC_ROW_SPLIT, n_rows, dp), F32),
        compiler_params=_params("arbitrary"),
        name="moe_experts",
    )(*items, buf, w1, w3, w2)


def _expert_items(counts, n_rows):
    e = counts.shape[0]
    n_blk = n_rows // EXPERT_BLOCK
    n_items = n_blk + e - 1
    end = jnp.cumsum(counts)
    start = end - counts
    first_blk = start // EXPERT_BLOCK
    per_expert = jnp.where(counts > 0, (end - 1) // EXPERT_BLOCK - first_blk + 1, 0)
    cum = jnp.cumsum(per_expert)
    total = cum[-1]
    w = jnp.minimum(jnp.arange(n_items, dtype=jnp.int32), total - 1)
    onehot = ((cum - per_expert)[None, :] <= w[:, None]) & (w[:, None] < cum[None, :])
    pick = lambda tab: jnp.sum(jnp.where(onehot, tab[None, :], 0), axis=1).astype(jnp.int32)
    expert = pick(jnp.arange(e, dtype=jnp.int32))
    blk = pick(first_blk) + w - pick(cum - per_expert)
    lo = jnp.clip(pick(start) - blk * EXPERT_BLOCK, 0, EXPERT_BLOCK)
    hi = jnp.clip(pick(end) - blk * EXPERT_BLOCK, 0, EXPERT_BLOCK)
    return blk, expert, lo, hi, total.astype(jnp.int32).reshape(1)


def _sc_scatter(x, pos):
    pieces, t, dp = x.shape
    n = pos.shape[0]
    kk = n // t
    offs = jnp.arange(pieces, dtype=jnp.int32)[:, None] * n
    idx = [(offs + pos[k * t:(k + 1) * t][None, :]).reshape(1, pieces * t) for k in range(kk)]
    mesh = plsc.VectorSubcoreMesh(core_axis_name="core", subcore_axis_name="subcore")

    @pl.kernel(out_type=jax.ShapeDtypeStruct((pieces * n, dp), x.dtype), mesh=mesh, scratch_types=[])
    def scatter(x_hbm, *refs):
        i_hbm, o_hbm = refs[:kk], refs[kk]

        def body(x_vmem, *i_vmem):
            for iv in i_vmem:
                pltpu.sync_copy(x_vmem, o_hbm.at[iv.at[0]])

        pltpu.emit_pipeline(
            body,
            grid=(pieces * t // SC_GATHER_WINDOW,),
            in_specs=[pl.BlockSpec((SC_GATHER_WINDOW, dp), lambda i: (i, 0))]
                     + [pl.BlockSpec((1, SC_GATHER_WINDOW), lambda i: (0, i))] * kk,
            out_specs=[],
            core_axis_name=("core", "subcore"),
            dimension_semantics=(pltpu.PARALLEL,),
        )(x_hbm, *i_hbm)

    return scatter(x.reshape(pieces * t, dp), *idx).reshape(pieces, n, dp)


def _sc_gather(x, idx):
    pieces, t, dp = x.shape
    n = idx.shape[0]
    flat = (jnp.arange(pieces, dtype=jnp.int32)[:, None] * t + idx[None, :]).reshape(1, pieces * n)
    mesh = plsc.VectorSubcoreMesh(core_axis_name="core", subcore_axis_name="subcore")

    @pl.kernel(out_type=jax.ShapeDtypeStruct((pieces * n, dp), x.dtype), mesh=mesh, scratch_types=[])
    def gather(x_hbm, i_hbm, o_hbm):
        def body(i_vmem, o_vmem):
            pltpu.sync_copy(x_hbm.at[i_vmem.at[0]], o_vmem)

        pltpu.emit_pipeline(
            body,
            grid=(pieces * n // SC_GATHER_WINDOW,),
            in_specs=[pl.BlockSpec((1, SC_GATHER_WINDOW), lambda i: (0, i))],
            out_specs=[pl.BlockSpec((SC_GATHER_WINDOW, dp), lambda i: (i, 0))],
            core_axis_name=("core", "subcore"),
            dimension_semantics=(pltpu.PARALLEL,),
        )(i_hbm, o_hbm)

    return gather(x.reshape(pieces * t, dp), flat).reshape(pieces, n, dp)


def _combine_kernel(y_ref, w_ref, x_ref, gate_ref, o_ref, *, row0):
    w = w_ref[...]
    f = w[:, 0:1] * _load_pieces(y_ref[:, 0]) + w[:, 1:2] * _load_pieces(y_ref[:, 1])
    row = jnp.minimum(pl.program_id(0) + row0, 1)
    o_ref[...] = x_ref[...] + gate_ref[row] * f


def _combine(y2, weights, x, gate2, has_ctx):
    t, d = x.shape
    bm = ROW_BLOCK
    pieces, _, _, dp = y2.shape
    return pl.pallas_call(
        functools.partial(_combine_kernel, row0=0 if has_ctx else 1),
        grid=(t // bm,),
        in_specs=[pl.BlockSpec((pieces, TOP_K, bm, dp), lambda i: (0, 0, i, 0)),
                  pl.BlockSpec((bm, TOP_K), lambda i: (i, 0)),
                  pl.BlockSpec((bm, d), lambda i: (i, 0)),
                  pl.BlockSpec((2, 1, d), lambda i: (0, 0, 0))],
        out_specs=pl.BlockSpec((bm, d), lambda i: (i, 0)),
        out_shape=jax.ShapeDtypeStruct((t, d), F32),
        compiler_params=_params("parallel"),
        name="moe_combine",
    )(y2, weights, x, gate2)


def _top1_rows(p):
    m = jnp.max(p, axis=0, keepdims=True)
    idx = lax.broadcasted_iota(jnp.int32, p.shape, 0)
    return m, jnp.min(jnp.where(p == m, idx, p.shape[0]), axis=0, keepdims=True)


def _prefix_rank(onehot):
    e, n = onehot.shape
    blk = EXPERT_BLOCK
    nb = n // blk
    earlier = (jnp.arange(blk)[:, None] < jnp.arange(blk)[None, :]).astype(F32)
    within = jnp.dot(onehot.reshape(e * nb, blk), earlier).reshape(e, n)
    tot = jnp.sum(onehot.reshape(e, nb, blk), axis=2)
    before = jnp.repeat(jnp.cumsum(tot, axis=1) - tot, blk, axis=1)
    return jnp.sum((within + before) * onehot, axis=0).astype(jnp.int32)


def _hier_moe(tokens, logits_t, b_group, b_expert, w1, w3, w2, layer):
    pieces, t, dp = tokens.shape
    pg = jax.nn.softmax(logits_t[:N_GROUPS] + b_group.astype(F32)[:, None], axis=0)
    g_prob, g_idx = _top1_rows(pg)
    le = (logits_t[N_GROUPS:N_GROUPS + N_EXPERTS] + b_expert.astype(F32)[:, None]).reshape(N_GROUPS, EXPERTS_PER_GROUP, t)
    group_iota = lax.broadcasted_iota(jnp.int32, (N_GROUPS, 1, t), 0)
    le = jnp.sum(jnp.where(group_iota == g_idx[None], le, 0.0), axis=0)
    pe = jax.nn.softmax(le, axis=0)
    p1, i1 = _top1_rows(pe)
    p2, i2 = _top1_rows(jnp.where(lax.broadcasted_iota(jnp.int32, pe.shape, 0) == i1, -1.0, pe))
    e_prob, e_idx = jnp.concatenate([p1, p2], axis=0), jnp.concatenate([i1, i2], axis=0)
    weights = g_prob * e_prob / jnp.sum(e_prob, axis=0, keepdims=True)
    flat_e = (g_idx * EXPERTS_PER_GROUP + e_idx).reshape(1, TOP_K * t)
    onehot = (flat_e == lax.broadcasted_iota(jnp.int32, (N_EXPERTS, TOP_K * t), 0)).astype(F32)
    rank = _prefix_rank(onehot)
    counts = jnp.sum(onehot, axis=1).astype(jnp.int32)
    first_row = jnp.sum(onehot * (jnp.cumsum(counts) - counts).astype(F32)[:, None], axis=0).astype(jnp.int32)
    dest = first_row + rank
    buf = _sc_scatter(tokens, dest)
    yb = _moe_experts(buf, _expert_items(counts, TOP_K * t), w1, w3, w2, layer)
    return _sc_gather(yb, dest).reshape(SC_ROW_SPLIT, TOP_K, t, dp), weights.T


def kernel(x, c, ctx, c_ctx, w_mod, b_mod, norm_mix, norm_ffn, w_in_ab, w_out_ab, qn_a, kn_a, sink_a, qn_b, kn_b, w_in_c, w_out_c, qn_c, kn_c, lam_c, subln_c, w_group, b_group, w_expert, b_expert, w1, w3, w2):
    b, s_lat, d = x.shape
    n_ctx = ctx.shape[1]
    assert b == 1 and n_ctx == ROW_BLOCK and s_lat % (2 * ROW_BLOCK) == 0
    depth = w_mod.shape[0]
    cos, sin = _rope_tables(s_lat, n_ctx)
    mods = _mod_vectors(c, c_ctx, w_mod, b_mod)
    xs = jnp.concatenate([ctx[0], x[0]], axis=0)
    has_ctx = True
    pending = None
    for l in range(depth):
        last = l == depth - 1
        i = l // 2
        sh1, sc1, gt1, sh2, sc2, gt2 = [mods[l, :2, j * d:(j + 1) * d].reshape(2, 1, d) for j in range(6)]
        gain1 = norm_mix[l].astype(F32) * (1 + sc1)
        if l % 2 == 0:
            xs, heads = _inproj_ab(xs, pending, gain1, sh1, w_in_ab[i].astype(BF16), qn_a[i], kn_a[i], qn_b[i], kn_b[i], cos, sin)
            o_lat, o_ctx = _mixer_ab(heads, n_ctx, sink_a[i], not last,
                                     _logits_bounded(qn_a[i], kn_a[i]), _logits_bounded(qn_b[i], kn_b[i]))
            w_out = w_out_ab[i]
        else:
            lam_init = 0.8 - 0.6 * math.exp(-0.3 * l)
            xs, heads = _inproj_c(xs, pending, gain1, sh1, w_in_c[i].astype(BF16), qn_c[i], kn_c[i], cos, sin)
            o_lat, o_ctx = _mixer_c(heads, n_ctx, lam_c[i], subln_c[i], lam_init, not last,
                                    _logits_bounded(qn_c[i], kn_c[i]))
            w_out = w_out_c[i]
        if last:
            has_ctx = False
        gain2 = norm_ffn[l].astype(F32) * (1 + sc2)
        w_router = jnp.zeros((d, LANES), F32).at[:, :N_GROUPS].set(w_group[l]).at[:, N_GROUPS:N_GROUPS + N_EXPERTS].set(w_expert[l])
        w_router = jnp.stack(_split_bf16(w_router))
        xs, tokens, logits = _out_proj(o_lat, o_ctx, w_out.astype(BF16), xs, gt1, gain2, sh2, w_router)
        y2, weights = _hier_moe(tokens, logits, b_group[l], b_expert[l], w1, w3, w2, l)
        pending = (y2, weights, gt2)
    return _combine(y2, weights, xs, gt2, has_ctx).reshape(b, s_lat, d)
```

```python
import functools
import math

import jax
import jax.numpy as jnp
from jax import lax
from jax.experimental import pallas as pl
from jax.experimental.pallas import tpu as pltpu
from jax.experimental.pallas import tpu_sc as plsc

F32 = jnp.float32
BF16 = jnp.bfloat16

GRID_W = 64
HEAD_DIM = 64
WINDOW = 128
ROPE_THETA = 10000.0
EPS = 1e-6
NEG = -1e30
N_HEADS_A, N_KV_A = 8, 2
N_HEADS_B, N_KV_B = 8, 2
GROUP = N_HEADS_A // N_KV_A
QA_W, KVA_W = N_HEADS_A * HEAD_DIM, N_KV_A * HEAD_DIM
QB_W, KVB_W = N_HEADS_B * HEAD_DIM, N_KV_B * HEAD_DIM
N_HEADS_C = 8
DV_C = 2 * HEAD_DIM
N_GROUPS, EXPERTS_PER_GROUP, TOP_K = 4, 8, 2
N_EXPERTS = N_GROUPS * EXPERTS_PER_GROUP
LANES = 128
ROW_BLOCK = 256
EXPERT_BLOCK = 512
SC_GATHER_WINDOW = 128
SC_ROW_SPLIT = 4
KEY_BLOCK = 3328
GQA_V_ROWS = 128
DIFF_V_ROWS = 128
VMEM_LIMIT = 48 * 1024 * 1024
LOG2E = math.log2(math.e)
Q_SCALE = HEAD_DIM ** -0.5 * LOG2E
LOG2_LOGIT_BOUND = 60.0


def _params(*sem):
    return pltpu.CompilerParams(dimension_semantics=sem, vmem_limit_bytes=VMEM_LIMIT)


def _lane(shape):
    return lax.broadcasted_iota(jnp.int32, shape, 1)


def _mod_vec_kernel(a_ref, w_ref, b_ref, o_ref):
    a = a_ref[...]
    a = a * jax.nn.sigmoid(a)
    o_ref[0] = jnp.dot(a, w_ref[0], preferred_element_type=F32, precision=lax.Precision.HIGHEST) + b_ref[0]


def _mod_vectors(c, c_ctx, w_mod, b_mod):
    depth, d, n = w_mod.shape
    a = jnp.zeros((8, d), F32).at[0].set(c_ctx).at[1].set(c[0])
    bn = 1024
    return pl.pallas_call(
        _mod_vec_kernel,
        grid=(depth, n // bn),
        in_specs=[pl.BlockSpec((8, d), lambda l, j: (0, 0)),
                  pl.BlockSpec((1, d, bn), lambda l, j: (l, 0, j)),
                  pl.BlockSpec((1, 1, bn), lambda l, j: (l, 0, j))],
        out_specs=pl.BlockSpec((1, 8, bn), lambda l, j: (l, 0, j)),
        out_shape=jax.ShapeDtypeStruct((depth, 8, n), F32),
        compiler_params=_params("parallel", "parallel"),
        name="mod_vectors",
    )(a, w_mod, b_mod.reshape(depth, 1, n))


def _modulated(x, g_ref, s_ref):
    y = x * lax.rsqrt(jnp.mean(x * x, axis=-1, keepdims=True) + EPS)
    row = jnp.minimum(pl.program_id(0), 1)
    return y * g_ref[row] + s_ref[row]


def _split_bf16(x):
    top = lax.bitcast_convert_type(lax.bitcast_convert_type(x, jnp.uint32) & jnp.uint32(0xFFFF0000), F32)
    return top.astype(BF16), (x - top).astype(BF16)


def _head_norm_rope(x, gain, cos, sin, seg_mean):
    hi, lo = _split_bf16(x * x)
    ms = jnp.dot(hi, seg_mean, preferred_element_type=F32) + jnp.dot(lo, seg_mean, preferred_element_type=F32)
    y = x * lax.rsqrt(ms + EPS) * gain
    first_half = (_lane(y.shape) & (HEAD_DIM - 1)) < HEAD_DIM // 2
    partner = jnp.where(first_half, pltpu.roll(y, LANES - HEAD_DIM // 2, 1), pltpu.roll(y, HEAD_DIM // 2, 1))
    return y * cos + partner * sin


def _store_value_heads(v_ref, v):
    lane = _lane(v.shape)
    ones_col = jnp.where(lane == HEAD_DIM, 1.0, 0.0)
    v_ref[0] = jnp.where(lane < HEAD_DIM, v, ones_col).astype(BF16)
    v_ref[1] = jnp.where(lane < HEAD_DIM, pltpu.roll(v, HEAD_DIM, 1), ones_col).astype(BF16)


def _ones_row_block(rows, cols):
    return jnp.where(lax.broadcasted_iota(jnp.int32, (rows, cols), 0) == 0, 1.0, 0.0)


def _store_value_heads_t(vt_ref, v):
    vt = v.T
    tail = _ones_row_block(GQA_V_ROWS - HEAD_DIM, v.shape[0])
    vt_ref[0, 0] = jnp.concatenate([vt[:HEAD_DIM], tail], axis=0).astype(BF16)
    vt_ref[1, 0] = jnp.concatenate([vt[HEAD_DIM:], tail], axis=0).astype(BF16)


N_INPROJ_INPUTS = 8


def _inproj_rows(refs, fused):
    if not fused:
        return refs[0][...], refs[1:N_INPROJ_INPUTS], refs[N_INPROJ_INPUTS:]
    y_ref, wts_ref, gate_ref, x_ref = refs[:4]
    ins, outs = refs[4:N_INPROJ_INPUTS + 3], refs[N_INPROJ_INPUTS + 3:]
    wts = wts_ref[...]
    f = wts[:, 0:1] * _load_pieces(y_ref[:, 0]) + wts[:, 1:2] * _load_pieces(y_ref[:, 1])
    x = x_ref[...] + gate_ref[jnp.minimum(pl.program_id(0), 1)] * f
    outs[0][...] = x
    return x, ins, outs[1:]


def _inproj_ab_kernel(*refs, fused):
    x, (g_ref, s_ref, w_ref, hg_ref, cos_ref, sin_ref, seg_ref), outs = _inproj_rows(refs, fused)
    qa_ref, qb_ref, ka_ref, kb_ref, va_ref, vat_ref, vbt_ref = outs
    bm = x.shape[0]
    h = _modulated(x, g_ref, s_ref).astype(BF16)
    proj = jnp.dot(h, w_ref[...], preferred_element_type=F32)
    cos, sin, seg = cos_ref[...], sin_ref[...], seg_ref[...]
    low = _lane((bm, LANES)) < HEAD_DIM

    def normed(j):
        cols = slice(j * LANES, (j + 1) * LANES)
        return _head_norm_rope(proj[:, cols], hg_ref[:, cols], cos, sin, seg)

    def store_queries(q_ref, tile0):
        for j in range(N_HEADS_A // 2):
            y = normed(tile0 + j)
            swapped = pltpu.roll(y, HEAD_DIM, 1)
            kv, g0 = (2 * j) // GROUP, (2 * j) % GROUP
            if kv == 0:
                even, odd = jnp.where(low, y, 0.0), jnp.where(low, swapped, 0.0)
            else:
                even, odd = jnp.where(low, 0.0, swapped), jnp.where(low, 0.0, y)
            q_ref[kv, 0, g0 * bm:(g0 + 1) * bm, :] = even.astype(BF16)
            q_ref[kv, 0, (g0 + 1) * bm:(g0 + 2) * bm, :] = odd.astype(BF16)

    tq = QA_W // LANES
    store_queries(qa_ref, 0)
    ka_ref[0] = normed(tq).astype(BF16)
    va = proj[:, (tq + 1) * LANES:(tq + 2) * LANES]
    _store_value_heads(va_ref, va)
    _store_value_heads_t(vat_ref, va)
    store_queries(qb_ref, tq + 2)
    kb_ref[0] = normed(2 * tq + 2).astype(BF16)
    _store_value_heads_t(vbt_ref, proj[:, (2 * tq + 3) * LANES:(2 * tq + 4) * LANES])


def _inproj_c_kernel(*refs, fused):
    x, (g_ref, s_ref, w_ref, hg_ref, cos_ref, sin_ref, seg_ref), (q_ref, k_ref, v_ref) = _inproj_rows(refs, fused)
    bm = x.shape[0]
    h = _modulated(x, g_ref, s_ref).astype(BF16)
    proj = jnp.dot(h, w_ref[...], preferred_element_type=F32)
    cos, sin, seg = cos_ref[...], sin_ref[...], seg_ref[...]
    low = _lane((bm, LANES)) < HEAD_DIM

    def normed(j):
        cols = slice(j * LANES, (j + 1) * LANES)
        return _head_norm_rope(proj[:, cols], hg_ref[:, cols], cos, sin, seg)

    for j in range(N_HEADS_C):
        y = normed(j)
        q_ref[j, 0, 0:bm, :] = jnp.where(low, y, 0.0).astype(BF16)
        q_ref[j, 0, bm:2 * bm, :] = jnp.where(low, 0.0, y).astype(BF16)
        k_ref[j] = normed(N_HEADS_C + j).astype(BF16)
        v = proj[:, (2 * N_HEADS_C + j) * LANES:(2 * N_HEADS_C + j + 1) * LANES]
        v_ref[j, 0] = v.T.astype(BF16)


def _rope_tables(seq, n_ctx):
    rows_n = seq // GRID_W
    rows = jnp.broadcast_to(jnp.arange(rows_n, dtype=F32)[:, None], (rows_n, GRID_W)).reshape(-1)
    cols = jnp.broadcast_to(jnp.arange(GRID_W, dtype=F32)[None, :], (rows_n, GRID_W)).reshape(-1)
    half = HEAD_DIM // 2
    inv = ROPE_THETA ** (-jnp.arange(0, half, 2, dtype=F32) / half)
    ang = jnp.concatenate([rows[:, None] * inv, cols[:, None] * inv], axis=-1)
    reps = LANES // half
    sign = jnp.tile(jnp.concatenate([-jnp.ones((half,), F32), jnp.ones((half,), F32)]), LANES // HEAD_DIM)
    cos = jnp.pad(jnp.tile(jnp.cos(ang), (1, reps)), ((n_ctx, 0), (0, 0)), constant_values=1.0)
    sin = jnp.pad(jnp.tile(jnp.sin(ang), (1, reps)) * sign, ((n_ctx, 0), (0, 0)))
    return cos, sin


def _segment_mean_matrix():
    idx = jnp.arange(LANES) // HEAD_DIM
    return jnp.where(idx[:, None] == idx[None, :], 1.0 / HEAD_DIM, 0.0).astype(BF16)


def _q_slot(i, nblk):
    return (i + nblk - 1) % nblk


def _inproj_common_specs(t, d, n):
    bm = ROW_BLOCK
    return [pl.BlockSpec((bm, d), lambda i: (i, 0)),
            pl.BlockSpec((2, 1, d), lambda i: (0, 0, 0)),
            pl.BlockSpec((2, 1, d), lambda i: (0, 0, 0)),
            pl.BlockSpec((d, n), lambda i: (0, 0)),
            pl.BlockSpec((1, n), lambda i: (0, 0)),
            pl.BlockSpec((bm, LANES), lambda i: (i, 0)),
            pl.BlockSpec((bm, LANES), lambda i: (i, 0)),
            pl.BlockSpec((LANES, LANES), lambda i: (0, 0))]


def _inproj_call(kernel_fn, name, x, pending, args, n, out_specs, out_shape):
    t, d = x.shape
    bm = ROW_BLOCK
    in_specs = _inproj_common_specs(t, d, n)
    args = [x] + list(args)
    fused = pending is not None
    if fused:
        y2, wts, gate2 = pending
        pieces, _, _, dp = y2.shape
        in_specs = [pl.BlockSpec((pieces, TOP_K, bm, dp), lambda i: (0, 0, i, 0)),
                    pl.BlockSpec((bm, TOP_K), lambda i: (i, 0)),
                    pl.BlockSpec((2, 1, d), lambda i: (0, 0, 0))] + in_specs
        args = [y2, wts, gate2] + args
        out_specs = [pl.BlockSpec((bm, d), lambda i: (i, 0))] + list(out_specs)
        out_shape = [jax.ShapeDtypeStruct((t, d), F32)] + list(out_shape)
    outs = pl.pallas_call(
        functools.partial(kernel_fn, fused=fused),
        grid=(t // bm,),
        in_specs=in_specs,
        out_specs=out_specs,
        out_shape=out_shape,
        compiler_params=_params("parallel"),
        name=name,
    )(*args)
    return (outs[0], outs[1:]) if fused else (x, outs)


def _inproj_ab(x, pending, gain2, shift2, w, qn_a, kn_a, qn_b, kn_b, cos, sin):
    t, d = x.shape
    n = w.shape[1]
    bm, nblk = ROW_BLOCK, t // ROW_BLOCK
    tile = lambda g, reps: jnp.tile(g.astype(F32), reps)
    ones_v = jnp.ones((KVA_W,), F32)
    hg = jnp.concatenate([tile(qn_a, N_HEADS_A) * Q_SCALE, tile(kn_a, N_KV_A), ones_v,
                          tile(qn_b, N_HEADS_B) * Q_SCALE, tile(kn_b, N_KV_B), ones_v]).reshape(1, n)
    q_shape = jax.ShapeDtypeStruct((N_KV_A, nblk, GROUP * bm, LANES), BF16)
    k_shape = jax.ShapeDtypeStruct((1, t, LANES), BF16)
    v_shape = jax.ShapeDtypeStruct((N_KV_A, t, LANES), BF16)
    vt_shape = jax.ShapeDtypeStruct((N_KV_A, nblk, GQA_V_ROWS, bm), BF16)
    q_spec = pl.BlockSpec((N_KV_A, 1, GROUP * bm, LANES), lambda i: (0, _q_slot(i, nblk), 0, 0))
    k_spec = pl.BlockSpec((1, bm, LANES), lambda i: (0, i, 0))
    v_spec = pl.BlockSpec((N_KV_A, bm, LANES), lambda i: (0, i, 0))
    vt_spec = pl.BlockSpec((N_KV_A, 1, GQA_V_ROWS, bm), lambda i: (0, i, 0, 0))
    return _inproj_call(_inproj_ab_kernel, "inproj_ab", x, pending,
                        [gain2, shift2, w, hg, cos, sin, _segment_mean_matrix()], n,
                        [q_spec, q_spec, k_spec, k_spec, v_spec, vt_spec, vt_spec],
                        [q_shape, q_shape, k_shape, k_shape, v_shape, vt_shape, vt_shape])


def _inproj_c(x, pending, gain2, shift2, w, qn, kn, cos, sin):
    t, d = x.shape
    n = w.shape[1]
    bm, nblk = ROW_BLOCK, t // ROW_BLOCK
    h = N_HEADS_C
    tile = lambda g: jnp.tile(g.astype(F32), 2 * h)
    hg = jnp.concatenate([tile(qn) * Q_SCALE, tile(kn), jnp.ones((h * DV_C,), F32)]).reshape(1, n)
    return _inproj_call(_inproj_c_kernel, "inproj_c", x, pending,
                        [gain2, shift2, w, hg, cos, sin, _segment_mean_matrix()], n,
                        [pl.BlockSpec((h, 1, 2 * bm, LANES), lambda i: (0, _q_slot(i, nblk), 0, 0)),
                         pl.BlockSpec((h, bm, LANES), lambda i: (0, i, 0)),
                         pl.BlockSpec((h, 1, DIFF_V_ROWS, bm), lambda i: (0, i, 0, 0))],
                        [jax.ShapeDtypeStruct((h, nblk, 2 * bm, LANES), BF16),
                         jax.ShapeDtypeStruct((h, t, LANES), BF16),
                         jax.ShapeDtypeStruct((h, nblk, DIFF_V_ROWS, bm), BF16)])


def _pad_rows(x):
    rows = x.shape[0]
    if rows >= LANES:
        return x[:LANES]
    return jnp.concatenate([x, jnp.zeros((LANES - rows, x.shape[1]), x.dtype)], axis=0)


def _merge_gqa_heads(o, bq):
    low = _lane((bq, LANES)) < HEAD_DIM
    pairs = [jnp.where(low, o[g * bq:(g + 1) * bq], pltpu.roll(o[(g + 1) * bq:(g + 2) * bq], HEAD_DIM, 1))
             for g in range(0, GROUP, 2)]
    return jnp.concatenate(pairs, axis=1)


def _flash_kernel(*refs, mode, online, n_keys, bk, dv, l0):
    refs = list(refs)
    q_ref, k_ref, vt_ref = refs[:3]
    pos = 3
    if online:
        m0_ref = refs[pos]
        pos += 1
    if mode == "diff":
        lam_ref, sub_ref = refs[pos:pos + 2]
        pos += 2
    o_ref = refs[pos]
    scratch = refs[pos + 1:]
    acc_sc = scratch[0]
    ones_row = acc_sc.shape[0] > dv
    m_sc = scratch[1] if online else None
    l_sc = None if ones_row else scratch[-1]

    nsub, rb = q_ref.shape[1], q_ref.shape[2]
    r = nsub * rb
    ch = vt_ref.shape[3]
    q = q_ref[0].reshape(r, LANES)
    acc_sc[...] = jnp.where(lax.broadcasted_iota(jnp.int32, acc_sc.shape, 0) == dv, l0, 0.0).astype(F32)
    if online:
        m_sc[...] = m0_ref[0]
    if not ones_row:
        l_sc[...] = jnp.full(l_sc.shape, l0, F32)

    def block(start, size):
        kb = k_ref[0, pl.ds(start, size), :]
        st = lax.dot_general(kb, q, (((1,), (1,)), ((), ())), preferred_element_type=F32)
        if online:
            m_prev = m_sc[...]
            m_new = jnp.maximum(m_prev, jnp.max(st, axis=0, keepdims=True))
            p = jnp.exp2(st - m_new)
            alpha = jnp.exp2(m_prev - m_new)
            acc = alpha * acc_sc[...]
            m_sc[...] = m_new
        else:
            p = jnp.exp2(st)
            alpha = 1.0
            acc = acc_sc[...]
        if not ones_row:
            l_sc[...] = alpha * l_sc[...] + jnp.sum(p, axis=0, keepdims=True)
        pt = p.astype(BF16)
        c0 = start // ch
        for c in range(size // ch):
            acc = acc + jnp.dot(vt_ref[0, c0 + c], pt[c * ch:(c + 1) * ch], preferred_element_type=F32)
        acc_sc[...] = acc

    n_full, tail = n_keys // bk, n_keys % bk
    if n_full:
        def body(i, carry):
            block(pl.multiple_of(i * bk, bk), bk)
            return carry
        lax.fori_loop(0, n_full, body, 0)
    if tail:
        block(n_full * bk, tail)

    acc = acc_sc[...]
    den = acc[dv:dv + 1] if ones_row else l_sc[...]
    ot = _pad_rows(acc / den)
    bq = rb // (GROUP if mode == "gqa" else 2)
    o = jnp.concatenate([ot[:, j * bq:(j + 1) * bq].T for j in range(r // bq)], axis=0)
    if mode == "gqa":
        for b in range(nsub):
            o_ref[b * bq:(b + 1) * bq, :] = _merge_gqa_heads(o[b * rb:(b + 1) * rb], bq).astype(o_ref.dtype)
    else:
        for b in range(nsub):
            d = o[b * rb:b * rb + bq] - lam_ref[...] * o[b * rb + bq:(b + 1) * rb]
            y = d * lax.rsqrt(jnp.mean(d * d, axis=-1, keepdims=True) + EPS) * sub_ref[...]
            o_ref[b * bq:(b + 1) * bq, :] = y.astype(o_ref.dtype)


def _flash_call(q, k, vt, extra, *, mode, online, n_keys, slot0, nsub, n_steps, l0=0.0):
    hkv, _, rb, _ = q.shape
    hk = k.shape[0]
    _, _, dvr, ch = vt.shape
    dv = HEAD_DIM if mode == "gqa" else DV_C
    bq = rb // (GROUP if mode == "gqa" else 2)
    ocols = GROUP * HEAD_DIM if mode == "gqa" else DV_C
    r = nsub * rb
    bk = min(KEY_BLOCK, n_keys)
    in_specs = [pl.BlockSpec((1, nsub, rb, LANES), lambda h, i: (h, slot0 // nsub + i, 0, 0)),
                pl.BlockSpec((1, n_keys, LANES), (lambda h, i: (h, 0, 0)) if hk > 1 else (lambda h, i: (0, 0, 0))),
                pl.BlockSpec((1, n_keys // ch, dvr, ch), lambda h, i: (h, 0, 0, 0))]
    args = [q, k, vt]
    scratch = [pltpu.VMEM((dvr, r), F32)]
    if online:
        m0 = extra.pop(0)
        in_specs.append(pl.BlockSpec((1, 1, r), lambda h, i: (h, 0, 0)))
        args.append(m0)
        scratch.append(pltpu.VMEM((1, r), F32))
    if dvr <= dv:
        scratch.append(pltpu.VMEM((1, r), F32))
    for a in extra:
        in_specs.append(pl.BlockSpec(a.shape, lambda h, i: (0, 0)))
        args.append(a)
    return pl.pallas_call(
        functools.partial(_flash_kernel, mode=mode, online=online, n_keys=n_keys, bk=bk, dv=dv, l0=l0),
        grid=(hkv, n_steps),
        in_specs=in_specs,
        out_specs=pl.BlockSpec((nsub * bq, ocols), lambda h, i: (i, h)),
        out_shape=jax.ShapeDtypeStruct((n_steps * nsub * bq, hkv * ocols), BF16),
        scratch_shapes=scratch,
        compiler_params=_params("parallel", "parallel"),
        name="flash_online" if online else "flash_bounded",
    )(*args)


def _logits_bounded(q_gain, k_gain):
    bound = HEAD_DIM * Q_SCALE * 1.02 * jnp.max(jnp.abs(q_gain.astype(F32))) * jnp.max(jnp.abs(k_gain.astype(F32)))
    return bound <= LOG2_LOGIT_BOUND


def _attend(q, k, v, extra, bounded, **kw):
    hkv, r = q.shape[0], kw["nsub"] * q.shape[2]
    fast = lambda q_, k_, v_, *e: _flash_call(q_, k_, v_, list(e), online=False, **kw)
    safe = lambda q_, k_, v_, *e: _flash_call(q_, k_, v_, [jnp.full((hkv, 1, r), NEG, F32)] + list(e), online=True, **kw)
    return lax.cond(bounded, fast, safe, q, k, v, *extra)


def _window_kernel(q_ref, k_ref, v_ref, sink_ref, o_ref, *, bq, n_ctx):
    q = q_ref[0, 0]
    r = q.shape[0]
    t = k_ref.shape[1]
    w = bq + 2 * WINDOW
    q0 = pl.program_id(1) * bq
    ws = pl.multiple_of(jnp.clip(n_ctx + q0 - WINDOW, 0, t - w), WINDOW)
    kw = k_ref[0, pl.ds(ws, w), :]
    vw = v_ref[0, pl.ds(ws, w), :]
    contract_last = (((1,), (1,)), ((), ()))
    s_loc = lax.dot_general(q, kw, contract_last, preferred_element_type=F32)
    qpos = q0 + (lax.broadcasted_iota(jnp.int32, (r, w), 0) & (bq - 1))
    kpos = ws - n_ctx + lax.broadcasted_iota(jnp.int32, (r, w), 1)
    mask = (kpos >= 0) & (kpos - qpos <= WINDOW) & (qpos - kpos <= WINDOW)
    s_loc = jnp.where(mask, s_loc, NEG)
    s_ctx = lax.dot_general(q, k_ref[0, 0:n_ctx, :], contract_last, preferred_element_type=F32)
    sink = sink_ref[0]
    m = jnp.maximum(sink, jnp.maximum(jnp.max(s_loc, axis=-1, keepdims=True), jnp.max(s_ctx, axis=-1, keepdims=True)))
    p_loc = jnp.exp2(s_loc - m)
    p_ctx = jnp.exp2(s_ctx - m)
    l = jnp.exp2(sink - m) + jnp.sum(p_loc, axis=-1, keepdims=True) + jnp.sum(p_ctx, axis=-1, keepdims=True)
    o = (jnp.dot(p_loc.astype(BF16), vw, preferred_element_type=F32)
         + jnp.dot(p_ctx.astype(BF16), v_ref[0, 0:n_ctx, :], preferred_element_type=F32))
    o_ref[...] = _merge_gqa_heads(o / l, bq).astype(o_ref.dtype)


def _window_attention(q, k, v, sink_rows, n_ctx):
    hkv, slots, r, _ = q.shape
    t = k.shape[1]
    bq = r // GROUP
    nq = slots - n_ctx // bq
    return pl.pallas_call(
        functools.partial(_window_kernel, bq=bq, n_ctx=n_ctx),
        grid=(hkv, nq),
        in_specs=[pl.BlockSpec((1, 1, r, LANES), lambda h, i: (h, i, 0, 0)),
                  pl.BlockSpec((1, t, LANES), lambda h, i: (0, 0, 0)),
                  pl.BlockSpec((1, t, LANES), lambda h, i: (h, 0, 0)),
                  pl.BlockSpec((1, r, 1), lambda h, i: (h, 0, 0))],
        out_specs=pl.BlockSpec((bq, GROUP * HEAD_DIM), lambda h, i: (i, h)),
        out_shape=jax.ShapeDtypeStruct((nq * bq, hkv * GROUP * HEAD_DIM), BF16),
        compiler_params=_params("parallel", "parallel"),
        name="window_attention",
    )(q, k, v, sink_rows)


def _window_bounded_kernel(q_ref, k_ref, vt_ref, mask_ref, sink_ref, o_ref, *, bq):
    hkv, r = q_ref.shape[0], q_ref.shape[2]
    q = q_ref[:, 0].reshape(hkv * r, LANES)
    ch = vt_ref.shape[3]
    cw = jnp.clip(pl.program_id(0), 0, vt_ref.shape[1] - 3)
    contract_last = (((1,), (1,)), ((), ()))
    k_win = k_ref[0, pl.ds(pl.multiple_of(cw * ch, ch), 3 * ch), :]
    st_win = lax.dot_general(k_win, q, contract_last, preferred_element_type=F32)
    st_ctx = lax.dot_general(k_ref[0, 0:ch, :], q, contract_last, preferred_element_type=F32)
    pt_win = jnp.exp2(st_win).astype(BF16) * jnp.concatenate([mask_ref[0]] * hkv, axis=1)
    pt_ctx = jnp.exp2(st_ctx).astype(BF16)
    for h in range(hkv):
        cols = slice(h * r, (h + 1) * r)
        acc = jnp.dot(vt_ref[h, 0], pt_ctx[:, cols], preferred_element_type=F32)
        for c in range(3):
            acc = acc + jnp.dot(vt_ref[h, cw + c], pt_win[c * ch:(c + 1) * ch, cols], preferred_element_type=F32)
        ot = _pad_rows(acc / (acc[HEAD_DIM:HEAD_DIM + 1] + jnp.exp2(sink_ref[h])))
        o = jnp.concatenate([ot[:, g * bq:(g + 1) * bq].T for g in range(GROUP)], axis=0)
        o_ref[:, h * GROUP * HEAD_DIM:(h + 1) * GROUP * HEAD_DIM] = _merge_gqa_heads(o, bq).astype(o_ref.dtype)


def _window_masks(bq, r):
    c = jnp.arange(3 * bq)[:, None]
    off = c - (jnp.arange(r)[None, :] & (bq - 1))
    centred = (off >= bq - WINDOW) & (off <= bq + WINDOW)
    shifted = (off >= 2 * bq - WINDOW) & (off <= 2 * bq + WINDOW)
    return jnp.stack([centred & (c >= bq), centred, shifted]).astype(BF16)


def _window_attention_bounded(q, k, vt, sink_cols, n_ctx):
    hkv, slots, r, _ = q.shape
    t = k.shape[1]
    bq = r // GROUP
    nblk = t // bq
    nq = nblk - 1
    assert n_ctx == bq and nblk >= 3
    return pl.pallas_call(
        functools.partial(_window_bounded_kernel, bq=bq),
        grid=(nq,),
        in_specs=[pl.BlockSpec((hkv, 1, r, LANES), lambda i: (0, i, 0, 0)),
                  pl.BlockSpec((1, t, LANES), lambda i: (0, 0, 0)),
                  pl.BlockSpec((hkv, nblk, GQA_V_ROWS, bq), lambda i: (0, 0, 0, 0)),
                  pl.BlockSpec((1, 3 * bq, r), lambda i: (jnp.where(i == 0, 0, jnp.where(i >= nblk - 2, 2, 1)), 0, 0)),
                  pl.BlockSpec((hkv, 1, r), lambda i: (0, 0, 0))],
        out_specs=pl.BlockSpec((bq, hkv * GROUP * HEAD_DIM), lambda i: (i, 0)),
        out_shape=jax.ShapeDtypeStruct((nq * bq, hkv * GROUP * HEAD_DIM), BF16),
        compiler_params=_params("parallel"),
        name="window_bounded",
    )(q, k, vt, _window_masks(bq, r), sink_cols)


def _sink_rows(sink, bq):
    hkv, g = sink.shape
    return jnp.broadcast_to(sink.astype(F32)[:, :, None] * LOG2E, (hkv, g, bq)).reshape(hkv, g * bq, 1)


def _mixer_ab(heads, n_ctx, sink_a, with_ctx, bounded_a, bounded):
    qa, qb, ka, kb, va, vat, vbt = heads
    t = ka.shape[1]
    bq = ROW_BLOCK
    n_lat = (t - n_ctx) // bq
    sink = _sink_rows(sink_a.reshape(N_KV_A, GROUP), bq)
    sink_cols = sink.reshape(N_KV_A, 1, GROUP * bq)
    bounded_a = bounded_a & (jnp.max(jnp.abs(sink)) <= LOG2_LOGIT_BOUND)
    oa = lax.cond(bounded_a,
                  lambda: _window_attention_bounded(qa, ka, vat, sink_cols, n_ctx),
                  lambda: _window_attention(qa, ka, va, sink, n_ctx))
    nsub = 2 if n_lat % 2 == 0 else 1
    ob = _attend(qb, kb, vbt, [], bounded, mode="gqa", n_keys=t, slot0=0, nsub=nsub, n_steps=n_lat // nsub)
    if not with_ctx:
        return (oa, ob), None
    oca = _flash_call(qa, ka, vat, [sink_cols], mode="gqa", online=True, n_keys=n_ctx,
                      slot0=n_lat, nsub=1, n_steps=1, l0=1.0)
    ocb = _attend(qb, kb, vbt, [], bounded, mode="gqa", n_keys=n_ctx, slot0=n_lat, nsub=1, n_steps=1)
    return (oa, ob), (oca, ocb)


def _mixer_c(heads, n_ctx, lam_p, subln, lam_init, with_ctx, bounded):
    q, k, v = heads
    t = k.shape[1]
    n_lat = (t - n_ctx) // ROW_BLOCK
    lp = lam_p.astype(F32)
    lam = jnp.exp(jnp.sum(lp[0] * lp[1])) - jnp.exp(jnp.sum(lp[2] * lp[3])) + lam_init
    extra = [jnp.full((1, DV_C), lam, F32), (subln.astype(F32) * (1 - lam_init)).reshape(1, DV_C)]
    nsub = 4 if n_lat % 4 == 0 else 1
    o_lat = _attend(q, k, v, extra, bounded, mode="diff", n_keys=t, slot0=0, nsub=nsub, n_steps=n_lat // nsub)
    if not with_ctx:
        return (o_lat,), None
    o_ctx = _attend(q, k, v, extra, bounded, mode="diff", n_keys=n_ctx, slot0=n_lat, nsub=1, n_steps=1)
    return (o_lat,), (o_ctx,)


def _out_proj_kernel(*refs, n_parts, has_ctx):
    lat = refs[:n_parts]
    ctx = refs[n_parts:2 * n_parts] if has_ctx else None
    w_ref, x_ref, gate_ref, g_ref, s_ref, wr_ref, y_ref, tok_ref, logit_ref = refs[-9:]
    is_ctx = pl.program_id(0) == 0 if has_ctx else False
    row = jnp.where(is_ctx, 0, 1) if has_ctx else 1
    acc = None
    col = 0
    for p in range(n_parts):
        o = lat[p][...]
        if has_ctx:
            o = jnp.where(is_ctx, ctx[p][...], o)
        width = o.shape[1]
        part = jnp.dot(o, w_ref[col:col + width, :], preferred_element_type=F32)
        acc = part if acc is None else acc + part
        col += width
    x1 = x_ref[...] + gate_ref[row] * acc
    y_ref[...] = x1
    h = x1 * lax.rsqrt(jnp.mean(x1 * x1, axis=-1, keepdims=True) + EPS) * g_ref[row] + s_ref[row]
    _store_pieces(tok_ref, _pack_bf16_pairs(h))
    hi, lo = _split_bf16(h)
    logits = (jnp.dot(hi, wr_ref[0], preferred_element_type=F32) + jnp.dot(lo, wr_ref[0], preferred_element_type=F32)
              + jnp.dot(hi, wr_ref[1], preferred_element_type=F32))
    logit_ref[...] = logits.T


def _out_proj(o_lat, o_ctx, w, x, gate2, gain2, shift2, w_router):
    d = x.shape[1]
    bm = ROW_BLOCK
    dp = d // SC_ROW_SPLIT
    has_ctx = o_ctx is not None
    row0 = 0 if has_ctx else (x.shape[0] - o_lat[0].shape[0]) // bm
    t = x.shape[0] - row0 * bm
    n_parts = len(o_lat)
    lat_map = (lambda i: (jnp.maximum(i - 1, 0), 0)) if has_ctx else (lambda i: (i, 0))
    in_specs = [pl.BlockSpec((bm, o.shape[1]), lat_map) for o in o_lat]
    args = list(o_lat)
    if has_ctx:
        in_specs += [pl.BlockSpec((bm, o.shape[1]), lambda i: (0, 0)) for o in o_ctx]
        args += list(o_ctx)
    vec_spec = pl.BlockSpec((2, 1, d), lambda i: (0, 0, 0))
    in_specs += [pl.BlockSpec(w.shape, lambda i: (0, 0)),
                 pl.BlockSpec((bm, d), lambda i: (i + row0, 0)),
                 vec_spec, vec_spec, vec_spec,
                 pl.BlockSpec((2, d, LANES), lambda i: (0, 0, 0))]
    return pl.pallas_call(
        functools.partial(_out_proj_kernel, n_parts=n_parts, has_ctx=has_ctx),
        grid=(t // bm,),
        in_specs=in_specs,
        out_specs=[pl.BlockSpec((bm, d), lambda i: (i, 0)),
                   pl.BlockSpec((SC_ROW_SPLIT // 2, bm, dp), lambda i: (0, i, 0)),
                   pl.BlockSpec((LANES, bm), lambda i: (0, i))],
        out_shape=[jax.ShapeDtypeStruct((t, d), F32),
                   jax.ShapeDtypeStruct((SC_ROW_SPLIT // 2, t, dp), F32),
                   jax.ShapeDtypeStruct((LANES, t), F32)],
        compiler_params=_params("parallel"),
        name="out_proj_router",
    )(*args, w, x, gate2, gain2, shift2, w_router)


def _pack_bf16_pairs(x):
    bits = lax.bitcast_convert_type(x, jnp.uint32)
    rounded = (bits + jnp.uint32(0x7FFF) + ((bits >> 16) & jnp.uint32(1))) & jnp.uint32(0xFFFF0000)
    half = x.shape[1] // 2
    return lax.bitcast_convert_type(rounded[:, half:] | (rounded[:, :half] >> 16), F32)


def _unpack_bf16_pairs(packed):
    bits = lax.bitcast_convert_type(packed, jnp.uint32)
    low = lax.bitcast_convert_type(bits << 16, F32)
    high = lax.bitcast_convert_type(bits & jnp.uint32(0xFFFF0000), F32)
    return jnp.concatenate([low, high], axis=1).astype(BF16)


def _store_pieces(ref, rows):
    dp = ref.shape[2]
    for j in range(ref.shape[0]):
        ref[j] = rows[:, j * dp:(j + 1) * dp]


def _load_pieces(planes):
    return jnp.concatenate([planes[j] for j in range(planes.shape[0])], axis=1)


def _moe_kernel(blk_ref, exp_ref, lo_ref, hi_ref, n_ref, x_ref, w1_ref, w3_ref, w2_ref, y_ref):
    i = pl.program_id(0)

    @pl.when(i < n_ref[0])
    def _():
        x = _unpack_bf16_pairs(_load_pieces(x_ref[...]))
        a = jnp.dot(x, w1_ref[0, 0].astype(BF16), preferred_element_type=F32)
        b = jnp.dot(x, w3_ref[0, 0].astype(BF16), preferred_element_type=F32)
        hidden = (a * jax.nn.sigmoid(a)) * b
        y = jnp.dot(hidden.astype(BF16), w2_ref[0, 0].astype(BF16), preferred_element_type=F32)
        rows = lax.broadcasted_iota(jnp.int32, (y.shape[0], 1), 0)
        y = jnp.where((rows >= lo_ref[i]) & (rows < hi_ref[i]), y, 0.0)
        first = (i == 0) | (blk_ref[i] != blk_ref[jnp.maximum(i - 1, 0)])

        @pl.when(first)
        def _():
            _store_pieces(y_ref, y)

        @pl.when(jnp.logical_not(first))
        def _():
            _store_pieces(y_ref, _load_pieces(y_ref[...]) + y)


def _moe_experts(buf, items, w1, w3, w2, layer):
    pieces, n_rows, dp = buf.shape
    d, de = w1.shape[2], w1.shape[3]
    n_items = items[0].shape[0]
    row_map = lambda i, blk, ex, lo, hi, n: (0, blk[i], 0)
    in_row_spec = pl.BlockSpec((pieces, EXPERT_BLOCK, dp), row_map)
    row_spec = pl.BlockSpec((SC_ROW_SPLIT, EXPERT_BLOCK, dp), row_map)
    w_map = lambda i, blk, ex, lo, hi, n: (layer, ex[i], 0, 0)
    grid_spec = pltpu.PrefetchScalarGridSpec(
        num_scalar_prefetch=5,
        grid=(n_items,),
        in_specs=[in_row_spec,
                  pl.BlockSpec((1, 1, d, de), w_map),
                  pl.BlockSpec((1, 1, d, de), w_map),
                  pl.BlockSpec((1, 1, de, d), w_map)],
        out_specs=row_spec,
    )
    return pl.pallas_call(
        _moe_kernel,
        grid_spec=grid_spec,
        out_shape=jax.ShapeDtypeStruct((SC_ROW_SPLIT, n_rows, dp), F32),
        compiler_params=_params("arbitrary"),
        name="moe_experts",
    )(*items, buf, w1, w3, w2)


def _expert_items(counts, n_rows):
    e = counts.shape[0]
    n_blk = n_rows // EXPERT_BLOCK
    n_items = n_blk + e - 1
    end = jnp.cumsum(counts)
    start = end - counts
    first_blk = start // EXPERT_BLOCK
    per_expert = jnp.where(counts > 0, (end - 1) // EXPERT_BLOCK - first_blk + 1, 0)
    cum = jnp.cumsum(per_expert)
    total = cum[-1]
    w = jnp.minimum(jnp.arange(n_items, dtype=jnp.int32), total - 1)
    onehot = ((cum - per_expert)[None, :] <= w[:, None]) & (w[:, None] < cum[None, :])
    pick = lambda tab: jnp.sum(jnp.where(onehot, tab[None, :], 0), axis=1).astype(jnp.int32)
    expert = pick(jnp.arange(e, dtype=jnp.int32))
    blk = pick(first_blk) + w - pick(cum - per_expert)
    lo = jnp.clip(pick(start) - blk * EXPERT_BLOCK, 0, EXPERT_BLOCK)
    hi = jnp.clip(pick(end) - blk * EXPERT_BLOCK, 0, EXPERT_BLOCK)
    return blk, expert, lo, hi, total.astype(jnp.int32).reshape(1)


def _sc_scatter(x, pos):
    pieces, t, dp = x.shape
    n = pos.shape[0]
    kk = n // t
    offs = jnp.arange(pieces, dtype=jnp.int32)[:, None] * n
    idx = [(offs + pos[k * t:(k + 1) * t][None, :]).reshape(1, pieces * t) for k in range(kk)]
    mesh = plsc.VectorSubcoreMesh(core_axis_name="core", subcore_axis_name="subcore")

    @pl.kernel(out_type=jax.ShapeDtypeStruct((pieces * n, dp), x.dtype), mesh=mesh, scratch_types=[])
    def scatter(x_hbm, *refs):
        i_hbm, o_hbm = refs[:kk], refs[kk]

        def body(x_vmem, *i_vmem):
            for iv in i_vmem:
                pltpu.sync_copy(x_vmem, o_hbm.at[iv.at[0]])

        pltpu.emit_pipeline(
            body,
            grid=(pieces * t // SC_GATHER_WINDOW,),
            in_specs=[pl.BlockSpec((SC_GATHER_WINDOW, dp), lambda i: (i, 0))]
                     + [pl.BlockSpec((1, SC_GATHER_WINDOW), lambda i: (0, i))] * kk,
            out_specs=[],
            core_axis_name=("core", "subcore"),
            dimension_semantics=(pltpu.PARALLEL,),
        )(x_hbm, *i_hbm)

    return scatter(x.reshape(pieces * t, dp), *idx).reshape(pieces, n, dp)


def _sc_gather(x, idx):
    pieces, t, dp = x.shape
    n = idx.shape[0]
    flat = (jnp.arange(pieces, dtype=jnp.int32)[:, None] * t + idx[None, :]).reshape(1, pieces * n)
    mesh = plsc.VectorSubcoreMesh(core_axis_name="core", subcore_axis_name="subcore")

    @pl.kernel(out_type=jax.ShapeDtypeStruct((pieces * n, dp), x.dtype), mesh=mesh, scratch_types=[])
    def gather(x_hbm, i_hbm, o_hbm):
        def body(i_vmem, o_vmem):
            pltpu.sync_copy(x_hbm.at[i_vmem.at[0]], o_vmem)

        pltpu.emit_pipeline(
            body,
            grid=(pieces * n // SC_GATHER_WINDOW,),
            in_specs=[pl.BlockSpec((1, SC_GATHER_WINDOW), lambda i: (0, i))],
            out_specs=[pl.BlockSpec((SC_GATHER_WINDOW, dp), lambda i: (i, 0))],
            core_axis_name=("core", "subcore"),
            dimension_semantics=(pltpu.PARALLEL,),
        )(i_hbm, o_hbm)

    return gather(x.reshape(pieces * t, dp), flat).reshape(pieces, n, dp)


def _combine_kernel(y_ref, w_ref, x_ref, gate_ref, o_ref, *, row0):
    w = w_ref[...]
    f = w[:, 0:1] * _load_pieces(y_ref[:, 0]) + w[:, 1:2] * _load_pieces(y_ref[:, 1])
    row = jnp.minimum(pl.program_id(0) + row0, 1)
    o_ref[...] = x_ref[...] + gate_ref[row] * f


def _combine(y2, weights, x, gate2, has_ctx):
    t, d = x.shape
    bm = ROW_BLOCK
    pieces, _, _, dp = y2.shape
    return pl.pallas_call(
        functools.partial(_combine_kernel, row0=0 if has_ctx else 1),
        grid=(t // bm,),
        in_specs=[pl.BlockSpec((pieces, TOP_K, bm, dp), lambda i: (0, 0, i, 0)),
                  pl.BlockSpec((bm, TOP_K), lambda i: (i, 0)),
                  pl.BlockSpec((bm, d), lambda i: (i, 0)),
                  pl.BlockSpec((2, 1, d), lambda i: (0, 0, 0))],
        out_specs=pl.BlockSpec((bm, d), lambda i: (i, 0)),
        out_shape=jax.ShapeDtypeStruct((t, d), F32),
        compiler_params=_params("parallel"),
        name="moe_combine",
    )(y2, weights, x, gate2)


def _top1_rows(p):
    m = jnp.max(p, axis=0, keepdims=True)
    idx = lax.broadcasted_iota(jnp.int32, p.shape, 0)
    return m, jnp.min(jnp.where(p == m, idx, p.shape[0]), axis=0, keepdims=True)


def _prefix_rank(onehot):
    e, n = onehot.shape
    blk = EXPERT_BLOCK
    nb = n // blk
    earlier = (jnp.arange(blk)[:, None] < jnp.arange(blk)[None, :]).astype(F32)
    within = jnp.dot(onehot.reshape(e * nb, blk), earlier).reshape(e, n)
    tot = jnp.sum(onehot.reshape(e, nb, blk), axis=2)
    before = jnp.repeat(jnp.cumsum(tot, axis=1) - tot, blk, axis=1)
    return jnp.sum((within + before) * onehot, axis=0).astype(jnp.int32)


def _hier_moe(tokens, logits_t, b_group, b_expert, w1, w3, w2, layer):
    pieces, t, dp = tokens.shape
    pg = jax.nn.softmax(logits_t[:N_GROUPS] + b_group.astype(F32)[:, None], axis=0)
    g_prob, g_idx = _top1_rows(pg)
    le = (logits_t[N_GROUPS:N_GROUPS + N_EXPERTS] + b_expert.astype(F32)[:, None]).reshape(N_GROUPS, EXPERTS_PER_GROUP, t)
    group_iota = lax.broadcasted_iota(jnp.int32, (N_GROUPS, 1, t), 0)
    le = jnp.sum(jnp.where(group_iota == g_idx[None], le, 0.0), axis=0)
    pe = jax.nn.softmax(le, axis=0)
    p1, i1 = _top1_rows(pe)
    p2, i2 = _top1_rows(jnp.where(lax.broadcasted_iota(jnp.int32, pe.shape, 0) == i1, -1.0, pe))
    e_prob, e_idx = jnp.concatenate([p1, p2], axis=0), jnp.concatenate([i1, i2], axis=0)
    weights = g_prob * e_prob / jnp.sum(e_prob, axis=0, keepdims=True)
    flat_e = (g_idx * EXPERTS_PER_GROUP + e_idx).reshape(1, TOP_K * t)
    onehot = (flat_e == lax.broadcasted_iota(jnp.int32, (N_EXPERTS, TOP_K * t), 0)).astype(F32)
    rank = _prefix_rank(onehot)
    counts = jnp.sum(onehot, axis=1).astype(jnp.int32)
    first_row = jnp.sum(onehot * (jnp.cumsum(counts) - counts).astype(F32)[:, None], axis=0).astype(jnp.int32)
    dest = first_row + rank
    buf = _sc_scatter(tokens, dest)
    yb = _moe_experts(buf, _expert_items(counts, TOP_K * t), w1, w3, w2, layer)
    return _sc_gather(yb, dest).reshape(SC_ROW_SPLIT, TOP_K, t, dp), weights.T


def kernel(x, c, ctx, c_ctx, w_mod, b_mod, norm_mix, norm_ffn, w_in_ab, w_out_ab, qn_a, kn_a, sink_a, qn_b, kn_b, w_in_c, w_out_c, qn_c, kn_c, lam_c, subln_c, w_group, b_group, w_expert, b_expert, w1, w3, w2):
    b, s_lat, d = x.shape
    n_ctx = ctx.shape[1]
    assert b == 1 and n_ctx == ROW_BLOCK and s_lat % (2 * ROW_BLOCK) == 0
    depth = w_mod.shape[0]
    cos, sin = _rope_tables(s_lat, n_ctx)
    mods = _mod_vectors(c, c_ctx, w_mod, b_mod)
    xs = jnp.concatenate([ctx[0], x[0]], axis=0)
    has_ctx = True
    pending = None
    for l in range(depth):
        last = l == depth - 1
        i = l // 2
        sh1, sc1, gt1, sh2, sc2, gt2 = [mods[l, :2, j * d:(j + 1) * d].reshape(2, 1, d) for j in range(6)]
        gain1 = norm_mix[l].astype(F32) * (1 + sc1)
        if l % 2 == 0:
            xs, heads = _inproj_ab(xs, pending, gain1, sh1, w_in_ab[i].astype(BF16), qn_a[i], kn_a[i], qn_b[i], kn_b[i], cos, sin)
            o_lat, o_ctx = _mixer_ab(heads, n_ctx, sink_a[i], not last,
                                     _logits_bounded(qn_a[i], kn_a[i]), _logits_bounded(qn_b[i], kn_b[i]))
            w_out = w_out_ab[i]
        else:
            lam_init = 0.8 - 0.6 * math.exp(-0.3 * l)
            xs, heads = _inproj_c(xs, pending, gain1, sh1, w_in_c[i].astype(BF16), qn_c[i], kn_c[i], cos, sin)
            o_lat, o_ctx = _mixer_c(heads, n_ctx, lam_c[i], subln_c[i], lam_init, not last,
                                    _logits_bounded(qn_c[i], kn_c[i]))
            w_out = w_out_c[i]
        if last:
            has_ctx = False
        gain2 = norm_ffn[l].astype(F32) * (1 + sc2)
        w_router = jnp.zeros((d, LANES), F32).at[:, :N_GROUPS].set(w_group[l]).at[:, N_GROUPS:N_GROUPS + N_EXPERTS].set(w_expert[l])
        w_router = jnp.stack(_split_bf16(w_router))
        xs, tokens, logits = _out_proj(o_lat, o_ctx, w_out.astype(BF16), xs, gt1, gain2, sh2, w_router)
        y2, weights = _hier_moe(tokens, logits, b_group[l], b_expert[l], w1, w3, w2, l)
        pending = (y2, weights, gt2)
    return _combine(y2, weights, xs, gt2, has_ctx).reshape(b, s_lat, d)
```
